```python
import jax, jax.numpy as jnp
from jax import lax
import numpy as np

D_MODEL = 2048
BATCH = 4
SEQ = 4096
DEPTH = 2

CTX_LEN = 256
GRID_W = 64
HEAD_DIM = 128
N_HG = 4
N_NA = 8
N_CV = 4
HG_W = N_HG * HEAD_DIM
NA_W = N_NA * HEAD_DIM
CV_W = N_CV * HEAD_DIM
MIX_W = HG_W + NA_W + CV_W
IN_SPLITS = (HG_W, HG_W, HG_W, NA_W, NA_W, HG_W, HG_W, NA_W, CV_W, CV_W, CV_W)
IN_W = sum(IN_SPLITS)
CTX_PIECES = 5
HG_CHUNK = 64
NA_WIN_R = 8
NA_WIN_C = 16
NA_QBLK_C = 16
NA_KEY_C = 32
CONV_W = 3
D_FF = 5632
EPS = 1e-6
F_FLOOR = 1e-30
ATTN_SCALE = HEAD_DIM ** -0.5
NEG_INF = -1e30

kernel_name = 'hybrid_hgrn2_natten_shortconv_dit_block'


def _rms_norm(x, w):
    xf = x.astype(jnp.float32)
    y = xf * lax.rsqrt(jnp.mean(xf * xf, axis=-1, keepdims=True) + EPS)
    return (y * w.astype(jnp.float32)).astype(x.dtype)


def _modulate(h, shift, scale):
    return h * (1 + scale) + shift


def _heads(a):
    return a.reshape(a.shape[:-1] + (a.shape[-1] // HEAD_DIM, HEAD_DIM))


def _merge(a):
    return a.reshape(a.shape[:-2] + (a.shape[-2] * a.shape[-1],))


def _flip(a):
    return a[:, ::-1]


def _split_cols(u, n_pieces):
    cuts = [int(v) for v in np.cumsum(IN_SPLITS[:n_pieces])[:-1]]
    return jnp.split(u, cuts, axis=-1)


def _dwconv(x, w):
    l_ = x.shape[1]
    pad = CONV_W // 2
    xp = jnp.pad(x, ((0, 0), (pad, pad), (0, 0)))
    out = xp[:, :l_] * w[0]
    for t in range(1, CONV_W):
        out = out + xp[:, t:t + l_] * w[t]
    return out


def _hgrn2_forget(z, lb):
    zf = z.astype(jnp.float32)
    f = lb + (1.0 - lb) * jax.nn.sigmoid(zf)
    log_f = jnp.log(jnp.maximum(f, F_FLOOR))
    k = (1.0 - lb) * jax.nn.sigmoid(-zf)
    return _heads(log_f), _heads(k)


def _hgrn2_scan(log_f, k, v, s0, q=None):
    b_, l_, h_, _ = log_f.shape
    n_chunks = l_ // HG_CHUNK

    def chunks(a):
        return a.reshape(b_, n_chunks, HG_CHUNK, h_, a.shape[-1]).transpose(1, 0, 3, 2, 4)

    tri = jnp.tril(jnp.ones((HG_CHUNK, HG_CHUNK), dtype=bool))[:, :, None]
    with_out = q is not None
    xs = (chunks(log_f), chunks(k), chunks(v)) + ((chunks(q),) if with_out else ())

    def step(s, inp):
        g, kc, vc = inp[0], inp[1], inp[2]
        cum = jnp.cumsum(g, axis=2)
        tot = cum[:, :, -1]
        s_new = jnp.exp(tot)[..., None] * s + jnp.einsum('bhcd,bhce->bhde', kc * jnp.exp(tot[:, :, None] - cum), vc)
        if not with_out:
            return s_new, None
        qc = inp[3]
        o_inter = jnp.einsum('bhcd,bhde->bhce', qc * jnp.exp(cum), s)
        diff = cum[:, :, :, None, :] - cum[:, :, None, :, :]
        decay = jnp.exp(jnp.where(tri, diff, NEG_INF))
        att = jnp.einsum('bhid,bhjd,bhijd->bhij', qc, kc, decay)
        return s_new, o_inter + jnp.einsum('bhij,bhje->bhie', att, vc)

    s_fin, o = lax.scan(step, s0, xs)
    if not with_out:
        return s_fin, None
    return s_fin, o.transpose(1, 0, 3, 2, 4).reshape(b_, l_, h_, v.shape[-1])


def _hgrn2_readout(o, g, w):
    return _merge(_rms_norm(o, w)).astype(g.dtype) * jax.nn.silu(g)


def _neighbourhood_attention(q, k, v, k_ctx, v_ctx, rpb, rows):
    b_, n_, h_, d_ = q.shape
    win_r = min(NA_WIN_R, rows)
    n_cb = GRID_W // NA_QBLK_C
    qcol = np.arange(GRID_W).reshape(n_cb, NA_QBLK_C)
    kstart = np.clip(np.arange(n_cb) * NA_QBLK_C - NA_WIN_C // 2, 0, GRID_W - NA_KEY_C)
    kcol = kstart[:, None] + np.arange(NA_KEY_C)[None]
    wstart = np.clip(qcol - NA_WIN_C // 2, 0, GRID_W - NA_WIN_C)
    col_ok = (kcol[:, None, :] >= wstart[:, :, None]) & (kcol[:, None, :] < wstart[:, :, None] + NA_WIN_C)
    mask = jnp.asarray(col_ok[:, :, None, :])
    idx_c = jnp.asarray(np.clip(kcol[:, None, :] - qcol[:, :, None] + NA_WIN_C - 1, 0, 2 * NA_WIN_C - 2)[:, :, None, :])
    qg = q.reshape(b_, rows, GRID_W, h_, d_)
    kg = k.reshape(b_, rows, GRID_W, h_, d_)
    vg = v.reshape(b_, rows, GRID_W, h_, d_)
    n_loc = win_r * NA_KEY_C

    def row_block(r):
        rs = jnp.clip(r - win_r // 2, 0, rows - win_r)
        qr = lax.dynamic_index_in_dim(qg, r, axis=1, keepdims=False).reshape(b_, n_cb, NA_QBLK_C, h_, d_)
        kr = lax.dynamic_slice_in_dim(kg, rs, win_r, axis=1)[:, :, kcol]
        vr = lax.dynamic_slice_in_dim(vg, rs, win_r, axis=1)[:, :, kcol]
        idx_r = (rs + jnp.arange(win_r) - r + NA_WIN_R - 1)[None, None, :, None]
        bias = rpb[:, idx_r, idx_c].astype(jnp.float32)
        s_loc = jnp.einsum('bmqhd,brmkhd->bhmqrk', qr, kr).astype(jnp.float32) * ATTN_SCALE
        s_loc = jnp.where(mask, s_loc + bias, NEG_INF)
        s_ctx = jnp.einsum('bmqhd,bkhd->bhmqk', qr, k_ctx).astype(jnp.float32) * ATTN_SCALE
        logits = jnp.concatenate([s_loc.reshape(s_loc.shape[:4] + (n_loc,)), s_ctx], axis=-1)
        p = jax.nn.softmax(logits, axis=-1).astype(v.dtype)
        p_loc = p[..., :n_loc].reshape(s_loc.shape)
        o = jnp.einsum('bhmqrk,brmkhd->bmqhd', p_loc, vr) + jnp.einsum('bhmqk,bkhd->bmqhd', p[..., n_loc:], v_ctx)
        return o.reshape(b_, GRID_W, h_, d_)

    o = lax.map(row_block, jnp.arange(rows))
    return jnp.moveaxis(o, 0, 1).reshape(b_, n_, h_, d_)


def _context_attention(q, k, v):
    s = jnp.einsum('bqhd,bkhd->bhqk', q, k).astype(jnp.float32) * ATTN_SCALE
    p = jax.nn.softmax(s, axis=-1).astype(v.dtype)
    return jnp.einsum('bhqk,bkhd->bqhd', p, v)


def _short_conv(b_gate, c_gate, v, w):
    return b_gate * _dwconv(c_gate * v, w)


def _conv_ffn(h, w_up, cw, cb, w_down):
    u = _dwconv(h @ w_up, cw) + cb
    gate, val = jnp.split(u, 2, axis=-1)
    return (jax.nn.silu(gate) * val) @ w_down


def _layer(x, ctx, ada, ada_ctx, lb_fw, lb_bw, ln1, ln2, w_in, hg_norm, q_norm, k_norm, rpb, na_onorm,
           cv_w, cv_onorm, w_out, w_up, f_cw, f_cb, w_down, rows, last):
    sh1, sc1, g1, sh2, sc2, g2 = jnp.split(ada[:, None, :], 6, axis=-1)
    cm = jnp.split(ada_ctx, 2 if last else 6)
    h = _modulate(_rms_norm(x, ln1), sh1, sc1)
    hc = _modulate(_rms_norm(ctx, ln1), cm[0], cm[1])
    (f_fw, f_bw, i_v, na_k, na_v, hg_q, hg_g, na_q, cv_b, cv_c, cv_v) = _split_cols(h @ w_in, len(IN_SPLITS))
    n_c = CTX_PIECES if last else len(IN_SPLITS)
    uc = _split_cols(hc @ w_in[:, :sum(IN_SPLITS[:n_c])], n_c)

    s0 = jnp.zeros((x.shape[0], N_HG, HEAD_DIM, HEAD_DIM), jnp.float32)
    cf_fw, ck_fw = _hgrn2_forget(uc[0], lb_fw)
    cf_bw, ck_bw = _hgrn2_forget(uc[1], lb_bw)
    c_val = _heads(uc[2].astype(jnp.float32))
    c_q = None if last else _heads(jax.nn.silu(uc[5]).astype(jnp.float32))
    s_fw, co_fw = _hgrn2_scan(cf_fw, ck_fw, c_val, s0, c_q)
    s_bw, co_bw = _hgrn2_scan(_flip(cf_bw), _flip(ck_bw), _flip(c_val), s0, None if last else _flip(c_q))
    lf_fw, lk_fw = _hgrn2_forget(f_fw, lb_fw)
    lf_bw, lk_bw = _hgrn2_forget(f_bw, lb_bw)
    l_val = _heads(i_v.astype(jnp.float32))
    l_q = _heads(jax.nn.silu(hg_q).astype(jnp.float32))
    _, o_fw = _hgrn2_scan(lf_fw, lk_fw, l_val, s_fw, l_q)
    _, o_bw = _hgrn2_scan(_flip(lf_bw), _flip(lk_bw), _flip(l_val), s_bw, _flip(l_q))
    hg_out = _hgrn2_readout(o_fw + _flip(o_bw), hg_g, hg_norm)

    k_c = _rms_norm(_heads(uc[3]), k_norm)
    v_c = _heads(uc[4])
    na_o = _neighbourhood_attention(_rms_norm(_heads(na_q), q_norm), _rms_norm(_heads(na_k), k_norm),
                                    _heads(na_v), k_c, v_c, rpb, rows)
    na_onorm_h = na_onorm.reshape(N_NA, HEAD_DIM)
    na_out = _merge(_rms_norm(na_o, na_onorm_h))

    cv_onorm_h = cv_onorm.reshape(N_CV, HEAD_DIM)
    cv_out = _merge(_rms_norm(_heads(_short_conv(cv_b, cv_c, cv_v, cv_w)), cv_onorm_h))

    x = x + g1 * (jnp.concatenate([hg_out, na_out, cv_out], axis=-1) @ w_out)
    x = x + g2 * _conv_ffn(_modulate(_rms_norm(x, ln2), sh2, sc2), w_up, f_cw, f_cb, w_down)
    if last:
        return x, None

    hg_c = _hgrn2_readout(co_fw + _flip(co_bw), uc[6], hg_norm)
    na_c = _merge(_rms_norm(_context_attention(_rms_norm(_heads(uc[7]), q_norm), k_c, v_c), na_onorm_h))
    cv_co = _merge(_rms_norm(_heads(_short_conv(uc[8], uc[9], uc[10], cv_w)), cv_onorm_h))
    ctx = ctx + cm[2] * (jnp.concatenate([hg_c, na_c, cv_co], axis=-1) @ w_out)
    ctx = ctx + cm[5] * _conv_ffn(_modulate(_rms_norm(ctx, ln2), cm[3], cm[4]), w_up, f_cw, f_cb, w_down)
    return x, ctx


def setup_inputs(seed: int = 0) -> dict:
    key = jax.random.key(seed)
    ks = jax.random.split(key, 24)
    d = D_MODEL

    def nrm(k, shape, scale):
        return jax.random.normal(k, shape, jnp.float32) * scale

    return {
        'x': nrm(ks[0], (BATCH, SEQ, d), 1.0),
        'c': nrm(ks[1], (BATCH, d), 1.0),
        'ctx': nrm(ks[2], (BATCH, CTX_LEN, d), 1.0),
        'c_ctx': nrm(ks[3], (d,), 1.0),
        'w_ada': nrm(ks[4], (DEPTH, d, 6 * d), 0.5 * d ** -0.5),
        'b_ada': nrm(ks[5], (DEPTH, 6 * d), 0.01),
        'ln1_w': 1.0 + nrm(ks[6], (DEPTH, d), 0.02),
        'ln2_w': 1.0 + nrm(ks[7], (DEPTH, d), 0.02),
        'w_in': nrm(ks[8], (DEPTH, d, IN_W), d ** -0.5),
        'hg_lb_logits': nrm(ks[9], (2, DEPTH, HG_W), 0.5),
        'hg_norm_w': 1.0 + nrm(ks[10], (DEPTH, HEAD_DIM), 0.02),
        'na_q_norm_w': 1.0 + nrm(ks[11], (DEPTH, HEAD_DIM), 0.02),
        'na_k_norm_w': 1.0 + nrm(ks[12], (DEPTH, HEAD_DIM), 0.02),
        'na_rpb': nrm(ks[13], (DEPTH, N_NA, 2 * NA_WIN_R - 1, 2 * NA_WIN_C - 1), 0.1),
        'na_out_norm_w': 1.0 + nrm(ks[14], (DEPTH, NA_W), 0.02),
        'cv_w': nrm(ks[15], (DEPTH, CONV_W, CV_W), CONV_W ** -0.5),
        'cv_out_norm_w': 1.0 + nrm(ks[16], (DEPTH, CV_W), 0.02),
        'w_out': nrm(ks[17], (DEPTH, MIX_W, d), MIX_W ** -0.5),
        'w_up': nrm(ks[18], (DEPTH, d, 2 * D_FF), d ** -0.5),
        'ffn_conv_w': nrm(ks[19], (DEPTH, CONV_W, 2 * D_FF), CONV_W ** -0.5),
        'ffn_conv_b': nrm(ks[20], (DEPTH, 2 * D_FF), 0.01),
        'w_down': nrm(ks[21], (DEPTH, D_FF, d), D_FF ** -0.5),
    }


def reference(x, c, ctx, c_ctx, w_ada, b_ada, ln1_w, ln2_w, w_in, hg_lb_logits, hg_norm_w, na_q_norm_w,
              na_k_norm_w, na_rpb, na_out_norm_w, cv_w, cv_out_norm_w, w_out, w_up, ffn_conv_w, ffn_conv_b, w_down):
    rows = x.shape[1] // GRID_W
    lb_sm = jax.nn.softmax(hg_lb_logits.astype(jnp.float32), axis=1)
    lb_all = jnp.cumsum(lb_sm, axis=1) - lb_sm[:, :1]
    silu_c = jax.nn.silu(c)
    silu_cc = jax.nn.silu(c_ctx)
    for l in range(DEPTH):
        last = l == DEPTH - 1
        ada = silu_c @ w_ada[l] + b_ada[l]
        n_ada = (2 if last else 6) * D_MODEL
        ada_ctx = silu_cc @ w_ada[l][:, :n_ada] + b_ada[l][:n_ada]
        x, ctx = _layer(x, ctx, ada, ada_ctx, lb_all[0, l], lb_all[1, l], ln1_w[l], ln2_w[l], w_in[l],
                        hg_norm_w[l], na_q_norm_w[l], na_k_norm_w[l], na_rpb[l], na_out_norm_w[l], cv_w[l],
                        cv_out_norm_w[l], w_out[l], w_up[l], ffn_conv_w[l], ffn_conv_b[l], w_down[l], rows, last)
    return x
```

```python
import functools

import numpy as np
import jax
import jax.numpy as jnp
from jax import lax
from jax.experimental import pallas as pl
from jax.experimental.pallas import tpu as pltpu

F32 = jnp.float32
BF16 = jnp.bfloat16

D_MODEL = 2048
DEPTH = 2
GRID_W = 64
HEAD_DIM = 128
N_HG = 4
N_NA = 8
N_CV = 4
HG_W = N_HG * HEAD_DIM
NA_W = N_NA * HEAD_DIM
CV_W = N_CV * HEAD_DIM
IN_SPLITS = (HG_W, HG_W, HG_W, NA_W, NA_W, HG_W, HG_W, NA_W, CV_W, CV_W, CV_W)
IN_W = sum(IN_SPLITS)
(OFF_FFW, OFF_FBW, OFF_I, OFF_NAK, OFF_NAV, OFF_HGQ, OFF_HGG, OFF_NAQ, OFF_CVB, OFF_CVC,
 OFF_CVV) = [int(v) // HEAD_DIM for v in np.cumsum((0,) + IN_SPLITS[:-1])]
NA_WIN_R = 8
NA_WIN_C = 16
NA_QBLK_C = 16
NA_KEY_C = 32
NA_KEY_R = 16
CONV_W = 3
D_FF = 5632
EPS = 1e-6
F_FLOOR = 1e-30
ATTN_SCALE = HEAD_DIM ** -0.5
NEG_INF = -1e30

HG_CHUNK = 256
SUBLANE = 8
HALO = 16
VMEM_LIMIT = 56 * 1024 * 1024


def _params(*sem):
    return pltpu.CompilerParams(dimension_semantics=sem, vmem_limit_bytes=VMEM_LIMIT)


def _dot(a, b):
    return jnp.dot(a, b, preferred_element_type=F32)


def _dot_nt(a, b):
    return lax.dot_general(a, b, (((1,), (1,)), ((), ())), preferred_element_type=F32)


def _dot_tn(a, b):
    return lax.dot_general(a, b, (((0,), (0,)), ((), ())), preferred_element_type=F32)


def _rms(x, w):
    return x * lax.rsqrt(jnp.mean(x * x, axis=-1, keepdims=True) + EPS) * w


def _silu(x):
    return x * jax.nn.sigmoid(x)


def _ada_kernel(s_ref, w_ref, b_ref, o_ref):
    s = _silu(s_ref[...]).astype(BF16)
    o_ref[0] = _dot(s, w_ref[0].astype(BF16)) + b_ref[0]


def _ada_table(cond, w_ada, b_ada):
    depth, d, n = w_ada.shape
    tn = 1024
    return pl.pallas_call(
        _ada_kernel,
        grid=(depth, n // tn),
        in_specs=[pl.BlockSpec((SUBLANE, d), lambda l, j: (0, 0)),
                  pl.BlockSpec((1, d, tn), lambda l, j: (l, 0, j)),
                  pl.BlockSpec((1, 1, tn), lambda l, j: (l, 0, j))],
        out_specs=pl.BlockSpec((1, SUBLANE, tn), lambda l, j: (l, 0, j)),
        out_shape=jax.ShapeDtypeStruct((depth, SUBLANE, n), F32),
        compiler_params=_params("parallel", "parallel"),
    )(cond, w_ada, b_ada.reshape(depth, 1, n))


def _norm_mod_kernel(x_ref, w_ref, sh_ref, sc_ref, o_ref):
    y = _rms(x_ref[...], w_ref[...])
    o_ref[...] = (y * (1.0 + sc_ref[0]) + sh_ref[0]).astype(o_ref.dtype)


def _norm_mod(x, w, shift, scale, mod_of_tile, tr):
    rows, d = x.shape
    mod = lambda i: (mod_of_tile(i), 0, 0)
    return pl.pallas_call(
        _norm_mod_kernel,
        grid=(rows // tr,),
        in_specs=[pl.BlockSpec((tr, d), lambda i: (i, 0)),
                  pl.BlockSpec((1, d), lambda i: (0, 0)),
                  pl.BlockSpec((1, 1, d), mod),
                  pl.BlockSpec((1, 1, d), mod)],
        out_specs=pl.BlockSpec((tr, d), lambda i: (i, 0)),
        out_shape=jax.ShapeDtypeStruct((rows, d), BF16),
        compiler_params=_params("parallel"),
    )(x, w.reshape(1, d), shift, scale)


def _mm_kernel(a_ref, w_ref, o_ref):
    o_ref[...] = _dot(a_ref[...], w_ref[...]).astype(o_ref.dtype)


def _matmul(a, w, out_dtype, tm, tn):
    m, k = a.shape
    n = w.shape[1]
    return pl.pallas_call(
        _mm_kernel,
        grid=(m // tm, n // tn),
        in_specs=[pl.BlockSpec((tm, k), lambda i, j: (i, 0)),
                  pl.BlockSpec((k, tn), lambda i, j: (0, j))],
        out_specs=pl.BlockSpec((tm, tn), lambda i, j: (i, j)),
        out_shape=jax.ShapeDtypeStruct((m, n), out_dtype),
        compiler_params=_params("parallel", "parallel"),
    )(a, w)


def _mm_res_kernel(*refs, splits):
    n_a = len(splits)
    a_refs = refs[:n_a]
    w_ref, x_ref, g_ref, o_ref = refs[n_a:]
    acc = None
    off = 0
    for a_ref, width in zip(a_refs, splits):
        part = _dot(a_ref[...], w_ref[off:off + width, :])
        acc = part if acc is None else acc + part
        off += width
    o_ref[...] = x_ref[...] + g_ref[0] * acc


def _matmul_residual(a_list, w, x, gate, mod_of_tile, tm, tn):
    m = x.shape[0]
    k, n = w.shape
    splits = tuple(a.shape[1] for a in a_list)
    assert sum(splits) == k
    in_specs = [pl.BlockSpec((tm, s), lambda i, j: (i, 0)) for s in splits]
    in_specs += [pl.BlockSpec((k, tn), lambda i, j: (0, j)),
                 pl.BlockSpec((tm, tn), lambda i, j: (i, j)),
                 pl.BlockSpec((1, 1, tn), lambda i, j: (mod_of_tile(i), 0, j))]
    return pl.pallas_call(
        functools.partial(_mm_res_kernel, splits=splits),
        grid=(m // tm, n // tn),
        in_specs=in_specs,
        out_specs=pl.BlockSpec((tm, tn), lambda i, j: (i, j)),
        out_shape=jax.ShapeDtypeStruct((m, n), F32),
        compiler_params=_params("parallel", "parallel"),
    )(*a_list, w, x, gate)


def _shift_rows(x, prev_row, next_row):
    r = x.shape[0]
    idx = lax.broadcasted_iota(jnp.int32, x.shape, 0)
    down = jnp.where(idx == 0, prev_row, pltpu.roll(x, 1, axis=0))
    up = jnp.where(idx == r - 1, next_row, pltpu.roll(x, r - 1, axis=0))
    return down, up


def _halo_specs(tr, tc, col_of, tiles_per_seq, n_rows):
    per = tr // HALO
    last_blk = n_rows // HALO - 1
    cur = pl.BlockSpec((tr, tc), lambda i, j: (i, col_of(j)))
    prev = pl.BlockSpec((HALO, tc), lambda i, j: (jnp.maximum(i * per - 1, 0), col_of(j)))
    nxt = pl.BlockSpec((HALO, tc), lambda i, j: (jnp.minimum((i + 1) * per, last_blk), col_of(j)))
    return cur, prev, nxt


def _seq_edges(tiles_per_seq):
    i = pl.program_id(0)
    pos = i % tiles_per_seq
    return (pos != 0).astype(F32), (pos != tiles_per_seq - 1).astype(F32)


def _ffn_conv_kernel(g_ref, gp_ref, gn_ref, v_ref, vp_ref, vn_ref, wg_ref, wv_ref, bg_ref, bv_ref, o_ref,
                     *, tiles_per_seq):
    has_prev, has_next = _seq_edges(tiles_per_seq)

    def conv(c_ref, p_ref, n_ref, w_ref, b_ref):
        x = c_ref[...].astype(F32)
        prev_row = p_ref[HALO - 1:HALO, :].astype(F32) * has_prev
        next_row = n_ref[0:1, :].astype(F32) * has_next
        down, up = _shift_rows(x, prev_row, next_row)
        return down * w_ref[0:1, :] + x * w_ref[1:2, :] + up * w_ref[2:3, :] + b_ref[...]

    gate = conv(g_ref, gp_ref, gn_ref, wg_ref, bg_ref)
    val = conv(v_ref, vp_ref, vn_ref, wv_ref, bv_ref)
    o_ref[...] = (_silu(gate) * val).astype(o_ref.dtype)


def _ffn_conv_gate(up, cw, cb, seq_len, tr, tc):
    rows = up.shape[0]
    nj = D_FF // tc
    tiles_per_seq = seq_len // tr
    g_specs = _halo_specs(tr, tc, lambda j: j, tiles_per_seq, rows)
    v_specs = _halo_specs(tr, tc, lambda j: j + nj, tiles_per_seq, rows)
    w_g = pl.BlockSpec((CONV_W, tc), lambda i, j: (0, j))
    w_v = pl.BlockSpec((CONV_W, tc), lambda i, j: (0, j + nj))
    b_g = pl.BlockSpec((1, tc), lambda i, j: (0, j))
    b_v = pl.BlockSpec((1, tc), lambda i, j: (0, j + nj))
    cb2 = cb.reshape(1, 2 * D_FF)
    return pl.pallas_call(
        functools.partial(_ffn_conv_kernel, tiles_per_seq=tiles_per_seq),
        grid=(rows // tr, nj),
        in_specs=[*g_specs, *v_specs, w_g, w_v, b_g, b_v],
        out_specs=pl.BlockSpec((tr, tc), lambda i, j: (i, j)),
        out_shape=jax.ShapeDtypeStruct((rows, D_FF), BF16),
        compiler_params=_params("parallel", "parallel"),
    )(up, up, up, up, up, up, cw, cw, cb2, cb2)


def _short_conv_kernel(b_ref, c_ref, cp_ref, cn_ref, v_ref, vp_ref, vn_ref, w_ref, nw_ref, o_ref, *, tiles_per_seq):
    has_prev, has_next = _seq_edges(tiles_per_seq)
    p = c_ref[...] * v_ref[...]
    prev_row = cp_ref[HALO - 1:HALO, :] * vp_ref[HALO - 1:HALO, :] * has_prev
    next_row = cn_ref[0:1, :] * vn_ref[0:1, :] * has_next
    down, up = _shift_rows(p, prev_row, next_row)
    y = b_ref[...] * (down * w_ref[0:1, :] + p * w_ref[1:2, :] + up * w_ref[2:3, :])
    for h in range(N_CV):
        sl = slice(h * HEAD_DIM, (h + 1) * HEAD_DIM)
        o_ref[:, sl] = _rms(y[:, sl], nw_ref[:, sl]).astype(o_ref.dtype)


def _short_conv(u, cv_w, cv_onorm, seq_len, tr):
    rows = u.shape[0]
    tiles_per_seq = seq_len // tr
    blk = lambda off: (lambda j: off * HEAD_DIM // CV_W)
    b_spec = pl.BlockSpec((tr, CV_W), lambda i, j: (i, OFF_CVB * HEAD_DIM // CV_W))
    c_specs = _halo_specs(tr, CV_W, blk(OFF_CVC), tiles_per_seq, rows)
    v_specs = _halo_specs(tr, CV_W, blk(OFF_CVV), tiles_per_seq, rows)
    return pl.pallas_call(
        functools.partial(_short_conv_kernel, tiles_per_seq=tiles_per_seq),
        grid=(rows // tr, 1),
        in_specs=[b_spec, *c_specs, *v_specs,
                  pl.BlockSpec((CONV_W, CV_W), lambda i, j: (0, 0)),
                  pl.BlockSpec((1, CV_W), lambda i, j: (0, 0))],
        out_specs=pl.BlockSpec((tr, CV_W), lambda i, j: (i, 0)),
        out_shape=jax.ShapeDtypeStruct((rows, CV_W), BF16),
        compiler_params=_params("parallel", "arbitrary"),
    )(u, u, u, u, u, u, u, cv_w, cv_onorm.reshape(1, CV_W))


def _hgrn_structure(chunk, reverse):
    idx = np.arange(chunk)
    i, t = idx[:, None], idx[None, :]
    mats = [t <= i, t > i]
    masks = []
    s = chunk // 2
    while s >= 1:
        blk = idx // (2 * s)
        upper = (idx % (2 * s)) >= s
        mid = blk * 2 * s + s - 1
        a = np.where(upper[:, None], (t > mid[:, None]) & (t <= i), (t > i) & (t <= mid[:, None]))
        mats.append(a)
        masks.append((blk[:, None] == blk[None, :]) & upper[:, None] & ~upper[None, :])
        s //= 2
    mats = np.stack(mats).astype(np.float32)
    masks = np.stack(masks).astype(np.float32)
    if reverse:
        mats = mats[:, ::-1, ::-1]
        masks = masks[:, ::-1, ::-1]
    return mats.reshape(-1, chunk), masks


def _hgrn_kernel(*refs, chunk, n_levels, reverse, final):
    if final:
        (z_ref, v_ref, q_ref, lb_ref, a_ref, m_ref, s0_ref, g_ref, ofw_ref, nw_ref,
         o_ref, sfin_ref, st_ref) = refs
    else:
        z_ref, v_ref, q_ref, lb_ref, a_ref, m_ref, s0_ref, o_ref, sfin_ref, st_ref = refs
    c = pl.program_id(1)

    @pl.when(c == 0)
    def _():
        st_ref[...] = s0_ref[0]

    z = z_ref[...]
    lb = lb_ref[...]
    f = lb + (1.0 - lb) * jax.nn.sigmoid(z)
    log_f = jnp.log(jnp.maximum(f, F_FLOOR))
    k_all = (1.0 - lb) * jax.nn.sigmoid(-z)
    q_all = _silu(q_ref[...])
    v_all = v_ref[...]
    hi = log_f.astype(BF16)
    lo = (log_f - hi.astype(F32)).astype(BF16)
    a = a_ref[...]
    expo = _dot(a, hi) + _dot(a, lo)
    tot_row = 0 if reverse else chunk - 1
    for h in range(N_HG):
        sl = slice(h * HEAD_DIM, (h + 1) * HEAD_DIM)
        q, k, v = q_all[:, sl], k_all[:, sl], v_all[:, sl]
        e_cum = jnp.exp(expo[0:chunk, sl])
        e_rem = jnp.exp(expo[chunk:2 * chunk, sl])
        e_tot = jnp.exp(expo[tot_row:tot_row + 1, sl])
        st = st_ref[h]
        o = _dot_nt((q * e_cum).astype(BF16), st.astype(BF16))
        att = None
        for lvl in range(n_levels):
            e = jnp.exp(expo[(2 + lvl) * chunk:(3 + lvl) * chunk, sl])
            part = m_ref[lvl] * _dot_nt((q * e).astype(BF16), (k * e).astype(BF16))
            att = part if att is None else att + part
        diag = jnp.sum(q * k, axis=-1, keepdims=True)
        o = o + _dot(att.astype(BF16), v.astype(BF16)) + diag * v
        st_ref[h] = st * e_tot + _dot_tn(v.astype(BF16), (k * e_rem).astype(BF16))
        if final:
            y = _rms(o + ofw_ref[:, sl], nw_ref[...]) * _silu(g_ref[:, sl])
            o_ref[:, sl] = y.astype(o_ref.dtype)
        else:
            o_ref[:, sl] = o.astype(o_ref.dtype)

    @pl.when(c == pl.num_programs(1) - 1)
    def _():
        sfin_ref[0] = st_ref[...]


def _hgrn_scan(u, lb, s0, seq_len, *, reverse, final=False, o_fw=None, norm_w=None):
    rows = u.shape[0]
    batch = rows // seq_len
    chunk = min(HG_CHUNK, seq_len)
    n_chunks = seq_len // chunk
    mats, masks = _hgrn_structure(chunk, reverse)
    n_levels = masks.shape[0]
    per_blk = HG_W // HEAD_DIM

    def tok(off):
        col = off // per_blk
        if reverse:
            return pl.BlockSpec((chunk, HG_W), lambda b, c: (b * n_chunks + n_chunks - 1 - c, col))
        return pl.BlockSpec((chunk, HG_W), lambda b, c: (b * n_chunks + c, col))

    const2 = lambda shape: pl.BlockSpec(shape, lambda b, c: (0, 0))
    state_spec = pl.BlockSpec((1, N_HG, HEAD_DIM, HEAD_DIM), lambda b, c: (b, 0, 0, 0))
    in_specs = [tok(OFF_FBW if reverse else OFF_FFW), tok(OFF_I), tok(OFF_HGQ),
                const2((1, HG_W)), const2(mats.shape),
                pl.BlockSpec(masks.shape, lambda b, c: (0, 0, 0)), state_spec]
    args = [u, u, u, lb.reshape(1, HG_W), jnp.asarray(mats, BF16), jnp.asarray(masks, F32), s0]
    if final:
        in_specs += [tok(OFF_HGG), tok(0), const2((1, HEAD_DIM))]
        args += [u, o_fw, norm_w.reshape(1, HEAD_DIM)]
    out, s_fin = pl.pallas_call(
        functools.partial(_hgrn_kernel, chunk=chunk, n_levels=n_levels, reverse=reverse, final=final),
        grid=(batch, n_chunks),
        in_specs=in_specs,
        out_specs=[tok(0), state_spec],
        out_shape=[jax.ShapeDtypeStruct((rows, HG_W), BF16 if final else F32),
                   jax.ShapeDtypeStruct((batch, N_HG, HEAD_DIM, HEAD_DIM), F32)],
        scratch_shapes=[pltpu.VMEM((N_HG, HEAD_DIM, HEAD_DIM), F32)],
        compiler_params=_params("parallel", "arbitrary"),
    )(*args)
    return out, s_fin


def _na_bias_table(rpb, rows):
    n_rb = rows // NA_WIN_R
    n_cb = GRID_W // NA_QBLK_C
    idx_r, idx_c, valid = [], [], []
    for rb in (0, min(1, n_rb - 1), n_rb - 1):
        k_row0 = int(np.clip(rb * NA_WIN_R - NA_WIN_R // 2, 0, rows - NA_KEY_R))
        q_r = rb * NA_WIN_R + np.arange(NA_WIN_R)
        k_r = k_row0 + np.arange(NA_KEY_R)
        r_start = np.clip(q_r - NA_WIN_R // 2, 0, rows - NA_WIN_R)
        ok_r = (k_r[None, :] >= r_start[:, None]) & (k_r[None, :] < r_start[:, None] + NA_WIN_R)
        ir = np.clip(k_r[None, :] - q_r[:, None] + NA_WIN_R - 1, 0, 2 * NA_WIN_R - 2)
        for cb in range(n_cb):
            k_col0 = int(np.clip(cb * NA_QBLK_C - NA_WIN_C // 2, 0, GRID_W - NA_KEY_C))
            q_c = cb * NA_QBLK_C + np.arange(NA_QBLK_C)
            k_c = k_col0 + np.arange(NA_KEY_C)
            c_start = np.clip(q_c - NA_WIN_C // 2, 0, GRID_W - NA_WIN_C)
            ok_c = (k_c[None, :] >= c_start[:, None]) & (k_c[None, :] < c_start[:, None] + NA_WIN_C)
            ic = np.clip(k_c[None, :] - q_c[:, None] + NA_WIN_C - 1, 0, 2 * NA_WIN_C - 2)
            shape = (NA_WIN_R, NA_QBLK_C, NA_KEY_R, NA_KEY_C)
            n_q, n_k = NA_WIN_R * NA_QBLK_C, NA_KEY_R * NA_KEY_C
            idx_r.append(np.broadcast_to(ir[:, None, :, None], shape).reshape(n_q, n_k))
            idx_c.append(np.broadcast_to(ic[None, :, None, :], shape).reshape(n_q, n_k))
            valid.append(np.broadcast_to(ok_r[:, None, :, None] & ok_c[None, :, None, :], shape).reshape(n_q, n_k))
    shape5 = (3, n_cb, NA_WIN_R * NA_QBLK_C, NA_KEY_R * NA_KEY_C)
    idx_r = np.stack(idx_r).reshape(shape5)
    idx_c = np.stack(idx_c).reshape(shape5)
    valid = np.stack(valid).reshape(shape5)
    bias = rpb.astype(F32)[:, idx_r, idx_c]
    return jnp.where(jnp.asarray(valid)[None], bias, NEG_INF)


def _na_kernel(q_ref, k0_ref, k1_ref, k2_ref, k3_ref, v0_ref, v1_ref, v2_ref, v3_ref, kc_ref, vc_ref, bias_ref,
               qw_ref, kw_ref, ow_ref, o_ref):
    n_cb = GRID_W // NA_QBLK_C
    q = _rms(q_ref[...], qw_ref[...]) * ATTN_SCALE
    q3 = q.reshape(NA_WIN_R, GRID_W, HEAD_DIM)
    k = jnp.concatenate([r[...] for r in (k0_ref, k1_ref, k2_ref, k3_ref)], axis=0)
    k3 = _rms(k, kw_ref[...]).reshape(NA_KEY_R, GRID_W, HEAD_DIM)
    v3 = jnp.concatenate([r[...] for r in (v0_ref, v1_ref, v2_ref, v3_ref)], axis=0).reshape(
        NA_KEY_R, GRID_W, HEAD_DIM)
    k_ctx = _rms(kc_ref[...], kw_ref[...]).astype(BF16)
    v_ctx = vc_ref[...].astype(BF16)
    n_q, n_k = NA_WIN_R * NA_QBLK_C, NA_KEY_R * NA_KEY_C
    for cb in range(n_cb):
        q0 = cb * NA_QBLK_C
        k_col0 = min(max(q0 - NA_WIN_C // 2, 0), GRID_W - NA_KEY_C)
        qm = q3[:, q0:q0 + NA_QBLK_C, :].reshape(n_q, HEAD_DIM).astype(BF16)
        km = k3[:, k_col0:k_col0 + NA_KEY_C, :].reshape(n_k, HEAD_DIM).astype(BF16)
        vm = v3[:, k_col0:k_col0 + NA_KEY_C, :].reshape(n_k, HEAD_DIM).astype(BF16)
        s_loc = _dot_nt(qm, km) + bias_ref[0, 0, cb]
        s_ctx = _dot_nt(qm, k_ctx)
        mx = jnp.maximum(jnp.max(s_loc, axis=-1, keepdims=True), jnp.max(s_ctx, axis=-1, keepdims=True))
        p_loc = jnp.exp(s_loc - mx)
        p_ctx = jnp.exp(s_ctx - mx)
        den = jnp.sum(p_loc, axis=-1, keepdims=True) + jnp.sum(p_ctx, axis=-1, keepdims=True)
        o = (_dot(p_loc.astype(BF16), vm) + _dot(p_ctx.astype(BF16), v_ctx)) / den
        o = _rms(o, ow_ref[0]).astype(o_ref.dtype)
        for r in range(NA_WIN_R):
            o_ref[r * GRID_W + q0:r * GRID_W + q0 + NA_QBLK_C, :] = o[r * NA_QBLK_C:(r + 1) * NA_QBLK_C, :]


def _neighbourhood_attention(u, u_ctx, bias, q_norm, k_norm, out_norm, batch, rows, ctx_len):
    assert rows % NA_WIN_R == 0 and rows >= NA_KEY_R
    n_rb = rows // NA_WIN_R
    q_blk = NA_WIN_R * GRID_W
    k_blk = NA_KEY_R * GRID_W // 4
    k_per_batch = rows * GRID_W // k_blk

    def key_spec(col, part):
        def index(h, rb, b):
            first = jnp.clip(rb * NA_WIN_R - NA_WIN_R // 2, 0, rows - NA_KEY_R) * GRID_W // k_blk
            return (b * k_per_batch + first + part, col + h)
        return pl.BlockSpec((k_blk, HEAD_DIM), index)

    def row_class(rb):
        return jnp.where(rb == 0, 0, jnp.where(rb == n_rb - 1, 2, 1))

    n_cb = GRID_W // NA_QBLK_C
    vec = pl.BlockSpec((1, HEAD_DIM), lambda h, rb, b: (0, 0))
    in_specs = [pl.BlockSpec((q_blk, HEAD_DIM), lambda h, rb, b: (b * n_rb + rb, OFF_NAQ + h))]
    in_specs += [key_spec(OFF_NAK, p) for p in range(4)] + [key_spec(OFF_NAV, p) for p in range(4)]
    in_specs += [pl.BlockSpec((ctx_len, HEAD_DIM), lambda h, rb, b: (b, OFF_NAK + h)),
                 pl.BlockSpec((ctx_len, HEAD_DIM), lambda h, rb, b: (b, OFF_NAV + h)),
                 pl.BlockSpec((1, 1, n_cb, NA_WIN_R * NA_QBLK_C, NA_KEY_R * NA_KEY_C),
                              lambda h, rb, b: (h, row_class(rb), 0, 0, 0)),
                 vec, vec,
                 pl.BlockSpec((1, 1, HEAD_DIM), lambda h, rb, b: (h, 0, 0))]
    return pl.pallas_call(
        _na_kernel,
        grid=(N_NA, n_rb, batch),
        in_specs=in_specs,
        out_specs=pl.BlockSpec((q_blk, HEAD_DIM), lambda h, rb, b: (b * n_rb + rb, h)),
        out_shape=jax.ShapeDtypeStruct((u.shape[0], NA_W), BF16),
        compiler_params=_params("parallel", "parallel", "parallel"),
    )(u, u, u, u, u, u, u, u, u, u_ctx, u_ctx, bias, q_norm.reshape(1, HEAD_DIM), k_norm.reshape(1, HEAD_DIM),
      out_norm.reshape(N_NA, 1, HEAD_DIM))


def _ctx_attn_kernel(q_ref, k_ref, v_ref, qw_ref, kw_ref, ow_ref, o_ref):
    q = (_rms(q_ref[...], qw_ref[...]) * ATTN_SCALE).astype(BF16)
    k = _rms(k_ref[...], kw_ref[...]).astype(BF16)
    s = _dot_nt(q, k)
    p = jnp.exp(s - jnp.max(s, axis=-1, keepdims=True))
    o = _dot(p.astype(BF16), v_ref[...].astype(BF16)) / jnp.sum(p, axis=-1, keepdims=True)
    o_ref[...] = _rms(o, ow_ref[0]).astype(o_ref.dtype)


def _context_attention(u_ctx, q_norm, k_norm, out_norm, batch, ctx_len):
    vec = pl.BlockSpec((1, HEAD_DIM), lambda b, h: (0, 0))
    tok = lambda off: pl.BlockSpec((ctx_len, HEAD_DIM), lambda b, h: (b, off + h))
    return pl.pallas_call(
        _ctx_attn_kernel,
        grid=(batch, N_NA),
        in_specs=[tok(OFF_NAQ), tok(OFF_NAK), tok(OFF_NAV), vec, vec,
                  pl.BlockSpec((1, 1, HEAD_DIM), lambda b, h: (h, 0, 0))],
        out_specs=tok(0),
        out_shape=jax.ShapeDtypeStruct((u_ctx.shape[0], NA_W), BF16),
        compiler_params=_params("parallel", "parallel"),
    )(u_ctx, u_ctx, u_ctx, q_norm.reshape(1, HEAD_DIM), k_norm.reshape(1, HEAD_DIM),
      out_norm.reshape(N_NA, 1, HEAD_DIM))


def _row_tile(rows, target):
    t = min(rows, target)
    assert rows % t == 0
    return t


def _token_stream(xs, seq_len, mod_row, p, ada_l, mixers, last_ctx=False):
    rows = xs.shape[0]
    sh2, sc2, g1, g2 = ada_l["sh2"], ada_l["sc2"], ada_l["g1"], ada_l["g2"]
    tm = _row_tile(rows, 1024)
    tiles_per_seq = max(seq_len // tm, 1)
    if mod_row is None:
        mod_of_tile = lambda i: 1 + i // tiles_per_seq
    else:
        mod_of_tile = lambda i: mod_row
    x1 = _matmul_residual(mixers, p["w_out"], xs, g1, mod_of_tile, tm, 1024)
    h2 = _norm_mod(x1, p["ln2"], sh2, sc2, mod_of_tile, tm)
    up = _matmul(h2, p["w_up"], BF16, tm, 1024)
    tr = _row_tile(seq_len, 256)
    act = _ffn_conv_gate(up, p["f_cw"], p["f_cb"], seq_len, tr, 512)
    return _matmul_residual([act], p["w_down"], x1, g2, mod_of_tile, tm, 512)


def kernel(x, c, ctx, c_ctx, w_ada, b_ada, ln1_w, ln2_w, w_in, hg_lb_logits, hg_norm_w, na_q_norm_w, na_k_norm_w,
           na_rpb, na_out_norm_w, cv_w, cv_out_norm_w, w_out, w_up, ffn_conv_w, ffn_conv_b, w_down):
    batch, seq, d = x.shape
    ctx_len = ctx.shape[1]
    rows = seq // GRID_W
    depth = w_ada.shape[0]
    xs = x.reshape(batch * seq, d)
    cs = ctx.reshape(batch * ctx_len, d)

    lb_sm = jax.nn.softmax(hg_lb_logits.astype(F32), axis=1)
    lb_all = jnp.cumsum(lb_sm, axis=1) - lb_sm[:, :1]

    cond = jnp.zeros((SUBLANE, d), F32).at[0].set(c_ctx).at[1:1 + batch].set(c)
    ada = _ada_table(cond, w_ada, b_ada).reshape(depth, SUBLANE, 6, 1, d)

    s_zero = jnp.zeros((batch, N_HG, HEAD_DIM, HEAD_DIM), F32)
    for l in range(depth):
        last = l == depth - 1
        names = ("sh1", "sc1", "g1", "sh2", "sc2", "g2")
        ada_l = {n: ada[l, :, i] for i, n in enumerate(names)}
        p = {"w_in": w_in[l].astype(BF16), "w_out": w_out[l].astype(BF16), "w_up": w_up[l].astype(BF16),
             "w_down": w_down[l].astype(BF16), "ln2": ln2_w[l], "f_cw": ffn_conv_w[l], "f_cb": ffn_conv_b[l]}
        tm = _row_tile(xs.shape[0], 1024)
        tiles_per_seq = seq // tm
        lat_mod = lambda i: 1 + i // tiles_per_seq
        ctx_mod = lambda i: 0
        h = _norm_mod(xs, ln1_w[l], ada_l["sh1"], ada_l["sc1"], lat_mod, tm)
        hc = _norm_mod(cs, ln1_w[l], ada_l["sh1"], ada_l["sc1"], ctx_mod, _row_tile(cs.shape[0], 1024))
        u = _matmul(h, p["w_in"], F32, tm, 1024)
        uc = _matmul(hc, p["w_in"], F32, _row_tile(cs.shape[0], 1024), 1024)

        co_fw, s_fw = _hgrn_scan(uc, lb_all[0, l], s_zero, ctx_len, reverse=False)
        hg_c, s_bw = _hgrn_scan(uc, lb_all[1, l], s_zero, ctx_len, reverse=True, final=True, o_fw=co_fw,
                                norm_w=hg_norm_w[l])
        o_fw, _ = _hgrn_scan(u, lb_all[0, l], s_fw, seq, reverse=False)
        hg_out, _ = _hgrn_scan(u, lb_all[1, l], s_bw, seq, reverse=True, final=True, o_fw=o_fw,
                               norm_w=hg_norm_w[l])

        bias = _na_bias_table(na_rpb[l], rows)
        na_out = _neighbourhood_attention(u, uc, bias, na_q_norm_w[l], na_k_norm_w[l], na_out_norm_w[l],
                                          batch, rows, ctx_len)
        cv_out = _short_conv(u, cv_w[l], cv_out_norm_w[l], seq, _row_tile(seq, 256))

        xs = _token_stream(xs, seq, None, p, ada_l, [hg_out, na_out, cv_out])
        if not last:
            na_c = _context_attention(uc, na_q_norm_w[l], na_k_norm_w[l], na_out_norm_w[l], batch, ctx_len)
            cv_c = _short_conv(uc, cv_w[l], cv_out_norm_w[l], ctx_len, _row_tile(ctx_len, 256))
            cs = _token_stream(cs, ctx_len, 0, p, ada_l, [hg_c, na_c, cv_c])
    return xs.reshape(batch, seq, d)
```

```python
import functools

import numpy as np
import jax
import jax.numpy as jnp
from jax import lax
from jax.experimental import pallas as pl
from jax.experimental.pallas import tpu as pltpu

F32 = jnp.float32
BF16 = jnp.bfloat16

D_MODEL = 2048
DEPTH = 2
GRID_W = 64
HEAD_DIM = 128
N_HG = 4
N_NA = 8
N_CV = 4
HG_W = N_HG * HEAD_DIM
NA_W = N_NA * HEAD_DIM
CV_W = N_CV * HEAD_DIM
IN_SPLITS = (HG_W, HG_W, HG_W, NA_W, NA_W, HG_W, HG_W, NA_W, CV_W, CV_W, CV_W)
IN_W = sum(IN_SPLITS)
(OFF_FFW, OFF_FBW, OFF_I, OFF_NAK, OFF_NAV, OFF_HGQ, OFF_HGG, OFF_NAQ, OFF_CVB, OFF_CVC,
 OFF_CVV) = [int(v) // HEAD_DIM for v in np.cumsum((0,) + IN_SPLITS[:-1])]
NA_WIN_R = 8
NA_WIN_C = 16
NA_QBLK_C = 16
NA_KEY_C = 32
NA_KEY_R = 16
CONV_W = 3
D_FF = 5632
EPS = 1e-6
F_FLOOR = 1e-30
ATTN_SCALE = HEAD_DIM ** -0.5
NEG_INF = -1e30

HG_CHUNK = 256
SUBLANE = 8
HALO = 16
VMEM_LIMIT = 56 * 1024 * 1024


def _params(*sem):
    return pltpu.CompilerParams(dimension_semantics=sem, vmem_limit_bytes=VMEM_LIMIT)


def _dot(a, b):
    return jnp.dot(a, b, preferred_element_type=F32)


def _dot_nt(a, b):
    return lax.dot_general(a, b, (((1,), (1,)), ((), ())), preferred_element_type=F32)


def _dot_tn(a, b):
    return lax.dot_general(a, b, (((0,), (0,)), ((), ())), preferred_element_type=F32)


def _rms(x, w):
    return x * lax.rsqrt(jnp.mean(x * x, axis=-1, keepdims=True) + EPS) * w


def _silu(x):
    return x * jax.nn.sigmoid(x)


def _ada_kernel(s_ref, w_ref, b_ref, o_ref):
    s = _silu(s_ref[...]).astype(BF16)
    o_ref[0] = _dot(s, w_ref[0].astype(BF16)) + b_ref[0]


def _ada_table(cond, w_ada, b_ada):
    depth, d, n = w_ada.shape
    tn = 1024
    return pl.pallas_call(
        _ada_kernel,
        grid=(depth, n // tn),
        in_specs=[pl.BlockSpec((SUBLANE, d), lambda l, j: (0, 0)),
                  pl.BlockSpec((1, d, tn), lambda l, j: (l, 0, j)),
                  pl.BlockSpec((1, 1, tn), lambda l, j: (l, 0, j))],
        out_specs=pl.BlockSpec((1, SUBLANE, tn), lambda l, j: (l, 0, j)),
        out_shape=jax.ShapeDtypeStruct((depth, SUBLANE, n), F32),
        compiler_params=_params("parallel", "parallel"),
    )(cond, w_ada, b_ada.reshape(depth, 1, n))


def _norm_mod_kernel(x_ref, w_ref, sh_ref, sc_ref, o_ref):
    y = _rms(x_ref[...], w_ref[...])
    o_ref[...] = (y * (1.0 + sc_ref[0]) + sh_ref[0]).astype(o_ref.dtype)


def _norm_mod(x, w, shift, scale, mod_of_tile, tr):
    rows, d = x.shape
    mod = lambda i: (mod_of_tile(i), 0, 0)
    return pl.pallas_call(
        _norm_mod_kernel,
        grid=(rows // tr,),
        in_specs=[pl.BlockSpec((tr, d), lambda i: (i, 0)),
                  pl.BlockSpec((1, d), lambda i: (0, 0)),
                  pl.BlockSpec((1, 1, d), mod),
                  pl.BlockSpec((1, 1, d), mod)],
        out_specs=pl.BlockSpec((tr, d), lambda i: (i, 0)),
        out_shape=jax.ShapeDtypeStruct((rows, d), BF16),
        compiler_params=_params("parallel"),
    )(x, w.reshape(1, d), shift, scale)


def _mm_kernel(a_ref, w_ref, o_ref):
    o_ref[...] = _dot(a_ref[...], w_ref[...]).astype(o_ref.dtype)


def _matmul(a, w, out_dtype, tm, tn):
    m, k = a.shape
    n = w.shape[1]
    return pl.pallas_call(
        _mm_kernel,
        grid=(m // tm, n // tn),
        in_specs=[pl.BlockSpec((tm, k), lambda i, j: (i, 0)),
                  pl.BlockSpec((k, tn), lambda i, j: (0, j))],
        out_specs=pl.BlockSpec((tm, tn), lambda i, j: (i, j)),
        out_shape=jax.ShapeDtypeStruct((m, n), out_dtype),
        compiler_params=_params("parallel", "parallel"),
    )(a, w)


def _mm_res_kernel(*refs, splits):
    n_a = len(splits)
    a_refs = refs[:n_a]
    w_ref, x_ref, g_ref, o_ref = refs[n_a:]
    acc = None
    off = 0
    for a_ref, width in zip(a_refs, splits):
        part = _dot(a_ref[...], w_ref[off:off + width, :])
        acc = part if acc is None else acc + part
        off += width
    o_ref[...] = x_ref[...] + g_ref[0] * acc


def _matmul_residual(a_list, w, x, gate, mod_of_tile, tm, tn):
    m = x.shape[0]
    k, n = w.shape
    splits = tuple(a.shape[1] for a in a_list)
    assert sum(splits) == k
    in_specs = [pl.BlockSpec((tm, s), lambda i, j: (i, 0)) for s in splits]
    in_specs += [pl.BlockSpec((k, tn), lambda i, j: (0, j)),
                 pl.BlockSpec((tm, tn), lambda i, j: (i, j)),
                 pl.BlockSpec((1, 1, tn), lambda i, j: (mod_of_tile(i), 0, j))]
    return pl.pallas_call(
        functools.partial(_mm_res_kernel, splits=splits),
        grid=(m // tm, n // tn),
        in_specs=in_specs,
        out_specs=pl.BlockSpec((tm, tn), lambda i, j: (i, j)),
        out_shape=jax.ShapeDtypeStruct((m, n), F32),
        compiler_params=_params("parallel", "parallel"),
    )(*a_list, w, x, gate)


def _shift_rows(x, prev_row, next_row):
    r = x.shape[0]
    idx = lax.broadcasted_iota(jnp.int32, x.shape, 0)
    down = jnp.where(idx == 0, prev_row, pltpu.roll(x, 1, axis=0))
    up = jnp.where(idx == r - 1, next_row, pltpu.roll(x, r - 1, axis=0))
    return down, up


def _halo_specs(tr, tc, col_of, tiles_per_seq, n_rows):
    per = tr // HALO
    last_blk = n_rows // HALO - 1
    cur = pl.BlockSpec((tr, tc), lambda i, j: (i, col_of(j)))
    prev = pl.BlockSpec((HALO, tc), lambda i, j: (jnp.maximum(i * per - 1, 0), col_of(j)))
    nxt = pl.BlockSpec((HALO, tc), lambda i, j: (jnp.minimum((i + 1) * per, last_blk), col_of(j)))
    return cur, prev, nxt


def _seq_edges(tiles_per_seq):
    i = pl.program_id(0)
    pos = i % tiles_per_seq
    return (pos != 0).astype(F32), (pos != tiles_per_seq - 1).astype(F32)


def _ffn_up_kernel(h_ref, hp_ref, hn_ref, wg_ref, wv_ref, cg_ref, cv_ref, bg_ref, bv_ref, o_ref, lhs_ref,
                   *, tm, tiles_per_seq):
    pos = pl.program_id(0) % tiles_per_seq
    has_prev = pos != 0
    has_next = pos != tiles_per_seq - 1

    @pl.when(pl.program_id(1) == 0)
    def _():
        lhs_ref[0:HALO, :] = jnp.where(has_prev, hp_ref[...], jnp.zeros_like(hp_ref))
        lhs_ref[HALO:HALO + tm, :] = h_ref[...]
        lhs_ref[HALO + tm:, :] = jnp.where(has_next, hn_ref[...], jnp.zeros_like(hn_ref))

    lhs = lhs_ref[...]

    def branch(w_ref, c_ref, b_ref):
        acc = _dot(lhs, w_ref[...])
        r = acc.shape[0]
        y = (pltpu.roll(acc, 1, axis=0) * c_ref[0:1, :] + acc * c_ref[1:2, :]
             + pltpu.roll(acc, r - 1, axis=0) * c_ref[2:3, :])
        return y[HALO:HALO + tm, :] + b_ref[...]

    gate = branch(wg_ref, cg_ref, bg_ref)
    val = branch(wv_ref, cv_ref, bv_ref)
    o_ref[...] = (_silu(gate) * val).astype(o_ref.dtype)


def _ffn_up_conv_gate(h, w_up, cw, cb, seq_len, tm, tn):
    rows, k = h.shape
    nj = D_FF // tn
    tiles_per_seq = seq_len // tm
    per = tm // HALO
    last_blk = rows // HALO - 1
    cb2 = cb.reshape(1, 2 * D_FF)
    in_specs = [pl.BlockSpec((tm, k), lambda i, j: (i, 0)),
                pl.BlockSpec((HALO, k), lambda i, j: (jnp.maximum(i * per - 1, 0), 0)),
                pl.BlockSpec((HALO, k), lambda i, j: (jnp.minimum((i + 1) * per, last_blk), 0)),
                pl.BlockSpec((k, tn), lambda i, j: (0, j)),
                pl.BlockSpec((k, tn), lambda i, j: (0, j + nj)),
                pl.BlockSpec((CONV_W, tn), lambda i, j: (0, j)),
                pl.BlockSpec((CONV_W, tn), lambda i, j: (0, j + nj)),
                pl.BlockSpec((1, tn), lambda i, j: (0, j)),
                pl.BlockSpec((1, tn), lambda i, j: (0, j + nj))]
    return pl.pallas_call(
        functools.partial(_ffn_up_kernel, tm=tm, tiles_per_seq=tiles_per_seq),
        grid=(rows // tm, nj),
        in_specs=in_specs,
        out_specs=pl.BlockSpec((tm, tn), lambda i, j: (i, j)),
        out_shape=jax.ShapeDtypeStruct((rows, D_FF), BF16),
        scratch_shapes=[pltpu.VMEM((tm + 2 * HALO, k), BF16)],
        compiler_params=_params("parallel", "arbitrary"),
    )(h, h, h, w_up, w_up, cw, cw, cb2, cb2)


def _short_conv_kernel(b_ref, c_ref, cp_ref, cn_ref, v_ref, vp_ref, vn_ref, w_ref, nw_ref, o_ref, *, tiles_per_seq):
    has_prev, has_next = _seq_edges(tiles_per_seq)
    p = c_ref[...] * v_ref[...]
    prev_row = cp_ref[HALO - 1:HALO, :] * vp_ref[HALO - 1:HALO, :] * has_prev
    next_row = cn_ref[0:1, :] * vn_ref[0:1, :] * has_next
    down, up = _shift_rows(p, prev_row, next_row)
    y = b_ref[...] * (down * w_ref[0:1, :] + p * w_ref[1:2, :] + up * w_ref[2:3, :])
    for h in range(N_CV):
        sl = slice(h * HEAD_DIM, (h + 1) * HEAD_DIM)
        o_ref[:, sl] = _rms(y[:, sl], nw_ref[:, sl]).astype(o_ref.dtype)


def _short_conv(u, cv_w, cv_onorm, seq_len, tr):
    rows = u.shape[0]
    tiles_per_seq = seq_len // tr
    blk = lambda off: (lambda j: off * HEAD_DIM // CV_W)
    b_spec = pl.BlockSpec((tr, CV_W), lambda i, j: (i, OFF_CVB * HEAD_DIM // CV_W))
    c_specs = _halo_specs(tr, CV_W, blk(OFF_CVC), tiles_per_seq, rows)
    v_specs = _halo_specs(tr, CV_W, blk(OFF_CVV), tiles_per_seq, rows)
    return pl.pallas_call(
        functools.partial(_short_conv_kernel, tiles_per_seq=tiles_per_seq),
        grid=(rows // tr, 1),
        in_specs=[b_spec, *c_specs, *v_specs,
                  pl.BlockSpec((CONV_W, CV_W), lambda i, j: (0, 0)),
                  pl.BlockSpec((1, CV_W), lambda i, j: (0, 0))],
        out_specs=pl.BlockSpec((tr, CV_W), lambda i, j: (i, 0)),
        out_shape=jax.ShapeDtypeStruct((rows, CV_W), BF16),
        compiler_params=_params("parallel", "arbitrary"),
    )(u, u, u, u, u, u, u, cv_w, cv_onorm.reshape(1, CV_W))


def _hgrn_structure(chunk, reverse):
    idx = np.arange(chunk)
    i, t = idx[:, None], idx[None, :]
    mats = [t <= i, t > i]
    masks = []
    s = chunk // 2
    while s >= 1:
        blk = idx // (2 * s)
        upper = (idx % (2 * s)) >= s
        mid = blk * 2 * s + s - 1
        a = np.where(upper[:, None], (t > mid[:, None]) & (t <= i), (t > i) & (t <= mid[:, None]))
        mats.append(a)
        masks.append((blk[:, None] == blk[None, :]) & upper[:, None] & ~upper[None, :])
        s //= 2
    mats = np.stack(mats).astype(np.float32)
    masks = np.stack(masks).astype(np.float32)
    if reverse:
        mats = mats[:, ::-1, ::-1]
        masks = masks[:, ::-1, ::-1]
    return mats.reshape(-1, chunk), masks


def _hgrn_kernel(*refs, chunk, n_levels, reverse, final):
    if final:
        (z_ref, v_ref, q_ref, lb_ref, a_ref, m_ref, s0_ref, g_ref, ofw_ref, nw_ref,
         o_ref, sfin_ref, st_ref) = refs
    else:
        z_ref, v_ref, q_ref, lb_ref, a_ref, m_ref, s0_ref, o_ref, sfin_ref, st_ref = refs
    c = pl.program_id(1)

    @pl.when(c == 0)
    def _():
        st_ref[...] = s0_ref[0]

    z = z_ref[...]
    lb = lb_ref[...]
    f = lb + (1.0 - lb) * jax.nn.sigmoid(z)
    log_f = jnp.log(jnp.maximum(f, F_FLOOR))
    k_all = (1.0 - lb) * jax.nn.sigmoid(-z)
    q_all = _silu(q_ref[...])
    v_all = v_ref[...]
    hi = log_f.astype(BF16)
    lo = (log_f - hi.astype(F32)).astype(BF16)
    a = a_ref[...]
    expo = _dot(a, hi) + _dot(a, lo)
    tot_row = 0 if reverse else chunk - 1
    for h in range(N_HG):
        sl = slice(h * HEAD_DIM, (h + 1) * HEAD_DIM)
        q, k, v = q_all[:, sl], k_all[:, sl], v_all[:, sl]
        e_cum = jnp.exp(expo[0:chunk, sl])
        e_rem = jnp.exp(expo[chunk:2 * chunk, sl])
        e_tot = jnp.exp(expo[tot_row:tot_row + 1, sl])
        st = st_ref[h]
        o = _dot_nt((q * e_cum).astype(BF16), st.astype(BF16))
        att = None
        for lvl in range(n_levels):
            e = jnp.exp(expo[(2 + lvl) * chunk:(3 + lvl) * chunk, sl])
            part = m_ref[lvl] * _dot_nt((q * e).astype(BF16), (k * e).astype(BF16))
            att = part if att is None else att + part
        diag = jnp.sum(q * k, axis=-1, keepdims=True)
        o = o + _dot(att.astype(BF16), v.astype(BF16)) + diag * v
        st_ref[h] = st * e_tot + _dot_tn(v.astype(BF16), (k * e_rem).astype(BF16))
        if final:
            y = _rms(o + ofw_ref[:, sl], nw_ref[...]) * _silu(g_ref[:, sl])
            o_ref[:, sl] = y.astype(o_ref.dtype)
        else:
            o_ref[:, sl] = o.astype(o_ref.dtype)

    @pl.when(c == pl.num_programs(1) - 1)
    def _():
        sfin_ref[0] = st_ref[...]


def _hgrn_scan(u, lb, s0, seq_len, *, reverse, final=False, o_fw=None, norm_w=None):
    rows = u.shape[0]
    batch = rows // seq_len
    chunk = min(HG_CHUNK, seq_len)
    n_chunks = seq_len // chunk
    mats, masks = _hgrn_structure(chunk, reverse)
    n_levels = masks.shape[0]
    per_blk = HG_W // HEAD_DIM

    def tok(off):
        col = off // per_blk
        if reverse:
            return pl.BlockSpec((chunk, HG_W), lambda b, c: (b * n_chunks + n_chunks - 1 - c, col))
        return pl.BlockSpec((chunk, HG_W), lambda b, c: (b * n_chunks + c, col))

    const2 = lambda shape: pl.BlockSpec(shape, lambda b, c: (0, 0))
    state_spec = pl.BlockSpec((1, N_HG, HEAD_DIM, HEAD_DIM), lambda b, c: (b, 0, 0, 0))
    in_specs = [tok(OFF_FBW if reverse else OFF_FFW), tok(OFF_I), tok(OFF_HGQ),
                const2((1, HG_W)), const2(mats.shape),
                pl.BlockSpec(masks.shape, lambda b, c: (0, 0, 0)), state_spec]
    args = [u, u, u, lb.reshape(1, HG_W), jnp.asarray(mats, BF16), jnp.asarray(masks, F32), s0]
    if final:
        in_specs += [tok(OFF_HGG), tok(0), const2((1, HEAD_DIM))]
        args += [u, o_fw, norm_w.reshape(1, HEAD_DIM)]
    out, s_fin = pl.pallas_call(
        functools.partial(_hgrn_kernel, chunk=chunk, n_levels=n_levels, reverse=reverse, final=final),
        grid=(batch, n_chunks),
        in_specs=in_specs,
        out_specs=[tok(0), state_spec],
        out_shape=[jax.ShapeDtypeStruct((rows, HG_W), BF16 if final else F32),
                   jax.ShapeDtypeStruct((batch, N_HG, HEAD_DIM, HEAD_DIM), F32)],
        scratch_shapes=[pltpu.VMEM((N_HG, HEAD_DIM, HEAD_DIM), F32)],
        compiler_params=_params("parallel", "arbitrary"),
    )(*args)
    return out, s_fin


def _na_bias_table(rpb, rows):
    n_rb = rows // NA_WIN_R
    n_cb = GRID_W // NA_QBLK_C
    n_dr, n_dc = 2 * NA_WIN_R - 1, 2 * NA_WIN_C - 1
    sel_r, ok_r = [], []
    for rb in (0, min(1, n_rb - 1), n_rb - 1):
        k_row0 = int(np.clip(rb * NA_WIN_R - NA_WIN_R // 2, 0, rows - NA_KEY_R))
        q_r = rb * NA_WIN_R + np.arange(NA_WIN_R)
        k_r = k_row0 + np.arange(NA_KEY_R)
        r_start = np.clip(q_r - NA_WIN_R // 2, 0, rows - NA_WIN_R)
        ok_r.append((k_r[None, :] >= r_start[:, None]) & (k_r[None, :] < r_start[:, None] + NA_WIN_R))
        dr = k_r[None, :] - q_r[:, None] + NA_WIN_R - 1
        sel_r.append(dr[:, :, None] == np.arange(n_dr))
    sel_c, ok_c = [], []
    for cb in range(n_cb):
        k_col0 = int(np.clip(cb * NA_QBLK_C - NA_WIN_C // 2, 0, GRID_W - NA_KEY_C))
        q_c = cb * NA_QBLK_C + np.arange(NA_QBLK_C)
        k_c = k_col0 + np.arange(NA_KEY_C)
        c_start = np.clip(q_c - NA_WIN_C // 2, 0, GRID_W - NA_WIN_C)
        ok_c.append((k_c[None, :] >= c_start[:, None]) & (k_c[None, :] < c_start[:, None] + NA_WIN_C))
        dc = k_c[None, :] - q_c[:, None] + NA_WIN_C - 1
        sel_c.append(dc[:, :, None] == np.arange(n_dc))
    sel_r = jnp.asarray(np.stack(sel_r), F32)
    sel_c = jnp.asarray(np.stack(sel_c), F32)
    ok_r, ok_c = jnp.asarray(np.stack(ok_r)), jnp.asarray(np.stack(ok_c))
    bias = jnp.einsum("xrka,hab,mcjb->hxmrckj", sel_r, rpb.astype(F32), sel_c, precision=lax.Precision.HIGHEST)
    valid = ok_r[:, None, :, None, :, None] & ok_c[None, :, None, :, None, :]
    bias = jnp.where(valid[None], bias, NEG_INF)
    return bias.reshape(rpb.shape[0], 3, n_cb, NA_WIN_R * NA_QBLK_C, NA_KEY_R * NA_KEY_C)


def _na_kernel(q_ref, k0_ref, k1_ref, k2_ref, k3_ref, v0_ref, v1_ref, v2_ref, v3_ref, kc_ref, vc_ref, bias_ref,
               qw_ref, kw_ref, ow_ref, o_ref):
    n_cb = GRID_W // NA_QBLK_C
    q = _rms(q_ref[...], qw_ref[...]) * ATTN_SCALE
    q3 = q.reshape(NA_WIN_R, GRID_W, HEAD_DIM)
    k = jnp.concatenate([r[...] for r in (k0_ref, k1_ref, k2_ref, k3_ref)], axis=0)
    k3 = _rms(k, kw_ref[...]).reshape(NA_KEY_R, GRID_W, HEAD_DIM)
    v3 = jnp.concatenate([r[...] for r in (v0_ref, v1_ref, v2_ref, v3_ref)], axis=0).reshape(
        NA_KEY_R, GRID_W, HEAD_DIM)
    k_ctx = _rms(kc_ref[...], kw_ref[...]).astype(BF16)
    v_ctx = vc_ref[...].astype(BF16)
    n_q, n_k = NA_WIN_R * NA_QBLK_C, NA_KEY_R * NA_KEY_C
    for cb in range(n_cb):
        q0 = cb * NA_QBLK_C
        k_col0 = min(max(q0 - NA_WIN_C // 2, 0), GRID_W - NA_KEY_C)
        qm = q3[:, q0:q0 + NA_QBLK_C, :].reshape(n_q, HEAD_DIM).astype(BF16)
        km = k3[:, k_col0:k_col0 + NA_KEY_C, :].reshape(n_k, HEAD_DIM).astype(BF16)
        vm = v3[:, k_col0:k_col0 + NA_KEY_C, :].reshape(n_k, HEAD_DIM).astype(BF16)
        s_loc = _dot_nt(qm, km) + bias_ref[0, 0, cb]
        s_ctx = _dot_nt(qm, k_ctx)
        mx = jnp.maximum(jnp.max(s_loc, axis=-1, keepdims=True), jnp.max(s_ctx, axis=-1, keepdims=True))
        p_loc = jnp.exp(s_loc - mx)
        p_ctx = jnp.exp(s_ctx - mx)
        den = jnp.sum(p_loc, axis=-1, keepdims=True) + jnp.sum(p_ctx, axis=-1, keepdims=True)
        o = (_dot(p_loc.astype(BF16), vm) + _dot(p_ctx.astype(BF16), v_ctx)) / den
        o = _rms(o, ow_ref[0]).astype(o_ref.dtype)
        for r in range(NA_WIN_R):
            o_ref[r * GRID_W + q0:r * GRID_W + q0 + NA_QBLK_C, :] = o[r * NA_QBLK_C:(r + 1) * NA_QBLK_C, :]


def _neighbourhood_attention(u, u_ctx, bias, q_norm, k_norm, out_norm, batch, rows, ctx_len):
    assert rows % NA_WIN_R == 0 and rows >= NA_KEY_R
    n_rb = rows // NA_WIN_R
    q_blk = NA_WIN_R * GRID_W
    k_blk = NA_KEY_R * GRID_W // 4
    k_per_batch = rows * GRID_W // k_blk

    def key_spec(col, part):
        def index(h, rb, b):
            first = jnp.clip(rb * NA_WIN_R - NA_WIN_R // 2, 0, rows - NA_KEY_R) * GRID_W // k_blk
            return (b * k_per_batch + first + part, col + h)
        return pl.BlockSpec((k_blk, HEAD_DIM), index)

    def row_class(rb):
        return jnp.where(rb == 0, 0, jnp.where(rb == n_rb - 1, 2, 1))

    n_cb = GRID_W // NA_QBLK_C
    vec = pl.BlockSpec((1, HEAD_DIM), lambda h, rb, b: (0, 0))
    in_specs = [pl.BlockSpec((q_blk, HEAD_DIM), lambda h, rb, b: (b * n_rb + rb, OFF_NAQ + h))]
    in_specs += [key_spec(OFF_NAK, p) for p in range(4)] + [key_spec(OFF_NAV, p) for p in range(4)]
    in_specs += [pl.BlockSpec((ctx_len, HEAD_DIM), lambda h, rb, b: (b, OFF_NAK + h)),
                 pl.BlockSpec((ctx_len, HEAD_DIM), lambda h, rb, b: (b, OFF_NAV + h)),
                 pl.BlockSpec((1, 1, n_cb, NA_WIN_R * NA_QBLK_C, NA_KEY_R * NA_KEY_C),
                              lambda h, rb, b: (h, row_class(rb), 0, 0, 0)),
                 vec, vec,
                 pl.BlockSpec((1, 1, HEAD_DIM), lambda h, rb, b: (h, 0, 0))]
    return pl.pallas_call(
        _na_kernel,
        grid=(N_NA, n_rb, batch),
        in_specs=in_specs,
        out_specs=pl.BlockSpec((q_blk, HEAD_DIM), lambda h, rb, b: (b * n_rb + rb, h)),
        out_shape=jax.ShapeDtypeStruct((u.shape[0], NA_W), BF16),
        compiler_params=_params("parallel", "parallel", "parallel"),
    )(u, u, u, u, u, u, u, u, u, u_ctx, u_ctx, bias, q_norm.reshape(1, HEAD_DIM), k_norm.reshape(1, HEAD_DIM),
      out_norm.reshape(N_NA, 1, HEAD_DIM))


def _ctx_attn_kernel(q_ref, k_ref, v_ref, qw_ref, kw_ref, ow_ref, o_ref):
    q = (_rms(q_ref[...], qw_ref[...]) * ATTN_SCALE).astype(BF16)
    k = _rms(k_ref[...], kw_ref[...]).astype(BF16)
    s = _dot_nt(q, k)
    p = jnp.exp(s - jnp.max(s, axis=-1, keepdims=True))
    o = _dot(p.astype(BF16), v_ref[...].astype(BF16)) / jnp.sum(p, axis=-1, keepdims=True)
    o_ref[...] = _rms(o, ow_ref[0]).astype(o_ref.dtype)


def _context_attention(u_ctx, q_norm, k_norm, out_norm, batch, ctx_len):
    vec = pl.BlockSpec((1, HEAD_DIM), lambda b, h: (0, 0))
    tok = lambda off: pl.BlockSpec((ctx_len, HEAD_DIM), lambda b, h: (b, off + h))
    return pl.pallas_call(
        _ctx_attn_kernel,
        grid=(batch, N_NA),
        in_specs=[tok(OFF_NAQ), tok(OFF_NAK), tok(OFF_NAV), vec, vec,
                  pl.BlockSpec((1, 1, HEAD_DIM), lambda b, h: (h, 0, 0))],
        out_specs=tok(0),
        out_shape=jax.ShapeDtypeStruct((u_ctx.shape[0], NA_W), BF16),
        compiler_params=_params("parallel", "parallel"),
    )(u_ctx, u_ctx, u_ctx, q_norm.reshape(1, HEAD_DIM), k_norm.reshape(1, HEAD_DIM),
      out_norm.reshape(N_NA, 1, HEAD_DIM))


def _row_tile(rows, target):
    t = min(rows, target)
    assert rows % t == 0
    return t


def _token_stream(xs, seq_len, mod_row, p, ada_l, mixers, last_ctx=False):
    rows = xs.shape[0]
    sh2, sc2, g1, g2 = ada_l["sh2"], ada_l["sc2"], ada_l["g1"], ada_l["g2"]
    tm = _row_tile(rows, 1024)
    tiles_per_seq = max(seq_len // tm, 1)
    if mod_row is None:
        mod_of_tile = lambda i: 1 + i // tiles_per_seq
    else:
        mod_of_tile = lambda i: mod_row
    x1 = _matmul_residual(mixers, p["w_out"], xs, g1, mod_of_tile, tm, 1024)
    h2 = _norm_mod(x1, p["ln2"], sh2, sc2, mod_of_tile, tm)
    act = _ffn_up_conv_gate(h2, p["w_up"], p["f_cw"], p["f_cb"], seq_len, _row_tile(seq_len, 1024), 512)
    return _matmul_residual([act], p["w_down"], x1, g2, mod_of_tile, tm, 512)


def kernel(x, c, ctx, c_ctx, w_ada, b_ada, ln1_w, ln2_w, w_in, hg_lb_logits, hg_norm_w, na_q_norm_w, na_k_norm_w,
           na_rpb, na_out_norm_w, cv_w, cv_out_norm_w, w_out, w_up, ffn_conv_w, ffn_conv_b, w_down):
    batch, seq, d = x.shape
    ctx_len = ctx.shape[1]
    rows = seq // GRID_W
    depth = w_ada.shape[0]
    xs = x.reshape(batch * seq, d)
    cs = ctx.reshape(batch * ctx_len, d)

    lb_sm = jax.nn.softmax(hg_lb_logits.astype(F32), axis=1)
    lb_all = jnp.cumsum(lb_sm, axis=1) - lb_sm[:, :1]

    cond = jnp.zeros((SUBLANE, d), F32).at[0].set(c_ctx).at[1:1 + batch].set(c)
    ada = _ada_table(cond, w_ada, b_ada).reshape(depth, SUBLANE, 6, 1, d)

    s_zero = jnp.zeros((batch, N_HG, HEAD_DIM, HEAD_DIM), F32)
    for l in range(depth):
        last = l == depth - 1
        names = ("sh1", "sc1", "g1", "sh2", "sc2", "g2")
        ada_l = {n: ada[l, :, i] for i, n in enumerate(names)}
        p = {"w_in": w_in[l].astype(BF16), "w_out": w_out[l].astype(BF16), "w_up": w_up[l].astype(BF16),
             "w_down": w_down[l].astype(BF16), "ln2": ln2_w[l], "f_cw": ffn_conv_w[l], "f_cb": ffn_conv_b[l]}
        tm = _row_tile(xs.shape[0], 1024)
        tiles_per_seq = seq // tm
        lat_mod = lambda i: 1 + i // tiles_per_seq
        ctx_mod = lambda i: 0
        h = _norm_mod(xs, ln1_w[l], ada_l["sh1"], ada_l["sc1"], lat_mod, tm)
        hc = _norm_mod(cs, ln1_w[l], ada_l["sh1"], ada_l["sc1"], ctx_mod, _row_tile(cs.shape[0], 1024))
        u = _matmul(h, p["w_in"], F32, tm, 1024)
        uc = _matmul(hc, p["w_in"], F32, _row_tile(cs.shape[0], 1024), 1024)

        co_fw, s_fw = _hgrn_scan(uc, lb_all[0, l], s_zero, ctx_len, reverse=False)
        hg_c, s_bw = _hgrn_scan(uc, lb_all[1, l], s_zero, ctx_len, reverse=True, final=True, o_fw=co_fw,
                                norm_w=hg_norm_w[l])
        o_fw, _ = _hgrn_scan(u, lb_all[0, l], s_fw, seq, reverse=False)
        hg_out, _ = _hgrn_scan(u, lb_all[1, l], s_bw, seq, reverse=True, final=True, o_fw=o_fw,
                               norm_w=hg_norm_w[l])

        bias = _na_bias_table(na_rpb[l], rows)
        na_out = _neighbourhood_attention(u, uc, bias, na_q_norm_w[l], na_k_norm_w[l], na_out_norm_w[l],
                                          batch, rows, ctx_len)
        cv_out = _short_conv(u, cv_w[l], cv_out_norm_w[l], seq, _row_tile(seq, 256))

        xs = _token_stream(xs, seq, None, p, ada_l, [hg_out, na_out, cv_out])
        if not last:
            na_c = _context_attention(uc, na_q_norm_w[l], na_k_norm_w[l], na_out_norm_w[l], batch, ctx_len)
            cv_c = _short_conv(uc, cv_w[l], cv_out_norm_w[l], ctx_len, _row_tile(ctx_len, 256))
            cs = _token_stream(cs, ctx_len, 0, p, ada_l, [hg_c, na_c, cv_c])
    return xs.reshape(batch, seq, d)
```

```python
import functools

import numpy as np
import jax
import jax.numpy as jnp
from jax import lax
from jax.experimental import pallas as pl
from jax.experimental.pallas import tpu as pltpu

F32 = jnp.float32
BF16 = jnp.bfloat16

D_MODEL = 2048
DEPTH = 2
GRID_W = 64
HEAD_DIM = 128
N_HG = 4
N_NA = 8
N_CV = 4
HG_W = N_HG * HEAD_DIM
NA_W = N_NA * HEAD_DIM
CV_W = N_CV * HEAD_DIM
IN_SPLITS = (HG_W, HG_W, HG_W, NA_W, NA_W, HG_W, HG_W, NA_W, CV_W, CV_W, CV_W)
IN_W = sum(IN_SPLITS)
(OFF_FFW, OFF_FBW, OFF_I, OFF_NAK, OFF_NAV, OFF_HGQ, OFF_HGG, OFF_NAQ, OFF_CVB, OFF_CVC,
 OFF_CVV) = [int(v) // HEAD_DIM for v in np.cumsum((0,) + IN_SPLITS[:-1])]
NA_WIN_R = 8
NA_WIN_C = 16
NA_QBLK_C = 16
NA_KEY_C = 32
NA_KEY_R = 16
CONV_W = 3
D_FF = 5632
EPS = 1e-6
F_FLOOR = 1e-30
ATTN_SCALE = HEAD_DIM ** -0.5
NEG_INF = -1e30

HG_CHUNK = 128
HG_BLOCK = 256
SUBLANE = 8
HALO = 16
VMEM_LIMIT = 56 * 1024 * 1024


def _params(*sem):
    return pltpu.CompilerParams(dimension_semantics=sem, vmem_limit_bytes=VMEM_LIMIT)


def _dot(a, b):
    return jnp.dot(a, b, preferred_element_type=F32)


def _dot_nt(a, b):
    return lax.dot_general(a, b, (((1,), (1,)), ((), ())), preferred_element_type=F32)


def _dot_tn(a, b):
    return lax.dot_general(a, b, (((0,), (0,)), ((), ())), preferred_element_type=F32)


def _rms(x, w):
    return x * lax.rsqrt(jnp.mean(x * x, axis=-1, keepdims=True) + EPS) * w


def _silu(x):
    return x * jax.nn.sigmoid(x)


def _ada_kernel(s_ref, w_ref, b_ref, o_ref):
    s = _silu(s_ref[...]).astype(BF16)
    o_ref[0] = _dot(s, w_ref[0].astype(BF16)) + b_ref[0]


def _ada_table(cond, w_ada, b_ada):
    depth, d, n = w_ada.shape
    tn = 1024
    return pl.pallas_call(
        _ada_kernel,
        grid=(depth, n // tn),
        in_specs=[pl.BlockSpec((SUBLANE, d), lambda l, j: (0, 0)),
                  pl.BlockSpec((1, d, tn), lambda l, j: (l, 0, j)),
                  pl.BlockSpec((1, 1, tn), lambda l, j: (l, 0, j))],
        out_specs=pl.BlockSpec((1, SUBLANE, tn), lambda l, j: (l, 0, j)),
        out_shape=jax.ShapeDtypeStruct((depth, SUBLANE, n), F32),
        compiler_params=_params("parallel", "parallel"),
    )(cond, w_ada, b_ada.reshape(depth, 1, n))


def _norm_mod_kernel(x_ref, w_ref, sh_ref, sc_ref, o_ref):
    y = _rms(x_ref[...], w_ref[...])
    o_ref[...] = (y * (1.0 + sc_ref[0]) + sh_ref[0]).astype(o_ref.dtype)


def _norm_mod(x, w, shift, scale, mod_of_tile, tr):
    rows, d = x.shape
    mod = lambda i: (mod_of_tile(i), 0, 0)
    return pl.pallas_call(
        _norm_mod_kernel,
        grid=(rows // tr,),
        in_specs=[pl.BlockSpec((tr, d), lambda i: (i, 0)),
                  pl.BlockSpec((1, d), lambda i: (0, 0)),
                  pl.BlockSpec((1, 1, d), mod),
                  pl.BlockSpec((1, 1, d), mod)],
        out_specs=pl.BlockSpec((tr, d), lambda i: (i, 0)),
        out_shape=jax.ShapeDtypeStruct((rows, d), BF16),
        compiler_params=_params("parallel"),
    )(x, w.reshape(1, d), shift, scale)


def _in_proj_kinds(n_cols, tn):
    kinds = []
    for head in range(n_cols // HEAD_DIM):
        if OFF_NAK <= head < OFF_NAV:
            kinds.append("k")
        elif OFF_NAQ <= head < OFF_CVB:
            kinds.append("q")
        elif OFF_HGQ <= head < OFF_HGG:
            kinds.append("silu")
        else:
            kinds.append("id")
    per = tn // HEAD_DIM
    return tuple(tuple(kinds[t * per:(t + 1) * per]) for t in range(n_cols // tn))


def _in_proj_kernel(a_ref, w_ref, qw_ref, kw_ref, o_ref, wb_ref, *, kinds):
    j = pl.program_id(0)

    @pl.when(pl.program_id(1) == 0)
    def _():
        wb_ref[...] = w_ref[...].astype(BF16)

    acc = _dot(a_ref[...], wb_ref[...])
    for jj, tile_kinds in enumerate(kinds):
        @pl.when(j == jj)
        def _(tile_kinds=tile_kinds):
            if all(kind == "id" for kind in tile_kinds):
                o_ref[...] = acc.astype(o_ref.dtype)
                return
            for s, kind in enumerate(tile_kinds):
                sl = slice(s * HEAD_DIM, (s + 1) * HEAD_DIM)
                y = acc[:, sl]
                if kind == "k":
                    y = _rms(y, kw_ref[...])
                elif kind == "q":
                    y = _rms(y, qw_ref[...]) * ATTN_SCALE
                elif kind == "silu":
                    y = _silu(y)
                o_ref[:, sl] = y.astype(o_ref.dtype)


def _in_proj(h, w_in, layer, q_norm, k_norm, n_cols, tm, tn):
    m, k = h.shape
    kinds = _in_proj_kinds(n_cols, tn)
    vec = pl.BlockSpec((1, HEAD_DIM), lambda j, i: (0, 0))
    return pl.pallas_call(
        functools.partial(_in_proj_kernel, kinds=kinds),
        grid=(n_cols // tn, m // tm),
        in_specs=[pl.BlockSpec((tm, k), lambda j, i: (i, 0)),
                  pl.BlockSpec((None, k, tn), lambda j, i: (layer, 0, j)),
                  vec, vec],
        out_specs=pl.BlockSpec((tm, tn), lambda j, i: (i, j)),
        out_shape=jax.ShapeDtypeStruct((m, n_cols), F32),
        scratch_shapes=[pltpu.VMEM((k, tn), BF16)],
        compiler_params=_params("parallel", "arbitrary"),
    )(h, w_in, q_norm.reshape(1, HEAD_DIM), k_norm.reshape(1, HEAD_DIM))


def _mm_res_kernel(*refs, splits):
    n_a = len(splits)
    a_refs = refs[:n_a]
    w_ref, x_ref, g_ref, o_ref, wb_ref = refs[n_a:]

    @pl.when(pl.program_id(1) == 0)
    def _():
        wb_ref[...] = w_ref[...].astype(BF16)

    acc = None
    off = 0
    for a_ref, width in zip(a_refs, splits):
        part = _dot(a_ref[...], wb_ref[off:off + width, :])
        acc = part if acc is None else acc + part
        off += width
    o_ref[...] = x_ref[...] + g_ref[0] * acc


def _matmul_residual(a_list, w, layer, x, gate, mod_of_tile, tm, tn):
    m = x.shape[0]
    _, k, n = w.shape
    splits = tuple(a.shape[1] for a in a_list)
    assert sum(splits) == k
    in_specs = [pl.BlockSpec((tm, s), lambda j, i: (i, 0)) for s in splits]
    in_specs += [pl.BlockSpec((None, k, tn), lambda j, i: (layer, 0, j), pipeline_mode=pl.Buffered(1)),
                 pl.BlockSpec((tm, tn), lambda j, i: (i, j)),
                 pl.BlockSpec((1, 1, tn), lambda j, i: (mod_of_tile(i), 0, j))]
    return pl.pallas_call(
        functools.partial(_mm_res_kernel, splits=splits),
        grid=(n // tn, m // tm),
        in_specs=in_specs,
        out_specs=pl.BlockSpec((tm, tn), lambda j, i: (i, j)),
        out_shape=jax.ShapeDtypeStruct((m, n), F32),
        scratch_shapes=[pltpu.VMEM((k, tn), BF16)],
        compiler_params=_params("parallel", "arbitrary"),
    )(*a_list, w, x, gate)


def _shift_rows(x, prev_row, next_row):
    r = x.shape[0]
    idx = lax.broadcasted_iota(jnp.int32, x.shape, 0)
    down = jnp.where(idx == 0, prev_row, pltpu.roll(x, 1, axis=0))
    up = jnp.where(idx == r - 1, next_row, pltpu.roll(x, r - 1, axis=0))
    return down, up


def _halo_specs(tr, tc, col_of, tiles_per_seq, n_rows):
    per = tr // HALO
    last_blk = n_rows // HALO - 1
    cur = pl.BlockSpec((tr, tc), lambda i, j: (i, col_of(j)))
    prev = pl.BlockSpec((HALO, tc), lambda i, j: (jnp.maximum(i * per - 1, 0), col_of(j)))
    nxt = pl.BlockSpec((HALO, tc), lambda i, j: (jnp.minimum((i + 1) * per, last_blk), col_of(j)))
    return cur, prev, nxt


def _seq_edges(tiles_per_seq):
    i = pl.program_id(0)
    pos = i % tiles_per_seq
    return (pos != 0).astype(F32), (pos != tiles_per_seq - 1).astype(F32)


def _ffn_up_kernel(h_ref, hp_ref, hn_ref, wg_ref, wv_ref, cg_ref, cv_ref, bg_ref, bv_ref, o_ref, lhs_ref,
                   *, tm, tiles_per_seq):
    pos = pl.program_id(0) % tiles_per_seq
    has_prev = pos != 0
    has_next = pos != tiles_per_seq - 1

    @pl.when(pl.program_id(1) == 0)
    def _():
        lhs_ref[0:HALO, :] = jnp.where(has_prev, hp_ref[...], jnp.zeros_like(hp_ref))
        lhs_ref[HALO:HALO + tm, :] = h_ref[...]
        lhs_ref[HALO + tm:, :] = jnp.where(has_next, hn_ref[...], jnp.zeros_like(hn_ref))

    lhs = lhs_ref[...]

    def branch(w_ref, c_ref, b_ref):
        acc = _dot(lhs, w_ref[...])
        r = acc.shape[0]
        y = (pltpu.roll(acc, 1, axis=0) * c_ref[0:1, :] + acc * c_ref[1:2, :]
             + pltpu.roll(acc, r - 1, axis=0) * c_ref[2:3, :])
        return y[HALO:HALO + tm, :] + b_ref[...]

    gate = branch(wg_ref, cg_ref, bg_ref)
    val = branch(wv_ref, cv_ref, bv_ref)
    o_ref[...] = (_silu(gate) * val).astype(o_ref.dtype)


def _ffn_up_conv_gate(h, w_up, layer, cw, cb, seq_len, tm, tn):
    rows, k = h.shape
    nj = D_FF // tn
    tiles_per_seq = seq_len // tm
    per = tm // HALO
    last_blk = rows // HALO - 1
    cb2 = cb.reshape(1, 2 * D_FF)
    in_specs = [pl.BlockSpec((tm, k), lambda i, j: (i, 0)),
                pl.BlockSpec((HALO, k), lambda i, j: (jnp.maximum(i * per - 1, 0), 0)),
                pl.BlockSpec((HALO, k), lambda i, j: (jnp.minimum((i + 1) * per, last_blk), 0)),
                pl.BlockSpec((None, k, tn), lambda i, j: (layer, 0, j)),
                pl.BlockSpec((None, k, tn), lambda i, j: (layer, 0, j + nj)),
                pl.BlockSpec((CONV_W, tn), lambda i, j: (0, j)),
                pl.BlockSpec((CONV_W, tn), lambda i, j: (0, j + nj)),
                pl.BlockSpec((1, tn), lambda i, j: (0, j)),
                pl.BlockSpec((1, tn), lambda i, j: (0, j + nj))]
    return pl.pallas_call(
        functools.partial(_ffn_up_kernel, tm=tm, tiles_per_seq=tiles_per_seq),
        grid=(rows // tm, nj),
        in_specs=in_specs,
        out_specs=pl.BlockSpec((tm, tn), lambda i, j: (i, j)),
        out_shape=jax.ShapeDtypeStruct((rows, D_FF), BF16),
        scratch_shapes=[pltpu.VMEM((tm + 2 * HALO, k), BF16)],
        compiler_params=_params("parallel", "arbitrary"),
    )(h, h, h, w_up, w_up, cw, cw, cb2, cb2)


def _short_conv_kernel(b_ref, c_ref, cp_ref, cn_ref, v_ref, vp_ref, vn_ref, w_ref, nw_ref, o_ref, *, tiles_per_seq):
    has_prev, has_next = _seq_edges(tiles_per_seq)
    p = c_ref[...] * v_ref[...]
    prev_row = cp_ref[HALO - 1:HALO, :] * vp_ref[HALO - 1:HALO, :] * has_prev
    next_row = cn_ref[0:1, :] * vn_ref[0:1, :] * has_next
    down, up = _shift_rows(p, prev_row, next_row)
    y = b_ref[...] * (down * w_ref[0:1, :] + p * w_ref[1:2, :] + up * w_ref[2:3, :])
    for h in range(N_CV):
        sl = slice(h * HEAD_DIM, (h + 1) * HEAD_DIM)
        o_ref[:, sl] = _rms(y[:, sl], nw_ref[:, sl]).astype(o_ref.dtype)


def _short_conv(u, cv_w, cv_onorm, seq_len, tr):
    rows = u.shape[0]
    tiles_per_seq = seq_len // tr
    blk = lambda off: (lambda j: off * HEAD_DIM // CV_W)
    b_spec = pl.BlockSpec((tr, CV_W), lambda i, j: (i, OFF_CVB * HEAD_DIM // CV_W))
    c_specs = _halo_specs(tr, CV_W, blk(OFF_CVC), tiles_per_seq, rows)
    v_specs = _halo_specs(tr, CV_W, blk(OFF_CVV), tiles_per_seq, rows)
    return pl.pallas_call(
        functools.partial(_short_conv_kernel, tiles_per_seq=tiles_per_seq),
        grid=(rows // tr, 1),
        in_specs=[b_spec, *c_specs, *v_specs,
                  pl.BlockSpec((CONV_W, CV_W), lambda i, j: (0, 0)),
                  pl.BlockSpec((1, CV_W), lambda i, j: (0, 0))],
        out_specs=pl.BlockSpec((tr, CV_W), lambda i, j: (i, 0)),
        out_shape=jax.ShapeDtypeStruct((rows, CV_W), BF16),
        compiler_params=_params("parallel", "arbitrary"),
    )(u, u, u, u, u, u, u, cv_w, cv_onorm.reshape(1, CV_W))


def _hgrn_structure(chunk, reverse):
    idx = np.arange(chunk)
    i, t = idx[:, None], idx[None, :]
    mats = [t <= i, t > i]
    masks = []
    s = chunk // 2
    while s >= 1:
        blk = idx // (2 * s)
        upper = (idx % (2 * s)) >= s
        mid = blk * 2 * s + s - 1
        a = np.where(upper[:, None], (t > mid[:, None]) & (t <= i), (t > i) & (t <= mid[:, None]))
        mats.append(a)
        masks.append((blk[:, None] == blk[None, :]) & upper[:, None] & ~upper[None, :])
        s //= 2
    mats = np.stack(mats).astype(np.float32)
    masks = np.stack(masks).astype(np.float32)
    if reverse:
        mats = mats[:, ::-1, ::-1]
        masks = masks[:, ::-1, ::-1]
    return mats.reshape(-1, chunk), masks


def _hgrn_kernel(*refs, chunk, n_sub, n_levels, reverse, mode):
    if mode == "final":
        (z_ref, v_ref, q_ref, lb_ref, a_ref, m_ref, s0_ref, g_ref, ofw_ref, nw_ref,
         o_ref, sfin_ref, st_ref) = refs
    elif mode == "raw":
        z_ref, v_ref, q_ref, lb_ref, a_ref, m_ref, s0_ref, o_ref, sfin_ref, st_ref = refs
    else:
        z_ref, v_ref, lb_ref, a_ref, s0_ref, sfin_ref, st_ref = refs
    c = pl.program_id(1)

    @pl.when(c == 0)
    def _():
        st_ref[...] = s0_ref[0]

    lb = lb_ref[...]
    a = a_ref[...]
    tot_row = 0 if reverse else chunk - 1
    subs = range(n_sub - 1, -1, -1) if reverse else range(n_sub)
    for sub in subs:
        rows = slice(sub * chunk, (sub + 1) * chunk)
        z = z_ref[rows, :]
        f = lb + (1.0 - lb) * jax.nn.sigmoid(z)
        log_f = jnp.log(jnp.maximum(f, F_FLOOR))
        k_all = (1.0 - lb) * jax.nn.sigmoid(-z)
        v_all = v_ref[rows, :]
        hi = log_f.astype(BF16)
        lo = (log_f - hi.astype(F32)).astype(BF16)
        expo = _dot(a, hi) + _dot(a, lo)
        for h in range(N_HG):
            sl = slice(h * HEAD_DIM, (h + 1) * HEAD_DIM)
            k, v = k_all[:, sl], v_all[:, sl]
            e_rem = jnp.exp(expo[chunk:2 * chunk, sl])
            e_tot = jnp.exp(expo[tot_row:tot_row + 1, sl])
            st = st_ref[h]
            st_ref[h] = st * e_tot + _dot_tn(v.astype(BF16), (k * e_rem).astype(BF16))
            if mode == "state":
                continue
            q = q_ref[rows, sl]
            e_cum = jnp.exp(expo[0:chunk, sl])
            o = _dot_nt((q * e_cum).astype(BF16), st.astype(BF16))
            att = None
            for lvl in range(n_levels):
                e = jnp.exp(expo[(2 + lvl) * chunk:(3 + lvl) * chunk, sl])
                part = m_ref[lvl] * _dot_nt((q * e).astype(BF16), (k * e).astype(BF16))
                att = part if att is None else att + part
            diag = jnp.sum(q * k, axis=-1, keepdims=True)
            o = o + _dot(att.astype(BF16), v.astype(BF16)) + diag * v
            if mode == "final":
                y = _rms(o + ofw_ref[rows, sl], nw_ref[...]) * _silu(g_ref[rows, sl])
                o_ref[rows, sl] = y.astype(o_ref.dtype)
            else:
                o_ref[rows, sl] = o.astype(o_ref.dtype)

    @pl.when(c == pl.num_programs(1) - 1)
    def _():
        sfin_ref[0] = st_ref[...]


def _hgrn_scan(u, lb, s0, seq_len, *, reverse, mode, o_fw=None, norm_w=None):
    rows = u.shape[0]
    batch = rows // seq_len
    blk = min(HG_BLOCK, seq_len)
    chunk = min(HG_CHUNK, blk)
    n_blocks = seq_len // blk
    mats, masks = _hgrn_structure(chunk, reverse)
    n_levels = masks.shape[0]
    if mode == "state":
        mats = mats[:2 * chunk]
    per_blk = HG_W // HEAD_DIM

    def tok(off):
        col = off // per_blk
        if reverse:
            return pl.BlockSpec((blk, HG_W), lambda b, c: (b * n_blocks + n_blocks - 1 - c, col))
        return pl.BlockSpec((blk, HG_W), lambda b, c: (b * n_blocks + c, col))

    const2 = lambda shape: pl.BlockSpec(shape, lambda b, c: (0, 0))
    state_spec = pl.BlockSpec((1, N_HG, HEAD_DIM, HEAD_DIM), lambda b, c: (b, 0, 0, 0))
    state_shape = jax.ShapeDtypeStruct((batch, N_HG, HEAD_DIM, HEAD_DIM), F32)
    z_spec = tok(OFF_FBW if reverse else OFF_FFW)
    lb2, mats_b = lb.reshape(1, HG_W), jnp.asarray(mats, BF16)
    if mode == "state":
        in_specs = [z_spec, tok(OFF_I), const2((1, HG_W)), const2(mats.shape), state_spec]
        args = [u, u, lb2, mats_b, s0]
        out_specs, out_shape = [state_spec], [state_shape]
    else:
        in_specs = [z_spec, tok(OFF_I), tok(OFF_HGQ), const2((1, HG_W)), const2(mats.shape),
                    pl.BlockSpec(masks.shape, lambda b, c: (0, 0, 0)), state_spec]
        args = [u, u, u, lb2, mats_b, jnp.asarray(masks, F32), s0]
        if mode == "final":
            in_specs += [tok(OFF_HGG), tok(0), const2((1, HEAD_DIM))]
            args += [u, o_fw, norm_w.reshape(1, HEAD_DIM)]
        out_specs = [tok(0), state_spec]
        out_shape = [jax.ShapeDtypeStruct((rows, HG_W), BF16 if mode == "final" else F32), state_shape]
    res = pl.pallas_call(
        functools.partial(_hgrn_kernel, chunk=chunk, n_sub=blk // chunk, n_levels=n_levels, reverse=reverse,
                          mode=mode),
        grid=(batch, n_blocks),
        in_specs=in_specs,
        out_specs=out_specs,
        out_shape=out_shape,
        scratch_shapes=[pltpu.VMEM((N_HG, HEAD_DIM, HEAD_DIM), F32)],
        compiler_params=_params("parallel", "arbitrary"),
    )(*args)
    return (None, res[0]) if mode == "state" else (res[0], res[1])


def _na_bias_table(rpb, rows):
    n_rb = rows // NA_WIN_R
    n_cb = GRID_W // NA_QBLK_C
    n_dr, n_dc = 2 * NA_WIN_R - 1, 2 * NA_WIN_C - 1
    sel_r, ok_r = [], []
    for rb in (0, min(1, n_rb - 1), n_rb - 1):
        k_row0 = int(np.clip(rb * NA_WIN_R - NA_WIN_R // 2, 0, rows - NA_KEY_R))
        q_r = rb * NA_WIN_R + np.arange(NA_WIN_R)
        k_r = k_row0 + np.arange(NA_KEY_R)
        r_start = np.clip(q_r - NA_WIN_R // 2, 0, rows - NA_WIN_R)
        ok_r.append((k_r[None, :] >= r_start[:, None]) & (k_r[None, :] < r_start[:, None] + NA_WIN_R))
        dr = k_r[None, :] - q_r[:, None] + NA_WIN_R - 1
        sel_r.append(dr[:, :, None] == np.arange(n_dr))
    sel_c, ok_c = [], []
    for cb in range(n_cb):
        k_col0 = int(np.clip(cb * NA_QBLK_C - NA_WIN_C // 2, 0, GRID_W - NA_KEY_C))
        q_c = cb * NA_QBLK_C + np.arange(NA_QBLK_C)
        k_c = k_col0 + np.arange(NA_KEY_C)
        c_start = np.clip(q_c - NA_WIN_C // 2, 0, GRID_W - NA_WIN_C)
        ok_c.append((k_c[None, :] >= c_start[:, None]) & (k_c[None, :] < c_start[:, None] + NA_WIN_C))
        dc = k_c[None, :] - q_c[:, None] + NA_WIN_C - 1
        sel_c.append(dc[:, :, None] == np.arange(n_dc))
    sel_r = jnp.asarray(np.stack(sel_r), F32)
    sel_c = jnp.asarray(np.stack(sel_c), F32)
    ok_r, ok_c = jnp.asarray(np.stack(ok_r)), jnp.asarray(np.stack(ok_c))
    bias = jnp.einsum("xrka,hab,mcjb->hxmrckj", sel_r, rpb.astype(F32), sel_c, precision=lax.Precision.HIGHEST)
    valid = ok_r[:, None, :, None, :, None] & ok_c[None, :, None, :, None, :]
    bias = jnp.where(valid[None], bias, NEG_INF)
    return bias.reshape(rpb.shape[0], 3, n_cb, NA_WIN_R * NA_QBLK_C, NA_KEY_R * NA_KEY_C)


def _na_kernel(q_ref, k0_ref, k1_ref, k2_ref, k3_ref, v0_ref, v1_ref, v2_ref, v3_ref, kc_ref, vc_ref, bias_ref,
               ow_ref, o_ref):
    n_cb = GRID_W // NA_QBLK_C
    q3 = q_ref[...].reshape(NA_WIN_R, GRID_W, HEAD_DIM)
    k3 = jnp.concatenate([r[...] for r in (k0_ref, k1_ref, k2_ref, k3_ref)], axis=0).reshape(
        NA_KEY_R, GRID_W, HEAD_DIM)
    v3 = jnp.concatenate([r[...] for r in (v0_ref, v1_ref, v2_ref, v3_ref)], axis=0).reshape(
        NA_KEY_R, GRID_W, HEAD_DIM)
    k_ctx = kc_ref[...].astype(BF16)
    v_ctx = vc_ref[...].astype(BF16)
    n_q, n_k = NA_WIN_R * NA_QBLK_C, NA_KEY_R * NA_KEY_C
    for cb in range(n_cb):
        q0 = cb * NA_QBLK_C
        k_col0 = min(max(q0 - NA_WIN_C // 2, 0), GRID_W - NA_KEY_C)
        qm = q3[:, q0:q0 + NA_QBLK_C, :].reshape(n_q, HEAD_DIM).astype(BF16)
        km = k3[:, k_col0:k_col0 + NA_KEY_C, :].reshape(n_k, HEAD_DIM).astype(BF16)
        vm = v3[:, k_col0:k_col0 + NA_KEY_C, :].reshape(n_k, HEAD_DIM).astype(BF16)
        s_loc = _dot_nt(qm, km) + bias_ref[0, 0, cb]
        s_ctx = _dot_nt(qm, k_ctx)
        mx = jnp.maximum(jnp.max(s_loc, axis=-1, keepdims=True), jnp.max(s_ctx, axis=-1, keepdims=True))
        p_loc = jnp.exp(s_loc - mx)
        p_ctx = jnp.exp(s_ctx - mx)
        den = jnp.sum(p_loc, axis=-1, keepdims=True) + jnp.sum(p_ctx, axis=-1, keepdims=True)
        o = (_dot(p_loc.astype(BF16), vm) + _dot(p_ctx.astype(BF16), v_ctx)) / den
        o = _rms(o, ow_ref[0]).astype(o_ref.dtype)
        for r in range(NA_WIN_R):
            o_ref[r * GRID_W + q0:r * GRID_W + q0 + NA_QBLK_C, :] = o[r * NA_QBLK_C:(r + 1) * NA_QBLK_C, :]


def _neighbourhood_attention(u, u_ctx, bias, out_norm, batch, rows, ctx_len):
    assert rows % NA_WIN_R == 0 and rows >= NA_KEY_R
    n_rb = rows // NA_WIN_R
    q_blk = NA_WIN_R * GRID_W
    k_blk = NA_KEY_R * GRID_W // 4
    k_per_batch = rows * GRID_W // k_blk

    def key_spec(col, part):
        def index(h, rb, b):
            first = jnp.clip(rb * NA_WIN_R - NA_WIN_R // 2, 0, rows - NA_KEY_R) * GRID_W // k_blk
            return (b * k_per_batch + first + part, col + h)
        return pl.BlockSpec((k_blk, HEAD_DIM), index)

    def row_class(rb):
        return jnp.where(rb == 0, 0, jnp.where(rb == n_rb - 1, 2, 1))

    n_cb = GRID_W // NA_QBLK_C
    in_specs = [pl.BlockSpec((q_blk, HEAD_DIM), lambda h, rb, b: (b * n_rb + rb, OFF_NAQ + h))]
    in_specs += [key_spec(OFF_NAK, p) for p in range(4)] + [key_spec(OFF_NAV, p) for p in range(4)]
    in_specs += [pl.BlockSpec((ctx_len, HEAD_DIM), lambda h, rb, b: (b, OFF_NAK + h)),
                 pl.BlockSpec((ctx_len, HEAD_DIM), lambda h, rb, b: (b, OFF_NAV + h)),
                 pl.BlockSpec((1, 1, n_cb, NA_WIN_R * NA_QBLK_C, NA_KEY_R * NA_KEY_C),
                              lambda h, rb, b: (h, row_class(rb), 0, 0, 0)),
                 pl.BlockSpec((1, 1, HEAD_DIM), lambda h, rb, b: (h, 0, 0))]
    return pl.pallas_call(
        _na_kernel,
        grid=(N_NA, n_rb, batch),
        in_specs=in_specs,
        out_specs=pl.BlockSpec((q_blk, HEAD_DIM), lambda h, rb, b: (b * n_rb + rb, h)),
        out_shape=jax.ShapeDtypeStruct((u.shape[0], NA_W), BF16),
        compiler_params=_params("parallel", "parallel", "parallel"),
    )(u, u, u, u, u, u, u, u, u, u_ctx, u_ctx, bias, out_norm.reshape(N_NA, 1, HEAD_DIM))


def _ctx_attn_kernel(q_ref, k_ref, v_ref, ow_ref, o_ref):
    s = _dot_nt(q_ref[...].astype(BF16), k_ref[...].astype(BF16))
    p = jnp.exp(s - jnp.max(s, axis=-1, keepdims=True))
    o = _dot(p.astype(BF16), v_ref[...].astype(BF16)) / jnp.sum(p, axis=-1, keepdims=True)
    o_ref[...] = _rms(o, ow_ref[0]).astype(o_ref.dtype)


def _context_attention(u_ctx, out_norm, batch, ctx_len):
    tok = lambda off: pl.BlockSpec((ctx_len, HEAD_DIM), lambda b, h: (b, off + h))
    return pl.pallas_call(
        _ctx_attn_kernel,
        grid=(batch, N_NA),
        in_specs=[tok(OFF_NAQ), tok(OFF_NAK), tok(OFF_NAV),
                  pl.BlockSpec((1, 1, HEAD_DIM), lambda b, h: (h, 0, 0))],
        out_specs=tok(0),
        out_shape=jax.ShapeDtypeStruct((u_ctx.shape[0], NA_W), BF16),
        compiler_params=_params("parallel", "parallel"),
    )(u_ctx, u_ctx, u_ctx, out_norm.reshape(N_NA, 1, HEAD_DIM))


def _row_tile(rows, target):
    t = min(rows, target)
    assert rows % t == 0
    return t


def _token_stream(xs, seq_len, mod_row, p, layer, ada_l, mixers):
    rows = xs.shape[0]
    sh2, sc2, g1, g2 = ada_l["sh2"], ada_l["sc2"], ada_l["g1"], ada_l["g2"]
    tm = _row_tile(rows, 1024)
    tiles_per_seq = max(seq_len // tm, 1)
    if mod_row is None:
        mod_of_tile = lambda i: 1 + i // tiles_per_seq
    else:
        mod_of_tile = lambda i: mod_row
    x1 = _matmul_residual(mixers, p["w_out"], layer, xs, g1, mod_of_tile, tm, 1024)
    h2 = _norm_mod(x1, p["ln2"][layer], sh2, sc2, mod_of_tile, tm)
    act = _ffn_up_conv_gate(h2, p["w_up"], layer, p["f_cw"][layer], p["f_cb"][layer], seq_len,
                            _row_tile(seq_len, 1024), 512)
    return _matmul_residual([act], p["w_down"], layer, x1, g2, mod_of_tile, tm, 512)


def kernel(x, c, ctx, c_ctx, w_ada, b_ada, ln1_w, ln2_w, w_in, hg_lb_logits, hg_norm_w, na_q_norm_w, na_k_norm_w,
           na_rpb, na_out_norm_w, cv_w, cv_out_norm_w, w_out, w_up, ffn_conv_w, ffn_conv_b, w_down):
    batch, seq, d = x.shape
    ctx_len = ctx.shape[1]
    rows = seq // GRID_W
    depth = w_ada.shape[0]
    xs = x.reshape(batch * seq, d)
    cs = ctx.reshape(batch * ctx_len, d)

    lb_sm = jax.nn.softmax(hg_lb_logits.astype(F32), axis=1)
    lb_all = jnp.cumsum(lb_sm, axis=1) - lb_sm[:, :1]

    cond = jnp.zeros((SUBLANE, d), F32).at[0].set(c_ctx).at[1:1 + batch].set(c)
    ada = _ada_table(cond, w_ada, b_ada).reshape(depth, SUBLANE, 6, 1, d)

    s_zero = jnp.zeros((batch, N_HG, HEAD_DIM, HEAD_DIM), F32)
    p = {"w_out": w_out, "w_up": w_up.astype(BF16), "w_down": w_down, "ln2": ln2_w, "f_cw": ffn_conv_w,
         "f_cb": ffn_conv_b}
    ctx_cols = sum(IN_SPLITS[:5])
    for l in range(depth):
        last = l == depth - 1
        names = ("sh1", "sc1", "g1", "sh2", "sc2", "g2")
        ada_l = {n: ada[l, :, i] for i, n in enumerate(names)}
        tm = _row_tile(xs.shape[0], 1024)
        tm_c = _row_tile(cs.shape[0], 1024)
        tiles_per_seq = seq // tm
        lat_mod = lambda i: 1 + i // tiles_per_seq
        ctx_mod = lambda i: 0
        h = _norm_mod(xs, ln1_w[l], ada_l["sh1"], ada_l["sc1"], lat_mod, tm)
        hc = _norm_mod(cs, ln1_w[l], ada_l["sh1"], ada_l["sc1"], ctx_mod, tm_c)
        u = _in_proj(h, w_in, l, na_q_norm_w[l], na_k_norm_w[l], IN_W, tm, 1024)
        uc = _in_proj(hc, w_in, l, na_q_norm_w[l], na_k_norm_w[l], ctx_cols if last else IN_W, tm_c, 512)

        if last:
            _, s_fw = _hgrn_scan(uc, lb_all[0, l], s_zero, ctx_len, reverse=False, mode="state")
            _, s_bw = _hgrn_scan(uc, lb_all[1, l], s_zero, ctx_len, reverse=True, mode="state")
        else:
            co_fw, s_fw = _hgrn_scan(uc, lb_all[0, l], s_zero, ctx_len, reverse=False, mode="raw")
            hg_c, s_bw = _hgrn_scan(uc, lb_all[1, l], s_zero, ctx_len, reverse=True, mode="final", o_fw=co_fw,
                                    norm_w=hg_norm_w[l])
        o_fw, _ = _hgrn_scan(u, lb_all[0, l], s_fw, seq, reverse=False, mode="raw")
        hg_out, _ = _hgrn_scan(u, lb_all[1, l], s_bw, seq, reverse=True, mode="final", o_fw=o_fw,
                               norm_w=hg_norm_w[l])

        bias = _na_bias_table(na_rpb[l], rows)
        na_out = _neighbourhood_attention(u, uc, bias, na_out_norm_w[l], batch, rows, ctx_len)
        cv_out = _short_conv(u, cv_w[l], cv_out_norm_w[l], seq, _row_tile(seq, 256))

        xs = _token_stream(xs, seq, None, p, l, ada_l, [hg_out, na_out, cv_out])
        if not last:
            na_c = _context_attention(uc, na_out_norm_w[l], batch, ctx_len)
            cv_c = _short_conv(uc, cv_w[l], cv_out_norm_w[l], ctx_len, _row_tile(ctx_len, 256))
            cs = _token_stream(cs, ctx_len, 0, p, l, ada_l, [hg_c, na_c, cv_c])
    return xs.reshape(batch, seq, d)
```

```python
import functools

import numpy as np
import jax
import jax.numpy as jnp
from jax import lax
from jax.experimental import pallas as pl
from jax.experimental.pallas import tpu as pltpu

F32 = jnp.float32
BF16 = jnp.bfloat16

D_MODEL = 2048
DEPTH = 2
GRID_W = 64
HEAD_DIM = 128
N_HG = 4
N_NA = 8
N_CV = 4
HG_W = N_HG * HEAD_DIM
NA_W = N_NA * HEAD_DIM
CV_W = N_CV * HEAD_DIM
IN_SPLITS = (HG_W, HG_W, HG_W, NA_W, NA_W, HG_W, HG_W, NA_W, CV_W, CV_W, CV_W)
IN_W = sum(IN_SPLITS)
(OFF_FFW, OFF_FBW, OFF_I, OFF_NAK, OFF_NAV, OFF_HGQ, OFF_HGG, OFF_NAQ, OFF_CVB, OFF_CVC,
 OFF_CVV) = [int(v) // HEAD_DIM for v in np.cumsum((0,) + IN_SPLITS[:-1])]
NA_WIN_R = 8
NA_WIN_C = 16
NA_QBLK_C = 16
NA_KEY_C = 32
NA_KEY_R = 16
CONV_W = 3
D_FF = 5632
EPS = 1e-6
F_FLOOR = 1e-30
ATTN_SCALE = HEAD_DIM ** -0.5
NEG_INF = -1e30

HG_CHUNK = 128
HG_BLOCK = 256
SUBLANE = 8
HALO = 16
MXU_N = 256
VMEM_LIMIT = 56 * 1024 * 1024


def _params(*sem):
    return pltpu.CompilerParams(dimension_semantics=sem, vmem_limit_bytes=VMEM_LIMIT)


def _dot(a, b):
    return jnp.dot(a, b, preferred_element_type=F32)


def _dot_nt(a, b):
    return lax.dot_general(a, b, (((1,), (1,)), ((), ())), preferred_element_type=F32)


def _dot_tn(a, b):
    return lax.dot_general(a, b, (((0,), (0,)), ((), ())), preferred_element_type=F32)


def _rms(x, w):
    return x * lax.rsqrt(jnp.mean(x * x, axis=-1, keepdims=True) + EPS) * w


def _silu(x):
    return x * jax.nn.sigmoid(x)


def _ada_kernel(s_ref, w_ref, b_ref, o_ref):
    s = _silu(s_ref[...]).astype(BF16)
    o_ref[0] = _dot(s, w_ref[0].astype(BF16)) + b_ref[0]


def _ada_table(cond, w_ada, b_ada):
    depth, d, n = w_ada.shape
    tn = 1024
    return pl.pallas_call(
        _ada_kernel,
        grid=(depth, n // tn),
        in_specs=[pl.BlockSpec((SUBLANE, d), lambda l, j: (0, 0)),
                  pl.BlockSpec((1, d, tn), lambda l, j: (l, 0, j)),
                  pl.BlockSpec((1, 1, tn), lambda l, j: (l, 0, j))],
        out_specs=pl.BlockSpec((1, SUBLANE, tn), lambda l, j: (l, 0, j)),
        out_shape=jax.ShapeDtypeStruct((depth, SUBLANE, n), F32),
        compiler_params=_params("parallel", "parallel"),
    )(cond, w_ada, b_ada.reshape(depth, 1, n))


def _norm_mod_kernel(x_ref, w_ref, sh_ref, sc_ref, o_ref):
    y = _rms(x_ref[...], w_ref[...])
    o_ref[...] = (y * (1.0 + sc_ref[0]) + sh_ref[0]).astype(o_ref.dtype)


def _norm_mod(x, w, shift, scale, mod_of_tile, tr):
    rows, d = x.shape
    mod = lambda i: (mod_of_tile(i), 0, 0)
    return pl.pallas_call(
        _norm_mod_kernel,
        grid=(rows // tr,),
        in_specs=[pl.BlockSpec((tr, d), lambda i: (i, 0)),
                  pl.BlockSpec((1, d), lambda i: (0, 0)),
                  pl.BlockSpec((1, 1, d), mod),
                  pl.BlockSpec((1, 1, d), mod)],
        out_specs=pl.BlockSpec((tr, d), lambda i: (i, 0)),
        out_shape=jax.ShapeDtypeStruct((rows, d), BF16),
        compiler_params=_params("parallel"),
    )(x, w.reshape(1, d), shift, scale)


def _in_proj_kinds(n_cols, tn):
    kinds = []
    for head in range(n_cols // HEAD_DIM):
        if OFF_NAK <= head < OFF_NAV:
            kinds.append("k")
        elif OFF_NAQ <= head < OFF_CVB:
            kinds.append("q")
        elif OFF_HGQ <= head < OFF_HGG:
            kinds.append("silu")
        else:
            kinds.append("id")
    per = tn // HEAD_DIM
    return tuple(tuple(kinds[t * per:(t + 1) * per]) for t in range(n_cols // tn))


def _in_proj_kernel(a_ref, w_ref, qw_ref, kw_ref, o_ref, wb_ref, *, kinds):
    j = pl.program_id(0)

    @pl.when(pl.program_id(1) == 0)
    def _():
        wb_ref[...] = w_ref[...].astype(BF16)

    def transform(y, kind):
        if kind == "k":
            return _rms(y, kw_ref[...])
        if kind == "q":
            return _rms(y, qw_ref[...]) * ATTN_SCALE
        if kind == "silu":
            return _silu(y)
        return y

    per_slab = MXU_N // HEAD_DIM
    for pattern in sorted(set(kinds)):
        hit = functools.reduce(jnp.logical_or, [j == jj for jj, tk in enumerate(kinds) if tk == pattern])

        @pl.when(hit)
        def _(pattern=pattern):
            a = a_ref[...]
            for c in range(len(pattern) // per_slab):
                cols = slice(c * MXU_N, (c + 1) * MXU_N)
                acc = _dot(a, wb_ref[:, cols])
                parts = [transform(acc[:, s * HEAD_DIM:(s + 1) * HEAD_DIM], pattern[c * per_slab + s])
                         for s in range(per_slab)]
                o_ref[:, cols] = jnp.concatenate(parts, axis=1).astype(o_ref.dtype)


def _in_proj(h, w_in, layer, q_norm, k_norm, n_cols, tm, tn):
    m, k = h.shape
    kinds = _in_proj_kinds(n_cols, tn)
    vec = pl.BlockSpec((1, HEAD_DIM), lambda j, i: (0, 0))
    return pl.pallas_call(
        functools.partial(_in_proj_kernel, kinds=kinds),
        grid=(n_cols // tn, m // tm),
        in_specs=[pl.BlockSpec((tm, k), lambda j, i: (i, 0)),
                  pl.BlockSpec((None, k, tn), lambda j, i: (layer, 0, j)),
                  vec, vec],
        out_specs=pl.BlockSpec((tm, tn), lambda j, i: (i, j)),
        out_shape=jax.ShapeDtypeStruct((m, n_cols), F32),
        scratch_shapes=[pltpu.VMEM((k, tn), BF16)],
        compiler_params=_params("parallel", "arbitrary"),
    )(h, w_in, q_norm.reshape(1, HEAD_DIM), k_norm.reshape(1, HEAD_DIM))


def _mm_res_kernel(*refs, splits):
    n_a = len(splits)
    a_refs = refs[:n_a]
    w_ref, x_ref, g_ref, o_ref, wb_ref = refs[n_a:]

    @pl.when(pl.program_id(1) == 0)
    def _():
        wb_ref[...] = w_ref[...].astype(BF16)

    acc = None
    off = 0
    for a_ref, width in zip(a_refs, splits):
        part = _dot(a_ref[...], wb_ref[off:off + width, :])
        acc = part if acc is None else acc + part
        off += width
    o_ref[...] = x_ref[...] + g_ref[0] * acc


def _matmul_residual(a_list, w, layer, x, gate, mod_of_tile, tm, tn):
    m = x.shape[0]
    _, k, n = w.shape
    splits = tuple(a.shape[1] for a in a_list)
    assert sum(splits) == k
    in_specs = [pl.BlockSpec((tm, s), lambda j, i: (i, 0)) for s in splits]
    in_specs += [pl.BlockSpec((None, k, tn), lambda j, i: (layer, 0, j), pipeline_mode=pl.Buffered(1)),
                 pl.BlockSpec((tm, tn), lambda j, i: (i, j)),
                 pl.BlockSpec((1, 1, tn), lambda j, i: (mod_of_tile(i), 0, j))]
    return pl.pallas_call(
        functools.partial(_mm_res_kernel, splits=splits),
        grid=(n // tn, m // tm),
        in_specs=in_specs,
        out_specs=pl.BlockSpec((tm, tn), lambda j, i: (i, j)),
        out_shape=jax.ShapeDtypeStruct((m, n), F32),
        scratch_shapes=[pltpu.VMEM((k, tn), BF16)],
        compiler_params=_params("parallel", "arbitrary"),
    )(*a_list, w, x, gate)


def _shift_rows(x, prev_row, next_row):
    r = x.shape[0]
    idx = lax.broadcasted_iota(jnp.int32, x.shape, 0)
    down = jnp.where(idx == 0, prev_row, pltpu.roll(x, 1, axis=0))
    up = jnp.where(idx == r - 1, next_row, pltpu.roll(x, r - 1, axis=0))
    return down, up


def _halo_specs(tr, tc, col_of, tiles_per_seq, n_rows):
    per = tr // HALO
    last_blk = n_rows // HALO - 1
    cur = pl.BlockSpec((tr, tc), lambda i, j: (i, col_of(j)))
    prev = pl.BlockSpec((HALO, tc), lambda i, j: (jnp.maximum(i * per - 1, 0), col_of(j)))
    nxt = pl.BlockSpec((HALO, tc), lambda i, j: (jnp.minimum((i + 1) * per, last_blk), col_of(j)))
    return cur, prev, nxt


def _seq_edges(tiles_per_seq):
    i = pl.program_id(0)
    pos = i % tiles_per_seq
    return (pos != 0).astype(F32), (pos != tiles_per_seq - 1).astype(F32)


def _ffn_up_kernel(h_ref, hp_ref, hn_ref, wg_ref, wv_ref, cg_ref, cv_ref, bg_ref, bv_ref, o_ref, lhs_ref,
                   *, tm, tiles_per_seq):
    pos = pl.program_id(0) % tiles_per_seq
    has_prev = pos != 0
    has_next = pos != tiles_per_seq - 1

    @pl.when(pl.program_id(1) == 0)
    def _():
        lhs_ref[0:HALO, :] = jnp.where(has_prev, hp_ref[...], jnp.zeros_like(hp_ref))
        lhs_ref[HALO:HALO + tm, :] = h_ref[...]
        lhs_ref[HALO + tm:, :] = jnp.where(has_next, hn_ref[...], jnp.zeros_like(hn_ref))

    lhs = lhs_ref[...]

    def branch(w_ref, c_ref, b_ref, cols):
        acc = _dot(lhs, w_ref[:, cols])
        r = acc.shape[0]
        y = (pltpu.roll(acc, 1, axis=0) * c_ref[0:1, cols] + acc * c_ref[1:2, cols]
             + pltpu.roll(acc, r - 1, axis=0) * c_ref[2:3, cols])
        return y[HALO:HALO + tm, :] + b_ref[:, cols]

    for c in range(o_ref.shape[1] // MXU_N):
        cols = slice(c * MXU_N, (c + 1) * MXU_N)
        gate = branch(wg_ref, cg_ref, bg_ref, cols)
        val = branch(wv_ref, cv_ref, bv_ref, cols)
        o_ref[:, cols] = (_silu(gate) * val).astype(o_ref.dtype)


def _ffn_up_conv_gate(h, w_up, layer, cw, cb, seq_len, tm, tn):
    rows, k = h.shape
    nj = D_FF // tn
    tiles_per_seq = seq_len // tm
    per = tm // HALO
    last_blk = rows // HALO - 1
    cb2 = cb.reshape(1, 2 * D_FF)
    in_specs = [pl.BlockSpec((tm, k), lambda i, j: (i, 0)),
                pl.BlockSpec((HALO, k), lambda i, j: (jnp.maximum(i * per - 1, 0), 0)),
                pl.BlockSpec((HALO, k), lambda i, j: (jnp.minimum((i + 1) * per, last_blk), 0)),
                pl.BlockSpec((None, k, tn), lambda i, j: (layer, 0, j)),
                pl.BlockSpec((None, k, tn), lambda i, j: (layer, 0, j + nj)),
                pl.BlockSpec((CONV_W, tn), lambda i, j: (0, j)),
                pl.BlockSpec((CONV_W, tn), lambda i, j: (0, j + nj)),
                pl.BlockSpec((1, tn), lambda i, j: (0, j)),
                pl.BlockSpec((1, tn), lambda i, j: (0, j + nj))]
    return pl.pallas_call(
        functools.partial(_ffn_up_kernel, tm=tm, tiles_per_seq=tiles_per_seq),
        grid=(rows // tm, nj),
        in_specs=in_specs,
        out_specs=pl.BlockSpec((tm, tn), lambda i, j: (i, j)),
        out_shape=jax.ShapeDtypeStruct((rows, D_FF), BF16),
        scratch_shapes=[pltpu.VMEM((tm + 2 * HALO, k), BF16)],
        compiler_params=_params("parallel", "arbitrary"),
    )(h, h, h, w_up, w_up, cw, cw, cb2, cb2)


def _short_conv_kernel(b_ref, c_ref, cp_ref, cn_ref, v_ref, vp_ref, vn_ref, w_ref, nw_ref, o_ref, *, tiles_per_seq):
    has_prev, has_next = _seq_edges(tiles_per_seq)
    p = c_ref[...] * v_ref[...]
    prev_row = cp_ref[HALO - 1:HALO, :] * vp_ref[HALO - 1:HALO, :] * has_prev
    next_row = cn_ref[0:1, :] * vn_ref[0:1, :] * has_next
    down, up = _shift_rows(p, prev_row, next_row)
    y = b_ref[...] * (down * w_ref[0:1, :] + p * w_ref[1:2, :] + up * w_ref[2:3, :])
    for h in range(N_CV):
        sl = slice(h * HEAD_DIM, (h + 1) * HEAD_DIM)
        o_ref[:, sl] = _rms(y[:, sl], nw_ref[:, sl]).astype(o_ref.dtype)


def _short_conv(u, cv_w, cv_onorm, seq_len, tr):
    rows = u.shape[0]
    tiles_per_seq = seq_len // tr
    blk = lambda off: (lambda j: off * HEAD_DIM // CV_W)
    b_spec = pl.BlockSpec((tr, CV_W), lambda i, j: (i, OFF_CVB * HEAD_DIM // CV_W))
    c_specs = _halo_specs(tr, CV_W, blk(OFF_CVC), tiles_per_seq, rows)
    v_specs = _halo_specs(tr, CV_W, blk(OFF_CVV), tiles_per_seq, rows)
    return pl.pallas_call(
        functools.partial(_short_conv_kernel, tiles_per_seq=tiles_per_seq),
        grid=(rows // tr, 1),
        in_specs=[b_spec, *c_specs, *v_specs,
                  pl.BlockSpec((CONV_W, CV_W), lambda i, j: (0, 0)),
                  pl.BlockSpec((1, CV_W), lambda i, j: (0, 0))],
        out_specs=pl.BlockSpec((tr, CV_W), lambda i, j: (i, 0)),
        out_shape=jax.ShapeDtypeStruct((rows, CV_W), BF16),
        compiler_params=_params("parallel", "arbitrary"),
    )(u, u, u, u, u, u, u, cv_w, cv_onorm.reshape(1, CV_W))


def _hgrn_structure(chunk, reverse):
    idx = np.arange(chunk)
    i, t = idx[:, None], idx[None, :]
    mats = [t <= i, t > i]
    masks = []
    s = chunk // 2
    while s >= 1:
        blk = idx // (2 * s)
        upper = (idx % (2 * s)) >= s
        mid = blk * 2 * s + s - 1
        a = np.where(upper[:, None], (t > mid[:, None]) & (t <= i), (t > i) & (t <= mid[:, None]))
        mats.append(a)
        masks.append((blk[:, None] == blk[None, :]) & upper[:, None] & ~upper[None, :])
        s //= 2
    mats = np.stack(mats).astype(np.float32)
    masks = np.stack(masks).astype(np.float32)
    if reverse:
        mats = mats[:, ::-1, ::-1]
        masks = masks[:, ::-1, ::-1]
    return mats.reshape(-1, chunk), masks


def _hgrn_kernel(*refs, chunk, n_sub, n_levels, reverse, mode):
    if mode == "final":
        (z_ref, v_ref, q_ref, lb_ref, a_ref, m_ref, s0_ref, g_ref, ofw_ref, nw_ref,
         o_ref, sfin_ref, st_ref) = refs
    elif mode == "raw":
        z_ref, v_ref, q_ref, lb_ref, a_ref, m_ref, s0_ref, o_ref, sfin_ref, st_ref = refs
    else:
        z_ref, v_ref, lb_ref, a_ref, s0_ref, sfin_ref, st_ref = refs
    c = pl.program_id(1)

    @pl.when(c == 0)
    def _():
        st_ref[...] = s0_ref[0]

    lb = lb_ref[...]
    a = a_ref[...]
    tot_row = 0 if reverse else chunk - 1
    subs = range(n_sub - 1, -1, -1) if reverse else range(n_sub)
    for sub in subs:
        rows = slice(sub * chunk, (sub + 1) * chunk)
        z = z_ref[rows, :]
        f = lb + (1.0 - lb) * jax.nn.sigmoid(z)
        log_f = jnp.log(jnp.maximum(f, F_FLOOR))
        k_all = (1.0 - lb) * jax.nn.sigmoid(-z)
        v_all = v_ref[rows, :]
        hi = log_f.astype(BF16)
        lo = (log_f - hi.astype(F32)).astype(BF16)
        expo = _dot(a, hi) + _dot(a, lo)
        for h in range(N_HG):
            sl = slice(h * HEAD_DIM, (h + 1) * HEAD_DIM)
            k, v = k_all[:, sl], v_all[:, sl]
            e_rem = jnp.exp(expo[chunk:2 * chunk, sl])
            e_tot = jnp.exp(expo[tot_row:tot_row + 1, sl])
            st = st_ref[h]
            st_ref[h] = st * e_tot + _dot_tn(v.astype(BF16), (k * e_rem).astype(BF16))
            if mode == "state":
                continue
            q = q_ref[rows, sl]
            e_cum = jnp.exp(expo[0:chunk, sl])
            o = _dot_nt((q * e_cum).astype(BF16), st.astype(BF16))
            att = None
            for lvl in range(n_levels):
                e = jnp.exp(expo[(2 + lvl) * chunk:(3 + lvl) * chunk, sl])
                part = m_ref[lvl] * _dot_nt((q * e).astype(BF16), (k * e).astype(BF16))
                att = part if att is None else att + part
            diag = jnp.sum(q * k, axis=-1, keepdims=True)
            o = o + _dot(att.astype(BF16), v.astype(BF16)) + diag * v
            if mode == "final":
                y = _rms(o + ofw_ref[rows, sl], nw_ref[...]) * _silu(g_ref[rows, sl])
                o_ref[rows, sl] = y.astype(o_ref.dtype)
            else:
                o_ref[rows, sl] = o.astype(o_ref.dtype)

    @pl.when(c == pl.num_programs(1) - 1)
    def _():
        sfin_ref[0] = st_ref[...]


def _hgrn_scan(u, lb, s0, seq_len, *, reverse, mode, o_fw=None, norm_w=None):
    rows = u.shape[0]
    batch = rows // seq_len
    blk = min(HG_BLOCK, seq_len)
    chunk = min(HG_CHUNK, blk)
    n_blocks = seq_len // blk
    mats, masks = _hgrn_structure(chunk, reverse)
    n_levels = masks.shape[0]
    if mode == "state":
        mats = mats[:2 * chunk]
    per_blk = HG_W // HEAD_DIM

    def tok(off):
        col = off // per_blk
        if reverse:
            return pl.BlockSpec((blk, HG_W), lambda b, c: (b * n_blocks + n_blocks - 1 - c, col))
        return pl.BlockSpec((blk, HG_W), lambda b, c: (b * n_blocks + c, col))

    const2 = lambda shape: pl.BlockSpec(shape, lambda b, c: (0, 0))
    state_spec = pl.BlockSpec((1, N_HG, HEAD_DIM, HEAD_DIM), lambda b, c: (b, 0, 0, 0))
    state_shape = jax.ShapeDtypeStruct((batch, N_HG, HEAD_DIM, HEAD_DIM), F32)
    z_spec = tok(OFF_FBW if reverse else OFF_FFW)
    lb2, mats_b = lb.reshape(1, HG_W), jnp.asarray(mats, BF16)
    if mode == "state":
        in_specs = [z_spec, tok(OFF_I), const2((1, HG_W)), const2(mats.shape), state_spec]
        args = [u, u, lb2, mats_b, s0]
        out_specs, out_shape = [state_spec], [state_shape]
    else:
        in_specs = [z_spec, tok(OFF_I), tok(OFF_HGQ), const2((1, HG_W)), const2(mats.shape),
                    pl.BlockSpec(masks.shape, lambda b, c: (0, 0, 0)), state_spec]
        args = [u, u, u, lb2, mats_b, jnp.asarray(masks, F32), s0]
        if mode == "final":
            in_specs += [tok(OFF_HGG), tok(0), const2((1, HEAD_DIM))]
            args += [u, o_fw, norm_w.reshape(1, HEAD_DIM)]
        out_specs = [tok(0), state_spec]
        out_shape = [jax.ShapeDtypeStruct((rows, HG_W), BF16 if mode == "final" else F32), state_shape]
    res = pl.pallas_call(
        functools.partial(_hgrn_kernel, chunk=chunk, n_sub=blk // chunk, n_levels=n_levels, reverse=reverse,
                          mode=mode),
        grid=(batch, n_blocks),
        in_specs=in_specs,
        out_specs=out_specs,
        out_shape=out_shape,
        scratch_shapes=[pltpu.VMEM((N_HG, HEAD_DIM, HEAD_DIM), F32)],
        compiler_params=_params("parallel", "arbitrary"),
    )(*args)
    return (None, res[0]) if mode == "state" else (res[0], res[1])


def _na_bias_table(rpb, rows):
    n_rb = rows // NA_WIN_R
    n_cb = GRID_W // NA_QBLK_C
    n_dr, n_dc = 2 * NA_WIN_R - 1, 2 * NA_WIN_C - 1
    n_k = NA_KEY_R * NA_KEY_C
    offs, row_mask = [], []
    for rb in (0, min(1, n_rb - 1), n_rb - 1):
        k_row0 = int(np.clip(rb * NA_WIN_R - NA_WIN_R // 2, 0, rows - NA_KEY_R))
        q_r = rb * NA_WIN_R + np.arange(NA_WIN_R)
        k_r = k_row0 + np.arange(NA_KEY_R)
        r_start = np.clip(q_r - NA_WIN_R // 2, 0, rows - NA_WIN_R)
        ok_r = (k_r[None, :] >= r_start[:, None]) & (k_r[None, :] < r_start[:, None] + NA_WIN_R)
        row_mask.append(np.repeat(np.where(ok_r, 0.0, NEG_INF), NA_KEY_C, axis=1))
        offs.append(k_row0 - rb * NA_WIN_R + NA_WIN_R - 1)
    row_mask = jnp.asarray(np.stack(row_mask), F32)
    pad_lo = NA_WIN_R - 1 - min(offs)
    n_a = NA_KEY_R - 1 + max(offs) + pad_lo + 1
    lanes = -(-(n_a * NA_KEY_C) // HEAD_DIM) * HEAD_DIM
    sel_c, ok_c = [], []
    for cb in range(n_cb):
        k_col0 = int(np.clip(cb * NA_QBLK_C - NA_WIN_C // 2, 0, GRID_W - NA_KEY_C))
        q_c = cb * NA_QBLK_C + np.arange(NA_QBLK_C)
        k_c = k_col0 + np.arange(NA_KEY_C)
        c_start = np.clip(q_c - NA_WIN_C // 2, 0, GRID_W - NA_WIN_C)
        ok_c.append((k_c[None, :] >= c_start[:, None]) & (k_c[None, :] < c_start[:, None] + NA_WIN_C))
        dc = k_c[None, :] - q_c[:, None] + NA_WIN_C - 1
        sel_c.append(dc[:, :, None] == np.arange(n_dc))
    sel_c = jnp.asarray(np.stack(sel_c), F32)
    ok_c = jnp.asarray(np.stack(ok_c))
    strip = jnp.einsum("hab,mcjb->hmcaj", rpb.astype(F32), sel_c, precision=lax.Precision.HIGHEST)
    strip = jnp.where(ok_c[None, :, :, None, :], strip, NEG_INF)
    strip = jnp.pad(strip, ((0, 0), (0, 0), (0, 0), (pad_lo, n_a - n_dr - pad_lo), (0, 0)))
    strip = strip.reshape(strip.shape[:3] + (n_a * NA_KEY_C,))
    strip = jnp.pad(strip, ((0, 0), (0, 0), (0, 0), (0, lanes - n_a * NA_KEY_C)))
    n_heads = rpb.shape[0]
    return pl.pallas_call(
        functools.partial(_na_bias_kernel, offs=tuple(offs), pad_lo=pad_lo),
        grid=(n_heads,),
        in_specs=[pl.BlockSpec((1, n_cb, NA_QBLK_C, lanes), lambda h: (h, 0, 0, 0)),
                  pl.BlockSpec((3, NA_WIN_R, n_k), lambda h: (0, 0, 0))],
        out_specs=pl.BlockSpec((1, 3, n_cb, NA_WIN_R * NA_QBLK_C, n_k), lambda h: (h, 0, 0, 0, 0)),
        out_shape=jax.ShapeDtypeStruct((n_heads, 3, n_cb, NA_WIN_R * NA_QBLK_C, n_k), F32),
        compiler_params=_params("parallel"),
    )(strip, row_mask)


def _na_bias_kernel(t_ref, rm_ref, o_ref, *, offs, pad_lo):
    n_cb = t_ref.shape[1]
    lanes = t_ref.shape[-1]
    per = HEAD_DIM // NA_KEY_C
    n_k = o_ref.shape[-1]
    for cb in range(n_cb):
        t = t_ref[0, cb]
        shifted = [t] + [pltpu.roll(t, lanes - NA_KEY_C * s, axis=1) for s in range(1, per)]
        for cls, off in enumerate(offs):
            for r in range(NA_WIN_R):
                m = off - r + pad_lo
                base = (m // per) * HEAD_DIM
                tile = shifted[m % per][:, base:base + n_k] + rm_ref[cls, r:r + 1, :]
                o_ref[0, cls, cb, r * NA_QBLK_C:(r + 1) * NA_QBLK_C, :] = tile


def _na_kernel(q_ref, k0_ref, k1_ref, k2_ref, k3_ref, v0_ref, v1_ref, v2_ref, v3_ref, kc_ref, vc_ref, bias_ref,
               ow_ref, o_ref):
    n_cb = GRID_W // NA_QBLK_C
    q3 = q_ref[...].reshape(NA_WIN_R, GRID_W, HEAD_DIM)
    k3 = jnp.concatenate([r[...] for r in (k0_ref, k1_ref, k2_ref, k3_ref)], axis=0).reshape(
        NA_KEY_R, GRID_W, HEAD_DIM)
    v3 = jnp.concatenate([r[...] for r in (v0_ref, v1_ref, v2_ref, v3_ref)], axis=0).reshape(
        NA_KEY_R, GRID_W, HEAD_DIM)
    k_ctx = kc_ref[...].astype(BF16)
    v_ctx = vc_ref[...].astype(BF16)
    n_q, n_k = NA_WIN_R * NA_QBLK_C, NA_KEY_R * NA_KEY_C
    for cb in range(n_cb):
        q0 = cb * NA_QBLK_C
        k_col0 = min(max(q0 - NA_WIN_C // 2, 0), GRID_W - NA_KEY_C)
        qm = q3[:, q0:q0 + NA_QBLK_C, :].reshape(n_q, HEAD_DIM).astype(BF16)
        km = k3[:, k_col0:k_col0 + NA_KEY_C, :].reshape(n_k, HEAD_DIM).astype(BF16)
        vm = v3[:, k_col0:k_col0 + NA_KEY_C, :].reshape(n_k, HEAD_DIM).astype(BF16)
        s_loc = _dot_nt(qm, km) + bias_ref[0, 0, cb]
        s_ctx = _dot_nt(qm, k_ctx)
        mx = jnp.maximum(jnp.max(s_loc, axis=-1, keepdims=True), jnp.max(s_ctx, axis=-1, keepdims=True))
        p_loc = jnp.exp(s_loc - mx)
        p_ctx = jnp.exp(s_ctx - mx)
        den = jnp.sum(p_loc, axis=-1, keepdims=True) + jnp.sum(p_ctx, axis=-1, keepdims=True)
        o = (_dot(p_loc.astype(BF16), vm) + _dot(p_ctx.astype(BF16), v_ctx)) / den
        o = _rms(o, ow_ref[0]).astype(o_ref.dtype)
        for r in range(NA_WIN_R):
            o_ref[r * GRID_W + q0:r * GRID_W + q0 + NA_QBLK_C, :] = o[r * NA_QBLK_C:(r + 1) * NA_QBLK_C, :]


def _neighbourhood_attention(u, u_ctx, bias, head0, out_norm, batch, rows, ctx_len):
    assert rows % NA_WIN_R == 0 and rows >= NA_KEY_R
    n_rb = rows // NA_WIN_R
    q_blk = NA_WIN_R * GRID_W
    k_blk = NA_KEY_R * GRID_W // 4
    k_per_batch = rows * GRID_W // k_blk

    def key_spec(col, part):
        def index(h, rb, b):
            first = jnp.clip(rb * NA_WIN_R - NA_WIN_R // 2, 0, rows - NA_KEY_R) * GRID_W // k_blk
            return (b * k_per_batch + first + part, col + h)
        return pl.BlockSpec((k_blk, HEAD_DIM), index)

    def row_class(rb):
        return jnp.where(rb == 0, 0, jnp.where(rb == n_rb - 1, 2, 1))

    n_cb = GRID_W // NA_QBLK_C
    in_specs = [pl.BlockSpec((q_blk, HEAD_DIM), lambda h, rb, b: (b * n_rb + rb, OFF_NAQ + h))]
    in_specs += [key_spec(OFF_NAK, p) for p in range(4)] + [key_spec(OFF_NAV, p) for p in range(4)]
    in_specs += [pl.BlockSpec((ctx_len, HEAD_DIM), lambda h, rb, b: (b, OFF_NAK + h)),
                 pl.BlockSpec((ctx_len, HEAD_DIM), lambda h, rb, b: (b, OFF_NAV + h)),
                 pl.BlockSpec((1, 1, n_cb, NA_WIN_R * NA_QBLK_C, NA_KEY_R * NA_KEY_C),
                              lambda h, rb, b: (head0 + h, row_class(rb), 0, 0, 0)),
                 pl.BlockSpec((1, 1, HEAD_DIM), lambda h, rb, b: (h, 0, 0))]
    return pl.pallas_call(
        _na_kernel,
        grid=(N_NA, n_rb, batch),
        in_specs=in_specs,
        out_specs=pl.BlockSpec((q_blk, HEAD_DIM), lambda h, rb, b: (b * n_rb + rb, h)),
        out_shape=jax.ShapeDtypeStruct((u.shape[0], NA_W), BF16),
        compiler_params=_params("parallel", "parallel", "parallel"),
    )(u, u, u, u, u, u, u, u, u, u_ctx, u_ctx, bias, out_norm.reshape(N_NA, 1, HEAD_DIM))


def _ctx_attn_kernel(q_ref, k_ref, v_ref, ow_ref, o_ref):
    s = _dot_nt(q_ref[...].astype(BF16), k_ref[...].astype(BF16))
    p = jnp.exp(s - jnp.max(s, axis=-1, keepdims=True))
    o = _dot(p.astype(BF16), v_ref[...].astype(BF16)) / jnp.sum(p, axis=-1, keepdims=True)
    o_ref[...] = _rms(o, ow_ref[0]).astype(o_ref.dtype)


def _context_attention(u_ctx, out_norm, batch, ctx_len):
    tok = lambda off: pl.BlockSpec((ctx_len, HEAD_DIM), lambda b, h: (b, off + h))
    return pl.pallas_call(
        _ctx_attn_kernel,
        grid=(batch, N_NA),
        in_specs=[tok(OFF_NAQ), tok(OFF_NAK), tok(OFF_NAV),
                  pl.BlockSpec((1, 1, HEAD_DIM), lambda b, h: (h, 0, 0))],
        out_specs=tok(0),
        out_shape=jax.ShapeDtypeStruct((u_ctx.shape[0], NA_W), BF16),
        compiler_params=_params("parallel", "parallel"),
    )(u_ctx, u_ctx, u_ctx, out_norm.reshape(N_NA, 1, HEAD_DIM))


def _row_tile(rows, target):
    t = min(rows, target)
    assert rows % t == 0
    return t


def _token_stream(xs, seq_len, mod_row, p, layer, ada_l, mixers):
    rows = xs.shape[0]
    sh2, sc2, g1, g2 = ada_l["sh2"], ada_l["sc2"], ada_l["g1"], ada_l["g2"]
    tm = _row_tile(rows, 1024)
    tiles_per_seq = max(seq_len // tm, 1)
    if mod_row is None:
        mod_of_tile = lambda i: 1 + i // tiles_per_seq
    else:
        mod_of_tile = lambda i: mod_row
    x1 = _matmul_residual(mixers, p["w_out"], layer, xs, g1, mod_of_tile, tm, 1024)
    h2 = _norm_mod(x1, p["ln2"][layer], sh2, sc2, mod_of_tile, tm)
    act = _ffn_up_conv_gate(h2, p["w_up"], layer, p["f_cw"][layer], p["f_cb"][layer], seq_len,
                            _row_tile(seq_len, 1024), 512)
    return _matmul_residual([act], p["w_down"], layer, x1, g2, mod_of_tile, tm, 512)


def kernel(x, c, ctx, c_ctx, w_ada, b_ada, ln1_w, ln2_w, w_in, hg_lb_logits, hg_norm_w, na_q_norm_w, na_k_norm_w,
           na_rpb, na_out_norm_w, cv_w, cv_out_norm_w, w_out, w_up, ffn_conv_w, ffn_conv_b, w_down):
    batch, seq, d = x.shape
    ctx_len = ctx.shape[1]
    rows = seq // GRID_W
    depth = w_ada.shape[0]
    xs = x.reshape(batch * seq, d)
    cs = ctx.reshape(batch * ctx_len, d)

    lb_sm = jax.nn.softmax(hg_lb_logits.astype(F32), axis=1)
    lb_all = jnp.cumsum(lb_sm, axis=1) - lb_sm[:, :1]

    cond = jnp.zeros((SUBLANE, d), F32).at[0].set(c_ctx).at[1:1 + batch].set(c)
    ada = _ada_table(cond, w_ada, b_ada).reshape(depth, SUBLANE, 6, 1, d)

    s_zero = jnp.zeros((batch, N_HG, HEAD_DIM, HEAD_DIM), F32)
    p = {"w_out": w_out, "w_up": w_up.astype(BF16), "w_down": w_down, "ln2": ln2_w, "f_cw": ffn_conv_w,
         "f_cb": ffn_conv_b}
    ctx_cols = sum(IN_SPLITS[:5])
    bias = _na_bias_table(na_rpb.reshape((depth * N_NA,) + na_rpb.shape[2:]), rows)
    for l in range(depth):
        last = l == depth - 1
        names = ("sh1", "sc1", "g1", "sh2", "sc2", "g2")
        ada_l = {n: ada[l, :, i] for i, n in enumerate(names)}
        tm = _row_tile(xs.shape[0], 1024)
        tm_c = _row_tile(cs.shape[0], 1024)
        tiles_per_seq = seq // tm
        lat_mod = lambda i: 1 + i // tiles_per_seq
        ctx_mod = lambda i: 0
        h = _norm_mod(xs, ln1_w[l], ada_l["sh1"], ada_l["sc1"], lat_mod, tm)
        hc = _norm_mod(cs, ln1_w[l], ada_l["sh1"], ada_l["sc1"], ctx_mod, tm_c)
        u = _in_proj(h, w_in, l, na_q_norm_w[l], na_k_norm_w[l], IN_W, tm, 1024)
        uc = _in_proj(hc, w_in, l, na_q_norm_w[l], na_k_norm_w[l], ctx_cols if last else IN_W, tm_c, 512)

        if last:
            _, s_fw = _hgrn_scan(uc, lb_all[0, l], s_zero, ctx_len, reverse=False, mode="state")
            _, s_bw = _hgrn_scan(uc, lb_all[1, l], s_zero, ctx_len, reverse=True, mode="state")
        else:
            co_fw, s_fw = _hgrn_scan(uc, lb_all[0, l], s_zero, ctx_len, reverse=False, mode="raw")
            hg_c, s_bw = _hgrn_scan(uc, lb_all[1, l], s_zero, ctx_len, reverse=True, mode="final", o_fw=co_fw,
                                    norm_w=hg_norm_w[l])
        o_fw, _ = _hgrn_scan(u, lb_all[0, l], s_fw, seq, reverse=False, mode="raw")
        hg_out, _ = _hgrn_scan(u, lb_all[1, l], s_bw, seq, reverse=True, mode="final", o_fw=o_fw,
                               norm_w=hg_norm_w[l])

        na_out = _neighbourhood_attention(u, uc, bias, l * N_NA, na_out_norm_w[l], batch, rows, ctx_len)
        cv_out = _short_conv(u, cv_w[l], cv_out_norm_w[l], seq, _row_tile(seq, 256))

        xs = _token_stream(xs, seq, None, p, l, ada_l, [hg_out, na_out, cv_out])
        if not last:
            na_c = _context_attention(uc, na_out_norm_w[l], batch, ctx_len)
            cv_c = _short_conv(uc, cv_w[l], cv_out_norm_w[l], ctx_len, _row_tile(ctx_len, 256))
            cs = _token_stream(cs, ctx_len, 0, p, l, ada_l, [hg_c, na_c, cv_c])
    return xs.reshape(batch, seq, d)
```

```python
import functools

import numpy as np
import jax
import jax.numpy as jnp
from jax import lax
from jax.experimental import pallas as pl
from jax.experimental.pallas import tpu as pltpu

F32 = jnp.float32
BF16 = jnp.bfloat16

D_MODEL = 2048
DEPTH = 2
GRID_W = 64
HEAD_DIM = 128
N_HG = 4
N_NA = 8
N_CV = 4
HG_W = N_HG * HEAD_DIM
NA_W = N_NA * HEAD_DIM
CV_W = N_CV * HEAD_DIM
IN_SPLITS = (HG_W, HG_W, HG_W, NA_W, NA_W, HG_W, HG_W, NA_W, CV_W, CV_W, CV_W)
IN_W = sum(IN_SPLITS)
(OFF_FFW, OFF_FBW, OFF_I, OFF_NAK, OFF_NAV, OFF_HGQ, OFF_HGG, OFF_NAQ, OFF_CVB, OFF_CVC,
 OFF_CVV) = [int(v) // HEAD_DIM for v in np.cumsum((0,) + IN_SPLITS[:-1])]
NA_WIN_R = 8
NA_WIN_C = 16
NA_QBLK_C = 16
NA_KEY_C = 32
NA_KEY_R = 16
NA_HEADS = 2
CONV_W = 3
D_FF = 5632
EPS = 1e-6
F_FLOOR = 1e-30
ATTN_SCALE = HEAD_DIM ** -0.5
NEG_INF = -1e30

HG_CHUNK = 128
HG_BLOCK = 512
SUBLANE = 8
HALO = 16
MXU_N = 256
VMEM_LIMIT = 56 * 1024 * 1024


def _params(*sem):
    return pltpu.CompilerParams(dimension_semantics=sem, vmem_limit_bytes=VMEM_LIMIT)


def _dot(a, b):
    return jnp.dot(a, b, preferred_element_type=F32)


def _dot_nt(a, b):
    return lax.dot_general(a, b, (((1,), (1,)), ((), ())), preferred_element_type=F32)


def _dot_tn(a, b):
    return lax.dot_general(a, b, (((0,), (0,)), ((), ())), preferred_element_type=F32)


def _rms(x, w):
    return x * lax.rsqrt(jnp.mean(x * x, axis=-1, keepdims=True) + EPS) * w


def _silu(x):
    return x * jax.nn.sigmoid(x)


def _ada_kernel(s_ref, w_ref, b_ref, o_ref):
    s = _silu(s_ref[...]).astype(BF16)
    o_ref[0] = _dot(s, w_ref[0].astype(BF16)) + b_ref[0]


def _ada_table(cond, w_ada, b_ada):
    depth, d, n = w_ada.shape
    tn = 1024
    return pl.pallas_call(
        _ada_kernel,
        grid=(depth, n // tn),
        in_specs=[pl.BlockSpec((SUBLANE, d), lambda l, j: (0, 0)),
                  pl.BlockSpec((1, d, tn), lambda l, j: (l, 0, j)),
                  pl.BlockSpec((1, 1, tn), lambda l, j: (l, 0, j))],
        out_specs=pl.BlockSpec((1, SUBLANE, tn), lambda l, j: (l, 0, j)),
        out_shape=jax.ShapeDtypeStruct((depth, SUBLANE, n), F32),
        compiler_params=_params("parallel", "parallel"),
    )(cond, w_ada, b_ada.reshape(depth, 1, n))


def _norm_mod_kernel(x_ref, w_ref, sh_ref, sc_ref, o_ref):
    y = _rms(x_ref[...], w_ref[...])
    o_ref[...] = (y * (1.0 + sc_ref[0]) + sh_ref[0]).astype(o_ref.dtype)


def _norm_mod(x, w, shift, scale, mod_of_tile, tr):
    rows, d = x.shape
    mod = lambda i: (mod_of_tile(i), 0, 0)
    return pl.pallas_call(
        _norm_mod_kernel,
        grid=(rows // tr,),
        in_specs=[pl.BlockSpec((tr, d), lambda i: (i, 0)),
                  pl.BlockSpec((1, d), lambda i: (0, 0)),
                  pl.BlockSpec((1, 1, d), mod),
                  pl.BlockSpec((1, 1, d), mod)],
        out_specs=pl.BlockSpec((tr, d), lambda i: (i, 0)),
        out_shape=jax.ShapeDtypeStruct((rows, d), BF16),
        compiler_params=_params("parallel"),
    )(x, w.reshape(1, d), shift, scale)


def _in_proj_kinds(n_cols, tn):
    kinds = []
    for head in range(n_cols // HEAD_DIM):
        if OFF_NAK <= head < OFF_NAV:
            kinds.append("k")
        elif OFF_NAQ <= head < OFF_CVB:
            kinds.append("q")
        elif OFF_HGQ <= head < OFF_HGG:
            kinds.append("silu")
        else:
            kinds.append("id")
    per = tn // HEAD_DIM
    return tuple(tuple(kinds[t * per:(t + 1) * per]) for t in range(n_cols // tn))


def _in_proj_kernel(a_ref, w_ref, qw_ref, kw_ref, o_ref, wb_ref, *, kinds):
    j = pl.program_id(0)

    @pl.when(pl.program_id(1) == 0)
    def _():
        wb_ref[...] = w_ref[...].astype(BF16)

    def transform(y, kind):
        if kind == "k":
            return _rms(y, kw_ref[...])
        if kind == "q":
            return _rms(y, qw_ref[...]) * ATTN_SCALE
        if kind == "silu":
            return _silu(y)
        return y

    per_slab = MXU_N // HEAD_DIM
    for pattern in sorted(set(kinds)):
        hit = functools.reduce(jnp.logical_or, [j == jj for jj, tk in enumerate(kinds) if tk == pattern])

        @pl.when(hit)
        def _(pattern=pattern):
            a = a_ref[...]
            for c in range(len(pattern) // per_slab):
                cols = slice(c * MXU_N, (c + 1) * MXU_N)
                acc = _dot(a, wb_ref[:, cols])
                parts = [transform(acc[:, s * HEAD_DIM:(s + 1) * HEAD_DIM], pattern[c * per_slab + s])
                         for s in range(per_slab)]
                o_ref[:, cols] = jnp.concatenate(parts, axis=1).astype(o_ref.dtype)


def _in_proj(h, w_in, layer, q_norm, k_norm, n_cols, tm, tn):
    m, k = h.shape
    kinds = _in_proj_kinds(n_cols, tn)
    vec = pl.BlockSpec((1, HEAD_DIM), lambda j, i: (0, 0))
    return pl.pallas_call(
        functools.partial(_in_proj_kernel, kinds=kinds),
        grid=(n_cols // tn, m // tm),
        in_specs=[pl.BlockSpec((tm, k), lambda j, i: (i, 0)),
                  pl.BlockSpec((None, k, tn), lambda j, i: (layer, 0, j)),
                  vec, vec],
        out_specs=pl.BlockSpec((tm, tn), lambda j, i: (i, j)),
        out_shape=jax.ShapeDtypeStruct((m, n_cols), BF16),
        scratch_shapes=[pltpu.VMEM((k, tn), BF16)],
        compiler_params=_params("parallel", "arbitrary"),
    )(h, w_in, q_norm.reshape(1, HEAD_DIM), k_norm.reshape(1, HEAD_DIM))


def _mm_res_kernel(*refs, splits):
    n_a = len(splits)
    a_refs = refs[:n_a]
    w_ref, x_ref, g_ref, o_ref, wb_ref = refs[n_a:]

    @pl.when(pl.program_id(1) == 0)
    def _():
        wb_ref[...] = w_ref[...].astype(BF16)

    acc = None
    off = 0
    for a_ref, width in zip(a_refs, splits):
        part = _dot(a_ref[...], wb_ref[off:off + width, :])
        acc = part if acc is None else acc + part
        off += width
    o_ref[...] = x_ref[...] + g_ref[0] * acc


def _matmul_residual(a_list, w, layer, x, gate, mod_of_tile, tm, tn):
    m = x.shape[0]
    _, k, n = w.shape
    splits = tuple(a.shape[1] for a in a_list)
    assert sum(splits) == k
    in_specs = [pl.BlockSpec((tm, s), lambda j, i: (i, 0)) for s in splits]
    in_specs += [pl.BlockSpec((None, k, tn), lambda j, i: (layer, 0, j), pipeline_mode=pl.Buffered(1)),
                 pl.BlockSpec((tm, tn), lambda j, i: (i, j)),
                 pl.BlockSpec((1, 1, tn), lambda j, i: (mod_of_tile(i), 0, j))]
    return pl.pallas_call(
        functools.partial(_mm_res_kernel, splits=splits),
        grid=(n // tn, m // tm),
        in_specs=in_specs,
        out_specs=pl.BlockSpec((tm, tn), lambda j, i: (i, j)),
        out_shape=jax.ShapeDtypeStruct((m, n), F32),
        scratch_shapes=[pltpu.VMEM((k, tn), BF16)],
        compiler_params=_params("parallel", "arbitrary"),
    )(*a_list, w, x, gate)


def _shift_rows(x, prev_row, next_row):
    r = x.shape[0]
    idx = lax.broadcasted_iota(jnp.int32, x.shape, 0)
    down = jnp.where(idx == 0, prev_row, pltpu.roll(x, 1, axis=0))
    up = jnp.where(idx == r - 1, next_row, pltpu.roll(x, r - 1, axis=0))
    return down, up


def _halo_specs(tr, tc, col_of, tiles_per_seq, n_rows):
    per = tr // HALO
    last_blk = n_rows // HALO - 1
    cur = pl.BlockSpec((tr, tc), lambda i, j: (i, col_of(j)))
    prev = pl.BlockSpec((HALO, tc), lambda i, j: (jnp.maximum(i * per - 1, 0), col_of(j)))
    nxt = pl.BlockSpec((HALO, tc), lambda i, j: (jnp.minimum((i + 1) * per, last_blk), col_of(j)))
    return cur, prev, nxt


def _seq_edges(tiles_per_seq):
    i = pl.program_id(0)
    pos = i % tiles_per_seq
    return (pos != 0).astype(F32), (pos != tiles_per_seq - 1).astype(F32)


def _ffn_up_kernel(h_ref, hp_ref, hn_ref, wg_ref, wv_ref, cg_ref, cv_ref, bg_ref, bv_ref, o_ref, lhs_ref,
                   *, tm, tiles_per_seq):
    pos = pl.program_id(0) % tiles_per_seq
    has_prev = pos != 0
    has_next = pos != tiles_per_seq - 1

    @pl.when(pl.program_id(1) == 0)
    def _():
        lhs_ref[0:HALO, :] = jnp.where(has_prev, hp_ref[...], jnp.zeros_like(hp_ref))
        lhs_ref[HALO:HALO + tm, :] = h_ref[...]
        lhs_ref[HALO + tm:, :] = jnp.where(has_next, hn_ref[...], jnp.zeros_like(hn_ref))

    lhs = lhs_ref[...]

    def branch(w_ref, c_ref, b_ref):
        acc = _dot(lhs, w_ref[...])
        r = acc.shape[0]
        y = (pltpu.roll(acc, 1, axis=0) * c_ref[0:1, :] + acc * c_ref[1:2, :]
             + pltpu.roll(acc, r - 1, axis=0) * c_ref[2:3, :])
        return y[HALO:HALO + tm, :] + b_ref[...]

    gate = branch(wg_ref, cg_ref, bg_ref)
    val = branch(wv_ref, cv_ref, bv_ref)
    o_ref[...] = (_silu(gate) * val).astype(o_ref.dtype)


def _ffn_up_conv_gate(h, w_up, layer, cw, cb, seq_len, tm, tn):
    rows, k = h.shape
    nj = D_FF // tn
    tiles_per_seq = seq_len // tm
    per = tm // HALO
    last_blk = rows // HALO - 1
    cb2 = cb.reshape(1, 2 * D_FF)
    in_specs = [pl.BlockSpec((tm, k), lambda i, j: (i, 0)),
                pl.BlockSpec((HALO, k), lambda i, j: (jnp.maximum(i * per - 1, 0), 0)),
                pl.BlockSpec((HALO, k), lambda i, j: (jnp.minimum((i + 1) * per, last_blk), 0)),
                pl.BlockSpec((None, k, tn), lambda i, j: (layer, 0, j)),
                pl.BlockSpec((None, k, tn), lambda i, j: (layer, 0, j + nj)),
                pl.BlockSpec((CONV_W, tn), lambda i, j: (0, j)),
                pl.BlockSpec((CONV_W, tn), lambda i, j: (0, j + nj)),
                pl.BlockSpec((1, tn), lambda i, j: (0, j)),
                pl.BlockSpec((1, tn), lambda i, j: (0, j + nj))]
    return pl.pallas_call(
        functools.partial(_ffn_up_kernel, tm=tm, tiles_per_seq=tiles_per_seq),
        grid=(rows // tm, nj),
        in_specs=in_specs,
        out_specs=pl.BlockSpec((tm, tn), lambda i, j: (i, j)),
        out_shape=jax.ShapeDtypeStruct((rows, D_FF), BF16),
        scratch_shapes=[pltpu.VMEM((tm + 2 * HALO, k), BF16)],
        compiler_params=_params("parallel", "arbitrary"),
    )(h, h, h, w_up, w_up, cw, cw, cb2, cb2)


def _short_conv_kernel(b_ref, c_ref, cp_ref, cn_ref, v_ref, vp_ref, vn_ref, w_ref, nw_ref, o_ref, *, tiles_per_seq):
    has_prev, has_next = _seq_edges(tiles_per_seq)
    p = c_ref[...].astype(F32) * v_ref[...].astype(F32)
    prev_row = cp_ref[HALO - 1:HALO, :].astype(F32) * vp_ref[HALO - 1:HALO, :].astype(F32) * has_prev
    next_row = cn_ref[0:1, :].astype(F32) * vn_ref[0:1, :].astype(F32) * has_next
    down, up = _shift_rows(p, prev_row, next_row)
    y = b_ref[...].astype(F32) * (down * w_ref[0:1, :] + p * w_ref[1:2, :] + up * w_ref[2:3, :])
    for h in range(N_CV):
        sl = slice(h * HEAD_DIM, (h + 1) * HEAD_DIM)
        o_ref[:, sl] = _rms(y[:, sl], nw_ref[:, sl]).astype(o_ref.dtype)


def _short_conv(u, cv_w, cv_onorm, seq_len, tr):
    rows = u.shape[0]
    tiles_per_seq = seq_len // tr
    blk = lambda off: (lambda j: off * HEAD_DIM // CV_W)
    b_spec = pl.BlockSpec((tr, CV_W), lambda i, j: (i, OFF_CVB * HEAD_DIM // CV_W))
    c_specs = _halo_specs(tr, CV_W, blk(OFF_CVC), tiles_per_seq, rows)
    v_specs = _halo_specs(tr, CV_W, blk(OFF_CVV), tiles_per_seq, rows)
    return pl.pallas_call(
        functools.partial(_short_conv_kernel, tiles_per_seq=tiles_per_seq),
        grid=(rows // tr, 1),
        in_specs=[b_spec, *c_specs, *v_specs,
                  pl.BlockSpec((CONV_W, CV_W), lambda i, j: (0, 0)),
                  pl.BlockSpec((1, CV_W), lambda i, j: (0, 0))],
        out_specs=pl.BlockSpec((tr, CV_W), lambda i, j: (i, 0)),
        out_shape=jax.ShapeDtypeStruct((rows, CV_W), BF16),
        compiler_params=_params("parallel", "arbitrary"),
    )(u, u, u, u, u, u, u, cv_w, cv_onorm.reshape(1, CV_W))


def _hgrn_structure(chunk, reverse):
    idx = np.arange(chunk)
    i, t = idx[:, None], idx[None, :]
    mats = [t <= i, t > i]
    masks = []
    s = chunk // 2
    while s >= 1:
        blk = idx // (2 * s)
        upper = (idx % (2 * s)) >= s
        mid = blk * 2 * s + s - 1
        a = np.where(upper[:, None], (t > mid[:, None]) & (t <= i), (t > i) & (t <= mid[:, None]))
        mats.append(a)
        masks.append((blk[:, None] == blk[None, :]) & upper[:, None] & ~upper[None, :])
        s //= 2
    mats = np.stack(mats).astype(np.float32)
    masks = np.stack(masks).astype(np.float32)
    if reverse:
        mats = mats[:, ::-1, ::-1]
        masks = masks[:, ::-1, ::-1]
    return mats.reshape(-1, chunk), masks


def _hgrn_kernel(*refs, chunk, n_sub, n_levels, reverse, mode):
    if mode == "final":
        (z_ref, v_ref, q_ref, lb_ref, a_ref, m_ref, s0_ref, g_ref, ofw_ref, nw_ref,
         o_ref, sfin_ref, st_ref) = refs
    elif mode == "raw":
        z_ref, v_ref, q_ref, lb_ref, a_ref, m_ref, s0_ref, o_ref, sfin_ref, st_ref = refs
    else:
        z_ref, v_ref, lb_ref, a_ref, s0_ref, sfin_ref, st_ref = refs
    c = pl.program_id(1)

    @pl.when(c == 0)
    def _():
        st_ref[...] = s0_ref[0]

    lb = lb_ref[...]
    a = a_ref[...]
    tot_row = 0 if reverse else chunk - 1
    subs = range(n_sub - 1, -1, -1) if reverse else range(n_sub)
    for sub in subs:
        rows = slice(sub * chunk, (sub + 1) * chunk)
        z = z_ref[rows, :].astype(F32)
        f = lb + (1.0 - lb) * jax.nn.sigmoid(z)
        log_f = jnp.log(jnp.maximum(f, F_FLOOR))
        k_all = (1.0 - lb) * jax.nn.sigmoid(-z)
        v_all = v_ref[rows, :].astype(F32)
        hi = log_f.astype(BF16)
        lo = (log_f - hi.astype(F32)).astype(BF16)
        expo = _dot(a, hi) + _dot(a, lo)
        for h in range(N_HG):
            sl = slice(h * HEAD_DIM, (h + 1) * HEAD_DIM)
            k, v = k_all[:, sl], v_all[:, sl]
            e_rem = jnp.exp(expo[chunk:2 * chunk, sl])
            e_tot = jnp.exp(expo[tot_row:tot_row + 1, sl])
            st = st_ref[h]
            st_ref[h] = st * e_tot + _dot_tn(v.astype(BF16), (k * e_rem).astype(BF16))
            if mode == "state":
                continue
            q = q_ref[rows, sl].astype(F32)
            e_cum = jnp.exp(expo[0:chunk, sl])
            o = _dot_nt((q * e_cum).astype(BF16), st.astype(BF16))
            att = None
            for lvl in range(n_levels):
                e = jnp.exp(expo[(2 + lvl) * chunk:(3 + lvl) * chunk, sl])
                part = m_ref[lvl] * _dot_nt((q * e).astype(BF16), (k * e).astype(BF16))
                att = part if att is None else att + part
            diag = jnp.sum(q * k, axis=-1, keepdims=True)
            o = o + _dot(att.astype(BF16), v.astype(BF16)) + diag * v
            if mode == "final":
                y = _rms(o + ofw_ref[rows, sl], nw_ref[...]) * _silu(g_ref[rows, sl].astype(F32))
                o_ref[rows, sl] = y.astype(o_ref.dtype)
            else:
                o_ref[rows, sl] = o.astype(o_ref.dtype)

    @pl.when(c == pl.num_programs(1) - 1)
    def _():
        sfin_ref[0] = st_ref[...]


def _hgrn_scan(u, lb, s0, seq_len, *, reverse, mode, o_fw=None, norm_w=None):
    rows = u.shape[0]
    batch = rows // seq_len
    blk = min(HG_BLOCK, seq_len)
    chunk = min(HG_CHUNK, blk)
    n_blocks = seq_len // blk
    mats, masks = _hgrn_structure(chunk, reverse)
    n_levels = masks.shape[0]
    if mode == "state":
        mats = mats[:2 * chunk]
    per_blk = HG_W // HEAD_DIM

    def tok(off):
        col = off // per_blk
        if reverse:
            return pl.BlockSpec((blk, HG_W), lambda b, c: (b * n_blocks + n_blocks - 1 - c, col))
        return pl.BlockSpec((blk, HG_W), lambda b, c: (b * n_blocks + c, col))

    const2 = lambda shape: pl.BlockSpec(shape, lambda b, c: (0, 0))
    state_spec = pl.BlockSpec((1, N_HG, HEAD_DIM, HEAD_DIM), lambda b, c: (b, 0, 0, 0))
    state_shape = jax.ShapeDtypeStruct((batch, N_HG, HEAD_DIM, HEAD_DIM), F32)
    z_spec = tok(OFF_FBW if reverse else OFF_FFW)
    lb2, mats_b = lb.reshape(1, HG_W), jnp.asarray(mats, BF16)
    if mode == "state":
        in_specs = [z_spec, tok(OFF_I), const2((1, HG_W)), const2(mats.shape), state_spec]
        args = [u, u, lb2, mats_b, s0]
        out_specs, out_shape = [state_spec], [state_shape]
    else:
        in_specs = [z_spec, tok(OFF_I), tok(OFF_HGQ), const2((1, HG_W)), const2(mats.shape),
                    pl.BlockSpec(masks.shape, lambda b, c: (0, 0, 0)), state_spec]
        args = [u, u, u, lb2, mats_b, jnp.asarray(masks, F32), s0]
        if mode == "final":
            in_specs += [tok(OFF_HGG), tok(0), const2((1, HEAD_DIM))]
            args += [u, o_fw, norm_w.reshape(1, HEAD_DIM)]
        out_specs = [tok(0), state_spec]
        out_shape = [jax.ShapeDtypeStruct((rows, HG_W), BF16 if mode == "final" else F32), state_shape]
    res = pl.pallas_call(
        functools.partial(_hgrn_kernel, chunk=chunk, n_sub=blk // chunk, n_levels=n_levels, reverse=reverse,
                          mode=mode),
        grid=(batch, n_blocks),
        in_specs=in_specs,
        out_specs=out_specs,
        out_shape=out_shape,
        scratch_shapes=[pltpu.VMEM((N_HG, HEAD_DIM, HEAD_DIM), F32)],
        compiler_params=_params("parallel", "arbitrary"),
    )(*args)
    return (None, res[0]) if mode == "state" else (res[0], res[1])


def _na_bias_table(rpb, rows):
    n_rb = rows // NA_WIN_R
    n_cb = GRID_W // NA_QBLK_C
    n_dr, n_dc = 2 * NA_WIN_R - 1, 2 * NA_WIN_C - 1
    n_k = NA_KEY_R * NA_KEY_C
    offs, row_mask = [], []
    for rb in (0, min(1, n_rb - 1), n_rb - 1):
        k_row0 = int(np.clip(rb * NA_WIN_R - NA_WIN_R // 2, 0, rows - NA_KEY_R))
        q_r = rb * NA_WIN_R + np.arange(NA_WIN_R)
        k_r = k_row0 + np.arange(NA_KEY_R)
        r_start = np.clip(q_r - NA_WIN_R // 2, 0, rows - NA_WIN_R)
        ok_r = (k_r[None, :] >= r_start[:, None]) & (k_r[None, :] < r_start[:, None] + NA_WIN_R)
        row_mask.append(np.repeat(np.where(ok_r, 0.0, NEG_INF), NA_KEY_C, axis=1))
        offs.append(k_row0 - rb * NA_WIN_R + NA_WIN_R - 1)
    row_mask = jnp.asarray(np.stack(row_mask), F32)
    pad_lo = NA_WIN_R - 1 - min(offs)
    n_a = NA_KEY_R - 1 + max(offs) + pad_lo + 1
    lanes = -(-(n_a * NA_KEY_C) // HEAD_DIM) * HEAD_DIM
    sel_c, ok_c = [], []
    for cb in range(n_cb):
        k_col0 = int(np.clip(cb * NA_QBLK_C - NA_WIN_C // 2, 0, GRID_W - NA_KEY_C))
        q_c = cb * NA_QBLK_C + np.arange(NA_QBLK_C)
        k_c = k_col0 + np.arange(NA_KEY_C)
        c_start = np.clip(q_c - NA_WIN_C // 2, 0, GRID_W - NA_WIN_C)
        ok_c.append((k_c[None, :] >= c_start[:, None]) & (k_c[None, :] < c_start[:, None] + NA_WIN_C))
        dc = k_c[None, :] - q_c[:, None] + NA_WIN_C - 1
        sel_c.append(dc[:, :, None] == np.arange(n_dc))
    sel_c = jnp.asarray(np.stack(sel_c), F32)
    ok_c = jnp.asarray(np.stack(ok_c))
    strip = jnp.einsum("hab,mcjb->hmcaj", rpb.astype(F32), sel_c, precision=lax.Precision.HIGHEST)
    strip = jnp.where(ok_c[None, :, :, None, :], strip, NEG_INF)
    strip = jnp.pad(strip, ((0, 0), (0, 0), (0, 0), (pad_lo, n_a - n_dr - pad_lo), (0, 0)))
    strip = strip.reshape(strip.shape[:3] + (n_a * NA_KEY_C,))
    strip = jnp.pad(strip, ((0, 0), (0, 0), (0, 0), (0, lanes - n_a * NA_KEY_C)))
    n_heads = rpb.shape[0]
    return pl.pallas_call(
        functools.partial(_na_bias_kernel, offs=tuple(offs), pad_lo=pad_lo),
        grid=(n_heads,),
        in_specs=[pl.BlockSpec((1, n_cb, NA_QBLK_C, lanes), lambda h: (h, 0, 0, 0)),
                  pl.BlockSpec((3, NA_WIN_R, n_k), lambda h: (0, 0, 0))],
        out_specs=pl.BlockSpec((1, 3, n_cb, NA_WIN_R * NA_QBLK_C, n_k), lambda h: (h, 0, 0, 0, 0)),
        out_shape=jax.ShapeDtypeStruct((n_heads, 3, n_cb, NA_WIN_R * NA_QBLK_C, n_k), F32),
        compiler_params=_params("parallel"),
    )(strip, row_mask)


def _na_bias_kernel(t_ref, rm_ref, o_ref, *, offs, pad_lo):
    n_cb = t_ref.shape[1]
    lanes = t_ref.shape[-1]
    per = HEAD_DIM // NA_KEY_C
    n_k = o_ref.shape[-1]
    for cb in range(n_cb):
        t = t_ref[0, cb]
        shifted = [t] + [pltpu.roll(t, lanes - NA_KEY_C * s, axis=1) for s in range(1, per)]
        for cls, off in enumerate(offs):
            for r in range(NA_WIN_R):
                m = off - r + pad_lo
                base = (m // per) * HEAD_DIM
                tile = shifted[m % per][:, base:base + n_k] + rm_ref[cls, r:r + 1, :]
                o_ref[0, cls, cb, r * NA_QBLK_C:(r + 1) * NA_QBLK_C, :] = tile


def _na_kernel(q_ref, k0_ref, k1_ref, k2_ref, k3_ref, v0_ref, v1_ref, v2_ref, v3_ref, kc_ref, vc_ref, bias_ref,
               ow_ref, o_ref):
    n_cb = GRID_W // NA_QBLK_C
    n_q, n_k = NA_WIN_R * NA_QBLK_C, NA_KEY_R * NA_KEY_C
    q_all = q_ref[...].astype(F32)
    k_all = jnp.concatenate([r[...] for r in (k0_ref, k1_ref, k2_ref, k3_ref)], axis=0).astype(F32)
    v_all = jnp.concatenate([r[...] for r in (v0_ref, v1_ref, v2_ref, v3_ref)], axis=0).astype(F32)
    for hh in range(NA_HEADS):
        hs = slice(hh * HEAD_DIM, (hh + 1) * HEAD_DIM)
        q3 = q_all[:, hs].reshape(NA_WIN_R, GRID_W, HEAD_DIM)
        k3 = k_all[:, hs].reshape(NA_KEY_R, GRID_W, HEAD_DIM)
        v3 = v_all[:, hs].reshape(NA_KEY_R, GRID_W, HEAD_DIM)
        k_ctx = kc_ref[:, hs]
        v_ctx = vc_ref[:, hs]
        for cb in range(n_cb):
            q0 = cb * NA_QBLK_C
            k_col0 = min(max(q0 - NA_WIN_C // 2, 0), GRID_W - NA_KEY_C)
            qm = q3[:, q0:q0 + NA_QBLK_C, :].reshape(n_q, HEAD_DIM).astype(BF16)
            km = k3[:, k_col0:k_col0 + NA_KEY_C, :].reshape(n_k, HEAD_DIM).astype(BF16)
            vm = v3[:, k_col0:k_col0 + NA_KEY_C, :].reshape(n_k, HEAD_DIM).astype(BF16)
            s_loc = _dot_nt(qm, km) + bias_ref[hh, 0, cb]
            s_ctx = _dot_nt(qm, k_ctx)
            mx = jnp.maximum(jnp.max(s_loc, axis=-1, keepdims=True), jnp.max(s_ctx, axis=-1, keepdims=True))
            p_loc = jnp.exp(s_loc - mx)
            p_ctx = jnp.exp(s_ctx - mx)
            den = jnp.sum(p_loc, axis=-1, keepdims=True) + jnp.sum(p_ctx, axis=-1, keepdims=True)
            o = (_dot(p_loc.astype(BF16), vm) + _dot(p_ctx.astype(BF16), v_ctx)) / den
            o = _rms(o, ow_ref[hh]).astype(o_ref.dtype)
            for r in range(NA_WIN_R):
                o_ref[r * GRID_W + q0:r * GRID_W + q0 + NA_QBLK_C, hs] = o[r * NA_QBLK_C:(r + 1) * NA_QBLK_C, :]


def _neighbourhood_attention(u, u_ctx, bias, head0, out_norm, batch, rows, ctx_len):
    assert rows % NA_WIN_R == 0 and rows >= NA_KEY_R
    n_rb = rows // NA_WIN_R
    q_blk = NA_WIN_R * GRID_W
    k_blk = NA_KEY_R * GRID_W // 4
    k_per_batch = rows * GRID_W // k_blk

    width = NA_HEADS * HEAD_DIM
    assert all(off % NA_HEADS == 0 for off in (OFF_NAQ, OFF_NAK, OFF_NAV, head0, N_NA))
    col_q, col_k, col_v = OFF_NAQ // NA_HEADS, OFF_NAK // NA_HEADS, OFF_NAV // NA_HEADS

    def key_spec(col, part):
        def index(h, rb, b):
            first = jnp.clip(rb * NA_WIN_R - NA_WIN_R // 2, 0, rows - NA_KEY_R) * GRID_W // k_blk
            return (b * k_per_batch + first + part, col + h)
        return pl.BlockSpec((k_blk, width), index)

    def row_class(rb):
        return jnp.where(rb == 0, 0, jnp.where(rb == n_rb - 1, 2, 1))

    n_cb = GRID_W // NA_QBLK_C
    in_specs = [pl.BlockSpec((q_blk, width), lambda h, rb, b: (b * n_rb + rb, col_q + h))]
    in_specs += [key_spec(col_k, p) for p in range(4)] + [key_spec(col_v, p) for p in range(4)]
    in_specs += [pl.BlockSpec((ctx_len, width), lambda h, rb, b: (b, col_k + h)),
                 pl.BlockSpec((ctx_len, width), lambda h, rb, b: (b, col_v + h)),
                 pl.BlockSpec((NA_HEADS, 1, n_cb, NA_WIN_R * NA_QBLK_C, NA_KEY_R * NA_KEY_C),
                              lambda h, rb, b: (head0 // NA_HEADS + h, row_class(rb), 0, 0, 0)),
                 pl.BlockSpec((NA_HEADS, 1, HEAD_DIM), lambda h, rb, b: (h, 0, 0))]
    return pl.pallas_call(
        _na_kernel,
        grid=(N_NA // NA_HEADS, n_rb, batch),
        in_specs=in_specs,
        out_specs=pl.BlockSpec((q_blk, width), lambda h, rb, b: (b * n_rb + rb, h)),
        out_shape=jax.ShapeDtypeStruct((u.shape[0], NA_W), BF16),
        compiler_params=_params("parallel", "parallel", "parallel"),
    )(u, u, u, u, u, u, u, u, u, u_ctx, u_ctx, bias, out_norm.reshape(N_NA, 1, HEAD_DIM))


def _ctx_attn_kernel(q_ref, k_ref, v_ref, ow_ref, o_ref):
    s = _dot_nt(q_ref[...].astype(BF16), k_ref[...].astype(BF16))
    p = jnp.exp(s - jnp.max(s, axis=-1, keepdims=True))
    o = _dot(p.astype(BF16), v_ref[...].astype(BF16)) / jnp.sum(p, axis=-1, keepdims=True)
    o_ref[...] = _rms(o, ow_ref[0]).astype(o_ref.dtype)


def _context_attention(u_ctx, out_norm, batch, ctx_len):
    tok = lambda off: pl.BlockSpec((ctx_len, HEAD_DIM), lambda b, h: (b, off + h))
    return pl.pallas_call(
        _ctx_attn_kernel,
        grid=(batch, N_NA),
        in_specs=[tok(OFF_NAQ), tok(OFF_NAK), tok(OFF_NAV),
                  pl.BlockSpec((1, 1, HEAD_DIM), lambda b, h: (h, 0, 0))],
        out_specs=tok(0),
        out_shape=jax.ShapeDtypeStruct((u_ctx.shape[0], NA_W), BF16),
        compiler_params=_params("parallel", "parallel"),
    )(u_ctx, u_ctx, u_ctx, out_norm.reshape(N_NA, 1, HEAD_DIM))


def _row_tile(rows, target):
    t = min(rows, target)
    assert rows % t == 0
    return t


def _token_stream(xs, seq_len, mod_row, p, layer, ada_l, mixers):
    rows = xs.shape[0]
    sh2, sc2, g1, g2 = ada_l["sh2"], ada_l["sc2"], ada_l["g1"], ada_l["g2"]
    tm = _row_tile(rows, 1024)
    tiles_per_seq = max(seq_len // tm, 1)
    if mod_row is None:
        mod_of_tile = lambda i: 1 + i // tiles_per_seq
    else:
        mod_of_tile = lambda i: mod_row
    x1 = _matmul_residual(mixers, p["w_out"], layer, xs, g1, mod_of_tile, tm, 1024)
    h2 = _norm_mod(x1, p["ln2"][layer], sh2, sc2, mod_of_tile, tm)
    act = _ffn_up_conv_gate(h2, p["w_up"], layer, p["f_cw"][layer], p["f_cb"][layer], seq_len,
                            _row_tile(seq_len, 2048), 512)
    return _matmul_residual([act], p["w_down"], layer, x1, g2, mod_of_tile, tm, 512)


def kernel(x, c, ctx, c_ctx, w_ada, b_ada, ln1_w, ln2_w, w_in, hg_lb_logits, hg_norm_w, na_q_norm_w, na_k_norm_w,
           na_rpb, na_out_norm_w, cv_w, cv_out_norm_w, w_out, w_up, ffn_conv_w, ffn_conv_b, w_down):
    batch, seq, d = x.shape
    ctx_len = ctx.shape[1]
    rows = seq // GRID_W
    depth = w_ada.shape[0]
    xs = x.reshape(batch * seq, d)
    cs = ctx.reshape(batch * ctx_len, d)

    lb_sm = jax.nn.softmax(hg_lb_logits.astype(F32), axis=1)
    lb_all = jnp.cumsum(lb_sm, axis=1) - lb_sm[:, :1]

    cond = jnp.zeros((SUBLANE, d), F32).at[0].set(c_ctx).at[1:1 + batch].set(c)
    ada = _ada_table(cond, w_ada, b_ada).reshape(depth, SUBLANE, 6, 1, d)

    s_zero = jnp.zeros((batch, N_HG, HEAD_DIM, HEAD_DIM), F32)
    p = {"w_out": w_out, "w_up": w_up.astype(BF16), "w_down": w_down, "ln2": ln2_w, "f_cw": ffn_conv_w,
         "f_cb": ffn_conv_b}
    ctx_cols = sum(IN_SPLITS[:5])
    bias = _na_bias_table(na_rpb.reshape((depth * N_NA,) + na_rpb.shape[2:]), rows)
    for l in range(depth):
        last = l == depth - 1
        names = ("sh1", "sc1", "g1", "sh2", "sc2", "g2")
        ada_l = {n: ada[l, :, i] for i, n in enumerate(names)}
        tm = _row_tile(xs.shape[0], 1024)
        tm_c = _row_tile(cs.shape[0], 1024)
        tiles_per_seq = seq // tm
        lat_mod = lambda i: 1 + i // tiles_per_seq
        ctx_mod = lambda i: 0
        h = _norm_mod(xs, ln1_w[l], ada_l["sh1"], ada_l["sc1"], lat_mod, tm)
        hc = _norm_mod(cs, ln1_w[l], ada_l["sh1"], ada_l["sc1"], ctx_mod, tm_c)
        u = _in_proj(h, w_in, l, na_q_norm_w[l], na_k_norm_w[l], IN_W, _row_tile(xs.shape[0], 2048), 1024)
        uc = _in_proj(hc, w_in, l, na_q_norm_w[l], na_k_norm_w[l], ctx_cols if last else IN_W, tm_c, 512)

        if last:
            _, s_fw = _hgrn_scan(uc, lb_all[0, l], s_zero, ctx_len, reverse=False, mode="state")
            _, s_bw = _hgrn_scan(uc, lb_all[1, l], s_zero, ctx_len, reverse=True, mode="state")
        else:
            co_fw, s_fw = _hgrn_scan(uc, lb_all[0, l], s_zero, ctx_len, reverse=False, mode="raw")
            hg_c, s_bw = _hgrn_scan(uc, lb_all[1, l], s_zero, ctx_len, reverse=True, mode="final", o_fw=co_fw,
                                    norm_w=hg_norm_w[l])
        o_fw, _ = _hgrn_scan(u, lb_all[0, l], s_fw, seq, reverse=False, mode="raw")
        hg_out, _ = _hgrn_scan(u, lb_all[1, l], s_bw, seq, reverse=True, mode="final", o_fw=o_fw,
                               norm_w=hg_norm_w[l])

        na_out = _neighbourhood_attention(u, uc, bias, l * N_NA, na_out_norm_w[l], batch, rows, ctx_len)
        cv_out = _short_conv(u, cv_w[l], cv_out_norm_w[l], seq, _row_tile(seq, 256))

        xs = _token_stream(xs, seq, None, p, l, ada_l, [hg_out, na_out, cv_out])
        if not last:
            na_c = _context_attention(uc, na_out_norm_w[l], batch, ctx_len)
            cv_c = _short_conv(uc, cv_w[l], cv_out_norm_w[l], ctx_len, _row_tile(ctx_len, 256))
            cs = _token_stream(cs, ctx_len, 0, p, l, ada_l, [hg_c, na_c, cv_c])
    return xs.reshape(batch, seq, d)
```

```python
import functools

import numpy as np
import jax
import jax.numpy as jnp
from jax import lax
from jax.experimental import pallas as pl
from jax.experimental.pallas import tpu as pltpu

F32 = jnp.float32
BF16 = jnp.bfloat16

D_MODEL = 2048
DEPTH = 2
GRID_W = 64
HEAD_DIM = 128
N_HG = 4
N_NA = 8
N_CV = 4
HG_W = N_HG * HEAD_DIM
NA_W = N_NA * HEAD_DIM
CV_W = N_CV * HEAD_DIM
IN_SPLITS = (HG_W, HG_W, HG_W, NA_W, NA_W, HG_W, HG_W, NA_W, CV_W, CV_W, CV_W)
IN_W = sum(IN_SPLITS)
(OFF_FFW, OFF_FBW, OFF_I, OFF_NAK, OFF_NAV, OFF_HGQ, OFF_HGG, OFF_NAQ, OFF_CVB, OFF_CVC,
 OFF_CVV) = [int(v) // HEAD_DIM for v in np.cumsum((0,) + IN_SPLITS[:-1])]
NA_WIN_R = 8
NA_WIN_C = 16
NA_QBLK_C = 16
NA_KEY_C = 32
NA_KEY_R = 16
NA_HEADS = 2
CONV_W = 3
D_FF = 5632
EPS = 1e-6
F_FLOOR = 1e-30
ATTN_SCALE = HEAD_DIM ** -0.5
NEG_INF = -1e30

HG_CHUNK = 128
HG_BLOCK = 512
SUBLANE = 8
HALO = 16
MXU_N = 256
VMEM_LIMIT = 56 * 1024 * 1024


def _params(*sem):
    return pltpu.CompilerParams(dimension_semantics=sem, vmem_limit_bytes=VMEM_LIMIT)


def _dot(a, b):
    return jnp.dot(a, b, preferred_element_type=F32)


def _dot_nt(a, b):
    return lax.dot_general(a, b, (((1,), (1,)), ((), ())), preferred_element_type=F32)


def _dot_tn(a, b):
    return lax.dot_general(a, b, (((0,), (0,)), ((), ())), preferred_element_type=F32)


def _rms(x, w):
    return x * lax.rsqrt(jnp.mean(x * x, axis=-1, keepdims=True) + EPS) * w


def _silu(x):
    half = 0.5 * x
    return half + half * jnp.tanh(half)


def _ada_kernel(s_ref, w_ref, b_ref, o_ref):
    s = _silu(s_ref[...]).astype(BF16)
    o_ref[0] = _dot(s, w_ref[0].astype(BF16)) + b_ref[0]


def _ada_table(cond, w_ada, b_ada):
    depth, d, n = w_ada.shape
    tn = 1024
    return pl.pallas_call(
        _ada_kernel,
        grid=(depth, n // tn),
        in_specs=[pl.BlockSpec((SUBLANE, d), lambda l, j: (0, 0)),
                  pl.BlockSpec((1, d, tn), lambda l, j: (l, 0, j)),
                  pl.BlockSpec((1, 1, tn), lambda l, j: (l, 0, j))],
        out_specs=pl.BlockSpec((1, SUBLANE, tn), lambda l, j: (l, 0, j)),
        out_shape=jax.ShapeDtypeStruct((depth, SUBLANE, n), F32),
        compiler_params=_params("parallel", "parallel"),
    )(cond, w_ada, b_ada.reshape(depth, 1, n))


def _norm_mod_kernel(x_ref, w_ref, sh_ref, sc_ref, o_ref):
    y = _rms(x_ref[...], w_ref[...])
    o_ref[...] = (y * (1.0 + sc_ref[0]) + sh_ref[0]).astype(o_ref.dtype)


def _norm_mod(x, w, shift, scale, mod_of_tile, tr):
    rows, d = x.shape
    mod = lambda i: (mod_of_tile(i), 0, 0)
    return pl.pallas_call(
        _norm_mod_kernel,
        grid=(rows // tr,),
        in_specs=[pl.BlockSpec((tr, d), lambda i: (i, 0)),
                  pl.BlockSpec((1, d), lambda i: (0, 0)),
                  pl.BlockSpec((1, 1, d), mod),
                  pl.BlockSpec((1, 1, d), mod)],
        out_specs=pl.BlockSpec((tr, d), lambda i: (i, 0)),
        out_shape=jax.ShapeDtypeStruct((rows, d), BF16),
        compiler_params=_params("parallel"),
    )(x, w.reshape(1, d), shift, scale)


def _in_proj_kinds(n_cols, tn):
    kinds = []
    for head in range(n_cols // HEAD_DIM):
        if OFF_NAK <= head < OFF_NAV:
            kinds.append("k")
        elif OFF_NAQ <= head < OFF_CVB:
            kinds.append("q")
        elif OFF_HGQ <= head < OFF_HGG:
            kinds.append("silu")
        else:
            kinds.append("id")
    per = tn // HEAD_DIM
    return tuple(tuple(kinds[t * per:(t + 1) * per]) for t in range(n_cols // tn))


def _in_proj_kernel(a_ref, w_ref, qw_ref, kw_ref, o_ref, wb_ref, *, kinds):
    j = pl.program_id(0)

    @pl.when(pl.program_id(1) == 0)
    def _():
        wb_ref[...] = w_ref[...].astype(BF16)

    def transform(y, kind):
        if kind == "k":
            return _rms(y, kw_ref[...])
        if kind == "q":
            return _rms(y, qw_ref[...]) * ATTN_SCALE
        if kind == "silu":
            return _silu(y)
        return y

    per_slab = MXU_N // HEAD_DIM
    for pattern in sorted(set(kinds)):
        hit = functools.reduce(jnp.logical_or, [j == jj for jj, tk in enumerate(kinds) if tk == pattern])

        @pl.when(hit)
        def _(pattern=pattern):
            a = a_ref[...]
            for c in range(len(pattern) // per_slab):
                cols = slice(c * MXU_N, (c + 1) * MXU_N)
                acc = _dot(a, wb_ref[:, cols])
                parts = [transform(acc[:, s * HEAD_DIM:(s + 1) * HEAD_DIM], pattern[c * per_slab + s])
                         for s in range(per_slab)]
                o_ref[:, cols] = jnp.concatenate(parts, axis=1).astype(o_ref.dtype)


def _in_proj(h, w_in, layer, q_norm, k_norm, n_cols, tm, tn):
    m, k = h.shape
    kinds = _in_proj_kinds(n_cols, tn)
    vec = pl.BlockSpec((1, HEAD_DIM), lambda j, i: (0, 0))
    return pl.pallas_call(
        functools.partial(_in_proj_kernel, kinds=kinds),
        grid=(n_cols // tn, m // tm),
        in_specs=[pl.BlockSpec((tm, k), lambda j, i: (i, 0)),
                  pl.BlockSpec((None, k, tn), lambda j, i: (layer, 0, j)),
                  vec, vec],
        out_specs=pl.BlockSpec((tm, tn), lambda j, i: (i, j)),
        out_shape=jax.ShapeDtypeStruct((m, n_cols), BF16),
        scratch_shapes=[pltpu.VMEM((k, tn), BF16)],
        compiler_params=_params("parallel", "arbitrary"),
    )(h, w_in, q_norm.reshape(1, HEAD_DIM), k_norm.reshape(1, HEAD_DIM))


def _mm_res_kernel(*refs, splits):
    n_a = len(splits)
    a_refs = refs[:n_a]
    w_ref, x_ref, g_ref, o_ref, wb_ref = refs[n_a:]

    @pl.when(pl.program_id(1) == 0)
    def _():
        wb_ref[...] = w_ref[...].astype(BF16)

    acc = None
    off = 0
    for a_ref, width in zip(a_refs, splits):
        part = _dot(a_ref[...], wb_ref[off:off + width, :])
        acc = part if acc is None else acc + part
        off += width
    o_ref[...] = x_ref[...] + g_ref[0] * acc


def _matmul_residual(a_list, w, layer, x, gate, mod_of_tile, tm, tn):
    m = x.shape[0]
    _, k, n = w.shape
    splits = tuple(a.shape[1] for a in a_list)
    assert sum(splits) == k
    in_specs = [pl.BlockSpec((tm, s), lambda j, i: (i, 0)) for s in splits]
    in_specs += [pl.BlockSpec((None, k, tn), lambda j, i: (layer, 0, j), pipeline_mode=pl.Buffered(1)),
                 pl.BlockSpec((tm, tn), lambda j, i: (i, j)),
                 pl.BlockSpec((1, 1, tn), lambda j, i: (mod_of_tile(i), 0, j))]
    return pl.pallas_call(
        functools.partial(_mm_res_kernel, splits=splits),
        grid=(n // tn, m // tm),
        in_specs=in_specs,
        out_specs=pl.BlockSpec((tm, tn), lambda j, i: (i, j)),
        out_shape=jax.ShapeDtypeStruct((m, n), F32),
        scratch_shapes=[pltpu.VMEM((k, tn), BF16)],
        compiler_params=_params("parallel", "arbitrary"),
    )(*a_list, w, x, gate)


def _shift_rows(x, prev_row, next_row):
    r = x.shape[0]
    idx = lax.broadcasted_iota(jnp.int32, x.shape, 0)
    down = jnp.where(idx == 0, prev_row, pltpu.roll(x, 1, axis=0))
    up = jnp.where(idx == r - 1, next_row, pltpu.roll(x, r - 1, axis=0))
    return down, up


def _halo_specs(tr, tc, col_of, tiles_per_seq, n_rows):
    per = tr // HALO
    last_blk = n_rows // HALO - 1
    cur = pl.BlockSpec((tr, tc), lambda i, j: (i, col_of(j)))
    prev = pl.BlockSpec((HALO, tc), lambda i, j: (jnp.maximum(i * per - 1, 0), col_of(j)))
    nxt = pl.BlockSpec((HALO, tc), lambda i, j: (jnp.minimum((i + 1) * per, last_blk), col_of(j)))
    return cur, prev, nxt


def _seq_edges(tiles_per_seq):
    i = pl.program_id(0)
    pos = i % tiles_per_seq
    return (pos != 0).astype(F32), (pos != tiles_per_seq - 1).astype(F32)


def _ffn_up_kernel(h_ref, hp_ref, hn_ref, wg_ref, wv_ref, cg_ref, cv_ref, bg_ref, bv_ref, o_ref, lhs_ref,
                   *, tm, seq_len):
    tiles_per_seq = max(seq_len // tm, 1)
    pos = pl.program_id(0) % tiles_per_seq
    has_prev = pos != 0
    has_next = pos != tiles_per_seq - 1

    @pl.when(pl.program_id(1) == 0)
    def _():
        lhs_ref[0:HALO, :] = jnp.where(has_prev, hp_ref[...], jnp.zeros_like(hp_ref))
        lhs_ref[HALO:HALO + tm, :] = h_ref[...]
        lhs_ref[HALO + tm:, :] = jnp.where(has_next, hn_ref[...], jnp.zeros_like(hn_ref))

    lhs = lhs_ref[...]

    def branch(w_ref, c_ref, b_ref):
        acc = _dot(lhs, w_ref[...])
        r = acc.shape[0]
        down, up = pltpu.roll(acc, 1, axis=0), pltpu.roll(acc, r - 1, axis=0)
        if seq_len < tm:
            tok = (lax.broadcasted_iota(jnp.int32, acc.shape, 0) + (seq_len - HALO)) % seq_len
            down = jnp.where(tok == 0, 0.0, down)
            up = jnp.where(tok == seq_len - 1, 0.0, up)
        y = down * c_ref[0:1, :] + acc * c_ref[1:2, :] + up * c_ref[2:3, :]
        return y[HALO:HALO + tm, :] + b_ref[...]

    gate = branch(wg_ref, cg_ref, bg_ref)
    val = branch(wv_ref, cv_ref, bv_ref)
    o_ref[...] = (_silu(gate) * val).astype(o_ref.dtype)


def _ffn_up_conv_gate(h, w_up, layer, cw, cb, seq_len, tm, tn):
    rows, k = h.shape
    assert seq_len % tm == 0 or (tm % seq_len == 0 and seq_len >= HALO)
    nj = D_FF // tn
    per = tm // HALO
    last_blk = rows // HALO - 1
    cb2 = cb.reshape(1, 2 * D_FF)
    in_specs = [pl.BlockSpec((tm, k), lambda i, j: (i, 0)),
                pl.BlockSpec((HALO, k), lambda i, j: (jnp.maximum(i * per - 1, 0), 0)),
                pl.BlockSpec((HALO, k), lambda i, j: (jnp.minimum((i + 1) * per, last_blk), 0)),
                pl.BlockSpec((None, k, tn), lambda i, j: (layer, 0, j)),
                pl.BlockSpec((None, k, tn), lambda i, j: (layer, 0, j + nj)),
                pl.BlockSpec((CONV_W, tn), lambda i, j: (0, j)),
                pl.BlockSpec((CONV_W, tn), lambda i, j: (0, j + nj)),
                pl.BlockSpec((1, tn), lambda i, j: (0, j)),
                pl.BlockSpec((1, tn), lambda i, j: (0, j + nj))]
    return pl.pallas_call(
        functools.partial(_ffn_up_kernel, tm=tm, seq_len=seq_len),
        grid=(rows // tm, nj),
        in_specs=in_specs,
        out_specs=pl.BlockSpec((tm, tn), lambda i, j: (i, j)),
        out_shape=jax.ShapeDtypeStruct((rows, D_FF), BF16),
        scratch_shapes=[pltpu.VMEM((tm + 2 * HALO, k), BF16)],
        compiler_params=_params("parallel", "arbitrary"),
    )(h, h, h, w_up, w_up, cw, cw, cb2, cb2)


def _short_conv_kernel(b_ref, c_ref, cp_ref, cn_ref, v_ref, vp_ref, vn_ref, w_ref, nw_ref, o_ref, *, tiles_per_seq):
    has_prev, has_next = _seq_edges(tiles_per_seq)
    p = c_ref[...].astype(F32) * v_ref[...].astype(F32)
    prev_row = cp_ref[HALO - 1:HALO, :].astype(F32) * vp_ref[HALO - 1:HALO, :].astype(F32) * has_prev
    next_row = cn_ref[0:1, :].astype(F32) * vn_ref[0:1, :].astype(F32) * has_next
    down, up = _shift_rows(p, prev_row, next_row)
    y = b_ref[...].astype(F32) * (down * w_ref[0:1, :] + p * w_ref[1:2, :] + up * w_ref[2:3, :])
    for h in range(N_CV):
        sl = slice(h * HEAD_DIM, (h + 1) * HEAD_DIM)
        o_ref[:, sl] = _rms(y[:, sl], nw_ref[:, sl]).astype(o_ref.dtype)


def _short_conv(u, cv_w, cv_onorm, seq_len, tr):
    rows = u.shape[0]
    tiles_per_seq = seq_len // tr
    blk = lambda off: (lambda j: off * HEAD_DIM // CV_W)
    b_spec = pl.BlockSpec((tr, CV_W), lambda i, j: (i, OFF_CVB * HEAD_DIM // CV_W))
    c_specs = _halo_specs(tr, CV_W, blk(OFF_CVC), tiles_per_seq, rows)
    v_specs = _halo_specs(tr, CV_W, blk(OFF_CVV), tiles_per_seq, rows)
    return pl.pallas_call(
        functools.partial(_short_conv_kernel, tiles_per_seq=tiles_per_seq),
        grid=(rows // tr, 1),
        in_specs=[b_spec, *c_specs, *v_specs,
                  pl.BlockSpec((CONV_W, CV_W), lambda i, j: (0, 0)),
                  pl.BlockSpec((1, CV_W), lambda i, j: (0, 0))],
        out_specs=pl.BlockSpec((tr, CV_W), lambda i, j: (i, 0)),
        out_shape=jax.ShapeDtypeStruct((rows, CV_W), BF16),
        compiler_params=_params("parallel", "arbitrary"),
    )(u, u, u, u, u, u, u, cv_w, cv_onorm.reshape(1, CV_W))


def _hgrn_structure(chunk, reverse):
    idx = np.arange(chunk)
    i, t = idx[:, None], idx[None, :]
    mats = [t <= i, t > i]
    masks = []
    s = chunk // 2
    while s >= 1:
        blk = idx // (2 * s)
        upper = (idx % (2 * s)) >= s
        mid = blk * 2 * s + s - 1
        a = np.where(upper[:, None], (t > mid[:, None]) & (t <= i), (t > i) & (t <= mid[:, None]))
        mats.append(a)
        masks.append((blk[:, None] == blk[None, :]) & upper[:, None] & ~upper[None, :])
        s //= 2
    mats = np.stack(mats).astype(np.float32)
    masks = np.stack(masks).astype(np.float32)
    if reverse:
        mats = mats[:, ::-1, ::-1]
        masks = masks[:, ::-1, ::-1]
    return mats.reshape(-1, chunk), masks


def _hgrn_kernel(*refs, chunk, n_sub, n_levels, reverse, mode):
    if mode == "final":
        (z_ref, v_ref, q_ref, lb_ref, a_ref, m_ref, s0_ref, g_ref, ofw_ref, nw_ref,
         o_ref, sfin_ref, st_ref) = refs
    elif mode == "raw":
        z_ref, v_ref, q_ref, lb_ref, a_ref, m_ref, s0_ref, o_ref, sfin_ref, st_ref = refs
    else:
        z_ref, v_ref, lb_ref, a_ref, s0_ref, sfin_ref, st_ref = refs
    c = pl.program_id(1)

    @pl.when(c == 0)
    def _():
        st_ref[...] = s0_ref[0]

    lb = lb_ref[...]
    a = a_ref[...]
    tot_row = 0 if reverse else chunk - 1
    subs = range(n_sub - 1, -1, -1) if reverse else range(n_sub)
    for sub in subs:
        rows = slice(sub * chunk, (sub + 1) * chunk)
        z = z_ref[rows, :].astype(F32)
        f = lb + (1.0 - lb) * jax.nn.sigmoid(z)
        log_f = jnp.log(jnp.maximum(f, F_FLOOR))
        k_all = (1.0 - lb) * jax.nn.sigmoid(-z)
        v_all = v_ref[rows, :].astype(F32)
        hi = log_f.astype(BF16)
        lo = (log_f - hi.astype(F32)).astype(BF16)
        expo = _dot(a, hi) + _dot(a, lo)
        for h in range(N_HG):
            sl = slice(h * HEAD_DIM, (h + 1) * HEAD_DIM)
            k, v = k_all[:, sl], v_all[:, sl]
            e_rem = jnp.exp(expo[chunk:2 * chunk, sl])
            e_tot = jnp.exp(expo[tot_row:tot_row + 1, sl])
            st = st_ref[h]
            st_ref[h] = st * e_tot + _dot_tn(v.astype(BF16), (k * e_rem).astype(BF16))
            if mode == "state":
                continue
            q = q_ref[rows, sl].astype(F32)
            e_cum = jnp.exp(expo[0:chunk, sl])
            o = _dot_nt((q * e_cum).astype(BF16), st.astype(BF16))
            att = None
            for lvl in range(n_levels):
                e = jnp.exp(expo[(2 + lvl) * chunk:(3 + lvl) * chunk, sl])
                part = m_ref[lvl] * _dot_nt((q * e).astype(BF16), (k * e).astype(BF16))
                att = part if att is None else att + part
            diag = jnp.sum(q * k, axis=-1, keepdims=True)
            o = o + _dot(att.astype(BF16), v.astype(BF16)) + diag * v
            if mode == "final":
                y = _rms(o + ofw_ref[rows, sl], nw_ref[...]) * _silu(g_ref[rows, sl].astype(F32))
                o_ref[rows, sl] = y.astype(o_ref.dtype)
            else:
                o_ref[rows, sl] = o.astype(o_ref.dtype)

    @pl.when(c == pl.num_programs(1) - 1)
    def _():
        sfin_ref[0] = st_ref[...]


def _hgrn_scan(u, lb, s0, seq_len, *, reverse, mode, o_fw=None, norm_w=None):
    rows = u.shape[0]
    batch = rows // seq_len
    blk = min(HG_BLOCK, seq_len)
    chunk = min(HG_CHUNK, blk)
    n_blocks = seq_len // blk
    mats, masks = _hgrn_structure(chunk, reverse)
    n_levels = masks.shape[0]
    if mode == "state":
        mats = mats[:2 * chunk]
    per_blk = HG_W // HEAD_DIM

    def tok(off):
        col = off // per_blk
        if reverse:
            return pl.BlockSpec((blk, HG_W), lambda b, c: (b * n_blocks + n_blocks - 1 - c, col))
        return pl.BlockSpec((blk, HG_W), lambda b, c: (b * n_blocks + c, col))

    const2 = lambda shape: pl.BlockSpec(shape, lambda b, c: (0, 0))
    state_spec = pl.BlockSpec((1, N_HG, HEAD_DIM, HEAD_DIM), lambda b, c: (b, 0, 0, 0))
    state_shape = jax.ShapeDtypeStruct((batch, N_HG, HEAD_DIM, HEAD_DIM), F32)
    z_spec = tok(OFF_FBW if reverse else OFF_FFW)
    lb2, mats_b = lb.reshape(1, HG_W), jnp.asarray(mats, BF16)
    if mode == "state":
        in_specs = [z_spec, tok(OFF_I), const2((1, HG_W)), const2(mats.shape), state_spec]
        args = [u, u, lb2, mats_b, s0]
        out_specs, out_shape = [state_spec], [state_shape]
    else:
        in_specs = [z_spec, tok(OFF_I), tok(OFF_HGQ), const2((1, HG_W)), const2(mats.shape),
                    pl.BlockSpec(masks.shape, lambda b, c: (0, 0, 0)), state_spec]
        args = [u, u, u, lb2, mats_b, jnp.asarray(masks, F32), s0]
        if mode == "final":
            in_specs += [tok(OFF_HGG), tok(0), const2((1, HEAD_DIM))]
            args += [u, o_fw, norm_w.reshape(1, HEAD_DIM)]
        out_specs = [tok(0), state_spec]
        out_shape = [jax.ShapeDtypeStruct((rows, HG_W), BF16 if mode == "final" else F32), state_shape]
    res = pl.pallas_call(
        functools.partial(_hgrn_kernel, chunk=chunk, n_sub=blk // chunk, n_levels=n_levels, reverse=reverse,
                          mode=mode),
        grid=(batch, n_blocks),
        in_specs=in_specs,
        out_specs=out_specs,
        out_shape=out_shape,
        scratch_shapes=[pltpu.VMEM((N_HG, HEAD_DIM, HEAD_DIM), F32)],
        compiler_params=_params("parallel", "arbitrary"),
    )(*args)
    return (None, res[0]) if mode == "state" else (res[0], res[1])


def _na_bias_table(rpb, rows):
    n_rb = rows // NA_WIN_R
    n_cb = GRID_W // NA_QBLK_C
    n_dr, n_dc = 2 * NA_WIN_R - 1, 2 * NA_WIN_C - 1
    n_k = NA_KEY_R * NA_KEY_C
    offs, row_mask = [], []
    for rb in (0, min(1, n_rb - 1), n_rb - 1):
        k_row0 = int(np.clip(rb * NA_WIN_R - NA_WIN_R // 2, 0, rows - NA_KEY_R))
        q_r = rb * NA_WIN_R + np.arange(NA_WIN_R)
        k_r = k_row0 + np.arange(NA_KEY_R)
        r_start = np.clip(q_r - NA_WIN_R // 2, 0, rows - NA_WIN_R)
        ok_r = (k_r[None, :] >= r_start[:, None]) & (k_r[None, :] < r_start[:, None] + NA_WIN_R)
        row_mask.append(np.repeat(np.where(ok_r, 0.0, NEG_INF), NA_KEY_C, axis=1))
        offs.append(k_row0 - rb * NA_WIN_R + NA_WIN_R - 1)
    row_mask = jnp.asarray(np.stack(row_mask), F32)
    pad_lo = NA_WIN_R - 1 - min(offs)
    n_a = NA_KEY_R - 1 + max(offs) + pad_lo + 1
    lanes = -(-(n_a * NA_KEY_C) // HEAD_DIM) * HEAD_DIM
    sel_c, ok_c = [], []
    for cb in range(n_cb):
        k_col0 = int(np.clip(cb * NA_QBLK_C - NA_WIN_C // 2, 0, GRID_W - NA_KEY_C))
        q_c = cb * NA_QBLK_C + np.arange(NA_QBLK_C)
        k_c = k_col0 + np.arange(NA_KEY_C)
        c_start = np.clip(q_c - NA_WIN_C // 2, 0, GRID_W - NA_WIN_C)
        ok_c.append((k_c[None, :] >= c_start[:, None]) & (k_c[None, :] < c_start[:, None] + NA_WIN_C))
        dc = k_c[None, :] - q_c[:, None] + NA_WIN_C - 1
        sel_c.append(dc[:, :, None] == np.arange(n_dc))
    sel_c = jnp.asarray(np.stack(sel_c), F32)
    ok_c = jnp.asarray(np.stack(ok_c))
    strip = jnp.einsum("hab,mcjb->hmcaj", rpb.astype(F32), sel_c, precision=lax.Precision.HIGHEST)
    strip = jnp.where(ok_c[None, :, :, None, :], strip, NEG_INF)
    strip = jnp.pad(strip, ((0, 0), (0, 0), (0, 0), (pad_lo, n_a - n_dr - pad_lo), (0, 0)))
    strip = strip.reshape(strip.shape[:3] + (n_a * NA_KEY_C,))
    strip = jnp.pad(strip, ((0, 0), (0, 0), (0, 0), (0, lanes - n_a * NA_KEY_C)))
    n_heads = rpb.shape[0]
    return pl.pallas_call(
        functools.partial(_na_bias_kernel, offs=tuple(offs), pad_lo=pad_lo),
        grid=(n_heads,),
        in_specs=[pl.BlockSpec((1, n_cb, NA_QBLK_C, lanes), lambda h: (h, 0, 0, 0)),
                  pl.BlockSpec((3, NA_WIN_R, n_k), lambda h: (0, 0, 0))],
        out_specs=pl.BlockSpec((1, 3, n_cb, NA_WIN_R * NA_QBLK_C, n_k), lambda h: (h, 0, 0, 0, 0)),
        out_shape=jax.ShapeDtypeStruct((n_heads, 3, n_cb, NA_WIN_R * NA_QBLK_C, n_k), F32),
        compiler_params=_params("parallel"),
    )(strip, row_mask)


def _na_bias_kernel(t_ref, rm_ref, o_ref, *, offs, pad_lo):
    n_cb = t_ref.shape[1]
    lanes = t_ref.shape[-1]
    per = HEAD_DIM // NA_KEY_C
    n_k = o_ref.shape[-1]
    for cb in range(n_cb):
        t = t_ref[0, cb]
        shifted = [t] + [pltpu.roll(t, lanes - NA_KEY_C * s, axis=1) for s in range(1, per)]
        for cls, off in enumerate(offs):
            for r in range(NA_WIN_R):
                m = off - r + pad_lo
                base = (m // per) * HEAD_DIM
                tile = shifted[m % per][:, base:base + n_k] + rm_ref[cls, r:r + 1, :]
                o_ref[0, cls, cb, r * NA_QBLK_C:(r + 1) * NA_QBLK_C, :] = tile


def _na_kernel(q_ref, k0_ref, k1_ref, k2_ref, k3_ref, v0_ref, v1_ref, v2_ref, v3_ref, kc_ref, vc_ref, bias_ref,
               ow_ref, o_ref):
    n_cb = GRID_W // NA_QBLK_C
    n_q, n_k = NA_WIN_R * NA_QBLK_C, NA_KEY_R * NA_KEY_C
    q_all = q_ref[...].astype(F32)
    k_all = jnp.concatenate([r[...] for r in (k0_ref, k1_ref, k2_ref, k3_ref)], axis=0).astype(F32)
    v_all = jnp.concatenate([r[...] for r in (v0_ref, v1_ref, v2_ref, v3_ref)], axis=0).astype(F32)
    for hh in range(NA_HEADS):
        hs = slice(hh * HEAD_DIM, (hh + 1) * HEAD_DIM)
        q3 = q_all[:, hs].reshape(NA_WIN_R, GRID_W, HEAD_DIM)
        k3 = k_all[:, hs].reshape(NA_KEY_R, GRID_W, HEAD_DIM)
        v3 = v_all[:, hs].reshape(NA_KEY_R, GRID_W, HEAD_DIM)
        k_ctx = kc_ref[:, hs]
        v_ctx = vc_ref[:, hs]
        for cb in range(n_cb):
            q0 = cb * NA_QBLK_C
            k_col0 = min(max(q0 - NA_WIN_C // 2, 0), GRID_W - NA_KEY_C)
            qm = q3[:, q0:q0 + NA_QBLK_C, :].reshape(n_q, HEAD_DIM).astype(BF16)
            km = k3[:, k_col0:k_col0 + NA_KEY_C, :].reshape(n_k, HEAD_DIM).astype(BF16)
            vm = v3[:, k_col0:k_col0 + NA_KEY_C, :].reshape(n_k, HEAD_DIM).astype(BF16)
            s_loc = _dot_nt(qm, km) + bias_ref[hh, 0, cb]
            s_ctx = _dot_nt(qm, k_ctx)
            mx = jnp.maximum(jnp.max(s_loc, axis=-1, keepdims=True), jnp.max(s_ctx, axis=-1, keepdims=True))
            p_loc = jnp.exp(s_loc - mx)
            p_ctx = jnp.exp(s_ctx - mx)
            den = jnp.sum(p_loc, axis=-1, keepdims=True) + jnp.sum(p_ctx, axis=-1, keepdims=True)
            o = (_dot(p_loc.astype(BF16), vm) + _dot(p_ctx.astype(BF16), v_ctx)) / den
            o = _rms(o, ow_ref[hh]).astype(o_ref.dtype)
            for r in range(NA_WIN_R):
                o_ref[r * GRID_W + q0:r * GRID_W + q0 + NA_QBLK_C, hs] = o[r * NA_QBLK_C:(r + 1) * NA_QBLK_C, :]


def _neighbourhood_attention(u, u_ctx, bias, head0, out_norm, batch, rows, ctx_len):
    assert rows % NA_WIN_R == 0 and rows >= NA_KEY_R
    n_rb = rows // NA_WIN_R
    q_blk = NA_WIN_R * GRID_W
    k_blk = NA_KEY_R * GRID_W // 4
    k_per_batch = rows * GRID_W // k_blk

    width = NA_HEADS * HEAD_DIM
    assert all(off % NA_HEADS == 0 for off in (OFF_NAQ, OFF_NAK, OFF_NAV, head0, N_NA))
    col_q, col_k, col_v = OFF_NAQ // NA_HEADS, OFF_NAK // NA_HEADS, OFF_NAV // NA_HEADS

    def key_spec(col, part):
        def index(h, rb, b):
            first = jnp.clip(rb * NA_WIN_R - NA_WIN_R // 2, 0, rows - NA_KEY_R) * GRID_W // k_blk
            return (b * k_per_batch + first + part, col + h)
        return pl.BlockSpec((k_blk, width), index)

    def row_class(rb):
        return jnp.where(rb == 0, 0, jnp.where(rb == n_rb - 1, 2, 1))

    n_cb = GRID_W // NA_QBLK_C
    in_specs = [pl.BlockSpec((q_blk, width), lambda h, rb, b: (b * n_rb + rb, col_q + h))]
    in_specs += [key_spec(col_k, p) for p in range(4)] + [key_spec(col_v, p) for p in range(4)]
    in_specs += [pl.BlockSpec((ctx_len, width), lambda h, rb, b: (b, col_k + h)),
                 pl.BlockSpec((ctx_len, width), lambda h, rb, b: (b, col_v + h)),
                 pl.BlockSpec((NA_HEADS, 1, n_cb, NA_WIN_R * NA_QBLK_C, NA_KEY_R * NA_KEY_C),
                              lambda h, rb, b: (head0 // NA_HEADS + h, row_class(rb), 0, 0, 0)),
                 pl.BlockSpec((NA_HEADS, 1, HEAD_DIM), lambda h, rb, b: (h, 0, 0))]
    return pl.pallas_call(
        _na_kernel,
        grid=(N_NA // NA_HEADS, n_rb, batch),
        in_specs=in_specs,
        out_specs=pl.BlockSpec((q_blk, width), lambda h, rb, b: (b * n_rb + rb, h)),
        out_shape=jax.ShapeDtypeStruct((u.shape[0], NA_W), BF16),
        compiler_params=_params("parallel", "parallel", "parallel"),
    )(u, u, u, u, u, u, u, u, u, u_ctx, u_ctx, bias, out_norm.reshape(N_NA, 1, HEAD_DIM))


def _ctx_attn_kernel(q_ref, k_ref, v_ref, ow_ref, o_ref):
    s = _dot_nt(q_ref[...].astype(BF16), k_ref[...].astype(BF16))
    p = jnp.exp(s - jnp.max(s, axis=-1, keepdims=True))
    o = _dot(p.astype(BF16), v_ref[...].astype(BF16)) / jnp.sum(p, axis=-1, keepdims=True)
    o_ref[...] = _rms(o, ow_ref[0]).astype(o_ref.dtype)


def _context_attention(u_ctx, out_norm, batch, ctx_len):
    tok = lambda off: pl.BlockSpec((ctx_len, HEAD_DIM), lambda b, h: (b, off + h))
    return pl.pallas_call(
        _ctx_attn_kernel,
        grid=(batch, N_NA),
        in_specs=[tok(OFF_NAQ), tok(OFF_NAK), tok(OFF_NAV),
                  pl.BlockSpec((1, 1, HEAD_DIM), lambda b, h: (h, 0, 0))],
        out_specs=tok(0),
        out_shape=jax.ShapeDtypeStruct((u_ctx.shape[0], NA_W), BF16),
        compiler_params=_params("parallel", "parallel"),
    )(u_ctx, u_ctx, u_ctx, out_norm.reshape(N_NA, 1, HEAD_DIM))


def _row_tile(rows, target):
    t = min(rows, target)
    assert rows % t == 0
    return t


def _token_stream(xs, seq_len, mod_row, p, layer, ada_l, mixers):
    rows = xs.shape[0]
    sh2, sc2, g1, g2 = ada_l["sh2"], ada_l["sc2"], ada_l["g1"], ada_l["g2"]
    tm = _row_tile(rows, 1024)
    tiles_per_seq = max(seq_len // tm, 1)
    if mod_row is None:
        mod_of_tile = lambda i: 1 + i // tiles_per_seq
    else:
        mod_of_tile = lambda i: mod_row
    x1 = _matmul_residual(mixers, p["w_out"], layer, xs, g1, mod_of_tile, tm, 1024)
    h2 = _norm_mod(x1, p["ln2"][layer], sh2, sc2, mod_of_tile, tm)
    act = _ffn_up_conv_gate(h2, p["w_up"], layer, p["f_cw"][layer], p["f_cb"][layer], seq_len,
                            tm, 512)
    return _matmul_residual([act], p["w_down"], layer, x1, g2, mod_of_tile, tm, 512)


def kernel(x, c, ctx, c_ctx, w_ada, b_ada, ln1_w, ln2_w, w_in, hg_lb_logits, hg_norm_w, na_q_norm_w, na_k_norm_w,
           na_rpb, na_out_norm_w, cv_w, cv_out_norm_w, w_out, w_up, ffn_conv_w, ffn_conv_b, w_down):
    batch, seq, d = x.shape
    ctx_len = ctx.shape[1]
    rows = seq // GRID_W
    depth = w_ada.shape[0]
    xs = x.reshape(batch * seq, d)
    cs = ctx.reshape(batch * ctx_len, d)

    lb_sm = jax.nn.softmax(hg_lb_logits.astype(F32), axis=1)
    lb_all = jnp.cumsum(lb_sm, axis=1) - lb_sm[:, :1]

    cond = jnp.zeros((SUBLANE, d), F32).at[0].set(c_ctx).at[1:1 + batch].set(c)
    ada = _ada_table(cond, w_ada, b_ada).reshape(depth, SUBLANE, 6, 1, d)

    s_zero = jnp.zeros((batch, N_HG, HEAD_DIM, HEAD_DIM), F32)
    p = {"w_out": w_out, "w_up": w_up.astype(BF16), "w_down": w_down, "ln2": ln2_w, "f_cw": ffn_conv_w,
         "f_cb": ffn_conv_b}
    ctx_cols = sum(IN_SPLITS[:5])
    bias = _na_bias_table(na_rpb.reshape((depth * N_NA,) + na_rpb.shape[2:]), rows)
    for l in range(depth):
        last = l == depth - 1
        names = ("sh1", "sc1", "g1", "sh2", "sc2", "g2")
        ada_l = {n: ada[l, :, i] for i, n in enumerate(names)}
        tm = _row_tile(xs.shape[0], 1024)
        tm_c = _row_tile(cs.shape[0], 1024)
        tiles_per_seq = seq // tm
        lat_mod = lambda i: 1 + i // tiles_per_seq
        ctx_mod = lambda i: 0
        h = _norm_mod(xs, ln1_w[l], ada_l["sh1"], ada_l["sc1"], lat_mod, tm)
        hc = _norm_mod(cs, ln1_w[l], ada_l["sh1"], ada_l["sc1"], ctx_mod, tm_c)
        u = _in_proj(h, w_in, l, na_q_norm_w[l], na_k_norm_w[l], IN_W, tm, 1024)
        uc = _in_proj(hc, w_in, l, na_q_norm_w[l], na_k_norm_w[l], ctx_cols if last else IN_W, tm_c, 512)

        if last:
            _, s_fw = _hgrn_scan(uc, lb_all[0, l], s_zero, ctx_len, reverse=False, mode="state")
            _, s_bw = _hgrn_scan(uc, lb_all[1, l], s_zero, ctx_len, reverse=True, mode="state")
        else:
            co_fw, s_fw = _hgrn_scan(uc, lb_all[0, l], s_zero, ctx_len, reverse=False, mode="raw")
            hg_c, s_bw = _hgrn_scan(uc, lb_all[1, l], s_zero, ctx_len, reverse=True, mode="final", o_fw=co_fw,
                                    norm_w=hg_norm_w[l])
        o_fw, _ = _hgrn_scan(u, lb_all[0, l], s_fw, seq, reverse=False, mode="raw")
        hg_out, _ = _hgrn_scan(u, lb_all[1, l], s_bw, seq, reverse=True, mode="final", o_fw=o_fw,
                               norm_w=hg_norm_w[l])

        na_out = _neighbourhood_attention(u, uc, bias, l * N_NA, na_out_norm_w[l], batch, rows, ctx_len)
        cv_out = _short_conv(u, cv_w[l], cv_out_norm_w[l], seq, _row_tile(seq, 256))

        xs = _token_stream(xs, seq, None, p, l, ada_l, [hg_out, na_out, cv_out])
        if not last:
            na_c = _context_attention(uc, na_out_norm_w[l], batch, ctx_len)
            cv_c = _short_conv(uc, cv_w[l], cv_out_norm_w[l], ctx_len, _row_tile(ctx_len, 256))
            cs = _token_stream(cs, ctx_len, 0, p, l, ada_l, [hg_c, na_c, cv_c])
    return xs.reshape(batch, seq, d)
```

```python
import functools

import numpy as np
import jax
import jax.numpy as jnp
from jax import lax
from jax.experimental import pallas as pl
from jax.experimental.pallas import tpu as pltpu

F32 = jnp.float32
BF16 = jnp.bfloat16

D_MODEL = 2048
DEPTH = 2
GRID_W = 64
HEAD_DIM = 128
N_HG = 4
N_NA = 8
N_CV = 4
HG_W = N_HG * HEAD_DIM
NA_W = N_NA * HEAD_DIM
CV_W = N_CV * HEAD_DIM
IN_SPLITS = (HG_W, HG_W, HG_W, NA_W, NA_W, HG_W, HG_W, NA_W, CV_W, CV_W, CV_W)
IN_W = sum(IN_SPLITS)
(OFF_FFW, OFF_FBW, OFF_I, OFF_NAK, OFF_NAV, OFF_HGQ, OFF_HGG, OFF_NAQ, OFF_CVB, OFF_CVC,
 OFF_CVV) = [int(v) // HEAD_DIM for v in np.cumsum((0,) + IN_SPLITS[:-1])]
NA_WIN_R = 8
NA_WIN_C = 16
NA_QBLK_C = 16
NA_KEY_C = 32
NA_KEY_R = 16
NA_HEADS = 2
CONV_W = 3
D_FF = 5632
EPS = 1e-6
F_FLOOR = 1e-30
ATTN_SCALE = HEAD_DIM ** -0.5
NEG_INF = -1e30

HG_CHUNK = 128
HG_BLOCK = 512
SUBLANE = 8
HALO = 16
MXU_N = 256
VMEM_LIMIT = 56 * 1024 * 1024


def _params(*sem):
    return pltpu.CompilerParams(dimension_semantics=sem, vmem_limit_bytes=VMEM_LIMIT)


def _dot(a, b):
    return jnp.dot(a, b, preferred_element_type=F32)


def _dot_nt(a, b):
    return lax.dot_general(a, b, (((1,), (1,)), ((), ())), preferred_element_type=F32)


def _dot_tn(a, b):
    return lax.dot_general(a, b, (((0,), (0,)), ((), ())), preferred_element_type=F32)


def _rms(x, w):
    return x * lax.rsqrt(jnp.mean(x * x, axis=-1, keepdims=True) + EPS) * w


def _silu(x):
    half = 0.5 * x
    return half + half * jnp.tanh(half)


def _ada_kernel(s_ref, w_ref, b_ref, o_ref):
    s = _silu(s_ref[...]).astype(BF16)
    o_ref[0] = _dot(s, w_ref[0].astype(BF16)) + b_ref[0]


def _ada_table(cond, w_ada, b_ada):
    depth, d, n = w_ada.shape
    tn = 1024
    return pl.pallas_call(
        _ada_kernel,
        grid=(depth, n // tn),
        in_specs=[pl.BlockSpec((SUBLANE, d), lambda l, j: (0, 0)),
                  pl.BlockSpec((1, d, tn), lambda l, j: (l, 0, j)),
                  pl.BlockSpec((1, 1, tn), lambda l, j: (l, 0, j))],
        out_specs=pl.BlockSpec((1, SUBLANE, tn), lambda l, j: (l, 0, j)),
        out_shape=jax.ShapeDtypeStruct((depth, SUBLANE, n), F32),
        compiler_params=_params("parallel", "parallel"),
    )(cond, w_ada, b_ada.reshape(depth, 1, n))


def _norm_mod_kernel(x_ref, w_ref, sh_ref, sc_ref, o_ref):
    y = _rms(x_ref[...], w_ref[...])
    o_ref[...] = (y * (1.0 + sc_ref[0]) + sh_ref[0]).astype(o_ref.dtype)


def _norm_mod(x, w, shift, scale, mod_of_tile, tr):
    rows, d = x.shape
    mod = lambda i: (mod_of_tile(i), 0, 0)
    return pl.pallas_call(
        _norm_mod_kernel,
        grid=(rows // tr,),
        in_specs=[pl.BlockSpec((tr, d), lambda i: (i, 0)),
                  pl.BlockSpec((1, d), lambda i: (0, 0)),
                  pl.BlockSpec((1, 1, d), mod),
                  pl.BlockSpec((1, 1, d), mod)],
        out_specs=pl.BlockSpec((tr, d), lambda i: (i, 0)),
        out_shape=jax.ShapeDtypeStruct((rows, d), BF16),
        compiler_params=_params("parallel"),
    )(x, w.reshape(1, d), shift, scale)


def _in_proj_kinds(n_cols, tn):
    kinds = []
    for head in range(n_cols // HEAD_DIM):
        if OFF_NAK <= head < OFF_NAV:
            kinds.append("k")
        elif OFF_NAQ <= head < OFF_CVB:
            kinds.append("q")
        elif OFF_HGQ <= head < OFF_HGG:
            kinds.append("silu")
        else:
            kinds.append("id")
    per = tn // HEAD_DIM
    return tuple(tuple(kinds[t * per:(t + 1) * per]) for t in range(n_cols // tn))


def _in_proj_kernel(*refs, kinds, with_cast):
    if with_cast:
        a_ref, w_ref, qw_ref, kw_ref, side_ref, o_ref, side_o_ref, wb_ref = refs
        side_o_ref[...] = side_ref[...].astype(BF16)
    else:
        a_ref, w_ref, qw_ref, kw_ref, o_ref, wb_ref = refs
    j = pl.program_id(0)

    @pl.when(pl.program_id(1) == 0)
    def _():
        wb_ref[...] = w_ref[...].astype(BF16)

    def transform(y, kind):
        if kind == "k":
            return _rms(y, kw_ref[...])
        if kind == "q":
            return _rms(y, qw_ref[...]) * ATTN_SCALE
        if kind == "silu":
            return _silu(y)
        return y

    per_slab = MXU_N // HEAD_DIM
    for pattern in sorted(set(kinds)):
        hit = functools.reduce(jnp.logical_or, [j == jj for jj, tk in enumerate(kinds) if tk == pattern])

        @pl.when(hit)
        def _(pattern=pattern):
            a = a_ref[...]
            for c in range(len(pattern) // per_slab):
                cols = slice(c * MXU_N, (c + 1) * MXU_N)
                acc = _dot(a, wb_ref[:, cols])
                parts = [transform(acc[:, s * HEAD_DIM:(s + 1) * HEAD_DIM], pattern[c * per_slab + s])
                         for s in range(per_slab)]
                o_ref[:, cols] = jnp.concatenate(parts, axis=1).astype(o_ref.dtype)


def _in_proj(h, w_in, layer, q_norm, k_norm, n_cols, tm, tn, side_w=None):
    m, k = h.shape
    kinds = _in_proj_kinds(n_cols, tn)
    nj, ni = n_cols // tn, m // tm
    vec = pl.BlockSpec((1, HEAD_DIM), lambda j, i: (0, 0))
    in_specs = [pl.BlockSpec((tm, k), lambda j, i: (i, 0)),
                pl.BlockSpec((None, k, tn), lambda j, i: (layer, 0, j)),
                vec, vec]
    args = [h, w_in, q_norm.reshape(1, HEAD_DIM), k_norm.reshape(1, HEAD_DIM)]
    out_specs = [pl.BlockSpec((tm, tn), lambda j, i: (i, j))]
    out_shape = [jax.ShapeDtypeStruct((m, n_cols), BF16)]
    if side_w is not None:
        _, k2, n2 = side_w.shape
        strips = n2 // HEAD_DIM
        assert strips <= nj * ni
        strip = lambda j, i: jnp.minimum(j * ni + i, strips - 1)
        in_specs.append(pl.BlockSpec((None, k2, HEAD_DIM), lambda j, i: (layer, 0, strip(j, i))))
        args.append(side_w)
        out_specs.append(pl.BlockSpec((k2, HEAD_DIM), lambda j, i: (0, strip(j, i))))
        out_shape.append(jax.ShapeDtypeStruct((k2, n2), BF16))
    res = pl.pallas_call(
        functools.partial(_in_proj_kernel, kinds=kinds, with_cast=side_w is not None),
        grid=(nj, ni),
        in_specs=in_specs,
        out_specs=out_specs,
        out_shape=out_shape,
        scratch_shapes=[pltpu.VMEM((k, tn), BF16)],
        compiler_params=_params("arbitrary", "arbitrary"),
    )(*args)
    return res if side_w is not None else res[0]


def _mm_res_kernel(*refs, splits):
    n_a = len(splits)
    a_refs = refs[:n_a]
    w_ref, x_ref, g_ref, o_ref, wb_ref = refs[n_a:]

    @pl.when(pl.program_id(1) == 0)
    def _():
        wb_ref[...] = w_ref[...].astype(BF16)

    acc = None
    off = 0
    for a_ref, width in zip(a_refs, splits):
        part = _dot(a_ref[...], wb_ref[off:off + width, :])
        acc = part if acc is None else acc + part
        off += width
    o_ref[...] = x_ref[...] + g_ref[0] * acc


def _matmul_residual(a_list, w, layer, x, gate, mod_of_tile, tm, tn):
    m = x.shape[0]
    _, k, n = w.shape
    splits = tuple(a.shape[1] for a in a_list)
    assert sum(splits) == k
    in_specs = [pl.BlockSpec((tm, s), lambda j, i: (i, 0)) for s in splits]
    in_specs += [pl.BlockSpec((None, k, tn), lambda j, i: (layer, 0, j), pipeline_mode=pl.Buffered(1)),
                 pl.BlockSpec((tm, tn), lambda j, i: (i, j)),
                 pl.BlockSpec((1, 1, tn), lambda j, i: (mod_of_tile(i), 0, j))]
    return pl.pallas_call(
        functools.partial(_mm_res_kernel, splits=splits),
        grid=(n // tn, m // tm),
        in_specs=in_specs,
        out_specs=pl.BlockSpec((tm, tn), lambda j, i: (i, j)),
        out_shape=jax.ShapeDtypeStruct((m, n), F32),
        scratch_shapes=[pltpu.VMEM((k, tn), BF16)],
        compiler_params=_params("parallel", "arbitrary"),
    )(*a_list, w, x, gate)


def _out_proj_norm_kernel(*refs, splits):
    n_a = len(splits)
    a_refs = refs[:n_a]
    w_ref, x_ref, g_ref, lw_ref, sc_ref, sh_ref, x1_ref, h2_ref, wb_ref = refs[n_a:]

    @pl.when(pl.program_id(0) == 0)
    def _():
        wb_ref[...] = w_ref[...].astype(BF16)

    tm, n = x1_ref.shape
    ssq = jnp.zeros((tm, HEAD_DIM), F32)
    for c in range(n // MXU_N):
        cols = slice(c * MXU_N, (c + 1) * MXU_N)
        acc = None
        off = 0
        for a_ref, width in zip(a_refs, splits):
            part = _dot(a_ref[...], wb_ref[off:off + width, cols])
            acc = part if acc is None else acc + part
            off += width
        x1 = x_ref[:, cols] + g_ref[0][:, cols] * acc
        x1_ref[:, cols] = x1
        sq = x1 * x1
        for s in range(MXU_N // HEAD_DIM):
            ssq = ssq + sq[:, s * HEAD_DIM:(s + 1) * HEAD_DIM]
    inv = lax.rsqrt(jnp.sum(ssq, axis=-1, keepdims=True) * (1.0 / n) + EPS)
    gain = lw_ref[...] * (1.0 + sc_ref[0])
    for c in range(n // MXU_N):
        cols = slice(c * MXU_N, (c + 1) * MXU_N)
        h2_ref[:, cols] = (x1_ref[:, cols] * inv * gain[:, cols] + sh_ref[0][:, cols]).astype(h2_ref.dtype)


def _out_proj_norm(a_list, w, layer, x, gate, ln_w, scale, shift, mod_of_tile, tm):
    m, n = x.shape
    _, k, _ = w.shape
    splits = tuple(a.shape[1] for a in a_list)
    assert sum(splits) == k
    mod = lambda i: (mod_of_tile(i), 0, 0)
    in_specs = [pl.BlockSpec((tm, s), lambda i: (i, 0)) for s in splits]
    in_specs += [pl.BlockSpec((None, k, n), lambda i: (layer, 0, 0), pipeline_mode=pl.Buffered(1)),
                 pl.BlockSpec((tm, n), lambda i: (i, 0)),
                 pl.BlockSpec((1, 1, n), mod),
                 pl.BlockSpec((1, n), lambda i: (0, 0)),
                 pl.BlockSpec((1, 1, n), mod),
                 pl.BlockSpec((1, 1, n), mod)]
    return pl.pallas_call(
        functools.partial(_out_proj_norm_kernel, splits=splits),
        grid=(m // tm,),
        in_specs=in_specs,
        out_specs=[pl.BlockSpec((tm, n), lambda i: (i, 0)), pl.BlockSpec((tm, n), lambda i: (i, 0))],
        out_shape=[jax.ShapeDtypeStruct((m, n), F32), jax.ShapeDtypeStruct((m, n), BF16)],
        scratch_shapes=[pltpu.VMEM((k, n), BF16)],
        compiler_params=_params("arbitrary"),
    )(*a_list, w, x, gate, ln_w.reshape(1, n), scale, shift)


def _shift_rows(x, prev_row, next_row):
    r = x.shape[0]
    idx = lax.broadcasted_iota(jnp.int32, x.shape, 0)
    down = jnp.where(idx == 0, prev_row, pltpu.roll(x, 1, axis=0))
    up = jnp.where(idx == r - 1, next_row, pltpu.roll(x, r - 1, axis=0))
    return down, up


def _halo_specs(tr, tc, col_of, tiles_per_seq, n_rows):
    per = tr // HALO
    last_blk = n_rows // HALO - 1
    cur = pl.BlockSpec((tr, tc), lambda i, j: (i, col_of(j)))
    prev = pl.BlockSpec((HALO, tc), lambda i, j: (jnp.maximum(i * per - 1, 0), col_of(j)))
    nxt = pl.BlockSpec((HALO, tc), lambda i, j: (jnp.minimum((i + 1) * per, last_blk), col_of(j)))
    return cur, prev, nxt


def _seq_edges(tiles_per_seq):
    i = pl.program_id(0)
    pos = i % tiles_per_seq
    return (pos != 0).astype(F32), (pos != tiles_per_seq - 1).astype(F32)


def _ffn_up_kernel(h_ref, hp_ref, hn_ref, wg_ref, wv_ref, cg_ref, cv_ref, bg_ref, bv_ref, o_ref, lhs_ref,
                   *, tm, seq_len):
    tiles_per_seq = max(seq_len // tm, 1)
    pos = pl.program_id(0) % tiles_per_seq
    has_prev = pos != 0
    has_next = pos != tiles_per_seq - 1

    @pl.when(pl.program_id(1) == 0)
    def _():
        lhs_ref[0:HALO, :] = jnp.where(has_prev, hp_ref[...], jnp.zeros_like(hp_ref))
        lhs_ref[HALO:HALO + tm, :] = h_ref[...]
        lhs_ref[HALO + tm:, :] = jnp.where(has_next, hn_ref[...], jnp.zeros_like(hn_ref))

    lhs = lhs_ref[...]

    def branch(w_ref, c_ref, b_ref):
        acc = _dot(lhs, w_ref[...])
        r = acc.shape[0]
        down, up = pltpu.roll(acc, 1, axis=0), pltpu.roll(acc, r - 1, axis=0)
        if seq_len < tm:
            tok = (lax.broadcasted_iota(jnp.int32, acc.shape, 0) + (seq_len - HALO)) % seq_len
            down = jnp.where(tok == 0, 0.0, down)
            up = jnp.where(tok == seq_len - 1, 0.0, up)
        y = down * c_ref[0:1, :] + acc * c_ref[1:2, :] + up * c_ref[2:3, :]
        return y[HALO:HALO + tm, :] + b_ref[...]

    gate = branch(wg_ref, cg_ref, bg_ref)
    val = branch(wv_ref, cv_ref, bv_ref)
    o_ref[...] = (_silu(gate) * val).astype(o_ref.dtype)


def _ffn_up_conv_gate(h, w_up, cw, cb, seq_len, tm, tn):
    rows, k = h.shape
    assert seq_len % tm == 0 or (tm % seq_len == 0 and seq_len >= HALO)
    nj = D_FF // tn
    per = tm // HALO
    last_blk = rows // HALO - 1
    cb2 = cb.reshape(1, 2 * D_FF)
    in_specs = [pl.BlockSpec((tm, k), lambda i, j: (i, 0)),
                pl.BlockSpec((HALO, k), lambda i, j: (jnp.maximum(i * per - 1, 0), 0)),
                pl.BlockSpec((HALO, k), lambda i, j: (jnp.minimum((i + 1) * per, last_blk), 0)),
                pl.BlockSpec((k, tn), lambda i, j: (0, j)),
                pl.BlockSpec((k, tn), lambda i, j: (0, j + nj)),
                pl.BlockSpec((CONV_W, tn), lambda i, j: (0, j)),
                pl.BlockSpec((CONV_W, tn), lambda i, j: (0, j + nj)),
                pl.BlockSpec((1, tn), lambda i, j: (0, j)),
                pl.BlockSpec((1, tn), lambda i, j: (0, j + nj))]
    return pl.pallas_call(
        functools.partial(_ffn_up_kernel, tm=tm, seq_len=seq_len),
        grid=(rows // tm, nj),
        in_specs=in_specs,
        out_specs=pl.BlockSpec((tm, tn), lambda i, j: (i, j)),
        out_shape=jax.ShapeDtypeStruct((rows, D_FF), BF16),
        scratch_shapes=[pltpu.VMEM((tm + 2 * HALO, k), BF16)],
        compiler_params=_params("parallel", "arbitrary"),
    )(h, h, h, w_up, w_up, cw, cw, cb2, cb2)


def _short_conv_kernel(b_ref, c_ref, cp_ref, cn_ref, v_ref, vp_ref, vn_ref, w_ref, nw_ref, o_ref, *, tiles_per_seq):
    has_prev, has_next = _seq_edges(tiles_per_seq)
    p = c_ref[...].astype(F32) * v_ref[...].astype(F32)
    prev_row = cp_ref[HALO - 1:HALO, :].astype(F32) * vp_ref[HALO - 1:HALO, :].astype(F32) * has_prev
    next_row = cn_ref[0:1, :].astype(F32) * vn_ref[0:1, :].astype(F32) * has_next
    down, up = _shift_rows(p, prev_row, next_row)
    y = b_ref[...].astype(F32) * (down * w_ref[0:1, :] + p * w_ref[1:2, :] + up * w_ref[2:3, :])
    for h in range(N_CV):
        sl = slice(h * HEAD_DIM, (h + 1) * HEAD_DIM)
        o_ref[:, sl] = _rms(y[:, sl], nw_ref[:, sl]).astype(o_ref.dtype)


def _short_conv(u, cv_w, cv_onorm, seq_len, tr):
    rows = u.shape[0]
    tiles_per_seq = seq_len // tr
    blk = lambda off: (lambda j: off * HEAD_DIM // CV_W)
    b_spec = pl.BlockSpec((tr, CV_W), lambda i, j: (i, OFF_CVB * HEAD_DIM // CV_W))
    c_specs = _halo_specs(tr, CV_W, blk(OFF_CVC), tiles_per_seq, rows)
    v_specs = _halo_specs(tr, CV_W, blk(OFF_CVV), tiles_per_seq, rows)
    return pl.pallas_call(
        functools.partial(_short_conv_kernel, tiles_per_seq=tiles_per_seq),
        grid=(rows // tr, 1),
        in_specs=[b_spec, *c_specs, *v_specs,
                  pl.BlockSpec((CONV_W, CV_W), lambda i, j: (0, 0)),
                  pl.BlockSpec((1, CV_W), lambda i, j: (0, 0))],
        out_specs=pl.BlockSpec((tr, CV_W), lambda i, j: (i, 0)),
        out_shape=jax.ShapeDtypeStruct((rows, CV_W), BF16),
        compiler_params=_params("parallel", "arbitrary"),
    )(u, u, u, u, u, u, u, cv_w, cv_onorm.reshape(1, CV_W))


def _hgrn_structure(chunk, reverse):
    idx = np.arange(chunk)
    i, t = idx[:, None], idx[None, :]
    mats = [t <= i, t > i]
    masks = []
    s = chunk // 2
    while s >= 1:
        blk = idx // (2 * s)
        upper = (idx % (2 * s)) >= s
        mid = blk * 2 * s + s - 1
        a = np.where(upper[:, None], (t > mid[:, None]) & (t <= i), (t > i) & (t <= mid[:, None]))
        mats.append(a)
        masks.append((blk[:, None] == blk[None, :]) & upper[:, None] & ~upper[None, :])
        s //= 2
    mats = np.stack(mats).astype(np.float32)
    masks = np.stack(masks).astype(np.float32)
    if reverse:
        mats = mats[:, ::-1, ::-1]
        masks = masks[:, ::-1, ::-1]
    return mats.reshape(-1, chunk), masks


def _hgrn_kernel(*refs, chunk, n_sub, n_levels, reverse, mode):
    if mode == "final":
        (z_ref, v_ref, q_ref, lb_ref, a_ref, m_ref, s0_ref, g_ref, ofw_ref, nw_ref,
         o_ref, sfin_ref, st_ref) = refs
    elif mode == "raw":
        z_ref, v_ref, q_ref, lb_ref, a_ref, m_ref, s0_ref, o_ref, sfin_ref, st_ref = refs
    else:
        z_ref, v_ref, lb_ref, a_ref, s0_ref, sfin_ref, st_ref = refs
    c = pl.program_id(1)

    @pl.when(c == 0)
    def _():
        st_ref[...] = s0_ref[0]

    lb = lb_ref[...]
    a = a_ref[...]
    tot_row = 0 if reverse else chunk - 1
    subs = range(n_sub - 1, -1, -1) if reverse else range(n_sub)
    for sub in subs:
        rows = slice(sub * chunk, (sub + 1) * chunk)
        z = z_ref[rows, :].astype(F32)
        f = lb + (1.0 - lb) * jax.nn.sigmoid(z)
        log_f = jnp.log(jnp.maximum(f, F_FLOOR))
        k_all = (1.0 - lb) * jax.nn.sigmoid(-z)
        v_all = v_ref[rows, :].astype(F32)
        expo = _dot(a, log_f.astype(BF16))
        for h in range(N_HG):
            sl = slice(h * HEAD_DIM, (h + 1) * HEAD_DIM)
            k, v = k_all[:, sl], v_all[:, sl]
            e_rem = jnp.exp(expo[chunk:2 * chunk, sl])
            e_tot = jnp.exp(expo[tot_row:tot_row + 1, sl])
            st = st_ref[h]
            st_ref[h] = st * e_tot + _dot_tn(v.astype(BF16), (k * e_rem).astype(BF16))
            if mode == "state":
                continue
            q = q_ref[rows, sl].astype(F32)
            e_cum = jnp.exp(expo[0:chunk, sl])
            o = _dot_nt((q * e_cum).astype(BF16), st.astype(BF16))
            att = None
            for lvl in range(n_levels):
                e = jnp.exp(expo[(2 + lvl) * chunk:(3 + lvl) * chunk, sl])
                part = m_ref[lvl] * _dot_nt((q * e).astype(BF16), (k * e).astype(BF16))
                att = part if att is None else att + part
            diag = jnp.sum(q * k, axis=-1, keepdims=True)
            o = o + _dot(att.astype(BF16), v.astype(BF16)) + diag * v
            if mode == "final":
                y = _rms(o + ofw_ref[rows, sl], nw_ref[...]) * _silu(g_ref[rows, sl].astype(F32))
                o_ref[rows, sl] = y.astype(o_ref.dtype)
            else:
                o_ref[rows, sl] = o.astype(o_ref.dtype)

    @pl.when(c == pl.num_programs(1) - 1)
    def _():
        sfin_ref[0] = st_ref[...]


def _hgrn_scan(u, lb, s0, seq_len, *, reverse, mode, o_fw=None, norm_w=None):
    rows = u.shape[0]
    batch = rows // seq_len
    blk = min(HG_BLOCK, seq_len)
    chunk = min(HG_CHUNK, blk)
    n_blocks = seq_len // blk
    mats, masks = _hgrn_structure(chunk, reverse)
    n_levels = masks.shape[0]
    if mode == "state":
        mats = mats[:2 * chunk]
    per_blk = HG_W // HEAD_DIM

    def tok(off):
        col = off // per_blk
        if reverse:
            return pl.BlockSpec((blk, HG_W), lambda b, c: (b * n_blocks + n_blocks - 1 - c, col))
        return pl.BlockSpec((blk, HG_W), lambda b, c: (b * n_blocks + c, col))

    const2 = lambda shape: pl.BlockSpec(shape, lambda b, c: (0, 0))
    state_spec = pl.BlockSpec((1, N_HG, HEAD_DIM, HEAD_DIM), lambda b, c: (b, 0, 0, 0))
    state_shape = jax.ShapeDtypeStruct((batch, N_HG, HEAD_DIM, HEAD_DIM), F32)
    z_spec = tok(OFF_FBW if reverse else OFF_FFW)
    lb2, mats_b = lb.reshape(1, HG_W), jnp.asarray(mats, BF16)
    if mode == "state":
        in_specs = [z_spec, tok(OFF_I), const2((1, HG_W)), const2(mats.shape), state_spec]
        args = [u, u, lb2, mats_b, s0]
        out_specs, out_shape = [state_spec], [state_shape]
    else:
        in_specs = [z_spec, tok(OFF_I), tok(OFF_HGQ), const2((1, HG_W)), const2(mats.shape),
                    pl.BlockSpec(masks.shape, lambda b, c: (0, 0, 0)), state_spec]
        args = [u, u, u, lb2, mats_b, jnp.asarray(masks, F32), s0]
        if mode == "final":
            in_specs += [tok(OFF_HGG), tok(0), const2((1, HEAD_DIM))]
            args += [u, o_fw, norm_w.reshape(1, HEAD_DIM)]
        out_specs = [tok(0), state_spec]
        out_shape = [jax.ShapeDtypeStruct((rows, HG_W), BF16 if mode == "final" else F32), state_shape]
    res = pl.pallas_call(
        functools.partial(_hgrn_kernel, chunk=chunk, n_sub=blk // chunk, n_levels=n_levels, reverse=reverse,
                          mode=mode),
        grid=(batch, n_blocks),
        in_specs=in_specs,
        out_specs=out_specs,
        out_shape=out_shape,
        scratch_shapes=[pltpu.VMEM((N_HG, HEAD_DIM, HEAD_DIM), F32)],
        compiler_params=_params("parallel", "arbitrary"),
    )(*args)
    return (None, res[0]) if mode == "state" else (res[0], res[1])


def _na_bias_table(rpb, rows):
    n_rb = rows // NA_WIN_R
    n_cb = GRID_W // NA_QBLK_C
    n_dr, n_dc = 2 * NA_WIN_R - 1, 2 * NA_WIN_C - 1
    n_k = NA_KEY_R * NA_KEY_C
    offs, row_mask = [], []
    for rb in (0, min(1, n_rb - 1), n_rb - 1):
        k_row0 = int(np.clip(rb * NA_WIN_R - NA_WIN_R // 2, 0, rows - NA_KEY_R))
        q_r = rb * NA_WIN_R + np.arange(NA_WIN_R)
        k_r = k_row0 + np.arange(NA_KEY_R)
        r_start = np.clip(q_r - NA_WIN_R // 2, 0, rows - NA_WIN_R)
        ok_r = (k_r[None, :] >= r_start[:, None]) & (k_r[None, :] < r_start[:, None] + NA_WIN_R)
        row_mask.append(np.repeat(np.where(ok_r, 0.0, NEG_INF), NA_KEY_C, axis=1))
        offs.append(k_row0 - rb * NA_WIN_R + NA_WIN_R - 1)
    row_mask = jnp.asarray(np.stack(row_mask), F32)
    pad_lo = NA_WIN_R - 1 - min(offs)
    n_a = NA_KEY_R - 1 + max(offs) + pad_lo + 1
    lanes = -(-(n_a * NA_KEY_C) // HEAD_DIM) * HEAD_DIM
    sel_c, ok_c = [], []
    for cb in range(n_cb):
        k_col0 = int(np.clip(cb * NA_QBLK_C - NA_WIN_C // 2, 0, GRID_W - NA_KEY_C))
        q_c = cb * NA_QBLK_C + np.arange(NA_QBLK_C)
        k_c = k_col0 + np.arange(NA_KEY_C)
        c_start = np.clip(q_c - NA_WIN_C // 2, 0, GRID_W - NA_WIN_C)
        ok_c.append((k_c[None, :] >= c_start[:, None]) & (k_c[None, :] < c_start[:, None] + NA_WIN_C))
        dc = k_c[None, :] - q_c[:, None] + NA_WIN_C - 1
        sel_c.append(dc[:, :, None] == np.arange(n_dc))
    sel_c = jnp.asarray(np.stack(sel_c), F32)
    ok_c = jnp.asarray(np.stack(ok_c))
    strip = jnp.einsum("hab,mcjb->hmcaj", rpb.astype(F32), sel_c, precision=lax.Precision.HIGHEST)
    strip = jnp.where(ok_c[None, :, :, None, :], strip, NEG_INF)
    strip = jnp.pad(strip, ((0, 0), (0, 0), (0, 0), (pad_lo, n_a - n_dr - pad_lo), (0, 0)))
    strip = strip.reshape(strip.shape[:3] + (n_a * NA_KEY_C,))
    strip = jnp.pad(strip, ((0, 0), (0, 0), (0, 0), (0, lanes - n_a * NA_KEY_C)))
    n_heads = rpb.shape[0]
    return pl.pallas_call(
        functools.partial(_na_bias_kernel, offs=tuple(offs), pad_lo=pad_lo),
        grid=(n_heads,),
        in_specs=[pl.BlockSpec((1, n_cb, NA_QBLK_C, lanes), lambda h: (h, 0, 0, 0)),
                  pl.BlockSpec((3, NA_WIN_R, n_k), lambda h: (0, 0, 0))],
        out_specs=pl.BlockSpec((1, 3, n_cb, NA_WIN_R * NA_QBLK_C, n_k), lambda h: (h, 0, 0, 0, 0)),
        out_shape=jax.ShapeDtypeStruct((n_heads, 3, n_cb, NA_WIN_R * NA_QBLK_C, n_k), F32),
        compiler_params=_params("parallel"),
    )(strip, row_mask)


def _na_bias_kernel(t_ref, rm_ref, o_ref, *, offs, pad_lo):
    n_cb = t_ref.shape[1]
    lanes = t_ref.shape[-1]
    per = HEAD_DIM // NA_KEY_C
    n_k = o_ref.shape[-1]
    for cb in range(n_cb):
        t = t_ref[0, cb]
        shifted = [t] + [pltpu.roll(t, lanes - NA_KEY_C * s, axis=1) for s in range(1, per)]
        for cls, off in enumerate(offs):
            for r in range(NA_WIN_R):
                m = off - r + pad_lo
                base = (m // per) * HEAD_DIM
                tile = shifted[m % per][:, base:base + n_k] + rm_ref[cls, r:r + 1, :]
                o_ref[0, cls, cb, r * NA_QBLK_C:(r + 1) * NA_QBLK_C, :] = tile


def _na_kernel(q_ref, k0_ref, k1_ref, k2_ref, k3_ref, v0_ref, v1_ref, v2_ref, v3_ref, kc_ref, vc_ref, bias_ref,
               ow_ref, o_ref):
    n_cb = GRID_W // NA_QBLK_C
    n_q, n_k = NA_WIN_R * NA_QBLK_C, NA_KEY_R * NA_KEY_C
    q_all = q_ref[...].astype(F32)
    k_all = jnp.concatenate([r[...] for r in (k0_ref, k1_ref, k2_ref, k3_ref)], axis=0).astype(F32)
    v_all = jnp.concatenate([r[...] for r in (v0_ref, v1_ref, v2_ref, v3_ref)], axis=0).astype(F32)
    for hh in range(NA_HEADS):
        hs = slice(hh * HEAD_DIM, (hh + 1) * HEAD_DIM)
        q3 = q_all[:, hs].reshape(NA_WIN_R, GRID_W, HEAD_DIM)
        k3 = k_all[:, hs].reshape(NA_KEY_R, GRID_W, HEAD_DIM)
        v3 = v_all[:, hs].reshape(NA_KEY_R, GRID_W, HEAD_DIM)
        k_ctx = kc_ref[:, hs]
        v_ctx = vc_ref[:, hs]
        for cb in range(n_cb):
            q0 = cb * NA_QBLK_C
            k_col0 = min(max(q0 - NA_WIN_C // 2, 0), GRID_W - NA_KEY_C)
            qm = q3[:, q0:q0 + NA_QBLK_C, :].reshape(n_q, HEAD_DIM).astype(BF16)
            km = k3[:, k_col0:k_col0 + NA_KEY_C, :].reshape(n_k, HEAD_DIM).astype(BF16)
            vm = v3[:, k_col0:k_col0 + NA_KEY_C, :].reshape(n_k, HEAD_DIM).astype(BF16)
            s_loc = _dot_nt(qm, km) + bias_ref[hh, 0, cb]
            s_ctx = _dot_nt(qm, k_ctx)
            mx = jnp.maximum(jnp.max(s_loc, axis=-1, keepdims=True), jnp.max(s_ctx, axis=-1, keepdims=True))
            p_loc = jnp.exp(s_loc - mx)
            p_ctx = jnp.exp(s_ctx - mx)
            den = jnp.sum(p_loc, axis=-1, keepdims=True) + jnp.sum(p_ctx, axis=-1, keepdims=True)
            o = (_dot(p_loc.astype(BF16), vm) + _dot(p_ctx.astype(BF16), v_ctx)) / den
            o = _rms(o, ow_ref[hh]).astype(o_ref.dtype)
            for r in range(NA_WIN_R):
                o_ref[r * GRID_W + q0:r * GRID_W + q0 + NA_QBLK_C, hs] = o[r * NA_QBLK_C:(r + 1) * NA_QBLK_C, :]


def _neighbourhood_attention(u, u_ctx, bias, head0, out_norm, batch, rows, ctx_len):
    assert rows % NA_WIN_R == 0 and rows >= NA_KEY_R
    n_rb = rows // NA_WIN_R
    q_blk = NA_WIN_R * GRID_W
    k_blk = NA_KEY_R * GRID_W // 4
    k_per_batch = rows * GRID_W // k_blk

    width = NA_HEADS * HEAD_DIM
    assert all(off % NA_HEADS == 0 for off in (OFF_NAQ, OFF_NAK, OFF_NAV, head0, N_NA))
    col_q, col_k, col_v = OFF_NAQ // NA_HEADS, OFF_NAK // NA_HEADS, OFF_NAV // NA_HEADS

    def key_spec(col, part):
        def index(h, rb, b):
            first = jnp.clip(rb * NA_WIN_R - NA_WIN_R // 2, 0, rows - NA_KEY_R) * GRID_W // k_blk
            return (b * k_per_batch + first + part, col + h)
        return pl.BlockSpec((k_blk, width), index)

    def row_class(rb):
        return jnp.where(rb == 0, 0, jnp.where(rb == n_rb - 1, 2, 1))

    n_cb = GRID_W // NA_QBLK_C
    in_specs = [pl.BlockSpec((q_blk, width), lambda h, rb, b: (b * n_rb + rb, col_q + h))]
    in_specs += [key_spec(col_k, p) for p in range(4)] + [key_spec(col_v, p) for p in range(4)]
    in_specs += [pl.BlockSpec((ctx_len, width), lambda h, rb, b: (b, col_k + h)),
                 pl.BlockSpec((ctx_len, width), lambda h, rb, b: (b, col_v + h)),
                 pl.BlockSpec((NA_HEADS, 1, n_cb, NA_WIN_R * NA_QBLK_C, NA_KEY_R * NA_KEY_C),
                              lambda h, rb, b: (head0 // NA_HEADS + h, row_class(rb), 0, 0, 0)),
                 pl.BlockSpec((NA_HEADS, 1, HEAD_DIM), lambda h, rb, b: (h, 0, 0))]
    return pl.pallas_call(
        _na_kernel,
        grid=(N_NA // NA_HEADS, n_rb, batch),
        in_specs=in_specs,
        out_specs=pl.BlockSpec((q_blk, width), lambda h, rb, b: (b * n_rb + rb, h)),
        out_shape=jax.ShapeDtypeStruct((u.shape[0], NA_W), BF16),
        compiler_params=_params("parallel", "parallel", "parallel"),
    )(u, u, u, u, u, u, u, u, u, u_ctx, u_ctx, bias, out_norm.reshape(N_NA, 1, HEAD_DIM))


def _ctx_attn_kernel(q_ref, k_ref, v_ref, ow_ref, o_ref):
    s = _dot_nt(q_ref[...].astype(BF16), k_ref[...].astype(BF16))
    p = jnp.exp(s - jnp.max(s, axis=-1, keepdims=True))
    o = _dot(p.astype(BF16), v_ref[...].astype(BF16)) / jnp.sum(p, axis=-1, keepdims=True)
    o_ref[...] = _rms(o, ow_ref[0]).astype(o_ref.dtype)


def _context_attention(u_ctx, out_norm, batch, ctx_len):
    tok = lambda off: pl.BlockSpec((ctx_len, HEAD_DIM), lambda b, h: (b, off + h))
    return pl.pallas_call(
        _ctx_attn_kernel,
        grid=(batch, N_NA),
        in_specs=[tok(OFF_NAQ), tok(OFF_NAK), tok(OFF_NAV),
                  pl.BlockSpec((1, 1, HEAD_DIM), lambda b, h: (h, 0, 0))],
        out_specs=tok(0),
        out_shape=jax.ShapeDtypeStruct((u_ctx.shape[0], NA_W), BF16),
        compiler_params=_params("parallel", "parallel"),
    )(u_ctx, u_ctx, u_ctx, out_norm.reshape(N_NA, 1, HEAD_DIM))


def _row_tile(rows, target):
    t = min(rows, target)
    assert rows % t == 0
    return t


def _token_stream(xs, seq_len, mod_row, p, layer, w_up_b, ada_l, mixers):
    rows = xs.shape[0]
    sh2, sc2, g1, g2 = ada_l["sh2"], ada_l["sc2"], ada_l["g1"], ada_l["g2"]

    def mod_of_tile(tile_rows):
        per = max(seq_len // tile_rows, 1)
        return (lambda i: 1 + i // per) if mod_row is None else (lambda i: mod_row)

    tm_o = _row_tile(rows, 512)
    x1, h2 = _out_proj_norm(mixers, p["w_out"], layer, xs, g1, p["ln2"][layer], sc2, sh2, mod_of_tile(tm_o), tm_o)
    tm = _row_tile(rows, 1024)
    act = _ffn_up_conv_gate(h2, w_up_b, p["f_cw"][layer], p["f_cb"][layer], seq_len, tm, 512)
    return _matmul_residual([act], p["w_down"], layer, x1, g2, mod_of_tile(tm), tm, 512)


def kernel(x, c, ctx, c_ctx, w_ada, b_ada, ln1_w, ln2_w, w_in, hg_lb_logits, hg_norm_w, na_q_norm_w, na_k_norm_w,
           na_rpb, na_out_norm_w, cv_w, cv_out_norm_w, w_out, w_up, ffn_conv_w, ffn_conv_b, w_down):
    batch, seq, d = x.shape
    ctx_len = ctx.shape[1]
    rows = seq // GRID_W
    depth = w_ada.shape[0]
    xs = x.reshape(batch * seq, d)
    cs = ctx.reshape(batch * ctx_len, d)

    lb_sm = jax.nn.softmax(hg_lb_logits.astype(F32), axis=1)
    lb_all = jnp.cumsum(lb_sm, axis=1) - lb_sm[:, :1]

    cond = jnp.zeros((SUBLANE, d), F32).at[0].set(c_ctx).at[1:1 + batch].set(c)
    ada = _ada_table(cond, w_ada, b_ada).reshape(depth, SUBLANE, 6, 1, d)

    s_zero = jnp.zeros((batch, N_HG, HEAD_DIM, HEAD_DIM), F32)
    p = {"w_out": w_out, "w_down": w_down, "ln2": ln2_w, "f_cw": ffn_conv_w,
         "f_cb": ffn_conv_b}
    ctx_cols = sum(IN_SPLITS[:5])
    bias = _na_bias_table(na_rpb.reshape((depth * N_NA,) + na_rpb.shape[2:]), rows)
    for l in range(depth):
        last = l == depth - 1
        names = ("sh1", "sc1", "g1", "sh2", "sc2", "g2")
        ada_l = {n: ada[l, :, i] for i, n in enumerate(names)}
        tm = _row_tile(xs.shape[0], 1024)
        tm_c = _row_tile(cs.shape[0], 1024)
        tiles_per_seq = seq // tm
        lat_mod = lambda i: 1 + i // tiles_per_seq
        ctx_mod = lambda i: 0
        h = _norm_mod(xs, ln1_w[l], ada_l["sh1"], ada_l["sc1"], lat_mod, tm)
        hc = _norm_mod(cs, ln1_w[l], ada_l["sh1"], ada_l["sc1"], ctx_mod, tm_c)
        if (IN_W // 1024) * (xs.shape[0] // tm) >= w_up.shape[2] // HEAD_DIM:
            u, w_up_b = _in_proj(h, w_in, l, na_q_norm_w[l], na_k_norm_w[l], IN_W, tm, 1024, side_w=w_up)
        else:
            u = _in_proj(h, w_in, l, na_q_norm_w[l], na_k_norm_w[l], IN_W, tm, 1024)
            w_up_b = w_up[l].astype(BF16)
        uc = _in_proj(hc, w_in, l, na_q_norm_w[l], na_k_norm_w[l], ctx_cols if last else IN_W, tm_c, 512)

        if last:
            _, s_fw = _hgrn_scan(uc, lb_all[0, l], s_zero, ctx_len, reverse=False, mode="state")
            _, s_bw = _hgrn_scan(uc, lb_all[1, l], s_zero, ctx_len, reverse=True, mode="state")
        else:
            co_fw, s_fw = _hgrn_scan(uc, lb_all[0, l], s_zero, ctx_len, reverse=False, mode="raw")
            hg_c, s_bw = _hgrn_scan(uc, lb_all[1, l], s_zero, ctx_len, reverse=True, mode="final", o_fw=co_fw,
                                    norm_w=hg_norm_w[l])
        o_fw, _ = _hgrn_scan(u, lb_all[0, l], s_fw, seq, reverse=False, mode="raw")
        hg_out, _ = _hgrn_scan(u, lb_all[1, l], s_bw, seq, reverse=True, mode="final", o_fw=o_fw,
                               norm_w=hg_norm_w[l])

        na_out = _neighbourhood_attention(u, uc, bias, l * N_NA, na_out_norm_w[l], batch, rows, ctx_len)
        cv_out = _short_conv(u, cv_w[l], cv_out_norm_w[l], seq, _row_tile(seq, 256))

        xs = _token_stream(xs, seq, None, p, l, w_up_b, ada_l, [hg_out, na_out, cv_out])
        if not last:
            na_c = _context_attention(uc, na_out_norm_w[l], batch, ctx_len)
            cv_c = _short_conv(uc, cv_w[l], cv_out_norm_w[l], ctx_len, _row_tile(ctx_len, 256))
            cs = _token_stream(cs, ctx_len, 0, p, l, w_up_b, ada_l, [hg_c, na_c, cv_c])
    return xs.reshape(batch, seq, d)
```

```python
import functools

import numpy as np
import jax
import jax.numpy as jnp
from jax import lax
from jax.experimental import pallas as pl
from jax.experimental.pallas import tpu as pltpu

F32 = jnp.float32
BF16 = jnp.bfloat16

D_MODEL = 2048
DEPTH = 2
GRID_W = 64
HEAD_DIM = 128
N_HG = 4
N_NA = 8
N_CV = 4
HG_W = N_HG * HEAD_DIM
NA_W = N_NA * HEAD_DIM
CV_W = N_CV * HEAD_DIM
IN_SPLITS = (HG_W, HG_W, HG_W, NA_W, NA_W, HG_W, HG_W, NA_W, CV_W, CV_W, CV_W)
IN_W = sum(IN_SPLITS)
(OFF_FFW, OFF_FBW, OFF_I, OFF_NAK, OFF_NAV, OFF_HGQ, OFF_HGG, OFF_NAQ, OFF_CVB, OFF_CVC,
 OFF_CVV) = [int(v) // HEAD_DIM for v in np.cumsum((0,) + IN_SPLITS[:-1])]
NA_WIN_R = 8
NA_WIN_C = 16
NA_QBLK_C = 16
NA_KEY_C = 32
NA_KEY_R = 16
NA_HEADS = 2
CONV_W = 3
D_FF = 5632
EPS = 1e-6
F_FLOOR = 1e-30
ATTN_SCALE = HEAD_DIM ** -0.5
NEG_INF = -1e30

HG_CHUNK = 128
HG_BLOCK = 512
SUBLANE = 8
HALO = 16
MXU_N = 256
TRACE_AHEAD = 3
VMEM_LIMIT = 56 * 1024 * 1024


def _params(*sem):
    return pltpu.CompilerParams(dimension_semantics=sem, vmem_limit_bytes=VMEM_LIMIT)


def _dot(a, b):
    return jnp.dot(a, b, preferred_element_type=F32)


def _dot_nt(a, b):
    return lax.dot_general(a, b, (((1,), (1,)), ((), ())), preferred_element_type=F32)


def _dot_tn(a, b):
    return lax.dot_general(a, b, (((0,), (0,)), ((), ())), preferred_element_type=F32)


def _rms(x, w):
    return x * lax.rsqrt(jnp.mean(x * x, axis=-1, keepdims=True) + EPS) * w


def _silu(x):
    half = 0.5 * x
    return half + half * jnp.tanh(half)


def _ada_kernel(s_ref, w_ref, b_ref, o_ref):
    s = _silu(s_ref[...]).astype(BF16)
    o_ref[0] = _dot(s, w_ref[0].astype(BF16)) + b_ref[0]


def _ada_table(cond, w_ada, b_ada):
    depth, d, n = w_ada.shape
    tn = 1024
    return pl.pallas_call(
        _ada_kernel,
        grid=(depth, n // tn),
        in_specs=[pl.BlockSpec((SUBLANE, d), lambda l, j: (0, 0)),
                  pl.BlockSpec((1, d, tn), lambda l, j: (l, 0, j)),
                  pl.BlockSpec((1, 1, tn), lambda l, j: (l, 0, j))],
        out_specs=pl.BlockSpec((1, SUBLANE, tn), lambda l, j: (l, 0, j)),
        out_shape=jax.ShapeDtypeStruct((depth, SUBLANE, n), F32),
        compiler_params=_params("parallel", "parallel"),
    )(cond, w_ada, b_ada.reshape(depth, 1, n))


def _norm_mod_kernel(x_ref, w_ref, sh_ref, sc_ref, o_ref):
    y = _rms(x_ref[...], w_ref[...])
    o_ref[...] = (y * (1.0 + sc_ref[0]) + sh_ref[0]).astype(o_ref.dtype)


def _norm_mod(x, w, shift, scale, mod_of_tile, tr):
    rows, d = x.shape
    mod = lambda i: (mod_of_tile(i), 0, 0)
    return pl.pallas_call(
        _norm_mod_kernel,
        grid=(rows // tr,),
        in_specs=[pl.BlockSpec((tr, d), lambda i: (i, 0)),
                  pl.BlockSpec((1, d), lambda i: (0, 0)),
                  pl.BlockSpec((1, 1, d), mod),
                  pl.BlockSpec((1, 1, d), mod)],
        out_specs=pl.BlockSpec((tr, d), lambda i: (i, 0)),
        out_shape=jax.ShapeDtypeStruct((rows, d), BF16),
        compiler_params=_params("parallel"),
    )(x, w.reshape(1, d), shift, scale)


def _in_proj_kinds(n_cols, tn):
    kinds = []
    for head in range(n_cols // HEAD_DIM):
        if OFF_NAK <= head < OFF_NAV:
            kinds.append("k")
        elif OFF_NAQ <= head < OFF_CVB:
            kinds.append("q")
        elif OFF_HGQ <= head < OFF_HGG:
            kinds.append("silu")
        else:
            kinds.append("id")
    per = tn // HEAD_DIM
    return tuple(tuple(kinds[t * per:(t + 1) * per]) for t in range(n_cols // tn))


def _in_proj_kernel(*refs, kinds, with_cast):
    if with_cast:
        a_ref, w_ref, qw_ref, kw_ref, side_ref, o_ref, side_o_ref, wb_ref = refs
        side_o_ref[...] = side_ref[...].astype(BF16)
    else:
        a_ref, w_ref, qw_ref, kw_ref, o_ref, wb_ref = refs
    j = pl.program_id(0)

    @pl.when(pl.program_id(1) == 0)
    def _():
        wb_ref[...] = w_ref[...].astype(BF16)

    def transform(y, kind):
        if kind == "k":
            return _rms(y, kw_ref[...])
        if kind == "q":
            return _rms(y, qw_ref[...]) * ATTN_SCALE
        if kind == "silu":
            return _silu(y)
        return y

    per_slab = MXU_N // HEAD_DIM
    for pattern in sorted(set(kinds)):
        hit = functools.reduce(jnp.logical_or, [j == jj for jj, tk in enumerate(kinds) if tk == pattern])

        @pl.when(hit)
        def _(pattern=pattern):
            a = a_ref[...]
            for c in range(len(pattern) // per_slab):
                cols = slice(c * MXU_N, (c + 1) * MXU_N)
                acc = _dot(a, wb_ref[:, cols])
                parts = [transform(acc[:, s * HEAD_DIM:(s + 1) * HEAD_DIM], pattern[c * per_slab + s])
                         for s in range(per_slab)]
                o_ref[:, cols] = jnp.concatenate(parts, axis=1).astype(o_ref.dtype)


def _in_proj(h, w_in, layer, q_norm, k_norm, n_cols, tm, tn, side_w=None):
    m, k = h.shape
    kinds = _in_proj_kinds(n_cols, tn)
    nj, ni = n_cols // tn, m // tm
    vec = pl.BlockSpec((1, HEAD_DIM), lambda j, i: (0, 0))
    in_specs = [pl.BlockSpec((tm, k), lambda j, i: (i, 0)),
                pl.BlockSpec((None, k, tn), lambda j, i: (layer, 0, j)),
                vec, vec]
    args = [h, w_in, q_norm.reshape(1, HEAD_DIM), k_norm.reshape(1, HEAD_DIM)]
    out_specs = [pl.BlockSpec((tm, tn), lambda j, i: (i, j))]
    out_shape = [jax.ShapeDtypeStruct((m, n_cols), BF16)]
    if side_w is not None:
        _, k2, n2 = side_w.shape
        strips = n2 // HEAD_DIM
        assert strips <= nj * ni
        strip = lambda j, i: jnp.minimum(j * ni + i, strips - 1)
        in_specs.append(pl.BlockSpec((None, k2, HEAD_DIM), lambda j, i: (layer, 0, strip(j, i))))
        args.append(side_w)
        out_specs.append(pl.BlockSpec((k2, HEAD_DIM), lambda j, i: (0, strip(j, i))))
        out_shape.append(jax.ShapeDtypeStruct((k2, n2), BF16))
    res = pl.pallas_call(
        functools.partial(_in_proj_kernel, kinds=kinds, with_cast=side_w is not None),
        grid=(nj, ni),
        in_specs=in_specs,
        out_specs=out_specs,
        out_shape=out_shape,
        scratch_shapes=[pltpu.VMEM((k, tn), BF16)],
        compiler_params=_params("arbitrary", "arbitrary"),
    )(*args)
    return res if side_w is not None else res[0]


def _mm_res_kernel(*refs, splits):
    n_a = len(splits)
    a_refs = refs[:n_a]
    w_ref, x_ref, g_ref, o_ref, wb_ref = refs[n_a:]

    @pl.when(pl.program_id(1) == 0)
    def _():
        wb_ref[...] = w_ref[...].astype(BF16)

    acc = None
    off = 0
    for a_ref, width in zip(a_refs, splits):
        part = _dot(a_ref[...], wb_ref[off:off + width, :])
        acc = part if acc is None else acc + part
        off += width
    o_ref[...] = x_ref[...] + g_ref[0] * acc


def _matmul_residual(a_list, w, layer, x, gate, mod_of_tile, tm, tn):
    m = x.shape[0]
    _, k, n = w.shape
    splits = tuple(a.shape[1] for a in a_list)
    assert sum(splits) == k
    in_specs = [pl.BlockSpec((tm, s), lambda j, i: (i, 0)) for s in splits]
    in_specs += [pl.BlockSpec((None, k, tn), lambda j, i: (layer, 0, j), pipeline_mode=pl.Buffered(1)),
                 pl.BlockSpec((tm, tn), lambda j, i: (i, j)),
                 pl.BlockSpec((1, 1, tn), lambda j, i: (mod_of_tile(i), 0, j))]
    return pl.pallas_call(
        functools.partial(_mm_res_kernel, splits=splits),
        grid=(n // tn, m // tm),
        in_specs=in_specs,
        out_specs=pl.BlockSpec((tm, tn), lambda j, i: (i, j)),
        out_shape=jax.ShapeDtypeStruct((m, n), F32),
        scratch_shapes=[pltpu.VMEM((k, tn), BF16)],
        compiler_params=_params("parallel", "arbitrary"),
    )(*a_list, w, x, gate)


def _out_proj_norm_kernel(*refs, splits):
    n_a = len(splits)
    a_refs = refs[:n_a]
    w_ref, x_ref, g_ref, lw_ref, sc_ref, sh_ref, x1_ref, h2_ref, wb_ref = refs[n_a:]

    @pl.when(pl.program_id(0) == 0)
    def _():
        wb_ref[...] = w_ref[...].astype(BF16)

    tm, n = x1_ref.shape
    ssq = jnp.zeros((tm, HEAD_DIM), F32)
    for c in range(n // MXU_N):
        cols = slice(c * MXU_N, (c + 1) * MXU_N)
        acc = None
        off = 0
        for a_ref, width in zip(a_refs, splits):
            part = _dot(a_ref[...], wb_ref[off:off + width, cols])
            acc = part if acc is None else acc + part
            off += width
        x1 = x_ref[:, cols] + g_ref[0][:, cols] * acc
        x1_ref[:, cols] = x1
        sq = x1 * x1
        for s in range(MXU_N // HEAD_DIM):
            ssq = ssq + sq[:, s * HEAD_DIM:(s + 1) * HEAD_DIM]
    inv = lax.rsqrt(jnp.sum(ssq, axis=-1, keepdims=True) * (1.0 / n) + EPS)
    gain = lw_ref[...] * (1.0 + sc_ref[0])
    for c in range(n // MXU_N):
        cols = slice(c * MXU_N, (c + 1) * MXU_N)
        h2_ref[:, cols] = (x1_ref[:, cols] * inv * gain[:, cols] + sh_ref[0][:, cols]).astype(h2_ref.dtype)


def _out_proj_norm(a_list, w, layer, x, gate, ln_w, scale, shift, mod_of_tile, tm):
    m, n = x.shape
    _, k, _ = w.shape
    splits = tuple(a.shape[1] for a in a_list)
    assert sum(splits) == k
    mod = lambda i: (mod_of_tile(i), 0, 0)
    in_specs = [pl.BlockSpec((tm, s), lambda i: (i, 0)) for s in splits]
    in_specs += [pl.BlockSpec((None, k, n), lambda i: (layer, 0, 0), pipeline_mode=pl.Buffered(1)),
                 pl.BlockSpec((tm, n), lambda i: (i, 0)),
                 pl.BlockSpec((1, 1, n), mod),
                 pl.BlockSpec((1, n), lambda i: (0, 0)),
                 pl.BlockSpec((1, 1, n), mod),
                 pl.BlockSpec((1, 1, n), mod)]
    return pl.pallas_call(
        functools.partial(_out_proj_norm_kernel, splits=splits),
        grid=(m // tm,),
        in_specs=in_specs,
        out_specs=[pl.BlockSpec((tm, n), lambda i: (i, 0)), pl.BlockSpec((tm, n), lambda i: (i, 0))],
        out_shape=[jax.ShapeDtypeStruct((m, n), F32), jax.ShapeDtypeStruct((m, n), BF16)],
        scratch_shapes=[pltpu.VMEM((k, n), BF16)],
        compiler_params=_params("arbitrary"),
    )(*a_list, w, x, gate, ln_w.reshape(1, n), scale, shift)


def _shift_rows(x, prev_row, next_row):
    r = x.shape[0]
    idx = lax.broadcasted_iota(jnp.int32, x.shape, 0)
    down = jnp.where(idx == 0, prev_row, pltpu.roll(x, 1, axis=0))
    up = jnp.where(idx == r - 1, next_row, pltpu.roll(x, r - 1, axis=0))
    return down, up


def _halo_specs(tr, tc, col_of, tiles_per_seq, n_rows):
    per = tr // HALO
    last_blk = n_rows // HALO - 1
    cur = pl.BlockSpec((tr, tc), lambda i, j: (i, col_of(j)))
    prev = pl.BlockSpec((HALO, tc), lambda i, j: (jnp.maximum(i * per - 1, 0), col_of(j)))
    nxt = pl.BlockSpec((HALO, tc), lambda i, j: (jnp.minimum((i + 1) * per, last_blk), col_of(j)))
    return cur, prev, nxt


def _seq_edges(tiles_per_seq):
    i = pl.program_id(0)
    pos = i % tiles_per_seq
    return (pos != 0).astype(F32), (pos != tiles_per_seq - 1).astype(F32)


def _ffn_up_kernel(h_ref, hp_ref, hn_ref, wg_ref, wv_ref, cg_ref, cv_ref, bg_ref, bv_ref, o_ref, lhs_ref,
                   *, tm, seq_len):
    tiles_per_seq = max(seq_len // tm, 1)
    pos = pl.program_id(0) % tiles_per_seq
    has_prev = pos != 0
    has_next = pos != tiles_per_seq - 1

    @pl.when(pl.program_id(1) == 0)
    def _():
        lhs_ref[0:HALO, :] = jnp.where(has_prev, hp_ref[...], jnp.zeros_like(hp_ref))
        lhs_ref[HALO:HALO + tm, :] = h_ref[...]
        lhs_ref[HALO + tm:, :] = jnp.where(has_next, hn_ref[...], jnp.zeros_like(hn_ref))

    lhs = lhs_ref[...]

    def branch(w_ref, c_ref, b_ref):
        acc = _dot(lhs, w_ref[...])
        r = acc.shape[0]
        down, up = pltpu.roll(acc, 1, axis=0), pltpu.roll(acc, r - 1, axis=0)
        if seq_len < tm:
            tok = (lax.broadcasted_iota(jnp.int32, acc.shape, 0) + (seq_len - HALO)) % seq_len
            down = jnp.where(tok == 0, 0.0, down)
            up = jnp.where(tok == seq_len - 1, 0.0, up)
        y = down * c_ref[0:1, :] + acc * c_ref[1:2, :] + up * c_ref[2:3, :]
        return y[HALO:HALO + tm, :] + b_ref[...]

    gate = branch(wg_ref, cg_ref, bg_ref)
    val = branch(wv_ref, cv_ref, bv_ref)
    o_ref[...] = (_silu(gate) * val).astype(o_ref.dtype)


def _ffn_up_conv_gate(h, w_up, cw, cb, seq_len, tm, tn):
    rows, k = h.shape
    assert seq_len % tm == 0 or (tm % seq_len == 0 and seq_len >= HALO)
    nj = D_FF // tn
    per = tm // HALO
    last_blk = rows // HALO - 1
    cb2 = cb.reshape(1, 2 * D_FF)
    in_specs = [pl.BlockSpec((tm, k), lambda i, j: (i, 0)),
                pl.BlockSpec((HALO, k), lambda i, j: (jnp.maximum(i * per - 1, 0), 0)),
                pl.BlockSpec((HALO, k), lambda i, j: (jnp.minimum((i + 1) * per, last_blk), 0)),
                pl.BlockSpec((k, tn), lambda i, j: (0, j)),
                pl.BlockSpec((k, tn), lambda i, j: (0, j + nj)),
                pl.BlockSpec((CONV_W, tn), lambda i, j: (0, j)),
                pl.BlockSpec((CONV_W, tn), lambda i, j: (0, j + nj)),
                pl.BlockSpec((1, tn), lambda i, j: (0, j)),
                pl.BlockSpec((1, tn), lambda i, j: (0, j + nj))]
    return pl.pallas_call(
        functools.partial(_ffn_up_kernel, tm=tm, seq_len=seq_len),
        grid=(rows // tm, nj),
        in_specs=in_specs,
        out_specs=pl.BlockSpec((tm, tn), lambda i, j: (i, j)),
        out_shape=jax.ShapeDtypeStruct((rows, D_FF), BF16),
        scratch_shapes=[pltpu.VMEM((tm + 2 * HALO, k), BF16)],
        compiler_params=_params("parallel", "arbitrary"),
    )(h, h, h, w_up, w_up, cw, cw, cb2, cb2)


def _short_conv_kernel(b_ref, c_ref, cp_ref, cn_ref, v_ref, vp_ref, vn_ref, w_ref, nw_ref, o_ref, *, tiles_per_seq):
    has_prev, has_next = _seq_edges(tiles_per_seq)
    p = c_ref[...].astype(F32) * v_ref[...].astype(F32)
    prev_row = cp_ref[HALO - 1:HALO, :].astype(F32) * vp_ref[HALO - 1:HALO, :].astype(F32) * has_prev
    next_row = cn_ref[0:1, :].astype(F32) * vn_ref[0:1, :].astype(F32) * has_next
    down, up = _shift_rows(p, prev_row, next_row)
    y = b_ref[...].astype(F32) * (down * w_ref[0:1, :] + p * w_ref[1:2, :] + up * w_ref[2:3, :])
    for h in range(N_CV):
        sl = slice(h * HEAD_DIM, (h + 1) * HEAD_DIM)
        o_ref[:, sl] = _rms(y[:, sl], nw_ref[:, sl]).astype(o_ref.dtype)


def _short_conv(u, cv_w, cv_onorm, seq_len, tr):
    rows = u.shape[0]
    tiles_per_seq = seq_len // tr
    blk = lambda off: (lambda j: off * HEAD_DIM // CV_W)
    b_spec = pl.BlockSpec((tr, CV_W), lambda i, j: (i, OFF_CVB * HEAD_DIM // CV_W))
    c_specs = _halo_specs(tr, CV_W, blk(OFF_CVC), tiles_per_seq, rows)
    v_specs = _halo_specs(tr, CV_W, blk(OFF_CVV), tiles_per_seq, rows)
    return pl.pallas_call(
        functools.partial(_short_conv_kernel, tiles_per_seq=tiles_per_seq),
        grid=(rows // tr, 1),
        in_specs=[b_spec, *c_specs, *v_specs,
                  pl.BlockSpec((CONV_W, CV_W), lambda i, j: (0, 0)),
                  pl.BlockSpec((1, CV_W), lambda i, j: (0, 0))],
        out_specs=pl.BlockSpec((tr, CV_W), lambda i, j: (i, 0)),
        out_shape=jax.ShapeDtypeStruct((rows, CV_W), BF16),
        compiler_params=_params("parallel", "arbitrary"),
    )(u, u, u, u, u, u, u, cv_w, cv_onorm.reshape(1, CV_W))


def _hgrn_structure(chunk, reverse):
    idx = np.arange(chunk)
    i, t = idx[:, None], idx[None, :]
    mats = [t <= i, t > i]
    masks = []
    s = chunk // 2
    while s >= 1:
        blk = idx // (2 * s)
        upper = (idx % (2 * s)) >= s
        mid = blk * 2 * s + s - 1
        a = np.where(upper[:, None], (t > mid[:, None]) & (t <= i), (t > i) & (t <= mid[:, None]))
        mats.append(a)
        masks.append((blk[:, None] == blk[None, :]) & upper[:, None] & ~upper[None, :])
        s //= 2
    mats = np.stack(mats).astype(np.float32)
    masks = np.stack(masks).astype(np.float32)
    if reverse:
        mats = mats[:, ::-1, ::-1]
        masks = masks[:, ::-1, ::-1]
    return mats.reshape(-1, chunk), masks


def _hgrn_kernel(*refs, chunk, n_sub, n_levels, reverse, mode):
    if mode == "final":
        (z_ref, v_ref, q_ref, lb_ref, a_ref, m_ref, s0_ref, g_ref, ofw_ref, nw_ref,
         o_ref, sfin_ref, st_ref) = refs
    elif mode == "raw":
        z_ref, v_ref, q_ref, lb_ref, a_ref, m_ref, s0_ref, o_ref, sfin_ref, st_ref = refs
    else:
        z_ref, v_ref, lb_ref, a_ref, s0_ref, sfin_ref, st_ref = refs
    c = pl.program_id(1)

    @pl.when(c == 0)
    def _():
        st_ref[...] = s0_ref[0]

    lb = lb_ref[...]
    a = a_ref[...]
    tot_row = 0 if reverse else chunk - 1
    subs = list(range(n_sub - 1, -1, -1) if reverse else range(n_sub))

    def gates(sub):
        rows = slice(sub * chunk, (sub + 1) * chunk)
        z = z_ref[rows, :].astype(F32)
        f = lb + (1.0 - lb) * jax.nn.sigmoid(z)
        log_f = jnp.log(jnp.maximum(f, F_FLOOR))
        k_all = (1.0 - lb) * jax.nn.sigmoid(-z)
        expo = _dot(a, log_f.astype(BF16))
        return k_all, v_ref[rows, :].astype(F32), expo

    def products(sub, h, k_all, v_all, expo):
        rows = slice(sub * chunk, (sub + 1) * chunk)
        sl = slice(h * HEAD_DIM, (h + 1) * HEAD_DIM)
        k, v = k_all[:, sl], v_all[:, sl]
        e_rem = jnp.exp(expo[chunk:2 * chunk, sl])
        e_tot = jnp.exp(expo[tot_row:tot_row + 1, sl])
        st = st_ref[h]
        st_ref[h] = st * e_tot + _dot_tn(v.astype(BF16), (k * e_rem).astype(BF16))
        if mode == "state":
            return None
        q = q_ref[rows, sl].astype(F32)
        e_cum = jnp.exp(expo[0:chunk, sl])
        o_inter = _dot_nt((q * e_cum).astype(BF16), st.astype(BF16))
        parts = []
        for lvl in range(n_levels):
            e = jnp.exp(expo[(2 + lvl) * chunk:(3 + lvl) * chunk, sl])
            parts.append(_dot_nt((q * e).astype(BF16), (k * e).astype(BF16)))
        diag = jnp.sum(q * k, axis=-1, keepdims=True)
        return o_inter, parts, diag, v

    def readout(sub, h, o_inter, parts, diag, v):
        rows = slice(sub * chunk, (sub + 1) * chunk)
        sl = slice(h * HEAD_DIM, (h + 1) * HEAD_DIM)
        att = None
        for lvl, part in enumerate(parts):
            att = m_ref[lvl] * part if att is None else att + m_ref[lvl] * part
        o = o_inter + _dot(att.astype(BF16), v.astype(BF16)) + diag * v
        if mode == "final":
            y = _rms(o + ofw_ref[rows, sl], nw_ref[...]) * _silu(g_ref[rows, sl].astype(F32))
            o_ref[rows, sl] = y.astype(o_ref.dtype)
        else:
            o_ref[rows, sl] = o.astype(o_ref.dtype)

    tiles = [(sub, h) for sub in subs for h in range(N_HG)]
    cache = {}

    def stage(sub, h):
        if sub not in cache:
            cache.clear()
            cache[sub] = gates(sub)
        return products(sub, h, *cache[sub])

    queue = [stage(*tile) for tile in tiles[:TRACE_AHEAD]]
    for t, tile in enumerate(tiles):
        if t + TRACE_AHEAD < len(tiles):
            queue.append(stage(*tiles[t + TRACE_AHEAD]))
        done = queue.pop(0)
        if mode != "state":
            readout(*tile, *done)

    @pl.when(c == pl.num_programs(1) - 1)
    def _():
        sfin_ref[0] = st_ref[...]


def _hgrn_scan(u, lb, s0, seq_len, *, reverse, mode, o_fw=None, norm_w=None):
    rows = u.shape[0]
    batch = rows // seq_len
    blk = min(HG_BLOCK, seq_len)
    chunk = min(HG_CHUNK, blk)
    n_blocks = seq_len // blk
    mats, masks = _hgrn_structure(chunk, reverse)
    n_levels = masks.shape[0]
    if mode == "state":
        mats = mats[:2 * chunk]
    per_blk = HG_W // HEAD_DIM

    def tok(off):
        col = off // per_blk
        if reverse:
            return pl.BlockSpec((blk, HG_W), lambda b, c: (b * n_blocks + n_blocks - 1 - c, col))
        return pl.BlockSpec((blk, HG_W), lambda b, c: (b * n_blocks + c, col))

    const2 = lambda shape: pl.BlockSpec(shape, lambda b, c: (0, 0))
    state_spec = pl.BlockSpec((1, N_HG, HEAD_DIM, HEAD_DIM), lambda b, c: (b, 0, 0, 0))
    state_shape = jax.ShapeDtypeStruct((batch, N_HG, HEAD_DIM, HEAD_DIM), F32)
    z_spec = tok(OFF_FBW if reverse else OFF_FFW)
    lb2, mats_b = lb.reshape(1, HG_W), jnp.asarray(mats, BF16)
    if mode == "state":
        in_specs = [z_spec, tok(OFF_I), const2((1, HG_W)), const2(mats.shape), state_spec]
        args = [u, u, lb2, mats_b, s0]
        out_specs, out_shape = [state_spec], [state_shape]
    else:
        in_specs = [z_spec, tok(OFF_I), tok(OFF_HGQ), const2((1, HG_W)), const2(mats.shape),
                    pl.BlockSpec(masks.shape, lambda b, c: (0, 0, 0)), state_spec]
        args = [u, u, u, lb2, mats_b, jnp.asarray(masks, F32), s0]
        if mode == "final":
            in_specs += [tok(OFF_HGG), tok(0), const2((1, HEAD_DIM))]
            args += [u, o_fw, norm_w.reshape(1, HEAD_DIM)]
        out_specs = [tok(0), state_spec]
        out_shape = [jax.ShapeDtypeStruct((rows, HG_W), BF16 if mode == "final" else F32), state_shape]
    res = pl.pallas_call(
        functools.partial(_hgrn_kernel, chunk=chunk, n_sub=blk // chunk, n_levels=n_levels, reverse=reverse,
                          mode=mode),
        grid=(batch, n_blocks),
        in_specs=in_specs,
        out_specs=out_specs,
        out_shape=out_shape,
        scratch_shapes=[pltpu.VMEM((N_HG, HEAD_DIM, HEAD_DIM), F32)],
        compiler_params=_params("parallel", "arbitrary"),
    )(*args)
    return (None, res[0]) if mode == "state" else (res[0], res[1])


def _na_bias_table(rpb, rows):
    n_rb = rows // NA_WIN_R
    n_cb = GRID_W // NA_QBLK_C
    n_dr, n_dc = 2 * NA_WIN_R - 1, 2 * NA_WIN_C - 1
    n_k = NA_KEY_R * NA_KEY_C
    offs, row_mask = [], []
    for rb in (0, min(1, n_rb - 1), n_rb - 1):
        k_row0 = int(np.clip(rb * NA_WIN_R - NA_WIN_R // 2, 0, rows - NA_KEY_R))
        q_r = rb * NA_WIN_R + np.arange(NA_WIN_R)
        k_r = k_row0 + np.arange(NA_KEY_R)
        r_start = np.clip(q_r - NA_WIN_R // 2, 0, rows - NA_WIN_R)
        ok_r = (k_r[None, :] >= r_start[:, None]) & (k_r[None, :] < r_start[:, None] + NA_WIN_R)
        row_mask.append(np.repeat(np.where(ok_r, 0.0, NEG_INF), NA_KEY_C, axis=1))
        offs.append(k_row0 - rb * NA_WIN_R + NA_WIN_R - 1)
    row_mask = jnp.asarray(np.stack(row_mask), F32)
    pad_lo = NA_WIN_R - 1 - min(offs)
    n_a = NA_KEY_R - 1 + max(offs) + pad_lo + 1
    lanes = -(-(n_a * NA_KEY_C) // HEAD_DIM) * HEAD_DIM
    sel_c, ok_c = [], []
    for cb in range(n_cb):
        k_col0 = int(np.clip(cb * NA_QBLK_C - NA_WIN_C // 2, 0, GRID_W - NA_KEY_C))
        q_c = cb * NA_QBLK_C + np.arange(NA_QBLK_C)
        k_c = k_col0 + np.arange(NA_KEY_C)
        c_start = np.clip(q_c - NA_WIN_C // 2, 0, GRID_W - NA_WIN_C)
        ok_c.append((k_c[None, :] >= c_start[:, None]) & (k_c[None, :] < c_start[:, None] + NA_WIN_C))
        dc = k_c[None, :] - q_c[:, None] + NA_WIN_C - 1
        sel_c.append(dc[:, :, None] == np.arange(n_dc))
    sel_c = jnp.asarray(np.stack(sel_c), F32)
    ok_c = jnp.asarray(np.stack(ok_c))
    strip = jnp.einsum("hab,mcjb->hmcaj", rpb.astype(F32), sel_c, precision=lax.Precision.HIGHEST)
    strip = jnp.where(ok_c[None, :, :, None, :], strip, NEG_INF)
    strip = jnp.pad(strip, ((0, 0), (0, 0), (0, 0), (pad_lo, n_a - n_dr - pad_lo), (0, 0)))
    strip = strip.reshape(strip.shape[:3] + (n_a * NA_KEY_C,))
    strip = jnp.pad(strip, ((0, 0), (0, 0), (0, 0), (0, lanes - n_a * NA_KEY_C)))
    n_heads = rpb.shape[0]
    return pl.pallas_call(
        functools.partial(_na_bias_kernel, offs=tuple(offs), pad_lo=pad_lo),
        grid=(n_heads,),
        in_specs=[pl.BlockSpec((1, n_cb, NA_QBLK_C, lanes), lambda h: (h, 0, 0, 0)),
                  pl.BlockSpec((3, NA_WIN_R, n_k), lambda h: (0, 0, 0))],
        out_specs=pl.BlockSpec((1, 3, n_cb, NA_WIN_R * NA_QBLK_C, n_k), lambda h: (h, 0, 0, 0, 0)),
        out_shape=jax.ShapeDtypeStruct((n_heads, 3, n_cb, NA_WIN_R * NA_QBLK_C, n_k), F32),
        compiler_params=_params("parallel"),
    )(strip, row_mask)


def _na_bias_kernel(t_ref, rm_ref, o_ref, *, offs, pad_lo):
    n_cb = t_ref.shape[1]
    lanes = t_ref.shape[-1]
    per = HEAD_DIM // NA_KEY_C
    n_k = o_ref.shape[-1]
    for cb in range(n_cb):
        t = t_ref[0, cb]
        shifted = [t] + [pltpu.roll(t, lanes - NA_KEY_C * s, axis=1) for s in range(1, per)]
        for cls, off in enumerate(offs):
            for r in range(NA_WIN_R):
                m = off - r + pad_lo
                base = (m // per) * HEAD_DIM
                tile = shifted[m % per][:, base:base + n_k] + rm_ref[cls, r:r + 1, :]
                o_ref[0, cls, cb, r * NA_QBLK_C:(r + 1) * NA_QBLK_C, :] = tile


def _na_kernel(q_ref, k0_ref, k1_ref, k2_ref, k3_ref, v0_ref, v1_ref, v2_ref, v3_ref, kc_ref, vc_ref, bias_ref,
               ow_ref, o_ref):
    n_cb = GRID_W // NA_QBLK_C
    q_all = q_ref[...].astype(F32)
    k_all = jnp.concatenate([r[...] for r in (k0_ref, k1_ref, k2_ref, k3_ref)], axis=0).astype(F32)
    v_all = jnp.concatenate([r[...] for r in (v0_ref, v1_ref, v2_ref, v3_ref)], axis=0).astype(F32)
    def patch(x_all, hh, rows, col0, cols):
        x3 = x_all[:, hh * HEAD_DIM:(hh + 1) * HEAD_DIM].reshape(rows, GRID_W, HEAD_DIM)
        return x3[:, col0:col0 + cols, :].reshape(rows * cols, HEAD_DIM).astype(BF16)

    def scores(hh, cb):
        q0 = cb * NA_QBLK_C
        k_col0 = min(max(q0 - NA_WIN_C // 2, 0), GRID_W - NA_KEY_C)
        qm = patch(q_all, hh, NA_WIN_R, q0, NA_QBLK_C)
        km = patch(k_all, hh, NA_KEY_R, k_col0, NA_KEY_C)
        s_loc = _dot_nt(qm, km) + bias_ref[hh, 0, cb]
        s_ctx = _dot_nt(qm, kc_ref[:, hh * HEAD_DIM:(hh + 1) * HEAD_DIM])
        return s_loc, s_ctx

    def finish(hh, cb, s_loc, s_ctx):
        q0 = cb * NA_QBLK_C
        k_col0 = min(max(q0 - NA_WIN_C // 2, 0), GRID_W - NA_KEY_C)
        hs = slice(hh * HEAD_DIM, (hh + 1) * HEAD_DIM)
        vm = patch(v_all, hh, NA_KEY_R, k_col0, NA_KEY_C)
        mx = jnp.maximum(jnp.max(s_loc, axis=-1, keepdims=True), jnp.max(s_ctx, axis=-1, keepdims=True))
        p_loc = jnp.exp(s_loc - mx)
        p_ctx = jnp.exp(s_ctx - mx)
        den = jnp.sum(p_loc, axis=-1, keepdims=True) + jnp.sum(p_ctx, axis=-1, keepdims=True)
        o = (_dot(p_loc.astype(BF16), vm) + _dot(p_ctx.astype(BF16), vc_ref[:, hs])) / den
        o = _rms(o, ow_ref[hh]).astype(o_ref.dtype)
        for r in range(NA_WIN_R):
            o_ref[r * GRID_W + q0:r * GRID_W + q0 + NA_QBLK_C, hs] = o[r * NA_QBLK_C:(r + 1) * NA_QBLK_C, :]

    tiles = [(hh, cb) for hh in range(NA_HEADS) for cb in range(n_cb)]
    queue = [scores(*tile) for tile in tiles[:TRACE_AHEAD]]
    for t, tile in enumerate(tiles):
        if t + TRACE_AHEAD < len(tiles):
            queue.append(scores(*tiles[t + TRACE_AHEAD]))
        finish(*tile, *queue.pop(0))


def _neighbourhood_attention(u, u_ctx, bias, head0, out_norm, batch, rows, ctx_len):
    assert rows % NA_WIN_R == 0 and rows >= NA_KEY_R
    n_rb = rows // NA_WIN_R
    q_blk = NA_WIN_R * GRID_W
    k_blk = NA_KEY_R * GRID_W // 4
    k_per_batch = rows * GRID_W // k_blk

    width = NA_HEADS * HEAD_DIM
    assert all(off % NA_HEADS == 0 for off in (OFF_NAQ, OFF_NAK, OFF_NAV, head0, N_NA))
    col_q, col_k, col_v = OFF_NAQ // NA_HEADS, OFF_NAK // NA_HEADS, OFF_NAV // NA_HEADS

    def key_spec(col, part):
        def index(h, rb, b):
            first = jnp.clip(rb * NA_WIN_R - NA_WIN_R // 2, 0, rows - NA_KEY_R) * GRID_W // k_blk
            return (b * k_per_batch + first + part, col + h)
        return pl.BlockSpec((k_blk, width), index)

    def row_class(rb):
        return jnp.where(rb == 0, 0, jnp.where(rb == n_rb - 1, 2, 1))

    n_cb = GRID_W // NA_QBLK_C
    in_specs = [pl.BlockSpec((q_blk, width), lambda h, rb, b: (b * n_rb + rb, col_q + h))]
    in_specs += [key_spec(col_k, p) for p in range(4)] + [key_spec(col_v, p) for p in range(4)]
    in_specs += [pl.BlockSpec((ctx_len, width), lambda h, rb, b: (b, col_k + h)),
                 pl.BlockSpec((ctx_len, width), lambda h, rb, b: (b, col_v + h)),
                 pl.BlockSpec((NA_HEADS, 1, n_cb, NA_WIN_R * NA_QBLK_C, NA_KEY_R * NA_KEY_C),
                              lambda h, rb, b: (head0 // NA_HEADS + h, row_class(rb), 0, 0, 0)),
                 pl.BlockSpec((NA_HEADS, 1, HEAD_DIM), lambda h, rb, b: (h, 0, 0))]
    return pl.pallas_call(
        _na_kernel,
        grid=(N_NA // NA_HEADS, n_rb, batch),
        in_specs=in_specs,
        out_specs=pl.BlockSpec((q_blk, width), lambda h, rb, b: (b * n_rb + rb, h)),
        out_shape=jax.ShapeDtypeStruct((u.shape[0], NA_W), BF16),
        compiler_params=_params("parallel", "parallel", "parallel"),
    )(u, u, u, u, u, u, u, u, u, u_ctx, u_ctx, bias, out_norm.reshape(N_NA, 1, HEAD_DIM))


def _ctx_attn_kernel(q_ref, k_ref, v_ref, ow_ref, o_ref):
    s = _dot_nt(q_ref[...].astype(BF16), k_ref[...].astype(BF16))
    p = jnp.exp(s - jnp.max(s, axis=-1, keepdims=True))
    o = _dot(p.astype(BF16), v_ref[...].astype(BF16)) / jnp.sum(p, axis=-1, keepdims=True)
    o_ref[...] = _rms(o, ow_ref[0]).astype(o_ref.dtype)


def _context_attention(u_ctx, out_norm, batch, ctx_len):
    tok = lambda off: pl.BlockSpec((ctx_len, HEAD_DIM), lambda b, h: (b, off + h))
    return pl.pallas_call(
        _ctx_attn_kernel,
        grid=(batch, N_NA),
        in_specs=[tok(OFF_NAQ), tok(OFF_NAK), tok(OFF_NAV),
                  pl.BlockSpec((1, 1, HEAD_DIM), lambda b, h: (h, 0, 0))],
        out_specs=tok(0),
        out_shape=jax.ShapeDtypeStruct((u_ctx.shape[0], NA_W), BF16),
        compiler_params=_params("parallel", "parallel"),
    )(u_ctx, u_ctx, u_ctx, out_norm.reshape(N_NA, 1, HEAD_DIM))


def _row_tile(rows, target):
    t = min(rows, target)
    assert rows % t == 0
    return t


def _token_stream(xs, seq_len, mod_row, p, layer, w_up_b, ada_l, mixers):
    rows = xs.shape[0]
    sh2, sc2, g1, g2 = ada_l["sh2"], ada_l["sc2"], ada_l["g1"], ada_l["g2"]

    def mod_of_tile(tile_rows):
        per = max(seq_len // tile_rows, 1)
        return (lambda i: 1 + i // per) if mod_row is None else (lambda i: mod_row)

    tm_o = _row_tile(rows, 512)
    x1, h2 = _out_proj_norm(mixers, p["w_out"], layer, xs, g1, p["ln2"][layer], sc2, sh2, mod_of_tile(tm_o), tm_o)
    tm = _row_tile(rows, 1024)
    act = _ffn_up_conv_gate(h2, w_up_b, p["f_cw"][layer], p["f_cb"][layer], seq_len, tm, 512)
    return _matmul_residual([act], p["w_down"], layer, x1, g2, mod_of_tile(tm), tm, 512)


def kernel(x, c, ctx, c_ctx, w_ada, b_ada, ln1_w, ln2_w, w_in, hg_lb_logits, hg_norm_w, na_q_norm_w, na_k_norm_w,
           na_rpb, na_out_norm_w, cv_w, cv_out_norm_w, w_out, w_up, ffn_conv_w, ffn_conv_b, w_down):
    batch, seq, d = x.shape
    ctx_len = ctx.shape[1]
    rows = seq // GRID_W
    depth = w_ada.shape[0]
    xs = x.reshape(batch * seq, d)
    cs = ctx.reshape(batch * ctx_len, d)

    lb_sm = jax.nn.softmax(hg_lb_logits.astype(F32), axis=1)
    lb_all = jnp.cumsum(lb_sm, axis=1) - lb_sm[:, :1]

    cond = jnp.zeros((SUBLANE, d), F32).at[0].set(c_ctx).at[1:1 + batch].set(c)
    ada = _ada_table(cond, w_ada, b_ada).reshape(depth, SUBLANE, 6, 1, d)

    s_zero = jnp.zeros((batch, N_HG, HEAD_DIM, HEAD_DIM), F32)
    p = {"w_out": w_out, "w_down": w_down, "ln2": ln2_w, "f_cw": ffn_conv_w,
         "f_cb": ffn_conv_b}
    ctx_cols = sum(IN_SPLITS[:5])
    bias = _na_bias_table(na_rpb.reshape((depth * N_NA,) + na_rpb.shape[2:]), rows)
    for l in range(depth):
        last = l == depth - 1
        names = ("sh1", "sc1", "g1", "sh2", "sc2", "g2")
        ada_l = {n: ada[l, :, i] for i, n in enumerate(names)}
        tm = _row_tile(xs.shape[0], 1024)
        tm_c = _row_tile(cs.shape[0], 1024)
        tiles_per_seq = seq // tm
        lat_mod = lambda i: 1 + i // tiles_per_seq
        ctx_mod = lambda i: 0
        h = _norm_mod(xs, ln1_w[l], ada_l["sh1"], ada_l["sc1"], lat_mod, tm)
        hc = _norm_mod(cs, ln1_w[l], ada_l["sh1"], ada_l["sc1"], ctx_mod, tm_c)
        if (IN_W // 1024) * (xs.shape[0] // tm) >= w_up.shape[2] // HEAD_DIM:
            u, w_up_b = _in_proj(h, w_in, l, na_q_norm_w[l], na_k_norm_w[l], IN_W, tm, 1024, side_w=w_up)
        else:
            u = _in_proj(h, w_in, l, na_q_norm_w[l], na_k_norm_w[l], IN_W, tm, 1024)
            w_up_b = w_up[l].astype(BF16)
        uc = _in_proj(hc, w_in, l, na_q_norm_w[l], na_k_norm_w[l], ctx_cols if last else IN_W, tm_c, 512)

        if last:
            _, s_fw = _hgrn_scan(uc, lb_all[0, l], s_zero, ctx_len, reverse=False, mode="state")
            _, s_bw = _hgrn_scan(uc, lb_all[1, l], s_zero, ctx_len, reverse=True, mode="state")
        else:
            co_fw, s_fw = _hgrn_scan(uc, lb_all[0, l], s_zero, ctx_len, reverse=False, mode="raw")
            hg_c, s_bw = _hgrn_scan(uc, lb_all[1, l], s_zero, ctx_len, reverse=True, mode="final", o_fw=co_fw,
                                    norm_w=hg_norm_w[l])
        o_fw, _ = _hgrn_scan(u, lb_all[0, l], s_fw, seq, reverse=False, mode="raw")
        hg_out, _ = _hgrn_scan(u, lb_all[1, l], s_bw, seq, reverse=True, mode="final", o_fw=o_fw,
                               norm_w=hg_norm_w[l])

        na_out = _neighbourhood_attention(u, uc, bias, l * N_NA, na_out_norm_w[l], batch, rows, ctx_len)
        cv_out = _short_conv(u, cv_w[l], cv_out_norm_w[l], seq, _row_tile(seq, 256))

        xs = _token_stream(xs, seq, None, p, l, w_up_b, ada_l, [hg_out, na_out, cv_out])
        if not last:
            na_c = _context_attention(uc, na_out_norm_w[l], batch, ctx_len)
            cv_c = _short_conv(uc, cv_w[l], cv_out_norm_w[l], ctx_len, _row_tile(ctx_len, 256))
            cs = _token_stream(cs, ctx_len, 0, p, l, w_up_b, ada_l, [hg_c, na_c, cv_c])
    return xs.reshape(batch, seq, d)
```

```python
import functools

import numpy as np
import jax
import jax.numpy as jnp
from jax import lax
from jax.experimental import pallas as pl
from jax.experimental.pallas import tpu as pltpu

F32 = jnp.float32
BF16 = jnp.bfloat16

D_MODEL = 2048
DEPTH = 2
GRID_W = 64
HEAD_DIM = 128
N_HG = 4
N_NA = 8
N_CV = 4
HG_W = N_HG * HEAD_DIM
NA_W = N_NA * HEAD_DIM
CV_W = N_CV * HEAD_DIM
IN_SPLITS = (HG_W, HG_W, HG_W, NA_W, NA_W, HG_W, HG_W, NA_W, CV_W, CV_W, CV_W)
IN_W = sum(IN_SPLITS)
(OFF_FFW, OFF_FBW, OFF_I, OFF_NAK, OFF_NAV, OFF_HGQ, OFF_HGG, OFF_NAQ, OFF_CVB, OFF_CVC,
 OFF_CVV) = [int(v) // HEAD_DIM for v in np.cumsum((0,) + IN_SPLITS[:-1])]
NA_WIN_R = 8
NA_WIN_C = 16
NA_QBLK_C = 16
NA_KEY_C = 32
NA_KEY_R = 16
NA_HEADS = 4
CONV_W = 3
D_FF = 5632
EPS = 1e-6
F_FLOOR = 1e-30
ATTN_SCALE = HEAD_DIM ** -0.5
NEG_INF = -1e30

HG_CHUNK = 128
HG_BLOCK = 512
SUBLANE = 8
HALO = 16
MXU_N = 256
TRACE_AHEAD = 3
VMEM_LIMIT = 56 * 1024 * 1024


def _params(*sem):
    return pltpu.CompilerParams(dimension_semantics=sem, vmem_limit_bytes=VMEM_LIMIT)


def _dot(a, b):
    return jnp.dot(a, b, preferred_element_type=F32)


def _dot_nt(a, b):
    return lax.dot_general(a, b, (((1,), (1,)), ((), ())), preferred_element_type=F32)


def _dot_tn(a, b):
    return lax.dot_general(a, b, (((0,), (0,)), ((), ())), preferred_element_type=F32)


def _rms(x, w):
    return x * lax.rsqrt(jnp.mean(x * x, axis=-1, keepdims=True) + EPS) * w


def _silu(x):
    half = 0.5 * x
    return half + half * jnp.tanh(half)


def _ada_kernel(s_ref, w_ref, b_ref, o_ref):
    s = _silu(s_ref[...]).astype(BF16)
    o_ref[0] = _dot(s, w_ref[0].astype(BF16)) + b_ref[0]


def _ada_table(cond, w_ada, b_ada):
    depth, d, n = w_ada.shape
    tn = 1024
    return pl.pallas_call(
        _ada_kernel,
        grid=(depth, n // tn),
        in_specs=[pl.BlockSpec((SUBLANE, d), lambda l, j: (0, 0)),
                  pl.BlockSpec((1, d, tn), lambda l, j: (l, 0, j)),
                  pl.BlockSpec((1, 1, tn), lambda l, j: (l, 0, j))],
        out_specs=pl.BlockSpec((1, SUBLANE, tn), lambda l, j: (l, 0, j)),
        out_shape=jax.ShapeDtypeStruct((depth, SUBLANE, n), F32),
        compiler_params=_params("parallel", "parallel"),
    )(cond, w_ada, b_ada.reshape(depth, 1, n))


def _norm_mod_kernel(x_ref, w_ref, sh_ref, sc_ref, o_ref):
    y = _rms(x_ref[...], w_ref[...])
    o_ref[...] = (y * (1.0 + sc_ref[0]) + sh_ref[0]).astype(o_ref.dtype)


def _norm_mod(x, w, shift, scale, mod_of_tile, tr):
    rows, d = x.shape
    mod = lambda i: (mod_of_tile(i), 0, 0)
    return pl.pallas_call(
        _norm_mod_kernel,
        grid=(rows // tr,),
        in_specs=[pl.BlockSpec((tr, d), lambda i: (i, 0)),
                  pl.BlockSpec((1, d), lambda i: (0, 0)),
                  pl.BlockSpec((1, 1, d), mod),
                  pl.BlockSpec((1, 1, d), mod)],
        out_specs=pl.BlockSpec((tr, d), lambda i: (i, 0)),
        out_shape=jax.ShapeDtypeStruct((rows, d), BF16),
        compiler_params=_params("parallel"),
    )(x, w.reshape(1, d), shift, scale)


def _in_proj_kinds(n_cols, tn):
    kinds = []
    for head in range(n_cols // HEAD_DIM):
        if OFF_NAK <= head < OFF_NAV:
            kinds.append("k")
        elif OFF_NAQ <= head < OFF_CVB:
            kinds.append("q")
        elif OFF_HGQ <= head < OFF_HGG:
            kinds.append("silu")
        else:
            kinds.append("id")
    per = tn // HEAD_DIM
    return tuple(tuple(kinds[t * per:(t + 1) * per]) for t in range(n_cols // tn))


def _in_proj_kernel(*refs, kinds, with_cast):
    if with_cast:
        a_ref, w_ref, qw_ref, kw_ref, side_ref, o_ref, side_o_ref, wb_ref = refs
        side_o_ref[...] = side_ref[...].astype(BF16)
    else:
        a_ref, w_ref, qw_ref, kw_ref, o_ref, wb_ref = refs
    j = pl.program_id(0)

    @pl.when(pl.program_id(1) == 0)
    def _():
        wb_ref[...] = w_ref[...].astype(BF16)

    def transform(y, kind):
        if kind == "k":
            return _rms(y, kw_ref[...])
        if kind == "q":
            return _rms(y, qw_ref[...]) * ATTN_SCALE
        if kind == "silu":
            return _silu(y)
        return y

    per_slab = MXU_N // HEAD_DIM
    for pattern in sorted(set(kinds)):
        hit = functools.reduce(jnp.logical_or, [j == jj for jj, tk in enumerate(kinds) if tk == pattern])

        @pl.when(hit)
        def _(pattern=pattern):
            a = a_ref[...]
            for c in range(len(pattern) // per_slab):
                cols = slice(c * MXU_N, (c + 1) * MXU_N)
                acc = _dot(a, wb_ref[:, cols])
                parts = [transform(acc[:, s * HEAD_DIM:(s + 1) * HEAD_DIM], pattern[c * per_slab + s])
                         for s in range(per_slab)]
                o_ref[:, cols] = jnp.concatenate(parts, axis=1).astype(o_ref.dtype)


def _in_proj(h, w_in, layer, q_norm, k_norm, n_cols, tm, tn, side_w=None):
    m, k = h.shape
    kinds = _in_proj_kinds(n_cols, tn)
    nj, ni = n_cols // tn, m // tm
    vec = pl.BlockSpec((1, HEAD_DIM), lambda j, i: (0, 0))
    in_specs = [pl.BlockSpec((tm, k), lambda j, i: (i, 0)),
                pl.BlockSpec((None, k, tn), lambda j, i: (layer, 0, j)),
                vec, vec]
    args = [h, w_in, q_norm.reshape(1, HEAD_DIM), k_norm.reshape(1, HEAD_DIM)]
    out_specs = [pl.BlockSpec((tm, tn), lambda j, i: (i, j))]
    out_shape = [jax.ShapeDtypeStruct((m, n_cols), BF16)]
    if side_w is not None:
        _, k2, n2 = side_w.shape
        strips = n2 // HEAD_DIM
        assert strips <= nj * ni
        strip = lambda j, i: jnp.minimum(j * ni + i, strips - 1)
        in_specs.append(pl.BlockSpec((None, k2, HEAD_DIM), lambda j, i: (layer, 0, strip(j, i))))
        args.append(side_w)
        out_specs.append(pl.BlockSpec((k2, HEAD_DIM), lambda j, i: (0, strip(j, i))))
        out_shape.append(jax.ShapeDtypeStruct((k2, n2), BF16))
    res = pl.pallas_call(
        functools.partial(_in_proj_kernel, kinds=kinds, with_cast=side_w is not None),
        grid=(nj, ni),
        in_specs=in_specs,
        out_specs=out_specs,
        out_shape=out_shape,
        scratch_shapes=[pltpu.VMEM((k, tn), BF16)],
        compiler_params=_params("arbitrary", "arbitrary"),
    )(*args)
    return res if side_w is not None else res[0]


def _mm_res_kernel(*refs, splits):
    n_a = len(splits)
    a_refs = refs[:n_a]
    w_ref, x_ref, g_ref, o_ref, wb_ref = refs[n_a:]

    @pl.when(pl.program_id(1) == 0)
    def _():
        wb_ref[...] = w_ref[...].astype(BF16)

    acc = None
    off = 0
    for a_ref, width in zip(a_refs, splits):
        part = _dot(a_ref[...], wb_ref[off:off + width, :])
        acc = part if acc is None else acc + part
        off += width
    o_ref[...] = x_ref[...] + g_ref[0] * acc


def _matmul_residual(a_list, w, layer, x, gate, mod_of_tile, tm, tn):
    m = x.shape[0]
    _, k, n = w.shape
    splits = tuple(a.shape[1] for a in a_list)
    assert sum(splits) == k
    in_specs = [pl.BlockSpec((tm, s), lambda j, i: (i, 0)) for s in splits]
    in_specs += [pl.BlockSpec((None, k, tn), lambda j, i: (layer, 0, j), pipeline_mode=pl.Buffered(1)),
                 pl.BlockSpec((tm, tn), lambda j, i: (i, j)),
                 pl.BlockSpec((1, 1, tn), lambda j, i: (mod_of_tile(i), 0, j))]
    return pl.pallas_call(
        functools.partial(_mm_res_kernel, splits=splits),
        grid=(n // tn, m // tm),
        in_specs=in_specs,
        out_specs=pl.BlockSpec((tm, tn), lambda j, i: (i, j)),
        out_shape=jax.ShapeDtypeStruct((m, n), F32),
        scratch_shapes=[pltpu.VMEM((k, tn), BF16)],
        compiler_params=_params("parallel", "arbitrary"),
    )(*a_list, w, x, gate)


def _proj_norm_kernel(*refs, splits, cast_weight, with_norm):
    n_a = len(splits)
    a_refs, rest = refs[:n_a], list(refs[n_a:])
    w_ref, x_ref, g_ref = rest[:3]
    rest = rest[3:]
    if with_norm:
        lw_ref, sc_ref, sh_ref, x1_ref, h2_ref = rest[:5]
        rest = rest[5:]
    else:
        x1_ref = rest.pop(0)
    if cast_weight:
        wb_ref = rest.pop(0)

        @pl.when(pl.program_id(0) == 0)
        def _():
            wb_ref[...] = w_ref[...].astype(BF16)
    else:
        wb_ref = w_ref

    tm, n = x1_ref.shape
    ssq = jnp.zeros((tm, HEAD_DIM), F32)
    for c in range(n // MXU_N):
        cols = slice(c * MXU_N, (c + 1) * MXU_N)
        acc = None
        off = 0
        for a_ref, width in zip(a_refs, splits):
            part = _dot(a_ref[...], wb_ref[off:off + width, cols])
            acc = part if acc is None else acc + part
            off += width
        x1 = x_ref[:, cols] + g_ref[0][:, cols] * acc
        x1_ref[:, cols] = x1
        if with_norm:
            sq = x1 * x1
            for s in range(MXU_N // HEAD_DIM):
                ssq = ssq + sq[:, s * HEAD_DIM:(s + 1) * HEAD_DIM]
    if with_norm:
        inv = lax.rsqrt(jnp.sum(ssq, axis=-1, keepdims=True) * (1.0 / n) + EPS)
        gain = lw_ref[...] * (1.0 + sc_ref[0])
        for c in range(n // MXU_N):
            cols = slice(c * MXU_N, (c + 1) * MXU_N)
            h2_ref[:, cols] = (x1_ref[:, cols] * inv * gain[:, cols] + sh_ref[0][:, cols]).astype(h2_ref.dtype)


def _proj_norm(a_list, w, layer, x, gate, mod_of_tile, tm, norm=None):
    m, n = x.shape
    cast_weight = w.ndim == 3
    k = w.shape[-2]
    splits = tuple(a.shape[1] for a in a_list)
    assert sum(splits) == k
    in_specs = [pl.BlockSpec((tm, s), lambda i: (i, 0)) for s in splits]
    if cast_weight:
        in_specs.append(pl.BlockSpec((None, k, n), lambda i: (layer, 0, 0), pipeline_mode=pl.Buffered(1)))
    else:
        in_specs.append(pl.BlockSpec((k, n), lambda i: (0, 0), pipeline_mode=pl.Buffered(1)))
    in_specs += [pl.BlockSpec((tm, n), lambda i: (i, 0)),
                 pl.BlockSpec((1, 1, n), lambda i: (mod_of_tile(i), 0, 0))]
    args = [*a_list, w, x, gate]
    out_specs = [pl.BlockSpec((tm, n), lambda i: (i, 0))]
    out_shape = [jax.ShapeDtypeStruct((m, n), F32)]
    if norm is not None:
        ln_w, scale, shift, norm_mod = norm
        mod = lambda i: (norm_mod(i), 0, 0)
        in_specs += [pl.BlockSpec((1, n), lambda i: (0, 0)), pl.BlockSpec((1, 1, n), mod),
                     pl.BlockSpec((1, 1, n), mod)]
        args += [ln_w.reshape(1, n), scale, shift]
        out_specs.append(pl.BlockSpec((tm, n), lambda i: (i, 0)))
        out_shape.append(jax.ShapeDtypeStruct((m, n), BF16))
    res = pl.pallas_call(
        functools.partial(_proj_norm_kernel, splits=splits, cast_weight=cast_weight, with_norm=norm is not None),
        grid=(m // tm,),
        in_specs=in_specs,
        out_specs=out_specs,
        out_shape=out_shape,
        scratch_shapes=[pltpu.VMEM((k, n), BF16)] if cast_weight else [],
        compiler_params=_params("arbitrary"),
    )(*args)
    return res if norm is not None else res[0]


def _shift_rows(x, prev_row, next_row):
    r = x.shape[0]
    idx = lax.broadcasted_iota(jnp.int32, x.shape, 0)
    down = jnp.where(idx == 0, prev_row, pltpu.roll(x, 1, axis=0))
    up = jnp.where(idx == r - 1, next_row, pltpu.roll(x, r - 1, axis=0))
    return down, up


def _halo_specs(tr, tc, col_of, tiles_per_seq, n_rows):
    per = tr // HALO
    last_blk = n_rows // HALO - 1
    cur = pl.BlockSpec((tr, tc), lambda i, j: (i, col_of(j)))
    prev = pl.BlockSpec((HALO, tc), lambda i, j: (jnp.maximum(i * per - 1, 0), col_of(j)))
    nxt = pl.BlockSpec((HALO, tc), lambda i, j: (jnp.minimum((i + 1) * per, last_blk), col_of(j)))
    return cur, prev, nxt


def _seq_edges(tiles_per_seq):
    i = pl.program_id(0)
    pos = i % tiles_per_seq
    return (pos != 0).astype(F32), (pos != tiles_per_seq - 1).astype(F32)


def _ffn_up_kernel(*refs, tm, seq_len, with_cast):
    if with_cast:
        (h_ref, hp_ref, hn_ref, wg_ref, wv_ref, cg_ref, cv_ref, bg_ref, bv_ref, side_ref,
         o_ref, side_o_ref, lhs_ref) = refs
        side_o_ref[...] = side_ref[...].astype(BF16)
    else:
        h_ref, hp_ref, hn_ref, wg_ref, wv_ref, cg_ref, cv_ref, bg_ref, bv_ref, o_ref, lhs_ref = refs
    tiles_per_seq = max(seq_len // tm, 1)
    pos = pl.program_id(0) % tiles_per_seq
    has_prev = pos != 0
    has_next = pos != tiles_per_seq - 1

    @pl.when(pl.program_id(1) == 0)
    def _():
        lhs_ref[0:HALO, :] = jnp.where(has_prev, hp_ref[...], jnp.zeros_like(hp_ref))
        lhs_ref[HALO:HALO + tm, :] = h_ref[...]
        lhs_ref[HALO + tm:, :] = jnp.where(has_next, hn_ref[...], jnp.zeros_like(hn_ref))

    lhs = lhs_ref[...]

    def branch(w_ref, c_ref, b_ref):
        acc = _dot(lhs, w_ref[...])
        r = acc.shape[0]
        down, up = pltpu.roll(acc, 1, axis=0), pltpu.roll(acc, r - 1, axis=0)
        if seq_len < tm:
            tok = (lax.broadcasted_iota(jnp.int32, acc.shape, 0) + (seq_len - HALO)) % seq_len
            down = jnp.where(tok == 0, 0.0, down)
            up = jnp.where(tok == seq_len - 1, 0.0, up)
        y = down * c_ref[0:1, :] + acc * c_ref[1:2, :] + up * c_ref[2:3, :]
        return y[HALO:HALO + tm, :] + b_ref[...]

    gate = branch(wg_ref, cg_ref, bg_ref)
    val = branch(wv_ref, cv_ref, bv_ref)
    o_ref[...] = (_silu(gate) * val).astype(o_ref.dtype)


def _ffn_up_conv_gate(h, w_up, cw, cb, seq_len, tm, tn, side_w=None, layer=0):
    rows, k = h.shape
    assert seq_len % tm == 0 or (tm % seq_len == 0 and seq_len >= HALO)
    nj = D_FF // tn
    per = tm // HALO
    last_blk = rows // HALO - 1
    cb2 = cb.reshape(1, 2 * D_FF)
    steps = (rows // tm) * nj
    if side_w is not None:
        _, k2, n2 = side_w.shape
        strips = max(d for d in range(1, steps + 1) if k2 % d == 0 and (k2 // d) % HALO == 0)
        strip = lambda i, j: jnp.minimum(i * nj + j, strips - 1)
        side_in = [pl.BlockSpec((None, k2 // strips, n2), lambda i, j: (layer, strip(i, j), 0))]
        side_out = [pl.BlockSpec((k2 // strips, n2), lambda i, j: (strip(i, j), 0))]
        side_shape = [jax.ShapeDtypeStruct((k2, n2), BF16)]
        side_args = [side_w]
    else:
        side_in, side_out, side_shape, side_args = [], [], [], []
    in_specs = [pl.BlockSpec((tm, k), lambda i, j: (i, 0)),
                pl.BlockSpec((HALO, k), lambda i, j: (jnp.maximum(i * per - 1, 0), 0)),
                pl.BlockSpec((HALO, k), lambda i, j: (jnp.minimum((i + 1) * per, last_blk), 0)),
                pl.BlockSpec((k, tn), lambda i, j: (0, j)),
                pl.BlockSpec((k, tn), lambda i, j: (0, j + nj)),
                pl.BlockSpec((CONV_W, tn), lambda i, j: (0, j)),
                pl.BlockSpec((CONV_W, tn), lambda i, j: (0, j + nj)),
                pl.BlockSpec((1, tn), lambda i, j: (0, j)),
                pl.BlockSpec((1, tn), lambda i, j: (0, j + nj))]
    res = pl.pallas_call(
        functools.partial(_ffn_up_kernel, tm=tm, seq_len=seq_len, with_cast=side_w is not None),
        grid=(rows // tm, nj),
        in_specs=in_specs + side_in,
        out_specs=[pl.BlockSpec((tm, tn), lambda i, j: (i, j))] + side_out,
        out_shape=[jax.ShapeDtypeStruct((rows, D_FF), BF16)] + side_shape,
        scratch_shapes=[pltpu.VMEM((tm + 2 * HALO, k), BF16)],
        compiler_params=_params("arbitrary", "arbitrary"),
    )(h, h, h, w_up, w_up, cw, cw, cb2, cb2, *side_args)
    return res if side_w is not None else res[0]


def _short_conv_kernel(b_ref, c_ref, cp_ref, cn_ref, v_ref, vp_ref, vn_ref, w_ref, nw_ref, o_ref, *, tiles_per_seq):
    has_prev, has_next = _seq_edges(tiles_per_seq)
    p = c_ref[...].astype(F32) * v_ref[...].astype(F32)
    prev_row = cp_ref[HALO - 1:HALO, :].astype(F32) * vp_ref[HALO - 1:HALO, :].astype(F32) * has_prev
    next_row = cn_ref[0:1, :].astype(F32) * vn_ref[0:1, :].astype(F32) * has_next
    down, up = _shift_rows(p, prev_row, next_row)
    y = b_ref[...].astype(F32) * (down * w_ref[0:1, :] + p * w_ref[1:2, :] + up * w_ref[2:3, :])
    for h in range(N_CV):
        sl = slice(h * HEAD_DIM, (h + 1) * HEAD_DIM)
        o_ref[:, sl] = _rms(y[:, sl], nw_ref[:, sl]).astype(o_ref.dtype)


def _short_conv(u, cv_w, cv_onorm, seq_len, tr):
    rows = u.shape[0]
    tiles_per_seq = seq_len // tr
    blk = lambda off: (lambda j: off * HEAD_DIM // CV_W)
    b_spec = pl.BlockSpec((tr, CV_W), lambda i, j: (i, OFF_CVB * HEAD_DIM // CV_W))
    c_specs = _halo_specs(tr, CV_W, blk(OFF_CVC), tiles_per_seq, rows)
    v_specs = _halo_specs(tr, CV_W, blk(OFF_CVV), tiles_per_seq, rows)
    return pl.pallas_call(
        functools.partial(_short_conv_kernel, tiles_per_seq=tiles_per_seq),
        grid=(rows // tr, 1),
        in_specs=[b_spec, *c_specs, *v_specs,
                  pl.BlockSpec((CONV_W, CV_W), lambda i, j: (0, 0)),
                  pl.BlockSpec((1, CV_W), lambda i, j: (0, 0))],
        out_specs=pl.BlockSpec((tr, CV_W), lambda i, j: (i, 0)),
        out_shape=jax.ShapeDtypeStruct((rows, CV_W), BF16),
        compiler_params=_params("parallel", "arbitrary"),
    )(u, u, u, u, u, u, u, cv_w, cv_onorm.reshape(1, CV_W))


def _hgrn_structure(chunk, reverse):
    idx = np.arange(chunk)
    i, t = idx[:, None], idx[None, :]
    mats = [t <= i, t > i]
    masks = []
    s = chunk // 2
    while s >= 1:
        blk = idx // (2 * s)
        upper = (idx % (2 * s)) >= s
        mid = blk * 2 * s + s - 1
        a = np.where(upper[:, None], (t > mid[:, None]) & (t <= i), (t > i) & (t <= mid[:, None]))
        mats.append(a)
        masks.append((blk[:, None] == blk[None, :]) & upper[:, None] & ~upper[None, :])
        s //= 2
    mats = np.stack(mats).astype(np.float32)
    masks = np.stack(masks).astype(np.float32)
    if reverse:
        mats = mats[:, ::-1, ::-1]
        masks = masks[:, ::-1, ::-1]
    return mats.reshape(-1, chunk), masks


def _hgrn_kernel(*refs, chunk, n_sub, n_levels, reverse, mode):
    if mode == "final":
        (z_ref, v_ref, q_ref, lb_ref, a_ref, m_ref, s0_ref, g_ref, ofw_ref, nw_ref,
         o_ref, sfin_ref, st_ref) = refs
    elif mode == "raw":
        z_ref, v_ref, q_ref, lb_ref, a_ref, m_ref, s0_ref, o_ref, sfin_ref, st_ref = refs
    else:
        z_ref, v_ref, lb_ref, a_ref, s0_ref, sfin_ref, st_ref = refs
    c = pl.program_id(1)

    @pl.when(c == 0)
    def _():
        st_ref[...] = s0_ref[0]

    lb = lb_ref[...]
    a = a_ref[...]
    tot_row = 0 if reverse else chunk - 1
    subs = list(range(n_sub - 1, -1, -1) if reverse else range(n_sub))

    def gates(sub):
        rows = slice(sub * chunk, (sub + 1) * chunk)
        z = z_ref[rows, :].astype(F32)
        f = lb + (1.0 - lb) * jax.nn.sigmoid(z)
        log_f = jnp.log(jnp.maximum(f, F_FLOOR))
        k_all = (1.0 - lb) * jax.nn.sigmoid(-z)
        expo = _dot(a, log_f.astype(BF16))
        return k_all, v_ref[rows, :].astype(F32), expo

    def products(sub, h, k_all, v_all, expo):
        rows = slice(sub * chunk, (sub + 1) * chunk)
        sl = slice(h * HEAD_DIM, (h + 1) * HEAD_DIM)
        k, v = k_all[:, sl], v_all[:, sl]
        e_rem = jnp.exp(expo[chunk:2 * chunk, sl])
        e_tot = jnp.exp(expo[tot_row:tot_row + 1, sl])
        st = st_ref[h]
        st_ref[h] = st * e_tot + _dot_tn(v.astype(BF16), (k * e_rem).astype(BF16))
        if mode == "state":
            return None
        q = q_ref[rows, sl].astype(F32)
        e_cum = jnp.exp(expo[0:chunk, sl])
        o_inter = _dot_nt((q * e_cum).astype(BF16), st.astype(BF16))
        parts = []
        for lvl in range(n_levels):
            e = jnp.exp(expo[(2 + lvl) * chunk:(3 + lvl) * chunk, sl])
            parts.append(_dot_nt((q * e).astype(BF16), (k * e).astype(BF16)))
        diag = jnp.sum(q * k, axis=-1, keepdims=True)
        return o_inter, parts, diag, v

    def readout(sub, h, o_inter, parts, diag, v):
        rows = slice(sub * chunk, (sub + 1) * chunk)
        sl = slice(h * HEAD_DIM, (h + 1) * HEAD_DIM)
        att = None
        for lvl, part in enumerate(parts):
            att = m_ref[lvl] * part if att is None else att + m_ref[lvl] * part
        o = o_inter + _dot(att.astype(BF16), v.astype(BF16)) + diag * v
        if mode == "final":
            y = _rms(o + ofw_ref[rows, sl], nw_ref[...]) * _silu(g_ref[rows, sl].astype(F32))
            o_ref[rows, sl] = y.astype(o_ref.dtype)
        else:
            o_ref[rows, sl] = o.astype(o_ref.dtype)

    tiles = [(sub, h) for sub in subs for h in range(N_HG)]
    cache = {}

    def stage(sub, h):
        if sub not in cache:
            cache.clear()
            cache[sub] = gates(sub)
        return products(sub, h, *cache[sub])

    queue = [stage(*tile) for tile in tiles[:TRACE_AHEAD]]
    for t, tile in enumerate(tiles):
        if t + TRACE_AHEAD < len(tiles):
            queue.append(stage(*tiles[t + TRACE_AHEAD]))
        done = queue.pop(0)
        if mode != "state":
            readout(*tile, *done)

    @pl.when(c == pl.num_programs(1) - 1)
    def _():
        sfin_ref[0] = st_ref[...]


def _hgrn_scan(u, lb, s0, seq_len, *, reverse, mode, o_fw=None, norm_w=None):
    rows = u.shape[0]
    batch = rows // seq_len
    blk = min(HG_BLOCK, seq_len)
    chunk = min(HG_CHUNK, blk)
    n_blocks = seq_len // blk
    mats, masks = _hgrn_structure(chunk, reverse)
    n_levels = masks.shape[0]
    if mode == "state":
        mats = mats[:2 * chunk]
    per_blk = HG_W // HEAD_DIM

    def tok(off):
        col = off // per_blk
        if reverse:
            return pl.BlockSpec((blk, HG_W), lambda b, c: (b * n_blocks + n_blocks - 1 - c, col))
        return pl.BlockSpec((blk, HG_W), lambda b, c: (b * n_blocks + c, col))

    const2 = lambda shape: pl.BlockSpec(shape, lambda b, c: (0, 0))
    state_spec = pl.BlockSpec((1, N_HG, HEAD_DIM, HEAD_DIM), lambda b, c: (b, 0, 0, 0))
    state_shape = jax.ShapeDtypeStruct((batch, N_HG, HEAD_DIM, HEAD_DIM), F32)
    z_spec = tok(OFF_FBW if reverse else OFF_FFW)
    lb2, mats_b = lb.reshape(1, HG_W), jnp.asarray(mats, BF16)
    if mode == "state":
        in_specs = [z_spec, tok(OFF_I), const2((1, HG_W)), const2(mats.shape), state_spec]
        args = [u, u, lb2, mats_b, s0]
        out_specs, out_shape = [state_spec], [state_shape]
    else:
        in_specs = [z_spec, tok(OFF_I), tok(OFF_HGQ), const2((1, HG_W)), const2(mats.shape),
                    pl.BlockSpec(masks.shape, lambda b, c: (0, 0, 0)), state_spec]
        args = [u, u, u, lb2, mats_b, jnp.asarray(masks, F32), s0]
        if mode == "final":
            in_specs += [tok(OFF_HGG), tok(0), const2((1, HEAD_DIM))]
            args += [u, o_fw, norm_w.reshape(1, HEAD_DIM)]
        out_specs = [tok(0), state_spec]
        out_shape = [jax.ShapeDtypeStruct((rows, HG_W), BF16 if mode == "final" else F32), state_shape]
    res = pl.pallas_call(
        functools.partial(_hgrn_kernel, chunk=chunk, n_sub=blk // chunk, n_levels=n_levels, reverse=reverse,
                          mode=mode),
        grid=(batch, n_blocks),
        in_specs=in_specs,
        out_specs=out_specs,
        out_shape=out_shape,
        scratch_shapes=[pltpu.VMEM((N_HG, HEAD_DIM, HEAD_DIM), F32)],
        compiler_params=_params("parallel", "arbitrary"),
    )(*args)
    return (None, res[0]) if mode == "state" else (res[0], res[1])


def _na_bias_table(rpb, rows):
    n_rb = rows // NA_WIN_R
    n_cb = GRID_W // NA_QBLK_C
    n_dr, n_dc = 2 * NA_WIN_R - 1, 2 * NA_WIN_C - 1
    n_k = NA_KEY_R * NA_KEY_C
    offs, row_mask = [], []
    for rb in (0, min(1, n_rb - 1), n_rb - 1):
        k_row0 = int(np.clip(rb * NA_WIN_R - NA_WIN_R // 2, 0, rows - NA_KEY_R))
        q_r = rb * NA_WIN_R + np.arange(NA_WIN_R)
        k_r = k_row0 + np.arange(NA_KEY_R)
        r_start = np.clip(q_r - NA_WIN_R // 2, 0, rows - NA_WIN_R)
        ok_r = (k_r[None, :] >= r_start[:, None]) & (k_r[None, :] < r_start[:, None] + NA_WIN_R)
        row_mask.append(np.repeat(np.where(ok_r, 0.0, NEG_INF), NA_KEY_C, axis=1))
        offs.append(k_row0 - rb * NA_WIN_R + NA_WIN_R - 1)
    row_mask = jnp.asarray(np.stack(row_mask), F32)
    pad_lo = NA_WIN_R - 1 - min(offs)
    n_a = NA_KEY_R - 1 + max(offs) + pad_lo + 1
    lanes = -(-(n_a * NA_KEY_C) // HEAD_DIM) * HEAD_DIM
    sel_c, ok_c = [], []
    for cb in range(n_cb):
        k_col0 = int(np.clip(cb * NA_QBLK_C - NA_WIN_C // 2, 0, GRID_W - NA_KEY_C))
        q_c = cb * NA_QBLK_C + np.arange(NA_QBLK_C)
        k_c = k_col0 + np.arange(NA_KEY_C)
        c_start = np.clip(q_c - NA_WIN_C // 2, 0, GRID_W - NA_WIN_C)
        ok_c.append((k_c[None, :] >= c_start[:, None]) & (k_c[None, :] < c_start[:, None] + NA_WIN_C))
        dc = k_c[None, :] - q_c[:, None] + NA_WIN_C - 1
        sel_c.append(dc[:, :, None] == np.arange(n_dc))
    sel_c = jnp.asarray(np.stack(sel_c), F32)
    ok_c = jnp.asarray(np.stack(ok_c))
    strip = jnp.einsum("hab,mcjb->hmcaj", rpb.astype(F32), sel_c, precision=lax.Precision.HIGHEST)
    strip = jnp.where(ok_c[None, :, :, None, :], strip, NEG_INF)
    strip = jnp.pad(strip, ((0, 0), (0, 0), (0, 0), (pad_lo, n_a - n_dr - pad_lo), (0, 0)))
    strip = strip.reshape(strip.shape[:3] + (n_a * NA_KEY_C,))
    strip = jnp.pad(strip, ((0, 0), (0, 0), (0, 0), (0, lanes - n_a * NA_KEY_C)))
    n_heads = rpb.shape[0]
    return pl.pallas_call(
        functools.partial(_na_bias_kernel, offs=tuple(offs), pad_lo=pad_lo),
        grid=(n_heads,),
        in_specs=[pl.BlockSpec((1, n_cb, NA_QBLK_C, lanes), lambda h: (h, 0, 0, 0)),
                  pl.BlockSpec((3, NA_WIN_R, n_k), lambda h: (0, 0, 0))],
        out_specs=pl.BlockSpec((1, 3, n_cb, NA_WIN_R * NA_QBLK_C, n_k), lambda h: (h, 0, 0, 0, 0)),
        out_shape=jax.ShapeDtypeStruct((n_heads, 3, n_cb, NA_WIN_R * NA_QBLK_C, n_k), F32),
        compiler_params=_params("parallel"),
    )(strip, row_mask)


def _na_bias_kernel(t_ref, rm_ref, o_ref, *, offs, pad_lo):
    n_cb = t_ref.shape[1]
    lanes = t_ref.shape[-1]
    per = HEAD_DIM // NA_KEY_C
    n_k = o_ref.shape[-1]
    for cb in range(n_cb):
        t = t_ref[0, cb]
        shifted = [t] + [pltpu.roll(t, lanes - NA_KEY_C * s, axis=1) for s in range(1, per)]
        for cls, off in enumerate(offs):
            for r in range(NA_WIN_R):
                m = off - r + pad_lo
                base = (m // per) * HEAD_DIM
                tile = shifted[m % per][:, base:base + n_k] + rm_ref[cls, r:r + 1, :]
                o_ref[0, cls, cb, r * NA_QBLK_C:(r + 1) * NA_QBLK_C, :] = tile


def _na_kernel(q_ref, k0_ref, k1_ref, k2_ref, k3_ref, v0_ref, v1_ref, v2_ref, v3_ref, kc_ref, vc_ref, bias_ref,
               ow_ref, o_ref):
    n_cb = GRID_W // NA_QBLK_C
    q_all = q_ref[...].astype(F32)
    k_all = jnp.concatenate([r[...] for r in (k0_ref, k1_ref, k2_ref, k3_ref)], axis=0).astype(F32)
    v_all = jnp.concatenate([r[...] for r in (v0_ref, v1_ref, v2_ref, v3_ref)], axis=0).astype(F32)
    def patch(x_all, hh, rows, col0, cols):
        x3 = x_all[:, hh * HEAD_DIM:(hh + 1) * HEAD_DIM].reshape(rows, GRID_W, HEAD_DIM)
        return x3[:, col0:col0 + cols, :].reshape(rows * cols, HEAD_DIM).astype(BF16)

    def scores(hh, cb):
        q0 = cb * NA_QBLK_C
        k_col0 = min(max(q0 - NA_WIN_C // 2, 0), GRID_W - NA_KEY_C)
        qm = patch(q_all, hh, NA_WIN_R, q0, NA_QBLK_C)
        km = patch(k_all, hh, NA_KEY_R, k_col0, NA_KEY_C)
        s_loc = _dot_nt(qm, km) + bias_ref[hh, 0, cb]
        s_ctx = _dot_nt(qm, kc_ref[:, hh * HEAD_DIM:(hh + 1) * HEAD_DIM])
        return s_loc, s_ctx

    def finish(hh, cb, s_loc, s_ctx):
        q0 = cb * NA_QBLK_C
        k_col0 = min(max(q0 - NA_WIN_C // 2, 0), GRID_W - NA_KEY_C)
        hs = slice(hh * HEAD_DIM, (hh + 1) * HEAD_DIM)
        vm = patch(v_all, hh, NA_KEY_R, k_col0, NA_KEY_C)
        mx = jnp.maximum(jnp.max(s_loc, axis=-1, keepdims=True), jnp.max(s_ctx, axis=-1, keepdims=True))
        p_loc = jnp.exp(s_loc - mx)
        p_ctx = jnp.exp(s_ctx - mx)
        den = jnp.sum(p_loc, axis=-1, keepdims=True) + jnp.sum(p_ctx, axis=-1, keepdims=True)
        o = (_dot(p_loc.astype(BF16), vm) + _dot(p_ctx.astype(BF16), vc_ref[:, hs])) / den
        o = _rms(o, ow_ref[hh]).astype(o_ref.dtype)
        for r in range(NA_WIN_R):
            o_ref[r * GRID_W + q0:r * GRID_W + q0 + NA_QBLK_C, hs] = o[r * NA_QBLK_C:(r + 1) * NA_QBLK_C, :]

    tiles = [(hh, cb) for hh in range(NA_HEADS) for cb in range(n_cb)]
    queue = [scores(*tile) for tile in tiles[:TRACE_AHEAD]]
    for t, tile in enumerate(tiles):
        if t + TRACE_AHEAD < len(tiles):
            queue.append(scores(*tiles[t + TRACE_AHEAD]))
        finish(*tile, *queue.pop(0))


def _neighbourhood_attention(u, u_ctx, bias, head0, out_norm, batch, rows, ctx_len):
    assert rows % NA_WIN_R == 0 and rows >= NA_KEY_R
    n_rb = rows // NA_WIN_R
    q_blk = NA_WIN_R * GRID_W
    k_blk = NA_KEY_R * GRID_W // 4
    k_per_batch = rows * GRID_W // k_blk

    width = NA_HEADS * HEAD_DIM
    assert all(off % NA_HEADS == 0 for off in (OFF_NAQ, OFF_NAK, OFF_NAV, head0, N_NA))
    col_q, col_k, col_v = OFF_NAQ // NA_HEADS, OFF_NAK // NA_HEADS, OFF_NAV // NA_HEADS

    def key_spec(col, part):
        def index(h, rb, b):
            first = jnp.clip(rb * NA_WIN_R - NA_WIN_R // 2, 0, rows - NA_KEY_R) * GRID_W // k_blk
            return (b * k_per_batch + first + part, col + h)
        return pl.BlockSpec((k_blk, width), index)

    def row_class(rb):
        return jnp.where(rb == 0, 0, jnp.where(rb == n_rb - 1, 2, 1))

    n_cb = GRID_W // NA_QBLK_C
    in_specs = [pl.BlockSpec((q_blk, width), lambda h, rb, b: (b * n_rb + rb, col_q + h))]
    in_specs += [key_spec(col_k, p) for p in range(4)] + [key_spec(col_v, p) for p in range(4)]
    in_specs += [pl.BlockSpec((ctx_len, width), lambda h, rb, b: (b, col_k + h)),
                 pl.BlockSpec((ctx_len, width), lambda h, rb, b: (b, col_v + h)),
                 pl.BlockSpec((NA_HEADS, 1, n_cb, NA_WIN_R * NA_QBLK_C, NA_KEY_R * NA_KEY_C),
                              lambda h, rb, b: (head0 // NA_HEADS + h, row_class(rb), 0, 0, 0)),
                 pl.BlockSpec((NA_HEADS, 1, HEAD_DIM), lambda h, rb, b: (h, 0, 0))]
    return pl.pallas_call(
        _na_kernel,
        grid=(N_NA // NA_HEADS, n_rb, batch),
        in_specs=in_specs,
        out_specs=pl.BlockSpec((q_blk, width), lambda h, rb, b: (b * n_rb + rb, h)),
        out_shape=jax.ShapeDtypeStruct((u.shape[0], NA_W), BF16),
        compiler_params=_params("parallel", "parallel", "parallel"),
    )(u, u, u, u, u, u, u, u, u, u_ctx, u_ctx, bias, out_norm.reshape(N_NA, 1, HEAD_DIM))


def _ctx_attn_kernel(q_ref, k_ref, v_ref, ow_ref, o_ref):
    s = _dot_nt(q_ref[...].astype(BF16), k_ref[...].astype(BF16))
    p = jnp.exp(s - jnp.max(s, axis=-1, keepdims=True))
    o = _dot(p.astype(BF16), v_ref[...].astype(BF16)) / jnp.sum(p, axis=-1, keepdims=True)
    o_ref[...] = _rms(o, ow_ref[0]).astype(o_ref.dtype)


def _context_attention(u_ctx, out_norm, batch, ctx_len):
    tok = lambda off: pl.BlockSpec((ctx_len, HEAD_DIM), lambda b, h: (b, off + h))
    return pl.pallas_call(
        _ctx_attn_kernel,
        grid=(batch, N_NA),
        in_specs=[tok(OFF_NAQ), tok(OFF_NAK), tok(OFF_NAV),
                  pl.BlockSpec((1, 1, HEAD_DIM), lambda b, h: (h, 0, 0))],
        out_specs=tok(0),
        out_shape=jax.ShapeDtypeStruct((u_ctx.shape[0], NA_W), BF16),
        compiler_params=_params("parallel", "parallel"),
    )(u_ctx, u_ctx, u_ctx, out_norm.reshape(N_NA, 1, HEAD_DIM))


def _row_tile(rows, target):
    t = min(rows, target)
    assert rows % t == 0
    return t


def _token_stream(xs, seq_len, mod_row, p, layer, w_up_b, ada_l, mixers, w_down_b=None, next_norm=None):
    rows = xs.shape[0]
    sh2, sc2, g1, g2 = ada_l["sh2"], ada_l["sc2"], ada_l["g1"], ada_l["g2"]

    def mod_of_tile(tile_rows):
        per = max(seq_len // tile_rows, 1)
        return (lambda i: 1 + i // per) if mod_row is None else (lambda i: mod_row)

    tm_o = _row_tile(rows, 512)
    x1, h2 = _proj_norm(mixers, p["w_out"], layer, xs, g1, mod_of_tile(tm_o), tm_o,
                        norm=(p["ln2"][layer], sc2, sh2, mod_of_tile(tm_o)))
    tm = _row_tile(rows, 1024)
    conv = (p["f_cw"][layer], p["f_cb"][layer])
    if w_down_b is None:
        act, w_down_b = _ffn_up_conv_gate(h2, w_up_b, *conv, seq_len, tm, 512, side_w=p["w_down"], layer=layer)
    else:
        act = _ffn_up_conv_gate(h2, w_up_b, *conv, seq_len, tm, 512)
    tm_d = _row_tile(rows, 256)
    norm = None if next_norm is None else (*next_norm, mod_of_tile(tm_d))
    return _proj_norm([act], w_down_b, None, x1, g2, mod_of_tile(tm_d), tm_d, norm=norm), w_down_b


def kernel(x, c, ctx, c_ctx, w_ada, b_ada, ln1_w, ln2_w, w_in, hg_lb_logits, hg_norm_w, na_q_norm_w, na_k_norm_w,
           na_rpb, na_out_norm_w, cv_w, cv_out_norm_w, w_out, w_up, ffn_conv_w, ffn_conv_b, w_down):
    batch, seq, d = x.shape
    ctx_len = ctx.shape[1]
    rows = seq // GRID_W
    depth = w_ada.shape[0]
    xs = x.reshape(batch * seq, d)
    cs = ctx.reshape(batch * ctx_len, d)

    lb_sm = jax.nn.softmax(hg_lb_logits.astype(F32), axis=1)
    lb_all = jnp.cumsum(lb_sm, axis=1) - lb_sm[:, :1]

    cond = jnp.zeros((SUBLANE, d), F32).at[0].set(c_ctx).at[1:1 + batch].set(c)
    ada = _ada_table(cond, w_ada, b_ada).reshape(depth, SUBLANE, 6, 1, d)

    s_zero = jnp.zeros((batch, N_HG, HEAD_DIM, HEAD_DIM), F32)
    p = {"w_out": w_out, "w_down": w_down, "ln2": ln2_w, "f_cw": ffn_conv_w,
         "f_cb": ffn_conv_b}
    ctx_cols = sum(IN_SPLITS[:5])
    bias = _na_bias_table(na_rpb.reshape((depth * N_NA,) + na_rpb.shape[2:]), rows)
    names = ("sh1", "sc1", "g1", "sh2", "sc2", "g2")
    ada_all = [{n: ada[l, :, i] for i, n in enumerate(names)} for l in range(depth)]
    tm = _row_tile(xs.shape[0], 1024)
    tm_c = _row_tile(cs.shape[0], 1024)
    tiles_per_seq = seq // tm
    h = _norm_mod(xs, ln1_w[0], ada_all[0]["sh1"], ada_all[0]["sc1"], lambda i: 1 + i // tiles_per_seq, tm)
    hc = _norm_mod(cs, ln1_w[0], ada_all[0]["sh1"], ada_all[0]["sc1"], lambda i: 0, tm_c)
    for l in range(depth):
        last = l == depth - 1
        ada_l = ada_all[l]
        next_norm = None if last else (ln1_w[l + 1], ada_all[l + 1]["sc1"], ada_all[l + 1]["sh1"])
        if (IN_W // 1024) * (xs.shape[0] // tm) >= w_up.shape[2] // HEAD_DIM:
            u, w_up_b = _in_proj(h, w_in, l, na_q_norm_w[l], na_k_norm_w[l], IN_W, tm, 1024, side_w=w_up)
        else:
            u = _in_proj(h, w_in, l, na_q_norm_w[l], na_k_norm_w[l], IN_W, tm, 1024)
            w_up_b = w_up[l].astype(BF16)
        uc = _in_proj(hc, w_in, l, na_q_norm_w[l], na_k_norm_w[l], ctx_cols if last else IN_W, tm_c, 512)

        if last:
            _, s_fw = _hgrn_scan(uc, lb_all[0, l], s_zero, ctx_len, reverse=False, mode="state")
            _, s_bw = _hgrn_scan(uc, lb_all[1, l], s_zero, ctx_len, reverse=True, mode="state")
        else:
            co_fw, s_fw = _hgrn_scan(uc, lb_all[0, l], s_zero, ctx_len, reverse=False, mode="raw")
            hg_c, s_bw = _hgrn_scan(uc, lb_all[1, l], s_zero, ctx_len, reverse=True, mode="final", o_fw=co_fw,
                                    norm_w=hg_norm_w[l])
        o_fw, _ = _hgrn_scan(u, lb_all[0, l], s_fw, seq, reverse=False, mode="raw")
        hg_out, _ = _hgrn_scan(u, lb_all[1, l], s_bw, seq, reverse=True, mode="final", o_fw=o_fw,
                               norm_w=hg_norm_w[l])

        na_out = _neighbourhood_attention(u, uc, bias, l * N_NA, na_out_norm_w[l], batch, rows, ctx_len)
        cv_out = _short_conv(u, cv_w[l], cv_out_norm_w[l], seq, _row_tile(seq, 1024))

        res, w_down_b = _token_stream(xs, seq, None, p, l, w_up_b, ada_l, [hg_out, na_out, cv_out],
                                      next_norm=next_norm)
        if last:
            xs = res
        else:
            xs, h = res
            na_c = _context_attention(uc, na_out_norm_w[l], batch, ctx_len)
            cv_c = _short_conv(uc, cv_w[l], cv_out_norm_w[l], ctx_len, _row_tile(ctx_len, 256))
            (cs, hc), _ = _token_stream(cs, ctx_len, 0, p, l, w_up_b, ada_l, [hg_c, na_c, cv_c],
                                        w_down_b=w_down_b, next_norm=next_norm)
    return xs.reshape(batch, seq, d)
```

```python
import functools

import numpy as np
import jax
import jax.numpy as jnp
from jax import lax
from jax.experimental import pallas as pl
from jax.experimental.pallas import tpu as pltpu

F32 = jnp.float32
BF16 = jnp.bfloat16

D_MODEL = 2048
DEPTH = 2
GRID_W = 64
HEAD_DIM = 128
N_HG = 4
N_NA = 8
N_CV = 4
HG_W = N_HG * HEAD_DIM
NA_W = N_NA * HEAD_DIM
CV_W = N_CV * HEAD_DIM
IN_SPLITS = (HG_W, HG_W, HG_W, NA_W, NA_W, HG_W, HG_W, NA_W, CV_W, CV_W, CV_W)
IN_W = sum(IN_SPLITS)
(OFF_FFW, OFF_FBW, OFF_I, OFF_NAK, OFF_NAV, OFF_HGQ, OFF_HGG, OFF_NAQ, OFF_CVB, OFF_CVC,
 OFF_CVV) = [int(v) // HEAD_DIM for v in np.cumsum((0,) + IN_SPLITS[:-1])]
NA_WIN_R = 8
NA_WIN_C = 16
NA_QBLK_C = 16
NA_KEY_C = 32
NA_KEY_R = 16
NA_HEADS = 4
CONV_W = 3
D_FF = 5632
EPS = 1e-6
F_FLOOR = 1e-30
ATTN_SCALE = HEAD_DIM ** -0.5
NEG_INF = -1e30

HG_CHUNK = 128
HG_BLOCK = 1024
TM_TOKENS = 1024
TN_IN = 1024
TN_IN_CTX = 512
TN_FF = 512
TM_OUT = 512
TM_DOWN = 256
SUBLANE = 8
HALO = 16
MXU_N = 256
TRACE_AHEAD = 3
VMEM_LIMIT = 56 * 1024 * 1024


def _params(*sem):
    return pltpu.CompilerParams(dimension_semantics=sem, vmem_limit_bytes=VMEM_LIMIT)


def _dot(a, b):
    return jnp.dot(a, b, preferred_element_type=F32)


def _dot_nt(a, b):
    return lax.dot_general(a, b, (((1,), (1,)), ((), ())), preferred_element_type=F32)


def _dot_tn(a, b):
    return lax.dot_general(a, b, (((0,), (0,)), ((), ())), preferred_element_type=F32)


def _rms(x, w):
    return x * lax.rsqrt(jnp.mean(x * x, axis=-1, keepdims=True) + EPS) * w


def _silu(x):
    half = 0.5 * x
    return half + half * jnp.tanh(half)


def _ada_kernel(s_ref, w_ref, b_ref, o_ref):
    s = _silu(s_ref[...]).astype(BF16)
    o_ref[0] = _dot(s, w_ref[0].astype(BF16)) + b_ref[0]


def _ada_table(cond, w_ada, b_ada):
    depth, d, n = w_ada.shape
    tn = 1024
    return pl.pallas_call(
        _ada_kernel,
        grid=(depth, n // tn),
        in_specs=[pl.BlockSpec((SUBLANE, d), lambda l, j: (0, 0)),
                  pl.BlockSpec((1, d, tn), lambda l, j: (l, 0, j)),
                  pl.BlockSpec((1, 1, tn), lambda l, j: (l, 0, j))],
        out_specs=pl.BlockSpec((1, SUBLANE, tn), lambda l, j: (l, 0, j)),
        out_shape=jax.ShapeDtypeStruct((depth, SUBLANE, n), F32),
        compiler_params=_params("parallel", "parallel"),
    )(cond, w_ada, b_ada.reshape(depth, 1, n))


def _norm_mod_kernel(x_ref, w_ref, sh_ref, sc_ref, o_ref):
    y = _rms(x_ref[...], w_ref[...])
    o_ref[...] = (y * (1.0 + sc_ref[0]) + sh_ref[0]).astype(o_ref.dtype)


def _norm_mod(x, w, shift, scale, mod_of_tile, tr):
    rows, d = x.shape
    mod = lambda i: (mod_of_tile(i), 0, 0)
    return pl.pallas_call(
        _norm_mod_kernel,
        grid=(rows // tr,),
        in_specs=[pl.BlockSpec((tr, d), lambda i: (i, 0)),
                  pl.BlockSpec((1, d), lambda i: (0, 0)),
                  pl.BlockSpec((1, 1, d), mod),
                  pl.BlockSpec((1, 1, d), mod)],
        out_specs=pl.BlockSpec((tr, d), lambda i: (i, 0)),
        out_shape=jax.ShapeDtypeStruct((rows, d), BF16),
        compiler_params=_params("parallel"),
    )(x, w.reshape(1, d), shift, scale)


def _in_proj_kinds(n_cols, tn):
    kinds = []
    for head in range(n_cols // HEAD_DIM):
        if OFF_NAK <= head < OFF_NAV:
            kinds.append("k")
        elif OFF_NAQ <= head < OFF_CVB:
            kinds.append("q")
        elif OFF_HGQ <= head < OFF_HGG:
            kinds.append("silu")
        else:
            kinds.append("id")
    per = tn // HEAD_DIM
    return tuple(tuple(kinds[t * per:(t + 1) * per]) for t in range(n_cols // tn))


def _in_proj_kernel(*refs, kinds, with_cast):
    if with_cast:
        a_ref, w_ref, qw_ref, kw_ref, side_ref, o_ref, side_o_ref, wb_ref = refs
        side_o_ref[...] = side_ref[...].astype(BF16)
    else:
        a_ref, w_ref, qw_ref, kw_ref, o_ref, wb_ref = refs
    j = pl.program_id(0)

    @pl.when(pl.program_id(1) == 0)
    def _():
        wb_ref[...] = w_ref[...].astype(BF16)

    def transform(y, kind):
        if kind == "k":
            return _rms(y, kw_ref[...])
        if kind == "q":
            return _rms(y, qw_ref[...]) * ATTN_SCALE
        if kind == "silu":
            return _silu(y)
        return y

    per_slab = MXU_N // HEAD_DIM
    for pattern in sorted(set(kinds)):
        hit = functools.reduce(jnp.logical_or, [j == jj for jj, tk in enumerate(kinds) if tk == pattern])

        @pl.when(hit)
        def _(pattern=pattern):
            a = a_ref[...]
            for c in range(len(pattern) // per_slab):
                cols = slice(c * MXU_N, (c + 1) * MXU_N)
                acc = _dot(a, wb_ref[:, cols])
                parts = [transform(acc[:, s * HEAD_DIM:(s + 1) * HEAD_DIM], pattern[c * per_slab + s])
                         for s in range(per_slab)]
                o_ref[:, cols] = jnp.concatenate(parts, axis=1).astype(o_ref.dtype)


def _in_proj(h, w_in, layer, q_norm, k_norm, n_cols, tm, tn, side_w=None):
    m, k = h.shape
    kinds = _in_proj_kinds(n_cols, tn)
    nj, ni = n_cols // tn, m // tm
    vec = pl.BlockSpec((1, HEAD_DIM), lambda j, i: (0, 0))
    in_specs = [pl.BlockSpec((tm, k), lambda j, i: (i, 0)),
                pl.BlockSpec((None, k, tn), lambda j, i: (layer, 0, j)),
                vec, vec]
    args = [h, w_in, q_norm.reshape(1, HEAD_DIM), k_norm.reshape(1, HEAD_DIM)]
    out_specs = [pl.BlockSpec((tm, tn), lambda j, i: (i, j))]
    out_shape = [jax.ShapeDtypeStruct((m, n_cols), BF16)]
    if side_w is not None:
        _, k2, n2 = side_w.shape
        strips = n2 // HEAD_DIM
        assert strips <= nj * ni
        strip = lambda j, i: jnp.minimum(j * ni + i, strips - 1)
        in_specs.append(pl.BlockSpec((None, k2, HEAD_DIM), lambda j, i: (layer, 0, strip(j, i))))
        args.append(side_w)
        out_specs.append(pl.BlockSpec((k2, HEAD_DIM), lambda j, i: (0, strip(j, i))))
        out_shape.append(jax.ShapeDtypeStruct((k2, n2), BF16))
    res = pl.pallas_call(
        functools.partial(_in_proj_kernel, kinds=kinds, with_cast=side_w is not None),
        grid=(nj, ni),
        in_specs=in_specs,
        out_specs=out_specs,
        out_shape=out_shape,
        scratch_shapes=[pltpu.VMEM((k, tn), BF16)],
        compiler_params=_params("arbitrary", "arbitrary"),
    )(*args)
    return res if side_w is not None else res[0]


def _proj_norm_kernel(*refs, splits, cast_weight, with_norm):
    n_a = len(splits)
    a_refs, rest = refs[:n_a], list(refs[n_a:])
    w_ref, x_ref, g_ref = rest[:3]
    rest = rest[3:]
    if with_norm:
        lw_ref, sc_ref, sh_ref, x1_ref, h2_ref = rest[:5]
        rest = rest[5:]
    else:
        x1_ref = rest.pop(0)
    if cast_weight:
        wb_ref = rest.pop(0)

        @pl.when(pl.program_id(0) == 0)
        def _():
            wb_ref[...] = w_ref[...].astype(BF16)
    else:
        wb_ref = w_ref

    tm, n = x1_ref.shape
    ssq = jnp.zeros((tm, HEAD_DIM), F32)
    for c in range(n // MXU_N):
        cols = slice(c * MXU_N, (c + 1) * MXU_N)
        acc = None
        off = 0
        for a_ref, width in zip(a_refs, splits):
            part = _dot(a_ref[...], wb_ref[off:off + width, cols])
            acc = part if acc is None else acc + part
            off += width
        x1 = x_ref[:, cols] + g_ref[0][:, cols] * acc
        x1_ref[:, cols] = x1
        if with_norm:
            sq = x1 * x1
            for s in range(MXU_N // HEAD_DIM):
                ssq = ssq + sq[:, s * HEAD_DIM:(s + 1) * HEAD_DIM]
    if with_norm:
        inv = lax.rsqrt(jnp.sum(ssq, axis=-1, keepdims=True) * (1.0 / n) + EPS)
        gain = lw_ref[...] * (1.0 + sc_ref[0])
        for c in range(n // MXU_N):
            cols = slice(c * MXU_N, (c + 1) * MXU_N)
            h2_ref[:, cols] = (x1_ref[:, cols] * inv * gain[:, cols] + sh_ref[0][:, cols]).astype(h2_ref.dtype)


def _proj_norm(a_list, w, layer, x, gate, mod_of_tile, tm, norm=None):
    m, n = x.shape
    cast_weight = w.ndim == 3
    k = w.shape[-2]
    splits = tuple(a.shape[1] for a in a_list)
    assert sum(splits) == k
    in_specs = [pl.BlockSpec((tm, s), lambda i: (i, 0)) for s in splits]
    if cast_weight:
        in_specs.append(pl.BlockSpec((None, k, n), lambda i: (layer, 0, 0), pipeline_mode=pl.Buffered(1)))
    else:
        in_specs.append(pl.BlockSpec((k, n), lambda i: (0, 0), pipeline_mode=pl.Buffered(1)))
    in_specs += [pl.BlockSpec((tm, n), lambda i: (i, 0)),
                 pl.BlockSpec((1, 1, n), lambda i: (mod_of_tile(i), 0, 0))]
    args = [*a_list, w, x, gate]
    out_specs = [pl.BlockSpec((tm, n), lambda i: (i, 0))]
    out_shape = [jax.ShapeDtypeStruct((m, n), F32)]
    if norm is not None:
        ln_w, scale, shift, norm_mod = norm
        mod = lambda i: (norm_mod(i), 0, 0)
        in_specs += [pl.BlockSpec((1, n), lambda i: (0, 0)), pl.BlockSpec((1, 1, n), mod),
                     pl.BlockSpec((1, 1, n), mod)]
        args += [ln_w.reshape(1, n), scale, shift]
        out_specs.append(pl.BlockSpec((tm, n), lambda i: (i, 0)))
        out_shape.append(jax.ShapeDtypeStruct((m, n), BF16))
    res = pl.pallas_call(
        functools.partial(_proj_norm_kernel, splits=splits, cast_weight=cast_weight, with_norm=norm is not None),
        grid=(m // tm,),
        in_specs=in_specs,
        out_specs=out_specs,
        out_shape=out_shape,
        scratch_shapes=[pltpu.VMEM((k, n), BF16)] if cast_weight else [],
        compiler_params=_params("arbitrary"),
    )(*args)
    return res if norm is not None else res[0]


def _shift_rows(x, prev_row, next_row):
    r = x.shape[0]
    idx = lax.broadcasted_iota(jnp.int32, x.shape, 0)
    down = jnp.where(idx == 0, prev_row, pltpu.roll(x, 1, axis=0))
    up = jnp.where(idx == r - 1, next_row, pltpu.roll(x, r - 1, axis=0))
    return down, up


def _halo_specs(tr, tc, col_of, tiles_per_seq, n_rows):
    per = tr // HALO
    last_blk = n_rows // HALO - 1
    cur = pl.BlockSpec((tr, tc), lambda i, j: (i, col_of(j)))
    prev = pl.BlockSpec((HALO, tc), lambda i, j: (jnp.maximum(i * per - 1, 0), col_of(j)))
    nxt = pl.BlockSpec((HALO, tc), lambda i, j: (jnp.minimum((i + 1) * per, last_blk), col_of(j)))
    return cur, prev, nxt


def _seq_edges(tiles_per_seq):
    i = pl.program_id(0)
    pos = i % tiles_per_seq
    return (pos != 0).astype(F32), (pos != tiles_per_seq - 1).astype(F32)


def _ffn_up_kernel(*refs, tm, seq_len, with_cast):
    if with_cast:
        (h_ref, hp_ref, hn_ref, wg_ref, wv_ref, cg_ref, cv_ref, bg_ref, bv_ref, side_ref,
         o_ref, side_o_ref, lhs_ref) = refs
        side_o_ref[...] = side_ref[...].astype(BF16)
    else:
        h_ref, hp_ref, hn_ref, wg_ref, wv_ref, cg_ref, cv_ref, bg_ref, bv_ref, o_ref, lhs_ref = refs
    tiles_per_seq = max(seq_len // tm, 1)
    pos = pl.program_id(0) % tiles_per_seq
    has_prev = pos != 0
    has_next = pos != tiles_per_seq - 1

    @pl.when(pl.program_id(1) == 0)
    def _():
        lhs_ref[0:HALO, :] = jnp.where(has_prev, hp_ref[...], jnp.zeros_like(hp_ref))
        lhs_ref[HALO:HALO + tm, :] = h_ref[...]
        lhs_ref[HALO + tm:, :] = jnp.where(has_next, hn_ref[...], jnp.zeros_like(hn_ref))

    lhs = lhs_ref[...]

    def branch(w_ref, c_ref, b_ref):
        acc = _dot(lhs, w_ref[...])
        r = acc.shape[0]
        down, up = pltpu.roll(acc, 1, axis=0), pltpu.roll(acc, r - 1, axis=0)
        if seq_len < tm:
            tok = (lax.broadcasted_iota(jnp.int32, acc.shape, 0) + (seq_len - HALO)) % seq_len
            down = jnp.where(tok == 0, 0.0, down)
            up = jnp.where(tok == seq_len - 1, 0.0, up)
        y = down * c_ref[0:1, :] + acc * c_ref[1:2, :] + up * c_ref[2:3, :]
        return y[HALO:HALO + tm, :] + b_ref[...]

    gate = branch(wg_ref, cg_ref, bg_ref)
    val = branch(wv_ref, cv_ref, bv_ref)
    o_ref[...] = (_silu(gate) * val).astype(o_ref.dtype)


def _ffn_up_conv_gate(h, w_up, cw, cb, seq_len, tm, tn, side_w=None, layer=0):
    rows, k = h.shape
    assert seq_len % tm == 0 or (tm % seq_len == 0 and seq_len >= HALO)
    nj = D_FF // tn
    per = tm // HALO
    last_blk = rows // HALO - 1
    cb2 = cb.reshape(1, 2 * D_FF)
    steps = (rows // tm) * nj
    if side_w is not None:
        _, k2, n2 = side_w.shape
        strips = max(d for d in range(1, steps + 1) if k2 % d == 0 and (k2 // d) % HALO == 0)
        strip = lambda i, j: jnp.minimum(i * nj + j, strips - 1)
        side_in = [pl.BlockSpec((None, k2 // strips, n2), lambda i, j: (layer, strip(i, j), 0))]
        side_out = [pl.BlockSpec((k2 // strips, n2), lambda i, j: (strip(i, j), 0))]
        side_shape = [jax.ShapeDtypeStruct((k2, n2), BF16)]
        side_args = [side_w]
    else:
        side_in, side_out, side_shape, side_args = [], [], [], []
    in_specs = [pl.BlockSpec((tm, k), lambda i, j: (i, 0)),
                pl.BlockSpec((HALO, k), lambda i, j: (jnp.maximum(i * per - 1, 0), 0)),
                pl.BlockSpec((HALO, k), lambda i, j: (jnp.minimum((i + 1) * per, last_blk), 0)),
                pl.BlockSpec((k, tn), lambda i, j: (0, j)),
                pl.BlockSpec((k, tn), lambda i, j: (0, j + nj)),
                pl.BlockSpec((CONV_W, tn), lambda i, j: (0, j)),
                pl.BlockSpec((CONV_W, tn), lambda i, j: (0, j + nj)),
                pl.BlockSpec((1, tn), lambda i, j: (0, j)),
                pl.BlockSpec((1, tn), lambda i, j: (0, j + nj))]
    res = pl.pallas_call(
        functools.partial(_ffn_up_kernel, tm=tm, seq_len=seq_len, with_cast=side_w is not None),
        grid=(rows // tm, nj),
        in_specs=in_specs + side_in,
        out_specs=[pl.BlockSpec((tm, tn), lambda i, j: (i, j))] + side_out,
        out_shape=[jax.ShapeDtypeStruct((rows, D_FF), BF16)] + side_shape,
        scratch_shapes=[pltpu.VMEM((tm + 2 * HALO, k), BF16)],
        compiler_params=_params("arbitrary", "arbitrary"),
    )(h, h, h, w_up, w_up, cw, cw, cb2, cb2, *side_args)
    return res if side_w is not None else res[0]


def _short_conv_kernel(b_ref, c_ref, cp_ref, cn_ref, v_ref, vp_ref, vn_ref, w_ref, nw_ref, o_ref, *, tiles_per_seq):
    has_prev, has_next = _seq_edges(tiles_per_seq)
    p = c_ref[...].astype(F32) * v_ref[...].astype(F32)
    prev_row = cp_ref[HALO - 1:HALO, :].astype(F32) * vp_ref[HALO - 1:HALO, :].astype(F32) * has_prev
    next_row = cn_ref[0:1, :].astype(F32) * vn_ref[0:1, :].astype(F32) * has_next
    down, up = _shift_rows(p, prev_row, next_row)
    y = b_ref[...].astype(F32) * (down * w_ref[0:1, :] + p * w_ref[1:2, :] + up * w_ref[2:3, :])
    for h in range(N_CV):
        sl = slice(h * HEAD_DIM, (h + 1) * HEAD_DIM)
        o_ref[:, sl] = _rms(y[:, sl], nw_ref[:, sl]).astype(o_ref.dtype)


def _short_conv(u, cv_w, cv_onorm, seq_len, tr):
    rows = u.shape[0]
    tiles_per_seq = seq_len // tr
    blk = lambda off: (lambda j: off * HEAD_DIM // CV_W)
    b_spec = pl.BlockSpec((tr, CV_W), lambda i, j: (i, OFF_CVB * HEAD_DIM // CV_W))
    c_specs = _halo_specs(tr, CV_W, blk(OFF_CVC), tiles_per_seq, rows)
    v_specs = _halo_specs(tr, CV_W, blk(OFF_CVV), tiles_per_seq, rows)
    return pl.pallas_call(
        functools.partial(_short_conv_kernel, tiles_per_seq=tiles_per_seq),
        grid=(rows // tr, 1),
        in_specs=[b_spec, *c_specs, *v_specs,
                  pl.BlockSpec((CONV_W, CV_W), lambda i, j: (0, 0)),
                  pl.BlockSpec((1, CV_W), lambda i, j: (0, 0))],
        out_specs=pl.BlockSpec((tr, CV_W), lambda i, j: (i, 0)),
        out_shape=jax.ShapeDtypeStruct((rows, CV_W), BF16),
        compiler_params=_params("parallel", "arbitrary"),
    )(u, u, u, u, u, u, u, cv_w, cv_onorm.reshape(1, CV_W))


def _hgrn_structure(chunk, reverse):
    idx = np.arange(chunk)
    i, t = idx[:, None], idx[None, :]
    mats = [t <= i, t > i]
    masks = []
    s = chunk // 2
    while s >= 1:
        blk = idx // (2 * s)
        upper = (idx % (2 * s)) >= s
        mid = blk * 2 * s + s - 1
        a = np.where(upper[:, None], (t > mid[:, None]) & (t <= i), (t > i) & (t <= mid[:, None]))
        mats.append(a)
        masks.append((blk[:, None] == blk[None, :]) & upper[:, None] & ~upper[None, :])
        s //= 2
    mats = np.stack(mats).astype(np.float32)
    masks = np.stack(masks).astype(np.float32)
    if reverse:
        mats = mats[:, ::-1, ::-1]
        masks = masks[:, ::-1, ::-1]
    return mats.reshape(-1, chunk), masks


def _hgrn_kernel(*refs, chunk, n_sub, n_levels, reverse, mode):
    if mode == "final":
        (z_ref, v_ref, q_ref, lb_ref, a_ref, m_ref, s0_ref, g_ref, ofw_ref, nw_ref,
         o_ref, sfin_ref, st_ref) = refs
    elif mode == "raw":
        z_ref, v_ref, q_ref, lb_ref, a_ref, m_ref, s0_ref, o_ref, sfin_ref, st_ref = refs
    else:
        z_ref, v_ref, lb_ref, a_ref, s0_ref, sfin_ref, st_ref = refs
    c = pl.program_id(1)

    @pl.when(c == 0)
    def _():
        st_ref[...] = s0_ref[0]

    lb = lb_ref[...]
    a = a_ref[...]
    tot_row = 0 if reverse else chunk - 1
    subs = list(range(n_sub - 1, -1, -1) if reverse else range(n_sub))

    def gates(sub):
        rows = slice(sub * chunk, (sub + 1) * chunk)
        z = z_ref[rows, :].astype(F32)
        f = lb + (1.0 - lb) * jax.nn.sigmoid(z)
        log_f = jnp.log(jnp.maximum(f, F_FLOOR))
        k_all = (1.0 - lb) * jax.nn.sigmoid(-z)
        expo = _dot(a, log_f.astype(BF16))
        return k_all, v_ref[rows, :].astype(F32), expo

    def products(sub, h, k_all, v_all, expo):
        rows = slice(sub * chunk, (sub + 1) * chunk)
        sl = slice(h * HEAD_DIM, (h + 1) * HEAD_DIM)
        k, v = k_all[:, sl], v_all[:, sl]
        e_rem = jnp.exp(expo[chunk:2 * chunk, sl])
        e_tot = jnp.exp(expo[tot_row:tot_row + 1, sl])
        st = st_ref[h]
        st_ref[h] = st * e_tot + _dot_tn(v.astype(BF16), (k * e_rem).astype(BF16))
        if mode == "state":
            return None
        q = q_ref[rows, sl].astype(F32)
        e_cum = jnp.exp(expo[0:chunk, sl])
        o_inter = _dot_nt((q * e_cum).astype(BF16), st.astype(BF16))
        parts = []
        for lvl in range(n_levels):
            e = jnp.exp(expo[(2 + lvl) * chunk:(3 + lvl) * chunk, sl])
            parts.append(_dot_nt((q * e).astype(BF16), (k * e).astype(BF16)))
        diag = jnp.sum(q * k, axis=-1, keepdims=True)
        return o_inter, parts, diag, v

    def readout(sub, h, o_inter, parts, diag, v):
        rows = slice(sub * chunk, (sub + 1) * chunk)
        sl = slice(h * HEAD_DIM, (h + 1) * HEAD_DIM)
        att = None
        for lvl, part in enumerate(parts):
            att = m_ref[lvl] * part if att is None else att + m_ref[lvl] * part
        o = o_inter + _dot(att.astype(BF16), v.astype(BF16)) + diag * v
        if mode == "final":
            y = _rms(o + ofw_ref[rows, sl], nw_ref[...]) * _silu(g_ref[rows, sl].astype(F32))
            o_ref[rows, sl] = y.astype(o_ref.dtype)
        else:
            o_ref[rows, sl] = o.astype(o_ref.dtype)

    tiles = [(sub, h) for sub in subs for h in range(N_HG)]
    cache = {}

    def stage(sub, h):
        if sub not in cache:
            cache.clear()
            cache[sub] = gates(sub)
        return products(sub, h, *cache[sub])

    queue = [stage(*tile) for tile in tiles[:TRACE_AHEAD]]
    for t, tile in enumerate(tiles):
        if t + TRACE_AHEAD < len(tiles):
            queue.append(stage(*tiles[t + TRACE_AHEAD]))
        done = queue.pop(0)
        if mode != "state":
            readout(*tile, *done)

    @pl.when(c == pl.num_programs(1) - 1)
    def _():
        sfin_ref[0] = st_ref[...]


def _hgrn_scan(u, lb, s0, seq_len, *, reverse, mode, o_fw=None, norm_w=None):
    rows = u.shape[0]
    batch = rows // seq_len
    blk = min(HG_BLOCK, seq_len)
    chunk = min(HG_CHUNK, blk)
    n_blocks = seq_len // blk
    mats, masks = _hgrn_structure(chunk, reverse)
    n_levels = masks.shape[0]
    if mode == "state":
        mats = mats[:2 * chunk]
    per_blk = HG_W // HEAD_DIM

    def tok(off):
        col = off // per_blk
        if reverse:
            return pl.BlockSpec((blk, HG_W), lambda b, c: (b * n_blocks + n_blocks - 1 - c, col))
        return pl.BlockSpec((blk, HG_W), lambda b, c: (b * n_blocks + c, col))

    const2 = lambda shape: pl.BlockSpec(shape, lambda b, c: (0, 0))
    state_spec = pl.BlockSpec((1, N_HG, HEAD_DIM, HEAD_DIM), lambda b, c: (b, 0, 0, 0))
    state_shape = jax.ShapeDtypeStruct((batch, N_HG, HEAD_DIM, HEAD_DIM), F32)
    z_spec = tok(OFF_FBW if reverse else OFF_FFW)
    lb2, mats_b = lb.reshape(1, HG_W), jnp.asarray(mats, BF16)
    if mode == "state":
        in_specs = [z_spec, tok(OFF_I), const2((1, HG_W)), const2(mats.shape), state_spec]
        args = [u, u, lb2, mats_b, s0]
        out_specs, out_shape = [state_spec], [state_shape]
    else:
        in_specs = [z_spec, tok(OFF_I), tok(OFF_HGQ), const2((1, HG_W)), const2(mats.shape),
                    pl.BlockSpec(masks.shape, lambda b, c: (0, 0, 0)), state_spec]
        args = [u, u, u, lb2, mats_b, jnp.asarray(masks, F32), s0]
        if mode == "final":
            in_specs += [tok(OFF_HGG), tok(0), const2((1, HEAD_DIM))]
            args += [u, o_fw, norm_w.reshape(1, HEAD_DIM)]
        out_specs = [tok(0), state_spec]
        out_shape = [jax.ShapeDtypeStruct((rows, HG_W), BF16 if mode == "final" else F32), state_shape]
    res = pl.pallas_call(
        functools.partial(_hgrn_kernel, chunk=chunk, n_sub=blk // chunk, n_levels=n_levels, reverse=reverse,
                          mode=mode),
        grid=(batch, n_blocks),
        in_specs=in_specs,
        out_specs=out_specs,
        out_shape=out_shape,
        scratch_shapes=[pltpu.VMEM((N_HG, HEAD_DIM, HEAD_DIM), F32)],
        compiler_params=_params("parallel", "arbitrary"),
    )(*args)
    return (None, res[0]) if mode == "state" else (res[0], res[1])


def _na_bias_table(rpb, rows):
    n_rb = rows // NA_WIN_R
    n_cb = GRID_W // NA_QBLK_C
    n_dr, n_dc = 2 * NA_WIN_R - 1, 2 * NA_WIN_C - 1
    n_k = NA_KEY_R * NA_KEY_C
    offs, row_mask = [], []
    for rb in (0, min(1, n_rb - 1), n_rb - 1):
        k_row0 = int(np.clip(rb * NA_WIN_R - NA_WIN_R // 2, 0, rows - NA_KEY_R))
        q_r = rb * NA_WIN_R + np.arange(NA_WIN_R)
        k_r = k_row0 + np.arange(NA_KEY_R)
        r_start = np.clip(q_r - NA_WIN_R // 2, 0, rows - NA_WIN_R)
        ok_r = (k_r[None, :] >= r_start[:, None]) & (k_r[None, :] < r_start[:, None] + NA_WIN_R)
        row_mask.append(np.repeat(np.where(ok_r, 0.0, NEG_INF), NA_KEY_C, axis=1))
        offs.append(k_row0 - rb * NA_WIN_R + NA_WIN_R - 1)
    row_mask = jnp.asarray(np.stack(row_mask), F32)
    pad_lo = NA_WIN_R - 1 - min(offs)
    per_vreg = HEAD_DIM // NA_KEY_C
    n_a = -(-(NA_KEY_R - 1 + max(offs) + pad_lo + 1) // per_vreg) * per_vreg
    lanes = n_a * NA_KEY_C
    sel_a = jnp.asarray(np.arange(n_a)[:, None] - pad_lo == np.arange(n_dr)[None, :], F32)
    sel_c, ok_c = [], []
    for cb in range(n_cb):
        k_col0 = int(np.clip(cb * NA_QBLK_C - NA_WIN_C // 2, 0, GRID_W - NA_KEY_C))
        q_c = cb * NA_QBLK_C + np.arange(NA_QBLK_C)
        k_c = k_col0 + np.arange(NA_KEY_C)
        c_start = np.clip(q_c - NA_WIN_C // 2, 0, GRID_W - NA_WIN_C)
        ok_c.append((k_c[None, :] >= c_start[:, None]) & (k_c[None, :] < c_start[:, None] + NA_WIN_C))
        dc = k_c[None, :] - q_c[:, None] + NA_WIN_C - 1
        sel_c.append(dc[:, :, None] == np.arange(n_dc))
    sel_c = jnp.asarray(np.stack(sel_c), F32)
    ok_c = jnp.asarray(np.stack(ok_c))
    strip = jnp.einsum("xa,hab,mcjb->hmcxj", sel_a, rpb.astype(F32), sel_c, precision=lax.Precision.HIGHEST)
    strip = jnp.where(ok_c[None, :, :, None, :], strip, NEG_INF)
    strip = strip.reshape(strip.shape[:3] + (lanes,))
    n_heads = rpb.shape[0]
    return pl.pallas_call(
        functools.partial(_na_bias_kernel, offs=tuple(offs), pad_lo=pad_lo),
        grid=(n_heads,),
        in_specs=[pl.BlockSpec((1, n_cb, NA_QBLK_C, lanes), lambda h: (h, 0, 0, 0)),
                  pl.BlockSpec((3, NA_WIN_R, n_k), lambda h: (0, 0, 0))],
        out_specs=pl.BlockSpec((1, 3, n_cb, NA_WIN_R * NA_QBLK_C, n_k), lambda h: (h, 0, 0, 0, 0)),
        out_shape=jax.ShapeDtypeStruct((n_heads, 3, n_cb, NA_WIN_R * NA_QBLK_C, n_k), F32),
        compiler_params=_params("parallel"),
    )(strip, row_mask)


def _na_bias_kernel(t_ref, rm_ref, o_ref, *, offs, pad_lo):
    n_cb = t_ref.shape[1]
    lanes = t_ref.shape[-1]
    per = HEAD_DIM // NA_KEY_C
    n_k = o_ref.shape[-1]
    for cb in range(n_cb):
        t = t_ref[0, cb]
        shifted = [t] + [pltpu.roll(t, lanes - NA_KEY_C * s, axis=1) for s in range(1, per)]
        for cls, off in enumerate(offs):
            for r in range(NA_WIN_R):
                m = off - r + pad_lo
                base = (m // per) * HEAD_DIM
                tile = shifted[m % per][:, base:base + n_k] + rm_ref[cls, r:r + 1, :]
                o_ref[0, cls, cb, r * NA_QBLK_C:(r + 1) * NA_QBLK_C, :] = tile


def _na_kernel(q_ref, k0_ref, k1_ref, k2_ref, k3_ref, v0_ref, v1_ref, v2_ref, v3_ref, kc_ref, vc_ref, bias_ref,
               ow_ref, o_ref):
    n_cb = GRID_W // NA_QBLK_C
    q_all = q_ref[...].astype(F32)
    k_all = jnp.concatenate([r[...] for r in (k0_ref, k1_ref, k2_ref, k3_ref)], axis=0).astype(F32)
    v_all = jnp.concatenate([r[...] for r in (v0_ref, v1_ref, v2_ref, v3_ref)], axis=0).astype(F32)
    def patch(x_all, hh, rows, col0, cols):
        x3 = x_all[:, hh * HEAD_DIM:(hh + 1) * HEAD_DIM].reshape(rows, GRID_W, HEAD_DIM)
        return x3[:, col0:col0 + cols, :].reshape(rows * cols, HEAD_DIM).astype(BF16)

    def scores(hh, cb):
        q0 = cb * NA_QBLK_C
        k_col0 = min(max(q0 - NA_WIN_C // 2, 0), GRID_W - NA_KEY_C)
        qm = patch(q_all, hh, NA_WIN_R, q0, NA_QBLK_C)
        km = patch(k_all, hh, NA_KEY_R, k_col0, NA_KEY_C)
        s_loc = _dot_nt(qm, km) + bias_ref[hh, 0, cb]
        s_ctx = _dot_nt(qm, kc_ref[:, hh * HEAD_DIM:(hh + 1) * HEAD_DIM])
        return s_loc, s_ctx

    def finish(hh, cb, s_loc, s_ctx):
        q0 = cb * NA_QBLK_C
        k_col0 = min(max(q0 - NA_WIN_C // 2, 0), GRID_W - NA_KEY_C)
        hs = slice(hh * HEAD_DIM, (hh + 1) * HEAD_DIM)
        vm = patch(v_all, hh, NA_KEY_R, k_col0, NA_KEY_C)
        mx = jnp.maximum(jnp.max(s_loc, axis=-1, keepdims=True), jnp.max(s_ctx, axis=-1, keepdims=True))
        p_loc = jnp.exp(s_loc - mx)
        p_ctx = jnp.exp(s_ctx - mx)
        den = jnp.sum(p_loc, axis=-1, keepdims=True) + jnp.sum(p_ctx, axis=-1, keepdims=True)
        o = (_dot(p_loc.astype(BF16), vm) + _dot(p_ctx.astype(BF16), vc_ref[:, hs])) / den
        o = _rms(o, ow_ref[hh]).astype(o_ref.dtype)
        for r in range(NA_WIN_R):
            o_ref[r * GRID_W + q0:r * GRID_W + q0 + NA_QBLK_C, hs] = o[r * NA_QBLK_C:(r + 1) * NA_QBLK_C, :]

    tiles = [(hh, cb) for hh in range(NA_HEADS) for cb in range(n_cb)]
    queue = [scores(*tile) for tile in tiles[:TRACE_AHEAD]]
    for t, tile in enumerate(tiles):
        if t + TRACE_AHEAD < len(tiles):
            queue.append(scores(*tiles[t + TRACE_AHEAD]))
        finish(*tile, *queue.pop(0))


def _neighbourhood_attention(u, u_ctx, bias, head0, out_norm, batch, rows, ctx_len):
    assert rows % NA_WIN_R == 0 and rows >= NA_KEY_R
    n_rb = rows // NA_WIN_R
    q_blk = NA_WIN_R * GRID_W
    k_blk = NA_KEY_R * GRID_W // 4
    k_per_batch = rows * GRID_W // k_blk

    width = NA_HEADS * HEAD_DIM
    assert all(off % NA_HEADS == 0 for off in (OFF_NAQ, OFF_NAK, OFF_NAV, head0, N_NA))
    col_q, col_k, col_v = OFF_NAQ // NA_HEADS, OFF_NAK // NA_HEADS, OFF_NAV // NA_HEADS

    def key_spec(col, part):
        def index(h, rb, b):
            first = jnp.clip(rb * NA_WIN_R - NA_WIN_R // 2, 0, rows - NA_KEY_R) * GRID_W // k_blk
            return (b * k_per_batch + first + part, col + h)
        return pl.BlockSpec((k_blk, width), index)

    def row_class(rb):
        return jnp.where(rb == 0, 0, jnp.where(rb == n_rb - 1, 2, 1))

    n_cb = GRID_W // NA_QBLK_C
    in_specs = [pl.BlockSpec((q_blk, width), lambda h, rb, b: (b * n_rb + rb, col_q + h))]
    in_specs += [key_spec(col_k, p) for p in range(4)] + [key_spec(col_v, p) for p in range(4)]
    in_specs += [pl.BlockSpec((ctx_len, width), lambda h, rb, b: (b, col_k + h)),
                 pl.BlockSpec((ctx_len, width), lambda h, rb, b: (b, col_v + h)),
                 pl.BlockSpec((NA_HEADS, 1, n_cb, NA_WIN_R * NA_QBLK_C, NA_KEY_R * NA_KEY_C),
                              lambda h, rb, b: (head0 // NA_HEADS + h, row_class(rb), 0, 0, 0)),
                 pl.BlockSpec((NA_HEADS, 1, HEAD_DIM), lambda h, rb, b: (h, 0, 0))]
    return pl.pallas_call(
        _na_kernel,
        grid=(N_NA // NA_HEADS, n_rb, batch),
        in_specs=in_specs,
        out_specs=pl.BlockSpec((q_blk, width), lambda h, rb, b: (b * n_rb + rb, h)),
        out_shape=jax.ShapeDtypeStruct((u.shape[0], NA_W), BF16),
        compiler_params=_params("parallel", "parallel", "parallel"),
    )(u, u, u, u, u, u, u, u, u, u_ctx, u_ctx, bias, out_norm.reshape(N_NA, 1, HEAD_DIM))


def _ctx_attn_kernel(q_ref, k_ref, v_ref, ow_ref, o_ref):
    s = _dot_nt(q_ref[...].astype(BF16), k_ref[...].astype(BF16))
    p = jnp.exp(s - jnp.max(s, axis=-1, keepdims=True))
    o = _dot(p.astype(BF16), v_ref[...].astype(BF16)) / jnp.sum(p, axis=-1, keepdims=True)
    o_ref[...] = _rms(o, ow_ref[0]).astype(o_ref.dtype)


def _context_attention(u_ctx, out_norm, batch, ctx_len):
    tok = lambda off: pl.BlockSpec((ctx_len, HEAD_DIM), lambda b, h: (b, off + h))
    return pl.pallas_call(
        _ctx_attn_kernel,
        grid=(batch, N_NA),
        in_specs=[tok(OFF_NAQ), tok(OFF_NAK), tok(OFF_NAV),
                  pl.BlockSpec((1, 1, HEAD_DIM), lambda b, h: (h, 0, 0))],
        out_specs=tok(0),
        out_shape=jax.ShapeDtypeStruct((u_ctx.shape[0], NA_W), BF16),
        compiler_params=_params("parallel", "parallel"),
    )(u_ctx, u_ctx, u_ctx, out_norm.reshape(N_NA, 1, HEAD_DIM))


def _row_tile(rows, target):
    t = min(rows, target)
    assert rows % t == 0
    return t


def _token_stream(xs, seq_len, mod_row, p, layer, w_up_b, ada_l, mixers, w_down_b=None, next_norm=None):
    rows = xs.shape[0]
    sh2, sc2, g1, g2 = ada_l["sh2"], ada_l["sc2"], ada_l["g1"], ada_l["g2"]

    def mod_of_tile(tile_rows):
        per = max(seq_len // tile_rows, 1)
        return (lambda i: 1 + i // per) if mod_row is None else (lambda i: mod_row)

    tm_o = _row_tile(rows, TM_OUT)
    x1, h2 = _proj_norm(mixers, p["w_out"], layer, xs, g1, mod_of_tile(tm_o), tm_o,
                        norm=(p["ln2"][layer], sc2, sh2, mod_of_tile(tm_o)))
    tm = _row_tile(rows, TM_TOKENS)
    conv = (p["f_cw"][layer], p["f_cb"][layer])
    if w_down_b is None:
        act, w_down_b = _ffn_up_conv_gate(h2, w_up_b, *conv, seq_len, tm, TN_FF, side_w=p["w_down"], layer=layer)
    else:
        act = _ffn_up_conv_gate(h2, w_up_b, *conv, seq_len, tm, TN_FF)
    tm_d = _row_tile(rows, TM_DOWN)
    norm = None if next_norm is None else (*next_norm, mod_of_tile(tm_d))
    return _proj_norm([act], w_down_b, None, x1, g2, mod_of_tile(tm_d), tm_d, norm=norm), w_down_b


def kernel(x, c, ctx, c_ctx, w_ada, b_ada, ln1_w, ln2_w, w_in, hg_lb_logits, hg_norm_w, na_q_norm_w, na_k_norm_w,
           na_rpb, na_out_norm_w, cv_w, cv_out_norm_w, w_out, w_up, ffn_conv_w, ffn_conv_b, w_down):
    batch, seq, d = x.shape
    ctx_len = ctx.shape[1]
    rows = seq // GRID_W
    depth = w_ada.shape[0]
    xs = x.reshape(batch * seq, d)
    cs = ctx.reshape(batch * ctx_len, d)

    lb_sm = jax.nn.softmax(hg_lb_logits.astype(F32), axis=1)
    lb_all = jnp.cumsum(lb_sm, axis=1) - lb_sm[:, :1]

    cond = jnp.zeros((SUBLANE, d), F32).at[0].set(c_ctx).at[1:1 + batch].set(c)
    ada = _ada_table(cond, w_ada, b_ada).reshape(depth, SUBLANE, 6, 1, d)

    s_zero = jnp.zeros((batch, N_HG, HEAD_DIM, HEAD_DIM), F32)
    p = {"w_out": w_out, "w_down": w_down, "ln2": ln2_w, "f_cw": ffn_conv_w,
         "f_cb": ffn_conv_b}
    ctx_cols = sum(IN_SPLITS[:5])
    bias = _na_bias_table(na_rpb.reshape((depth * N_NA,) + na_rpb.shape[2:]), rows)
    names = ("sh1", "sc1", "g1", "sh2", "sc2", "g2")
    ada_all = [{n: ada[l, :, i] for i, n in enumerate(names)} for l in range(depth)]
    tm = _row_tile(xs.shape[0], TM_TOKENS)
    tm_c = _row_tile(cs.shape[0], TM_TOKENS)
    tiles_per_seq = seq // tm
    h = _norm_mod(xs, ln1_w[0], ada_all[0]["sh1"], ada_all[0]["sc1"], lambda i: 1 + i // tiles_per_seq, tm)
    hc = _norm_mod(cs, ln1_w[0], ada_all[0]["sh1"], ada_all[0]["sc1"], lambda i: 0, tm_c)
    for l in range(depth):
        last = l == depth - 1
        ada_l = ada_all[l]
        next_norm = None if last else (ln1_w[l + 1], ada_all[l + 1]["sc1"], ada_all[l + 1]["sh1"])
        if (IN_W // TN_IN) * (xs.shape[0] // tm) >= w_up.shape[2] // HEAD_DIM:
            u, w_up_b = _in_proj(h, w_in, l, na_q_norm_w[l], na_k_norm_w[l], IN_W, tm, TN_IN, side_w=w_up)
        else:
            u = _in_proj(h, w_in, l, na_q_norm_w[l], na_k_norm_w[l], IN_W, tm, TN_IN)
            w_up_b = w_up[l].astype(BF16)
        if last:
            uc = _in_proj(hc, w_in, l, na_q_norm_w[l], na_k_norm_w[l], ctx_cols, tm_c, TN_IN_CTX)
        else:
            uc = _in_proj(hc, w_in, l, na_q_norm_w[l], na_k_norm_w[l], IN_W, tm_c, TN_IN)

        if last:
            _, s_fw = _hgrn_scan(uc, lb_all[0, l], s_zero, ctx_len, reverse=False, mode="state")
            _, s_bw = _hgrn_scan(uc, lb_all[1, l], s_zero, ctx_len, reverse=True, mode="state")
        else:
            co_fw, s_fw = _hgrn_scan(uc, lb_all[0, l], s_zero, ctx_len, reverse=False, mode="raw")
            hg_c, s_bw = _hgrn_scan(uc, lb_all[1, l], s_zero, ctx_len, reverse=True, mode="final", o_fw=co_fw,
                                    norm_w=hg_norm_w[l])
        o_fw, _ = _hgrn_scan(u, lb_all[0, l], s_fw, seq, reverse=False, mode="raw")
        hg_out, _ = _hgrn_scan(u, lb_all[1, l], s_bw, seq, reverse=True, mode="final", o_fw=o_fw,
                               norm_w=hg_norm_w[l])

        na_out = _neighbourhood_attention(u, uc, bias, l * N_NA, na_out_norm_w[l], batch, rows, ctx_len)
        cv_out = _short_conv(u, cv_w[l], cv_out_norm_w[l], seq, _row_tile(seq, TM_TOKENS))

        res, w_down_b = _token_stream(xs, seq, None, p, l, w_up_b, ada_l, [hg_out, na_out, cv_out],
                                      next_norm=next_norm)
        if last:
            xs = res
        else:
            xs, h = res
            na_c = _context_attention(uc, na_out_norm_w[l], batch, ctx_len)
            cv_c = _short_conv(uc, cv_w[l], cv_out_norm_w[l], ctx_len, _row_tile(ctx_len, TM_TOKENS))
            (cs, hc), _ = _token_stream(cs, ctx_len, 0, p, l, w_up_b, ada_l, [hg_c, na_c, cv_c],
                                        w_down_b=w_down_b, next_norm=next_norm)
    return xs.reshape(batch, seq, d)
```

```python
import functools

import numpy as np
import jax
import jax.numpy as jnp
from jax import lax
from jax.experimental import pallas as pl
from jax.experimental.pallas import tpu as pltpu

F32 = jnp.float32
BF16 = jnp.bfloat16

D_MODEL = 2048
DEPTH = 2
GRID_W = 64
HEAD_DIM = 128
N_HG = 4
N_NA = 8
N_CV = 4
HG_W = N_HG * HEAD_DIM
NA_W = N_NA * HEAD_DIM
CV_W = N_CV * HEAD_DIM
IN_SPLITS = (HG_W, HG_W, HG_W, NA_W, NA_W, HG_W, HG_W, NA_W, CV_W, CV_W, CV_W)
IN_W = sum(IN_SPLITS)
(OFF_FFW, OFF_FBW, OFF_I, OFF_NAK, OFF_NAV, OFF_HGQ, OFF_HGG, OFF_NAQ, OFF_CVB, OFF_CVC,
 OFF_CVV) = [int(v) // HEAD_DIM for v in np.cumsum((0,) + IN_SPLITS[:-1])]
NA_WIN_R = 8
NA_WIN_C = 16
NA_QBLK_C = 16
NA_KEY_C = 32
NA_KEY_R = 16
NA_HEADS = 4
CONV_W = 3
D_FF = 5632
EPS = 1e-6
F_FLOOR = 1e-30
ATTN_SCALE = HEAD_DIM ** -0.5
NEG_INF = -1e30

HG_CHUNK = 128
HG_BLOCK = 1024
TM_TOKENS = 1024
TN_IN = 1024
TN_IN_CTX = 512
TN_FF = 512
TM_OUT = 512
TM_DOWN = 256
SUBLANE = 8
HALO = 16
MXU_N = 256
TRACE_AHEAD = 3
VMEM_LIMIT = 56 * 1024 * 1024


def _params(*sem):
    return pltpu.CompilerParams(dimension_semantics=sem, vmem_limit_bytes=VMEM_LIMIT)


def _dot(a, b):
    return jnp.dot(a, b, preferred_element_type=F32)


def _dot_nt(a, b):
    return lax.dot_general(a, b, (((1,), (1,)), ((), ())), preferred_element_type=F32)


def _dot_tn(a, b):
    return lax.dot_general(a, b, (((0,), (0,)), ((), ())), preferred_element_type=F32)


def _rms(x, w):
    return x * lax.rsqrt(jnp.mean(x * x, axis=-1, keepdims=True) + EPS) * w


def _silu(x):
    half = 0.5 * x
    return half + half * jnp.tanh(half)


def _ada_kernel(s_ref, w_ref, b_ref, o_ref):
    s = _silu(s_ref[...]).astype(BF16)
    o_ref[0] = _dot(s, w_ref[0].astype(BF16)) + b_ref[0]


def _ada_table(cond, w_ada, b_ada):
    depth, d, n = w_ada.shape
    tn = TN_IN
    return pl.pallas_call(
        _ada_kernel,
        grid=(depth, n // tn),
        in_specs=[pl.BlockSpec((SUBLANE, d), lambda l, j: (0, 0)),
                  pl.BlockSpec((1, d, tn), lambda l, j: (l, 0, j)),
                  pl.BlockSpec((1, 1, tn), lambda l, j: (l, 0, j))],
        out_specs=pl.BlockSpec((1, SUBLANE, tn), lambda l, j: (l, 0, j)),
        out_shape=jax.ShapeDtypeStruct((depth, SUBLANE, n), F32),
        compiler_params=_params("parallel", "parallel"),
    )(cond, w_ada, b_ada.reshape(depth, 1, n))


def _norm_mod_kernel(x_ref, w_ref, sh_ref, sc_ref, o_ref):
    y = _rms(x_ref[...], w_ref[...])
    o_ref[...] = (y * (1.0 + sc_ref[0]) + sh_ref[0]).astype(o_ref.dtype)


def _norm_mod(x, w, shift, scale, mod_of_tile, tr):
    rows, d = x.shape
    mod = lambda i: (mod_of_tile(i), 0, 0)
    return pl.pallas_call(
        _norm_mod_kernel,
        grid=(rows // tr,),
        in_specs=[pl.BlockSpec((tr, d), lambda i: (i, 0)),
                  pl.BlockSpec((1, d), lambda i: (0, 0)),
                  pl.BlockSpec((1, 1, d), mod),
                  pl.BlockSpec((1, 1, d), mod)],
        out_specs=pl.BlockSpec((tr, d), lambda i: (i, 0)),
        out_shape=jax.ShapeDtypeStruct((rows, d), BF16),
        compiler_params=_params("parallel"),
    )(x, w.reshape(1, d), shift, scale)


def _in_proj_kinds(n_cols, tn):
    kinds = []
    for head in range(n_cols // HEAD_DIM):
        if OFF_NAK <= head < OFF_NAV:
            kinds.append("k")
        elif OFF_NAQ <= head < OFF_CVB:
            kinds.append("q")
        elif OFF_HGQ <= head < OFF_HGG:
            kinds.append("silu")
        else:
            kinds.append("id")
    per = tn // HEAD_DIM
    return tuple(tuple(kinds[t * per:(t + 1) * per]) for t in range(n_cols // tn))


def _in_proj_kernel(*refs, kinds, with_cast):
    if with_cast:
        a_ref, w_ref, qw_ref, kw_ref, side_ref, o_ref, side_o_ref, wb_ref = refs
        side_o_ref[...] = side_ref[...].astype(BF16)
    else:
        a_ref, w_ref, qw_ref, kw_ref, o_ref, wb_ref = refs
    j = pl.program_id(0)

    @pl.when(pl.program_id(1) == 0)
    def _():
        wb_ref[...] = w_ref[...].astype(BF16)

    def transform(y, kind):
        if kind == "k":
            return _rms(y, kw_ref[...])
        if kind == "q":
            return _rms(y, qw_ref[...]) * ATTN_SCALE
        if kind == "silu":
            return _silu(y)
        return y

    per_slab = MXU_N // HEAD_DIM
    for pattern in sorted(set(kinds)):
        hit = functools.reduce(jnp.logical_or, [j == jj for jj, tk in enumerate(kinds) if tk == pattern])

        @pl.when(hit)
        def _(pattern=pattern):
            a = a_ref[...]
            for c in range(len(pattern) // per_slab):
                cols = slice(c * MXU_N, (c + 1) * MXU_N)
                acc = _dot(a, wb_ref[:, cols])
                parts = [transform(acc[:, s * HEAD_DIM:(s + 1) * HEAD_DIM], pattern[c * per_slab + s])
                         for s in range(per_slab)]
                o_ref[:, cols] = jnp.concatenate(parts, axis=1).astype(o_ref.dtype)


def _in_proj(h, w_in, layer, q_norm, k_norm, n_cols, tm, tn, side_w=None):
    m, k = h.shape
    kinds = _in_proj_kinds(n_cols, tn)
    nj, ni = n_cols // tn, m // tm
    vec = pl.BlockSpec((1, HEAD_DIM), lambda j, i: (0, 0))
    in_specs = [pl.BlockSpec((tm, k), lambda j, i: (i, 0)),
                pl.BlockSpec((None, k, tn), lambda j, i: (layer, 0, j)),
                vec, vec]
    args = [h, w_in, q_norm.reshape(1, HEAD_DIM), k_norm.reshape(1, HEAD_DIM)]
    out_specs = [pl.BlockSpec((tm, tn), lambda j, i: (i, j))]
    out_shape = [jax.ShapeDtypeStruct((m, n_cols), BF16)]
    if side_w is not None:
        _, k2, n2 = side_w.shape
        strips = n2 // HEAD_DIM
        assert strips <= nj * ni
        strip = lambda j, i: jnp.minimum(j * ni + i, strips - 1)
        in_specs.append(pl.BlockSpec((None, k2, HEAD_DIM), lambda j, i: (layer, 0, strip(j, i))))
        args.append(side_w)
        out_specs.append(pl.BlockSpec((k2, HEAD_DIM), lambda j, i: (0, strip(j, i))))
        out_shape.append(jax.ShapeDtypeStruct((k2, n2), BF16))
    res = pl.pallas_call(
        functools.partial(_in_proj_kernel, kinds=kinds, with_cast=side_w is not None),
        grid=(nj, ni),
        in_specs=in_specs,
        out_specs=out_specs,
        out_shape=out_shape,
        scratch_shapes=[pltpu.VMEM((k, tn), BF16)],
        compiler_params=_params("arbitrary", "arbitrary"),
    )(*args)
    return res if side_w is not None else res[0]


def _proj_norm_kernel(*refs, splits, cast_weight, with_norm):
    n_a = len(splits)
    a_refs, rest = refs[:n_a], list(refs[n_a:])
    w_ref, x_ref, g_ref = rest[:3]
    rest = rest[3:]
    if with_norm:
        lw_ref, sc_ref, sh_ref, x1_ref, h2_ref = rest[:5]
        rest = rest[5:]
    else:
        x1_ref = rest.pop(0)
    if cast_weight:
        wb_ref = rest.pop(0)

        @pl.when(pl.program_id(0) == 0)
        def _():
            wb_ref[...] = w_ref[...].astype(BF16)
    else:
        wb_ref = w_ref

    tm, n = x1_ref.shape
    ssq = jnp.zeros((tm, HEAD_DIM), F32)
    for c in range(n // MXU_N):
        cols = slice(c * MXU_N, (c + 1) * MXU_N)
        acc = None
        off = 0
        for a_ref, width in zip(a_refs, splits):
            part = _dot(a_ref[...], wb_ref[off:off + width, cols])
            acc = part if acc is None else acc + part
            off += width
        x1 = x_ref[:, cols] + g_ref[0][:, cols] * acc
        x1_ref[:, cols] = x1
        if with_norm:
            sq = x1 * x1
            for s in range(MXU_N // HEAD_DIM):
                ssq = ssq + sq[:, s * HEAD_DIM:(s + 1) * HEAD_DIM]
    if with_norm:
        inv = lax.rsqrt(jnp.sum(ssq, axis=-1, keepdims=True) * (1.0 / n) + EPS)
        gain = lw_ref[...] * (1.0 + sc_ref[0])
        for c in range(n // MXU_N):
            cols = slice(c * MXU_N, (c + 1) * MXU_N)
            h2_ref[:, cols] = (x1_ref[:, cols] * inv * gain[:, cols] + sh_ref[0][:, cols]).astype(h2_ref.dtype)


def _proj_norm(a_list, w, layer, x, gate, mod_of_tile, tm, norm=None):
    m, n = x.shape
    cast_weight = w.ndim == 3
    k = w.shape[-2]
    splits = tuple(a.shape[1] for a in a_list)
    assert sum(splits) == k
    in_specs = [pl.BlockSpec((tm, s), lambda i: (i, 0)) for s in splits]
    if cast_weight:
        in_specs.append(pl.BlockSpec((None, k, n), lambda i: (layer, 0, 0), pipeline_mode=pl.Buffered(1)))
    else:
        in_specs.append(pl.BlockSpec((k, n), lambda i: (0, 0), pipeline_mode=pl.Buffered(1)))
    in_specs += [pl.BlockSpec((tm, n), lambda i: (i, 0)),
                 pl.BlockSpec((1, 1, n), lambda i: (mod_of_tile(i), 0, 0))]
    args = [*a_list, w, x, gate]
    out_specs = [pl.BlockSpec((tm, n), lambda i: (i, 0))]
    out_shape = [jax.ShapeDtypeStruct((m, n), F32)]
    if norm is not None:
        ln_w, scale, shift, norm_mod = norm
        mod = lambda i: (norm_mod(i), 0, 0)
        in_specs += [pl.BlockSpec((1, n), lambda i: (0, 0)), pl.BlockSpec((1, 1, n), mod),
                     pl.BlockSpec((1, 1, n), mod)]
        args += [ln_w.reshape(1, n), scale, shift]
        out_specs.append(pl.BlockSpec((tm, n), lambda i: (i, 0)))
        out_shape.append(jax.ShapeDtypeStruct((m, n), BF16))
    res = pl.pallas_call(
        functools.partial(_proj_norm_kernel, splits=splits, cast_weight=cast_weight, with_norm=norm is not None),
        grid=(m // tm,),
        in_specs=in_specs,
        out_specs=out_specs,
        out_shape=out_shape,
        scratch_shapes=[pltpu.VMEM((k, n), BF16)] if cast_weight else [],
        compiler_params=_params("arbitrary"),
    )(*args)
    return res if norm is not None else res[0]


def _shift_rows(x, prev_row, next_row):
    r = x.shape[0]
    idx = lax.broadcasted_iota(jnp.int32, x.shape, 0)
    down = jnp.where(idx == 0, prev_row, pltpu.roll(x, 1, axis=0))
    up = jnp.where(idx == r - 1, next_row, pltpu.roll(x, r - 1, axis=0))
    return down, up


def _halo_specs(tr, tc, col_of, tiles_per_seq, n_rows):
    per = tr // HALO
    last_blk = n_rows // HALO - 1
    cur = pl.BlockSpec((tr, tc), lambda i, j: (i, col_of(j)))
    prev = pl.BlockSpec((HALO, tc), lambda i, j: (jnp.maximum(i * per - 1, 0), col_of(j)))
    nxt = pl.BlockSpec((HALO, tc), lambda i, j: (jnp.minimum((i + 1) * per, last_blk), col_of(j)))
    return cur, prev, nxt


def _seq_edges(tiles_per_seq):
    i = pl.program_id(0)
    pos = i % tiles_per_seq
    return (pos != 0).astype(F32), (pos != tiles_per_seq - 1).astype(F32)


def _ffn_up_kernel(*refs, tm, seq_len, with_cast):
    if with_cast:
        (h_ref, hp_ref, hn_ref, wg_ref, wv_ref, cg_ref, cv_ref, bg_ref, bv_ref, side_ref,
         o_ref, side_o_ref, lhs_ref) = refs
        side_o_ref[...] = side_ref[...].astype(BF16)
    else:
        h_ref, hp_ref, hn_ref, wg_ref, wv_ref, cg_ref, cv_ref, bg_ref, bv_ref, o_ref, lhs_ref = refs
    tiles_per_seq = max(seq_len // tm, 1)
    pos = pl.program_id(0) % tiles_per_seq
    has_prev = pos != 0
    has_next = pos != tiles_per_seq - 1

    @pl.when(pl.program_id(1) == 0)
    def _():
        lhs_ref[0:HALO, :] = jnp.where(has_prev, hp_ref[...], jnp.zeros_like(hp_ref))
        lhs_ref[HALO:HALO + tm, :] = h_ref[...]
        lhs_ref[HALO + tm:, :] = jnp.where(has_next, hn_ref[...], jnp.zeros_like(hn_ref))

    lhs = lhs_ref[...]

    def branch(w_ref, c_ref, b_ref):
        acc = _dot(lhs, w_ref[...])
        r = acc.shape[0]
        down, up = pltpu.roll(acc, 1, axis=0), pltpu.roll(acc, r - 1, axis=0)
        if seq_len < tm:
            tok = (lax.broadcasted_iota(jnp.int32, acc.shape, 0) + (seq_len - HALO)) % seq_len
            down = jnp.where(tok == 0, 0.0, down)
            up = jnp.where(tok == seq_len - 1, 0.0, up)
        y = down * c_ref[0:1, :] + acc * c_ref[1:2, :] + up * c_ref[2:3, :]
        return y[HALO:HALO + tm, :] + b_ref[...]

    gate = branch(wg_ref, cg_ref, bg_ref)
    val = branch(wv_ref, cv_ref, bv_ref)
    o_ref[...] = (_silu(gate) * val).astype(o_ref.dtype)


def _ffn_up_conv_gate(h, w_up, cw, cb, seq_len, tm, tn, side_w=None, layer=0):
    rows, k = h.shape
    assert seq_len % tm == 0 or (tm % seq_len == 0 and seq_len >= HALO)
    nj = D_FF // tn
    per = tm // HALO
    last_blk = rows // HALO - 1
    cb2 = cb.reshape(1, 2 * D_FF)
    steps = (rows // tm) * nj
    if side_w is not None:
        _, k2, n2 = side_w.shape
        strips = max(d for d in range(1, steps + 1) if k2 % d == 0 and (k2 // d) % HALO == 0)
        strip = lambda i, j: jnp.minimum(i * nj + j, strips - 1)
        side_in = [pl.BlockSpec((None, k2 // strips, n2), lambda i, j: (layer, strip(i, j), 0))]
        side_out = [pl.BlockSpec((k2 // strips, n2), lambda i, j: (strip(i, j), 0))]
        side_shape = [jax.ShapeDtypeStruct((k2, n2), BF16)]
        side_args = [side_w]
    else:
        side_in, side_out, side_shape, side_args = [], [], [], []
    in_specs = [pl.BlockSpec((tm, k), lambda i, j: (i, 0)),
                pl.BlockSpec((HALO, k), lambda i, j: (jnp.maximum(i * per - 1, 0), 0)),
                pl.BlockSpec((HALO, k), lambda i, j: (jnp.minimum((i + 1) * per, last_blk), 0)),
                pl.BlockSpec((k, tn), lambda i, j: (0, j)),
                pl.BlockSpec((k, tn), lambda i, j: (0, j + nj)),
                pl.BlockSpec((CONV_W, tn), lambda i, j: (0, j)),
                pl.BlockSpec((CONV_W, tn), lambda i, j: (0, j + nj)),
                pl.BlockSpec((1, tn), lambda i, j: (0, j)),
                pl.BlockSpec((1, tn), lambda i, j: (0, j + nj))]
    res = pl.pallas_call(
        functools.partial(_ffn_up_kernel, tm=tm, seq_len=seq_len, with_cast=side_w is not None),
        grid=(rows // tm, nj),
        in_specs=in_specs + side_in,
        out_specs=[pl.BlockSpec((tm, tn), lambda i, j: (i, j))] + side_out,
        out_shape=[jax.ShapeDtypeStruct((rows, D_FF), BF16)] + side_shape,
        scratch_shapes=[pltpu.VMEM((tm + 2 * HALO, k), BF16)],
        compiler_params=_params("arbitrary", "arbitrary"),
    )(h, h, h, w_up, w_up, cw, cw, cb2, cb2, *side_args)
    return res if side_w is not None else res[0]


def _short_conv_kernel(b_ref, c_ref, cp_ref, cn_ref, v_ref, vp_ref, vn_ref, w_ref, nw_ref, o_ref, *, tiles_per_seq):
    has_prev, has_next = _seq_edges(tiles_per_seq)
    p = c_ref[...].astype(F32) * v_ref[...].astype(F32)
    prev_row = cp_ref[HALO - 1:HALO, :].astype(F32) * vp_ref[HALO - 1:HALO, :].astype(F32) * has_prev
    next_row = cn_ref[0:1, :].astype(F32) * vn_ref[0:1, :].astype(F32) * has_next
    down, up = _shift_rows(p, prev_row, next_row)
    y = b_ref[...].astype(F32) * (down * w_ref[0:1, :] + p * w_ref[1:2, :] + up * w_ref[2:3, :])
    for h in range(N_CV):
        sl = slice(h * HEAD_DIM, (h + 1) * HEAD_DIM)
        o_ref[:, sl] = _rms(y[:, sl], nw_ref[:, sl]).astype(o_ref.dtype)


def _short_conv(u, cv_w, cv_onorm, seq_len, tr):
    rows = u.shape[0]
    tiles_per_seq = seq_len // tr
    blk = lambda off: (lambda j: off * HEAD_DIM // CV_W)
    b_spec = pl.BlockSpec((tr, CV_W), lambda i, j: (i, OFF_CVB * HEAD_DIM // CV_W))
    c_specs = _halo_specs(tr, CV_W, blk(OFF_CVC), tiles_per_seq, rows)
    v_specs = _halo_specs(tr, CV_W, blk(OFF_CVV), tiles_per_seq, rows)
    return pl.pallas_call(
        functools.partial(_short_conv_kernel, tiles_per_seq=tiles_per_seq),
        grid=(rows // tr, 1),
        in_specs=[b_spec, *c_specs, *v_specs,
                  pl.BlockSpec((CONV_W, CV_W), lambda i, j: (0, 0)),
                  pl.BlockSpec((1, CV_W), lambda i, j: (0, 0))],
        out_specs=pl.BlockSpec((tr, CV_W), lambda i, j: (i, 0)),
        out_shape=jax.ShapeDtypeStruct((rows, CV_W), BF16),
        compiler_params=_params("parallel", "arbitrary"),
    )(u, u, u, u, u, u, u, cv_w, cv_onorm.reshape(1, CV_W))


def _hgrn_structure(chunk, reverse):
    idx = np.arange(chunk)
    i, t = idx[:, None], idx[None, :]
    mats = [t <= i, t > i]
    masks = []
    s = chunk // 2
    while s >= 1:
        blk = idx // (2 * s)
        upper = (idx % (2 * s)) >= s
        mid = blk * 2 * s + s - 1
        a = np.where(upper[:, None], (t > mid[:, None]) & (t <= i), (t > i) & (t <= mid[:, None]))
        mats.append(a)
        masks.append((blk[:, None] == blk[None, :]) & upper[:, None] & ~upper[None, :])
        s //= 2
    mats = np.stack(mats).astype(np.float32)
    masks = np.stack(masks).astype(np.float32)
    if reverse:
        mats = mats[:, ::-1, ::-1]
        masks = masks[:, ::-1, ::-1]
    return mats.reshape(-1, chunk), masks


def _hgrn_kernel(*refs, chunk, n_sub, n_levels, reverse, mode):
    if mode == "final":
        (z_ref, v_ref, q_ref, lb_ref, a_ref, m_ref, s0_ref, g_ref, ofw_ref, nw_ref,
         o_ref, sfin_ref, st_ref) = refs
    elif mode == "raw":
        z_ref, v_ref, q_ref, lb_ref, a_ref, m_ref, s0_ref, o_ref, sfin_ref, st_ref = refs
    else:
        z_ref, v_ref, lb_ref, a_ref, s0_ref, sfin_ref, st_ref = refs
    c = pl.program_id(1)

    @pl.when(c == 0)
    def _():
        st_ref[...] = s0_ref[0]

    lb = lb_ref[...]
    a = a_ref[...]
    tot_row = 0 if reverse else chunk - 1
    subs = list(range(n_sub - 1, -1, -1) if reverse else range(n_sub))

    def gates(sub):
        rows = slice(sub * chunk, (sub + 1) * chunk)
        z = z_ref[rows, :].astype(F32)
        f = lb + (1.0 - lb) * jax.nn.sigmoid(z)
        log_f = jnp.log(jnp.maximum(f, F_FLOOR))
        k_all = (1.0 - lb) * jax.nn.sigmoid(-z)
        expo = _dot(a, log_f.astype(BF16))
        return k_all, v_ref[rows, :].astype(F32), expo

    def products(sub, h, k_all, v_all, expo):
        rows = slice(sub * chunk, (sub + 1) * chunk)
        sl = slice(h * HEAD_DIM, (h + 1) * HEAD_DIM)
        k, v = k_all[:, sl], v_all[:, sl]
        e_rem = jnp.exp(expo[chunk:2 * chunk, sl])
        e_tot = jnp.exp(expo[tot_row:tot_row + 1, sl])
        st = st_ref[h]
        st_ref[h] = st * e_tot + _dot_tn(v.astype(BF16), (k * e_rem).astype(BF16))
        if mode == "state":
            return None
        q = q_ref[rows, sl].astype(F32)
        e_cum = jnp.exp(expo[0:chunk, sl])
        o_inter = _dot_nt((q * e_cum).astype(BF16), st.astype(BF16))
        parts = []
        for lvl in range(n_levels):
            e = jnp.exp(expo[(2 + lvl) * chunk:(3 + lvl) * chunk, sl])
            parts.append(_dot_nt((q * e).astype(BF16), (k * e).astype(BF16)))
        diag = jnp.sum(q * k, axis=-1, keepdims=True)
        return o_inter, parts, diag, v

    def readout(sub, h, o_inter, parts, diag, v):
        rows = slice(sub * chunk, (sub + 1) * chunk)
        sl = slice(h * HEAD_DIM, (h + 1) * HEAD_DIM)
        att = None
        for lvl, part in enumerate(parts):
            att = m_ref[lvl] * part if att is None else att + m_ref[lvl] * part
        o = o_inter + _dot(att.astype(BF16), v.astype(BF16)) + diag * v
        if mode == "final":
            y = _rms(o + ofw_ref[rows, sl], nw_ref[...]) * _silu(g_ref[rows, sl].astype(F32))
            o_ref[rows, sl] = y.astype(o_ref.dtype)
        else:
            o_ref[rows, sl] = o.astype(o_ref.dtype)

    tiles = [(sub, h) for sub in subs for h in range(N_HG)]
    cache = {}

    def stage(sub, h):
        if sub not in cache:
            cache.clear()
            cache[sub] = gates(sub)
        return products(sub, h, *cache[sub])

    queue = [stage(*tile) for tile in tiles[:TRACE_AHEAD]]
    for t, tile in enumerate(tiles):
        if t + TRACE_AHEAD < len(tiles):
            queue.append(stage(*tiles[t + TRACE_AHEAD]))
        done = queue.pop(0)
        if mode != "state":
            readout(*tile, *done)

    @pl.when(c == pl.num_programs(1) - 1)
    def _():
        sfin_ref[0] = st_ref[...]


def _hgrn_scan(u, lb, s0, seq_len, *, reverse, mode, o_fw=None, norm_w=None):
    rows = u.shape[0]
    batch = rows // seq_len
    blk = min(HG_BLOCK, seq_len)
    chunk = min(HG_CHUNK, blk)
    n_blocks = seq_len // blk
    mats, masks = _hgrn_structure(chunk, reverse)
    n_levels = masks.shape[0]
    if mode == "state":
        mats = mats[:2 * chunk]
    per_blk = HG_W // HEAD_DIM

    def tok(off):
        col = off // per_blk
        if reverse:
            return pl.BlockSpec((blk, HG_W), lambda b, c: (b * n_blocks + n_blocks - 1 - c, col))
        return pl.BlockSpec((blk, HG_W), lambda b, c: (b * n_blocks + c, col))

    const2 = lambda shape: pl.BlockSpec(shape, lambda b, c: (0, 0))
    state_spec = pl.BlockSpec((1, N_HG, HEAD_DIM, HEAD_DIM), lambda b, c: (b, 0, 0, 0))
    state_shape = jax.ShapeDtypeStruct((batch, N_HG, HEAD_DIM, HEAD_DIM), F32)
    z_spec = tok(OFF_FBW if reverse else OFF_FFW)
    lb2, mats_b = lb.reshape(1, HG_W), jnp.asarray(mats, BF16)
    if mode == "state":
        in_specs = [z_spec, tok(OFF_I), const2((1, HG_W)), const2(mats.shape), state_spec]
        args = [u, u, lb2, mats_b, s0]
        out_specs, out_shape = [state_spec], [state_shape]
    else:
        in_specs = [z_spec, tok(OFF_I), tok(OFF_HGQ), const2((1, HG_W)), const2(mats.shape),
                    pl.BlockSpec(masks.shape, lambda b, c: (0, 0, 0)), state_spec]
        args = [u, u, u, lb2, mats_b, jnp.asarray(masks, F32), s0]
        if mode == "final":
            in_specs += [tok(OFF_HGG), tok(0), const2((1, HEAD_DIM))]
            args += [u, o_fw, norm_w.reshape(1, HEAD_DIM)]
        out_specs = [tok(0), state_spec]
        out_shape = [jax.ShapeDtypeStruct((rows, HG_W), BF16 if mode == "final" else F32), state_shape]
    res = pl.pallas_call(
        functools.partial(_hgrn_kernel, chunk=chunk, n_sub=blk // chunk, n_levels=n_levels, reverse=reverse,
                          mode=mode),
        grid=(batch, n_blocks),
        in_specs=in_specs,
        out_specs=out_specs,
        out_shape=out_shape,
        scratch_shapes=[pltpu.VMEM((N_HG, HEAD_DIM, HEAD_DIM), F32)],
        compiler_params=_params("parallel", "arbitrary"),
    )(*args)
    return (None, res[0]) if mode == "state" else (res[0], res[1])


def _na_bias_table(rpb, rows):
    n_rb = rows // NA_WIN_R
    n_cb = GRID_W // NA_QBLK_C
    n_dr, n_dc = 2 * NA_WIN_R - 1, 2 * NA_WIN_C - 1
    n_k = NA_KEY_R * NA_KEY_C
    offs, row_mask = [], []
    for rb in (0, min(1, n_rb - 1), n_rb - 1):
        k_row0 = int(np.clip(rb * NA_WIN_R - NA_WIN_R // 2, 0, rows - NA_KEY_R))
        q_r = rb * NA_WIN_R + np.arange(NA_WIN_R)
        k_r = k_row0 + np.arange(NA_KEY_R)
        r_start = np.clip(q_r - NA_WIN_R // 2, 0, rows - NA_WIN_R)
        ok_r = (k_r[None, :] >= r_start[:, None]) & (k_r[None, :] < r_start[:, None] + NA_WIN_R)
        row_mask.append(np.repeat(np.where(ok_r, 0.0, NEG_INF), NA_KEY_C, axis=1))
        offs.append(k_row0 - rb * NA_WIN_R + NA_WIN_R - 1)
    row_mask = jnp.asarray(np.stack(row_mask), F32)
    pad_lo = NA_WIN_R - 1 - min(offs)
    per_vreg = HEAD_DIM // NA_KEY_C
    n_a = -(-(NA_KEY_R - 1 + max(offs) + pad_lo + 1) // per_vreg) * per_vreg
    lanes = n_a * NA_KEY_C
    sel_a = jnp.asarray(np.arange(n_a)[:, None] - pad_lo == np.arange(n_dr)[None, :], F32)
    sel_c, ok_c = [], []
    for cb in range(n_cb):
        k_col0 = int(np.clip(cb * NA_QBLK_C - NA_WIN_C // 2, 0, GRID_W - NA_KEY_C))
        q_c = cb * NA_QBLK_C + np.arange(NA_QBLK_C)
        k_c = k_col0 + np.arange(NA_KEY_C)
        c_start = np.clip(q_c - NA_WIN_C // 2, 0, GRID_W - NA_WIN_C)
        ok_c.append((k_c[None, :] >= c_start[:, None]) & (k_c[None, :] < c_start[:, None] + NA_WIN_C))
        dc = k_c[None, :] - q_c[:, None] + NA_WIN_C - 1
        sel_c.append(dc[:, :, None] == np.arange(n_dc))
    sel_c = jnp.asarray(np.stack(sel_c), F32)
    ok_c = jnp.asarray(np.stack(ok_c))
    strip = jnp.einsum("xa,hab,mcjb->hmcxj", sel_a, rpb.astype(F32), sel_c, precision=lax.Precision.HIGHEST)
    strip = jnp.where(ok_c[None, :, :, None, :], strip, NEG_INF)
    strip = strip.reshape(strip.shape[:3] + (lanes,))
    n_heads = rpb.shape[0]
    return pl.pallas_call(
        functools.partial(_na_bias_kernel, offs=tuple(offs), pad_lo=pad_lo),
        grid=(n_heads,),
        in_specs=[pl.BlockSpec((1, n_cb, NA_QBLK_C, lanes), lambda h: (h, 0, 0, 0)),
                  pl.BlockSpec((3, NA_WIN_R, n_k), lambda h: (0, 0, 0))],
        out_specs=pl.BlockSpec((1, 3, n_cb, NA_WIN_R * NA_QBLK_C, n_k), lambda h: (h, 0, 0, 0, 0)),
        out_shape=jax.ShapeDtypeStruct((n_heads, 3, n_cb, NA_WIN_R * NA_QBLK_C, n_k), F32),
        compiler_params=_params("parallel"),
    )(strip, row_mask)


def _na_bias_kernel(t_ref, rm_ref, o_ref, *, offs, pad_lo):
    n_cb = t_ref.shape[1]
    lanes = t_ref.shape[-1]
    per = HEAD_DIM // NA_KEY_C
    n_k = o_ref.shape[-1]
    for cb in range(n_cb):
        t = t_ref[0, cb]
        shifted = [t] + [pltpu.roll(t, lanes - NA_KEY_C * s, axis=1) for s in range(1, per)]
        for cls, off in enumerate(offs):
            for r in range(NA_WIN_R):
                m = off - r + pad_lo
                base = (m // per) * HEAD_DIM
                tile = shifted[m % per][:, base:base + n_k] + rm_ref[cls, r:r + 1, :]
                o_ref[0, cls, cb, r * NA_QBLK_C:(r + 1) * NA_QBLK_C, :] = tile


def _na_kernel(q_ref, k0_ref, k1_ref, k2_ref, k3_ref, v0_ref, v1_ref, v2_ref, v3_ref, kc_ref, vc_ref, bias_ref,
               ow_ref, o_ref):
    n_cb = GRID_W // NA_QBLK_C
    q_all = q_ref[...].astype(F32)
    k_all = jnp.concatenate([r[...] for r in (k0_ref, k1_ref, k2_ref, k3_ref)], axis=0).astype(F32)
    v_all = jnp.concatenate([r[...] for r in (v0_ref, v1_ref, v2_ref, v3_ref)], axis=0).astype(F32)
    def patch(x_all, hh, rows, col0, cols):
        x3 = x_all[:, hh * HEAD_DIM:(hh + 1) * HEAD_DIM].reshape(rows, GRID_W, HEAD_DIM)
        return x3[:, col0:col0 + cols, :].reshape(rows * cols, HEAD_DIM).astype(BF16)

    def scores(hh, cb):
        q0 = cb * NA_QBLK_C
        k_col0 = min(max(q0 - NA_WIN_C // 2, 0), GRID_W - NA_KEY_C)
        qm = patch(q_all, hh, NA_WIN_R, q0, NA_QBLK_C)
        km = patch(k_all, hh, NA_KEY_R, k_col0, NA_KEY_C)
        s_loc = _dot_nt(qm, km) + bias_ref[hh, 0, cb]
        s_ctx = _dot_nt(qm, kc_ref[:, hh * HEAD_DIM:(hh + 1) * HEAD_DIM])
        return s_loc, s_ctx

    def finish(hh, cb, s_loc, s_ctx):
        q0 = cb * NA_QBLK_C
        k_col0 = min(max(q0 - NA_WIN_C // 2, 0), GRID_W - NA_KEY_C)
        hs = slice(hh * HEAD_DIM, (hh + 1) * HEAD_DIM)
        vm = patch(v_all, hh, NA_KEY_R, k_col0, NA_KEY_C)
        mx = jnp.maximum(jnp.max(s_loc, axis=-1, keepdims=True), jnp.max(s_ctx, axis=-1, keepdims=True))
        p_loc = jnp.exp(s_loc - mx)
        p_ctx = jnp.exp(s_ctx - mx)
        den = jnp.sum(p_loc, axis=-1, keepdims=True) + jnp.sum(p_ctx, axis=-1, keepdims=True)
        o = (_dot(p_loc.astype(BF16), vm) + _dot(p_ctx.astype(BF16), vc_ref[:, hs])) / den
        o = _rms(o, ow_ref[hh]).astype(o_ref.dtype)
        for r in range(NA_WIN_R):
            o_ref[r * GRID_W + q0:r * GRID_W + q0 + NA_QBLK_C, hs] = o[r * NA_QBLK_C:(r + 1) * NA_QBLK_C, :]

    tiles = [(hh, cb) for hh in range(NA_HEADS) for cb in range(n_cb)]
    queue = [scores(*tile) for tile in tiles[:TRACE_AHEAD]]
    for t, tile in enumerate(tiles):
        if t + TRACE_AHEAD < len(tiles):
            queue.append(scores(*tiles[t + TRACE_AHEAD]))
        finish(*tile, *queue.pop(0))


def _neighbourhood_attention(u, u_ctx, bias, head0, out_norm, batch, rows, ctx_len):
    assert rows % NA_WIN_R == 0 and rows >= NA_KEY_R
    n_rb = rows // NA_WIN_R
    q_blk = NA_WIN_R * GRID_W
    k_blk = NA_KEY_R * GRID_W // 4
    k_per_batch = rows * GRID_W // k_blk

    width = NA_HEADS * HEAD_DIM
    assert all(off % NA_HEADS == 0 for off in (OFF_NAQ, OFF_NAK, OFF_NAV, head0, N_NA))
    col_q, col_k, col_v = OFF_NAQ // NA_HEADS, OFF_NAK // NA_HEADS, OFF_NAV // NA_HEADS

    def key_spec(col, part):
        def index(h, rb, b):
            first = jnp.clip(rb * NA_WIN_R - NA_WIN_R // 2, 0, rows - NA_KEY_R) * GRID_W // k_blk
            return (b * k_per_batch + first + part, col + h)
        return pl.BlockSpec((k_blk, width), index)

    def row_class(rb):
        return jnp.where(rb == 0, 0, jnp.where(rb == n_rb - 1, 2, 1))

    n_cb = GRID_W // NA_QBLK_C
    in_specs = [pl.BlockSpec((q_blk, width), lambda h, rb, b: (b * n_rb + rb, col_q + h))]
    in_specs += [key_spec(col_k, p) for p in range(4)] + [key_spec(col_v, p) for p in range(4)]
    in_specs += [pl.BlockSpec((ctx_len, width), lambda h, rb, b: (b, col_k + h)),
                 pl.BlockSpec((ctx_len, width), lambda h, rb, b: (b, col_v + h)),
                 pl.BlockSpec((NA_HEADS, 1, n_cb, NA_WIN_R * NA_QBLK_C, NA_KEY_R * NA_KEY_C),
                              lambda h, rb, b: (head0 // NA_HEADS + h, row_class(rb), 0, 0, 0)),
                 pl.BlockSpec((NA_HEADS, 1, HEAD_DIM), lambda h, rb, b: (h, 0, 0))]
    return pl.pallas_call(
        _na_kernel,
        grid=(N_NA // NA_HEADS, n_rb, batch),
        in_specs=in_specs,
        out_specs=pl.BlockSpec((q_blk, width), lambda h, rb, b: (b * n_rb + rb, h)),
        out_shape=jax.ShapeDtypeStruct((u.shape[0], NA_W), BF16),
        compiler_params=_params("parallel", "parallel", "parallel"),
    )(u, u, u, u, u, u, u, u, u, u_ctx, u_ctx, bias, out_norm.reshape(N_NA, 1, HEAD_DIM))


def _ctx_attn_kernel(q_ref, k_ref, v_ref, ow_ref, o_ref):
    s = _dot_nt(q_ref[...].astype(BF16), k_ref[...].astype(BF16))
    p = jnp.exp(s - jnp.max(s, axis=-1, keepdims=True))
    o = _dot(p.astype(BF16), v_ref[...].astype(BF16)) / jnp.sum(p, axis=-1, keepdims=True)
    o_ref[...] = _rms(o, ow_ref[0]).astype(o_ref.dtype)


def _context_attention(u_ctx, out_norm, batch, ctx_len):
    tok = lambda off: pl.BlockSpec((ctx_len, HEAD_DIM), lambda b, h: (b, off + h))
    return pl.pallas_call(
        _ctx_attn_kernel,
        grid=(batch, N_NA),
        in_specs=[tok(OFF_NAQ), tok(OFF_NAK), tok(OFF_NAV),
                  pl.BlockSpec((1, 1, HEAD_DIM), lambda b, h: (h, 0, 0))],
        out_specs=tok(0),
        out_shape=jax.ShapeDtypeStruct((u_ctx.shape[0], NA_W), BF16),
        compiler_params=_params("parallel", "parallel"),
    )(u_ctx, u_ctx, u_ctx, out_norm.reshape(N_NA, 1, HEAD_DIM))


def _row_tile(rows, target):
    t = min(rows, target)
    assert rows % t == 0
    return t


def _token_stream(xs, seq_len, mod_row, p, layer, w_up_b, ada_l, mixers, w_down_b=None, next_norm=None):
    rows = xs.shape[0]
    sh2, sc2, g1, g2 = ada_l["sh2"], ada_l["sc2"], ada_l["g1"], ada_l["g2"]

    def mod_of_tile(tile_rows):
        per = max(seq_len // tile_rows, 1)
        return (lambda i: 1 + i // per) if mod_row is None else (lambda i: mod_row)

    tm_o = _row_tile(rows, TM_OUT)
    x1, h2 = _proj_norm(mixers, p["w_out"], layer, xs, g1, mod_of_tile(tm_o), tm_o,
                        norm=(p["ln2"][layer], sc2, sh2, mod_of_tile(tm_o)))
    tm = _row_tile(rows, TM_TOKENS)
    conv = (p["f_cw"][layer], p["f_cb"][layer])
    if w_down_b is None:
        act, w_down_b = _ffn_up_conv_gate(h2, w_up_b, *conv, seq_len, tm, TN_FF, side_w=p["w_down"], layer=layer)
    else:
        act = _ffn_up_conv_gate(h2, w_up_b, *conv, seq_len, tm, TN_FF)
    tm_d = _row_tile(rows, TM_DOWN)
    norm = None if next_norm is None else (*next_norm, mod_of_tile(tm_d))
    return _proj_norm([act], w_down_b, None, x1, g2, mod_of_tile(tm_d), tm_d, norm=norm), w_down_b


def kernel(x, c, ctx, c_ctx, w_ada, b_ada, ln1_w, ln2_w, w_in, hg_lb_logits, hg_norm_w, na_q_norm_w, na_k_norm_w,
           na_rpb, na_out_norm_w, cv_w, cv_out_norm_w, w_out, w_up, ffn_conv_w, ffn_conv_b, w_down):
    batch, seq, d = x.shape
    ctx_len = ctx.shape[1]
    rows = seq // GRID_W
    depth = w_ada.shape[0]
    xs = x.reshape(batch * seq, d)
    cs = ctx.reshape(batch * ctx_len, d)

    lb_sm = jax.nn.softmax(hg_lb_logits.astype(F32), axis=1)
    lb_all = jnp.cumsum(lb_sm, axis=1) - lb_sm[:, :1]

    cond = jnp.zeros((SUBLANE, d), F32).at[0].set(c_ctx).at[1:1 + batch].set(c)
    ada = _ada_table(cond, w_ada, b_ada).reshape(depth, SUBLANE, 6, 1, d)

    s_zero = jnp.zeros((batch, N_HG, HEAD_DIM, HEAD_DIM), F32)
    p = {"w_out": w_out, "w_down": w_down, "ln2": ln2_w, "f_cw": ffn_conv_w,
         "f_cb": ffn_conv_b}
    ctx_cols = sum(IN_SPLITS[:5])
    bias = _na_bias_table(na_rpb.reshape((depth * N_NA,) + na_rpb.shape[2:]), rows)
    names = ("sh1", "sc1", "g1", "sh2", "sc2", "g2")
    ada_all = [{n: ada[l, :, i] for i, n in enumerate(names)} for l in range(depth)]
    tm = _row_tile(xs.shape[0], TM_TOKENS)
    tm_c = _row_tile(cs.shape[0], TM_TOKENS)
    tiles_per_seq = seq // tm
    h = _norm_mod(xs, ln1_w[0], ada_all[0]["sh1"], ada_all[0]["sc1"], lambda i: 1 + i // tiles_per_seq, tm)
    hc = _norm_mod(cs, ln1_w[0], ada_all[0]["sh1"], ada_all[0]["sc1"], lambda i: 0, tm_c)
    for l in range(depth):
        last = l == depth - 1
        ada_l = ada_all[l]
        next_norm = None if last else (ln1_w[l + 1], ada_all[l + 1]["sc1"], ada_all[l + 1]["sh1"])
        if (IN_W // TN_IN) * (xs.shape[0] // tm) >= w_up.shape[2] // HEAD_DIM:
            u, w_up_b = _in_proj(h, w_in, l, na_q_norm_w[l], na_k_norm_w[l], IN_W, tm, TN_IN, side_w=w_up)
        else:
            u = _in_proj(h, w_in, l, na_q_norm_w[l], na_k_norm_w[l], IN_W, tm, TN_IN)
            w_up_b = w_up[l].astype(BF16)
        if last:
            uc = _in_proj(hc, w_in, l, na_q_norm_w[l], na_k_norm_w[l], ctx_cols, tm_c, TN_IN_CTX)
        else:
            uc = _in_proj(hc, w_in, l, na_q_norm_w[l], na_k_norm_w[l], IN_W, tm_c, TN_IN)

        if last:
            _, s_fw = _hgrn_scan(uc, lb_all[0, l], s_zero, ctx_len, reverse=False, mode="state")
            _, s_bw = _hgrn_scan(uc, lb_all[1, l], s_zero, ctx_len, reverse=True, mode="state")
        else:
            co_fw, s_fw = _hgrn_scan(uc, lb_all[0, l], s_zero, ctx_len, reverse=False, mode="raw")
            hg_c, s_bw = _hgrn_scan(uc, lb_all[1, l], s_zero, ctx_len, reverse=True, mode="final", o_fw=co_fw,
                                    norm_w=hg_norm_w[l])
        o_fw, _ = _hgrn_scan(u, lb_all[0, l], s_fw, seq, reverse=False, mode="raw")
        hg_out, _ = _hgrn_scan(u, lb_all[1, l], s_bw, seq, reverse=True, mode="final", o_fw=o_fw,
                               norm_w=hg_norm_w[l])

        na_out = _neighbourhood_attention(u, uc, bias, l * N_NA, na_out_norm_w[l], batch, rows, ctx_len)
        cv_out = _short_conv(u, cv_w[l], cv_out_norm_w[l], seq, _row_tile(seq, TM_TOKENS))

        res, w_down_b = _token_stream(xs, seq, None, p, l, w_up_b, ada_l, [hg_out, na_out, cv_out],
                                      next_norm=next_norm)
        if last:
            xs = res
        else:
            xs, h = res
            na_c = _context_attention(uc, na_out_norm_w[l], batch, ctx_len)
            cv_c = _short_conv(uc, cv_w[l], cv_out_norm_w[l], ctx_len, _row_tile(ctx_len, TM_TOKENS))
            (cs, hc), _ = _token_stream(cs, ctx_len, 0, p, l, w_up_b, ada_l, [hg_c, na_c, cv_c],
                                        w_down_b=w_down_b, next_norm=next_norm)
    return xs.reshape(batch, seq, d)
```

```python
import functools

import numpy as np
import jax
import jax.numpy as jnp
from jax import lax
from jax.experimental import pallas as pl
from jax.experimental.pallas import tpu as pltpu

F32 = jnp.float32
BF16 = jnp.bfloat16

D_MODEL = 2048
DEPTH = 2
GRID_W = 64
HEAD_DIM = 128
N_HG = 4
N_NA = 8
N_CV = 4
HG_W = N_HG * HEAD_DIM
NA_W = N_NA * HEAD_DIM
CV_W = N_CV * HEAD_DIM
IN_SPLITS = (HG_W, HG_W, HG_W, NA_W, NA_W, HG_W, HG_W, NA_W, CV_W, CV_W, CV_W)
IN_W = sum(IN_SPLITS)
(OFF_FFW, OFF_FBW, OFF_I, OFF_NAK, OFF_NAV, OFF_HGQ, OFF_HGG, OFF_NAQ, OFF_CVB, OFF_CVC,
 OFF_CVV) = [int(v) // HEAD_DIM for v in np.cumsum((0,) + IN_SPLITS[:-1])]
NA_WIN_R = 8
NA_WIN_C = 16
NA_QBLK_C = 16
NA_KEY_C = 32
NA_KEY_R = 16
NA_HEADS = 4
CONV_W = 3
D_FF = 5632
EPS = 1e-6
F_FLOOR = 1e-30
ATTN_SCALE = HEAD_DIM ** -0.5
NEG_INF = -1e30

HG_CHUNK = 128
HG_BLOCK = 1024
TM_TOKENS = 1024
TN_IN = 1024
TN_IN_CTX = 512
TN_FF = 512
TM_OUT = 512
TM_DOWN = 256
SUBLANE = 8
HALO = 16
MXU_N = 256
TRACE_AHEAD = 3
VMEM_LIMIT = 56 * 1024 * 1024


def _params(*sem):
    return pltpu.CompilerParams(dimension_semantics=sem, vmem_limit_bytes=VMEM_LIMIT)


def _dot(a, b):
    return jnp.dot(a, b, preferred_element_type=F32)


def _dot_nt(a, b):
    return lax.dot_general(a, b, (((1,), (1,)), ((), ())), preferred_element_type=F32)


def _dot_tn(a, b):
    return lax.dot_general(a, b, (((0,), (0,)), ((), ())), preferred_element_type=F32)


def _rms(x, w):
    return x * lax.rsqrt(jnp.mean(x * x, axis=-1, keepdims=True) + EPS) * w


def _silu(x):
    half = 0.5 * x
    return half + half * jnp.tanh(half)


def _ada_kernel(s_ref, w_ref, b_ref, o_ref):
    s = _silu(s_ref[...]).astype(BF16)
    o_ref[0] = _dot(s, w_ref[0].astype(BF16)) + b_ref[0]


def _ada_table(cond, w_ada, b_ada):
    depth, d, n = w_ada.shape
    tn = TN_IN
    return pl.pallas_call(
        _ada_kernel,
        grid=(depth, n // tn),
        in_specs=[pl.BlockSpec((SUBLANE, d), lambda l, j: (0, 0)),
                  pl.BlockSpec((1, d, tn), lambda l, j: (l, 0, j)),
                  pl.BlockSpec((1, 1, tn), lambda l, j: (l, 0, j))],
        out_specs=pl.BlockSpec((1, SUBLANE, tn), lambda l, j: (l, 0, j)),
        out_shape=jax.ShapeDtypeStruct((depth, SUBLANE, n), F32),
        compiler_params=_params("parallel", "parallel"),
    )(cond, w_ada, b_ada.reshape(depth, 1, n))


def _norm_mod_kernel(x_ref, w_ref, sh_ref, sc_ref, o_ref):
    y = _rms(x_ref[...], w_ref[...])
    o_ref[...] = (y * (1.0 + sc_ref[0]) + sh_ref[0]).astype(o_ref.dtype)


def _norm_mod(x, w, shift, scale, mod_of_tile, tr):
    rows, d = x.shape
    mod = lambda i: (mod_of_tile(i), 0, 0)
    return pl.pallas_call(
        _norm_mod_kernel,
        grid=(rows // tr,),
        in_specs=[pl.BlockSpec((tr, d), lambda i: (i, 0)),
                  pl.BlockSpec((1, d), lambda i: (0, 0)),
                  pl.BlockSpec((1, 1, d), mod),
                  pl.BlockSpec((1, 1, d), mod)],
        out_specs=pl.BlockSpec((tr, d), lambda i: (i, 0)),
        out_shape=jax.ShapeDtypeStruct((rows, d), BF16),
        compiler_params=_params("parallel"),
    )(x, w.reshape(1, d), shift, scale)


def _in_proj_kinds(n_cols, tn):
    kinds = []
    for head in range(n_cols // HEAD_DIM):
        if OFF_NAK <= head < OFF_NAV:
            kinds.append("k")
        elif OFF_NAQ <= head < OFF_CVB:
            kinds.append("q")
        elif OFF_HGQ <= head < OFF_HGG:
            kinds.append("silu")
        else:
            kinds.append("id")
    per = tn // HEAD_DIM
    return tuple(tuple(kinds[t * per:(t + 1) * per]) for t in range(n_cols // tn))


def _in_proj_kernel(*refs, kinds, with_cast):
    if with_cast:
        a_ref, w_ref, qw_ref, kw_ref, side_ref, o_ref, side_o_ref, wb_ref = refs
        side_o_ref[...] = side_ref[...].astype(BF16)
    else:
        a_ref, w_ref, qw_ref, kw_ref, o_ref, wb_ref = refs
    j = pl.program_id(0)

    @pl.when(pl.program_id(1) == 0)
    def _():
        wb_ref[...] = w_ref[...].astype(BF16)

    def transform(y, kind):
        if kind == "k":
            return _rms(y, kw_ref[...])
        if kind == "q":
            return _rms(y, qw_ref[...]) * ATTN_SCALE
        if kind == "silu":
            return _silu(y)
        return y

    per_slab = MXU_N // HEAD_DIM
    for pattern in sorted(set(kinds)):
        hit = functools.reduce(jnp.logical_or, [j == jj for jj, tk in enumerate(kinds) if tk == pattern])

        @pl.when(hit)
        def _(pattern=pattern):
            a = a_ref[...]
            for c in range(len(pattern) // per_slab):
                cols = slice(c * MXU_N, (c + 1) * MXU_N)
                acc = _dot(a, wb_ref[:, cols])
                parts = [transform(acc[:, s * HEAD_DIM:(s + 1) * HEAD_DIM], pattern[c * per_slab + s])
                         for s in range(per_slab)]
                o_ref[:, cols] = jnp.concatenate(parts, axis=1).astype(o_ref.dtype)


def _in_proj(h, w_in, layer, q_norm, k_norm, n_cols, tm, tn, side_w=None):
    m, k = h.shape
    kinds = _in_proj_kinds(n_cols, tn)
    nj, ni = n_cols // tn, m // tm
    vec = pl.BlockSpec((1, HEAD_DIM), lambda j, i: (0, 0))
    in_specs = [pl.BlockSpec((tm, k), lambda j, i: (i, 0)),
                pl.BlockSpec((None, k, tn), lambda j, i: (layer, 0, j)),
                vec, vec]
    args = [h, w_in, q_norm.reshape(1, HEAD_DIM), k_norm.reshape(1, HEAD_DIM)]
    out_specs = [pl.BlockSpec((tm, tn), lambda j, i: (i, j))]
    out_shape = [jax.ShapeDtypeStruct((m, n_cols), BF16)]
    if side_w is not None:
        _, k2, n2 = side_w.shape
        strips = n2 // HEAD_DIM
        assert strips <= nj * ni
        strip = lambda j, i: jnp.minimum(j * ni + i, strips - 1)
        in_specs.append(pl.BlockSpec((None, k2, HEAD_DIM), lambda j, i: (layer, 0, strip(j, i))))
        args.append(side_w)
        out_specs.append(pl.BlockSpec((k2, HEAD_DIM), lambda j, i: (0, strip(j, i))))
        out_shape.append(jax.ShapeDtypeStruct((k2, n2), BF16))
    res = pl.pallas_call(
        functools.partial(_in_proj_kernel, kinds=kinds, with_cast=side_w is not None),
        grid=(nj, ni),
        in_specs=in_specs,
        out_specs=out_specs,
        out_shape=out_shape,
        scratch_shapes=[pltpu.VMEM((k, tn), BF16)],
        compiler_params=_params("arbitrary", "arbitrary"),
    )(*args)
    return res if side_w is not None else res[0]


def _proj_norm_kernel(*refs, splits, cast_weight, with_norm):
    n_a = len(splits)
    a_refs, rest = refs[:n_a], list(refs[n_a:])
    w_ref, x_ref, g_ref = rest[:3]
    rest = rest[3:]
    if with_norm:
        lw_ref, sc_ref, sh_ref, x1_ref, h2_ref = rest[:5]
        rest = rest[5:]
    else:
        x1_ref = rest.pop(0)
    if cast_weight:
        wb_ref = rest.pop(0)

        @pl.when(pl.program_id(0) == 0)
        def _():
            wb_ref[...] = w_ref[...].astype(BF16)
    else:
        wb_ref = w_ref

    tm, n = x1_ref.shape
    ssq = jnp.zeros((tm, HEAD_DIM), F32)
    for c in range(n // MXU_N):
        cols = slice(c * MXU_N, (c + 1) * MXU_N)
        acc = None
        off = 0
        for a_ref, width in zip(a_refs, splits):
            part = _dot(a_ref[...], wb_ref[off:off + width, cols])
            acc = part if acc is None else acc + part
            off += width
        x1 = x_ref[:, cols] + g_ref[0][:, cols] * acc
        x1_ref[:, cols] = x1
        if with_norm:
            sq = x1 * x1
            for s in range(MXU_N // HEAD_DIM):
                ssq = ssq + sq[:, s * HEAD_DIM:(s + 1) * HEAD_DIM]
    if with_norm:
        inv = lax.rsqrt(jnp.sum(ssq, axis=-1, keepdims=True) * (1.0 / n) + EPS)
        gain = lw_ref[...] * (1.0 + sc_ref[0])
        for c in range(n // MXU_N):
            cols = slice(c * MXU_N, (c + 1) * MXU_N)
            h2_ref[:, cols] = (x1_ref[:, cols] * inv * gain[:, cols] + sh_ref[0][:, cols]).astype(h2_ref.dtype)


def _proj_norm(a_list, w, layer, x, gate, mod_of_tile, tm, norm=None):
    m, n = x.shape
    cast_weight = w.ndim == 3
    k = w.shape[-2]
    splits = tuple(a.shape[1] for a in a_list)
    assert sum(splits) == k
    in_specs = [pl.BlockSpec((tm, s), lambda i: (i, 0)) for s in splits]
    if cast_weight:
        in_specs.append(pl.BlockSpec((None, k, n), lambda i: (layer, 0, 0), pipeline_mode=pl.Buffered(1)))
    else:
        in_specs.append(pl.BlockSpec((k, n), lambda i: (0, 0), pipeline_mode=pl.Buffered(1)))
    in_specs += [pl.BlockSpec((tm, n), lambda i: (i, 0)),
                 pl.BlockSpec((1, 1, n), lambda i: (mod_of_tile(i), 0, 0))]
    args = [*a_list, w, x, gate]
    out_specs = [pl.BlockSpec((tm, n), lambda i: (i, 0))]
    out_shape = [jax.ShapeDtypeStruct((m, n), F32)]
    if norm is not None:
        ln_w, scale, shift, norm_mod = norm
        mod = lambda i: (norm_mod(i), 0, 0)
        in_specs += [pl.BlockSpec((1, n), lambda i: (0, 0)), pl.BlockSpec((1, 1, n), mod),
                     pl.BlockSpec((1, 1, n), mod)]
        args += [ln_w.reshape(1, n), scale, shift]
        out_specs.append(pl.BlockSpec((tm, n), lambda i: (i, 0)))
        out_shape.append(jax.ShapeDtypeStruct((m, n), BF16))
    res = pl.pallas_call(
        functools.partial(_proj_norm_kernel, splits=splits, cast_weight=cast_weight, with_norm=norm is not None),
        grid=(m // tm,),
        in_specs=in_specs,
        out_specs=out_specs,
        out_shape=out_shape,
        scratch_shapes=[pltpu.VMEM((k, n), BF16)] if cast_weight else [],
        compiler_params=_params("arbitrary"),
    )(*args)
    return res if norm is not None else res[0]


def _shift_rows(x, prev_row, next_row):
    r = x.shape[0]
    idx = lax.broadcasted_iota(jnp.int32, x.shape, 0)
    down = jnp.where(idx == 0, prev_row, pltpu.roll(x, 1, axis=0))
    up = jnp.where(idx == r - 1, next_row, pltpu.roll(x, r - 1, axis=0))
    return down, up


def _halo_specs(tr, tc, col_of, tiles_per_seq, n_rows):
    per = tr // HALO
    last_blk = n_rows // HALO - 1
    cur = pl.BlockSpec((tr, tc), lambda i, j: (i, col_of(j)))
    prev = pl.BlockSpec((HALO, tc), lambda i, j: (jnp.maximum(i * per - 1, 0), col_of(j)))
    nxt = pl.BlockSpec((HALO, tc), lambda i, j: (jnp.minimum((i + 1) * per, last_blk), col_of(j)))
    return cur, prev, nxt


def _seq_edges(tiles_per_seq):
    i = pl.program_id(0)
    pos = i % tiles_per_seq
    return (pos != 0).astype(F32), (pos != tiles_per_seq - 1).astype(F32)


def _ffn_up_kernel(*refs, tm, seq_len, with_cast):
    if with_cast:
        (h_ref, hp_ref, hn_ref, wg_ref, wv_ref, cg_ref, cv_ref, bg_ref, bv_ref, side_ref,
         o_ref, side_o_ref, lhs_ref) = refs
        side_o_ref[...] = side_ref[...].astype(BF16)
    else:
        h_ref, hp_ref, hn_ref, wg_ref, wv_ref, cg_ref, cv_ref, bg_ref, bv_ref, o_ref, lhs_ref = refs
    tiles_per_seq = max(seq_len // tm, 1)
    pos = pl.program_id(0) % tiles_per_seq
    has_prev = pos != 0
    has_next = pos != tiles_per_seq - 1

    @pl.when(pl.program_id(1) == 0)
    def _():
        lhs_ref[0:HALO, :] = jnp.where(has_prev, hp_ref[...], jnp.zeros_like(hp_ref))
        lhs_ref[HALO:HALO + tm, :] = h_ref[...]
        lhs_ref[HALO + tm:, :] = jnp.where(has_next, hn_ref[...], jnp.zeros_like(hn_ref))

    lhs = lhs_ref[...]

    def branch(w_ref, c_ref, b_ref):
        acc = _dot(lhs, w_ref[...])
        r = acc.shape[0]
        down, up = pltpu.roll(acc, 1, axis=0), pltpu.roll(acc, r - 1, axis=0)
        if seq_len < tm:
            tok = (lax.broadcasted_iota(jnp.int32, acc.shape, 0) + (seq_len - HALO)) % seq_len
            down = jnp.where(tok == 0, 0.0, down)
            up = jnp.where(tok == seq_len - 1, 0.0, up)
        y = down * c_ref[0:1, :] + acc * c_ref[1:2, :] + up * c_ref[2:3, :]
        return y[HALO:HALO + tm, :] + b_ref[...]

    gate = branch(wg_ref, cg_ref, bg_ref)
    val = branch(wv_ref, cv_ref, bv_ref)
    o_ref[...] = (_silu(gate) * val).astype(o_ref.dtype)


def _ffn_up_conv_gate(h, w_up, cw, cb, seq_len, tm, tn, side_w=None, layer=0):
    rows, k = h.shape
    assert seq_len % tm == 0 or (tm % seq_len == 0 and seq_len >= HALO)
    nj = D_FF // tn
    per = tm // HALO
    last_blk = rows // HALO - 1
    cb2 = cb.reshape(1, 2 * D_FF)
    steps = (rows // tm) * nj
    if side_w is not None:
        _, k2, n2 = side_w.shape
        strips = max(d for d in range(1, steps + 1) if k2 % d == 0 and (k2 // d) % HALO == 0)
        strip = lambda i, j: jnp.minimum(i * nj + j, strips - 1)
        side_in = [pl.BlockSpec((None, k2 // strips, n2), lambda i, j: (layer, strip(i, j), 0))]
        side_out = [pl.BlockSpec((k2 // strips, n2), lambda i, j: (strip(i, j), 0))]
        side_shape = [jax.ShapeDtypeStruct((k2, n2), BF16)]
        side_args = [side_w]
    else:
        side_in, side_out, side_shape, side_args = [], [], [], []
    in_specs = [pl.BlockSpec((tm, k), lambda i, j: (i, 0)),
                pl.BlockSpec((HALO, k), lambda i, j: (jnp.maximum(i * per - 1, 0), 0)),
                pl.BlockSpec((HALO, k), lambda i, j: (jnp.minimum((i + 1) * per, last_blk), 0)),
                pl.BlockSpec((k, tn), lambda i, j: (0, j)),
                pl.BlockSpec((k, tn), lambda i, j: (0, j + nj)),
                pl.BlockSpec((CONV_W, tn), lambda i, j: (0, j)),
                pl.BlockSpec((CONV_W, tn), lambda i, j: (0, j + nj)),
                pl.BlockSpec((1, tn), lambda i, j: (0, j)),
                pl.BlockSpec((1, tn), lambda i, j: (0, j + nj))]
    res = pl.pallas_call(
        functools.partial(_ffn_up_kernel, tm=tm, seq_len=seq_len, with_cast=side_w is not None),
        grid=(rows // tm, nj),
        in_specs=in_specs + side_in,
        out_specs=[pl.BlockSpec((tm, tn), lambda i, j: (i, j))] + side_out,
        out_shape=[jax.ShapeDtypeStruct((rows, D_FF), BF16)] + side_shape,
        scratch_shapes=[pltpu.VMEM((tm + 2 * HALO, k), BF16)],
        compiler_params=_params("arbitrary", "arbitrary"),
    )(h, h, h, w_up, w_up, cw, cw, cb2, cb2, *side_args)
    return res if side_w is not None else res[0]


def _short_conv_kernel(b_ref, c_ref, cp_ref, cn_ref, v_ref, vp_ref, vn_ref, w_ref, nw_ref, o_ref, *, tiles_per_seq):
    has_prev, has_next = _seq_edges(tiles_per_seq)
    p = c_ref[...].astype(F32) * v_ref[...].astype(F32)
    prev_row = cp_ref[HALO - 1:HALO, :].astype(F32) * vp_ref[HALO - 1:HALO, :].astype(F32) * has_prev
    next_row = cn_ref[0:1, :].astype(F32) * vn_ref[0:1, :].astype(F32) * has_next
    down, up = _shift_rows(p, prev_row, next_row)
    y = b_ref[...].astype(F32) * (down * w_ref[0:1, :] + p * w_ref[1:2, :] + up * w_ref[2:3, :])
    for h in range(N_CV):
        sl = slice(h * HEAD_DIM, (h + 1) * HEAD_DIM)
        o_ref[:, sl] = _rms(y[:, sl], nw_ref[:, sl]).astype(o_ref.dtype)


def _short_conv(u, cv_w, cv_onorm, seq_len, tr):
    rows = u.shape[0]
    tiles_per_seq = seq_len // tr
    blk = lambda off: (lambda j: off * HEAD_DIM // CV_W)
    b_spec = pl.BlockSpec((tr, CV_W), lambda i, j: (i, OFF_CVB * HEAD_DIM // CV_W))
    c_specs = _halo_specs(tr, CV_W, blk(OFF_CVC), tiles_per_seq, rows)
    v_specs = _halo_specs(tr, CV_W, blk(OFF_CVV), tiles_per_seq, rows)
    return pl.pallas_call(
        functools.partial(_short_conv_kernel, tiles_per_seq=tiles_per_seq),
        grid=(rows // tr, 1),
        in_specs=[b_spec, *c_specs, *v_specs,
                  pl.BlockSpec((CONV_W, CV_W), lambda i, j: (0, 0)),
                  pl.BlockSpec((1, CV_W), lambda i, j: (0, 0))],
        out_specs=pl.BlockSpec((tr, CV_W), lambda i, j: (i, 0)),
        out_shape=jax.ShapeDtypeStruct((rows, CV_W), BF16),
        compiler_params=_params("parallel", "arbitrary"),
    )(u, u, u, u, u, u, u, cv_w, cv_onorm.reshape(1, CV_W))


def _hgrn_structure(chunk, reverse):
    idx = np.arange(chunk)
    i, t = idx[:, None], idx[None, :]
    mats = [t <= i, t > i]
    masks = []
    s = chunk // 2
    while s >= 1:
        blk = idx // (2 * s)
        upper = (idx % (2 * s)) >= s
        mid = blk * 2 * s + s - 1
        a = np.where(upper[:, None], (t > mid[:, None]) & (t <= i), (t > i) & (t <= mid[:, None]))
        mats.append(a)
        masks.append((blk[:, None] == blk[None, :]) & upper[:, None] & ~upper[None, :])
        s //= 2
    mats = np.stack(mats).astype(np.float32)
    masks = np.stack(masks).astype(np.float32)
    if reverse:
        mats = mats[:, ::-1, ::-1]
        masks = masks[:, ::-1, ::-1]
    return mats.reshape(-1, chunk), masks


def _hgrn_kernel(*refs, chunk, n_sub, n_levels, reverse, mode):
    if mode == "final":
        (z_ref, v_ref, q_ref, lb_ref, a_ref, m_ref, s0_ref, g_ref, ofw_ref, nw_ref,
         o_ref, sfin_ref, st_ref) = refs
    elif mode == "raw":
        z_ref, v_ref, q_ref, lb_ref, a_ref, m_ref, s0_ref, o_ref, sfin_ref, st_ref = refs
    else:
        z_ref, v_ref, lb_ref, a_ref, s0_ref, sfin_ref, st_ref = refs
    c = pl.program_id(1)

    @pl.when(c == 0)
    def _():
        st_ref[...] = s0_ref[0]

    lb = lb_ref[...]
    a = a_ref[...]
    tot_row = 0 if reverse else chunk - 1
    subs = list(range(n_sub - 1, -1, -1) if reverse else range(n_sub))

    def gates(sub):
        rows = slice(sub * chunk, (sub + 1) * chunk)
        z = z_ref[rows, :].astype(F32)
        f = lb + (1.0 - lb) * jax.nn.sigmoid(z)
        log_f = jnp.log(jnp.maximum(f, F_FLOOR))
        k_all = (1.0 - lb) * jax.nn.sigmoid(-z)
        expo = _dot(a, log_f.astype(BF16))
        return k_all, v_ref[rows, :].astype(F32), expo

    def products(sub, h, k_all, v_all, expo):
        rows = slice(sub * chunk, (sub + 1) * chunk)
        sl = slice(h * HEAD_DIM, (h + 1) * HEAD_DIM)
        k, v = k_all[:, sl], v_all[:, sl]
        e_rem = jnp.exp(expo[chunk:2 * chunk, sl])
        e_tot = jnp.exp(expo[tot_row:tot_row + 1, sl])
        st = st_ref[h]
        st_ref[h] = st * e_tot + _dot_tn(v.astype(BF16), (k * e_rem).astype(BF16))
        if mode == "state":
            return None
        q = q_ref[rows, sl].astype(F32)
        e_cum = jnp.exp(expo[0:chunk, sl])
        o_inter = _dot_nt((q * e_cum).astype(BF16), st.astype(BF16))
        parts = []
        for lvl in range(n_levels):
            e = jnp.exp(expo[(2 + lvl) * chunk:(3 + lvl) * chunk, sl])
            parts.append(_dot_nt((q * e).astype(BF16), (k * e).astype(BF16)))
        diag = jnp.sum(q * k, axis=-1, keepdims=True)
        return o_inter, parts, diag, v

    def readout(sub, h, o_inter, parts, diag, v):
        rows = slice(sub * chunk, (sub + 1) * chunk)
        sl = slice(h * HEAD_DIM, (h + 1) * HEAD_DIM)
        att = None
        for lvl, part in enumerate(parts):
            att = m_ref[lvl] * part if att is None else att + m_ref[lvl] * part
        o = o_inter + _dot(att.astype(BF16), v.astype(BF16)) + diag * v
        if mode == "final":
            y = _rms(o + ofw_ref[rows, sl], nw_ref[...]) * _silu(g_ref[rows, sl].astype(F32))
            o_ref[rows, sl] = y.astype(o_ref.dtype)
        else:
            o_ref[rows, sl] = o.astype(o_ref.dtype)

    tiles = [(sub, h) for sub in subs for h in range(N_HG)]
    cache = {}

    def stage(sub, h):
        if sub not in cache:
            cache.clear()
            cache[sub] = gates(sub)
        return products(sub, h, *cache[sub])

    queue = [stage(*tile) for tile in tiles[:TRACE_AHEAD]]
    for t, tile in enumerate(tiles):
        if t + TRACE_AHEAD < len(tiles):
            queue.append(stage(*tiles[t + TRACE_AHEAD]))
        done = queue.pop(0)
        if mode != "state":
            readout(*tile, *done)

    @pl.when(c == pl.num_programs(1) - 1)
    def _():
        sfin_ref[0] = st_ref[...]


def _hgrn_scan(u, lb, s0, seq_len, *, reverse, mode, o_fw=None, norm_w=None):
    rows = u.shape[0]
    batch = rows // seq_len
    blk = min(HG_BLOCK, seq_len)
    chunk = min(HG_CHUNK, blk)
    n_blocks = seq_len // blk
    mats, masks = _hgrn_structure(chunk, reverse)
    n_levels = masks.shape[0]
    if mode == "state":
        mats = mats[:2 * chunk]
    per_blk = HG_W // HEAD_DIM

    def tok(off):
        col = off // per_blk
        if reverse:
            return pl.BlockSpec((blk, HG_W), lambda b, c: (b * n_blocks + n_blocks - 1 - c, col))
        return pl.BlockSpec((blk, HG_W), lambda b, c: (b * n_blocks + c, col))

    const2 = lambda shape: pl.BlockSpec(shape, lambda b, c: (0, 0))
    state_spec = pl.BlockSpec((1, N_HG, HEAD_DIM, HEAD_DIM), lambda b, c: (b, 0, 0, 0))
    state_shape = jax.ShapeDtypeStruct((batch, N_HG, HEAD_DIM, HEAD_DIM), F32)
    z_spec = tok(OFF_FBW if reverse else OFF_FFW)
    lb2, mats_b = lb.reshape(1, HG_W), jnp.asarray(mats, BF16)
    if mode == "state":
        in_specs = [z_spec, tok(OFF_I), const2((1, HG_W)), const2(mats.shape), state_spec]
        args = [u, u, lb2, mats_b, s0]
        out_specs, out_shape = [state_spec], [state_shape]
    else:
        in_specs = [z_spec, tok(OFF_I), tok(OFF_HGQ), const2((1, HG_W)), const2(mats.shape),
                    pl.BlockSpec(masks.shape, lambda b, c: (0, 0, 0)), state_spec]
        args = [u, u, u, lb2, mats_b, jnp.asarray(masks, F32), s0]
        if mode == "final":
            in_specs += [tok(OFF_HGG), tok(0), const2((1, HEAD_DIM))]
            args += [u, o_fw, norm_w.reshape(1, HEAD_DIM)]
        out_specs = [tok(0), state_spec]
        out_shape = [jax.ShapeDtypeStruct((rows, HG_W), BF16 if mode == "final" else F32), state_shape]
    res = pl.pallas_call(
        functools.partial(_hgrn_kernel, chunk=chunk, n_sub=blk // chunk, n_levels=n_levels, reverse=reverse,
                          mode=mode),
        grid=(batch, n_blocks),
        in_specs=in_specs,
        out_specs=out_specs,
        out_shape=out_shape,
        scratch_shapes=[pltpu.VMEM((N_HG, HEAD_DIM, HEAD_DIM), F32)],
        compiler_params=_params("parallel", "arbitrary"),
    )(*args)
    return (None, res[0]) if mode == "state" else (res[0], res[1])


def _na_bias_table(rpb, rows):
    n_rb = rows // NA_WIN_R
    n_cb = GRID_W // NA_QBLK_C
    n_dr, n_dc = 2 * NA_WIN_R - 1, 2 * NA_WIN_C - 1
    n_k = NA_KEY_R * NA_KEY_C
    offs, row_mask = [], []
    for rb in (0, min(1, n_rb - 1), n_rb - 1):
        k_row0 = int(np.clip(rb * NA_WIN_R - NA_WIN_R // 2, 0, rows - NA_KEY_R))
        q_r = rb * NA_WIN_R + np.arange(NA_WIN_R)
        k_r = k_row0 + np.arange(NA_KEY_R)
        r_start = np.clip(q_r - NA_WIN_R // 2, 0, rows - NA_WIN_R)
        ok_r = (k_r[None, :] >= r_start[:, None]) & (k_r[None, :] < r_start[:, None] + NA_WIN_R)
        row_mask.append(np.repeat(np.where(ok_r, 0.0, NEG_INF), NA_KEY_C, axis=1))
        offs.append(k_row0 - rb * NA_WIN_R + NA_WIN_R - 1)
    row_mask = jnp.asarray(np.stack(row_mask), F32)
    pad_lo = NA_WIN_R - 1 - min(offs)
    per_vreg = HEAD_DIM // NA_KEY_C
    n_a = -(-(NA_KEY_R - 1 + max(offs) + pad_lo + 1) // per_vreg) * per_vreg
    lanes = n_a * NA_KEY_C
    dr_pad = -(-n_dr // SUBLANE) * SUBLANE
    assert n_dc <= HEAD_DIM and n_a >= n_dr + pad_lo
    starts, col_mask = [], []
    for cb in range(n_cb):
        k_col0 = int(np.clip(cb * NA_QBLK_C - NA_WIN_C // 2, 0, GRID_W - NA_KEY_C))
        q_c = cb * NA_QBLK_C + np.arange(NA_QBLK_C)
        k_c = k_col0 + np.arange(NA_KEY_C)
        c_start = np.clip(q_c - NA_WIN_C // 2, 0, GRID_W - NA_WIN_C)
        ok_c = (k_c[None, :] >= c_start[:, None]) & (k_c[None, :] < c_start[:, None] + NA_WIN_C)
        col_mask.append(np.tile(np.where(ok_c, 0.0, NEG_INF), (1, n_a)))
        starts.append(k_col0 - cb * NA_QBLK_C + NA_WIN_C - 1)
    col_mask = jnp.asarray(np.stack(col_mask), F32)
    lane = np.arange(lanes)
    tile_w = jnp.asarray(lane[None, :] % NA_KEY_C == np.arange(NA_KEY_C)[:, None], BF16)
    pick = jnp.asarray(lane[None, :] // NA_KEY_C == np.arange(dr_pad)[:, None] + pad_lo, F32)
    n_heads = rpb.shape[0]
    rpb_pad = jnp.pad(rpb.astype(F32), ((0, 0), (0, dr_pad - n_dr), (0, HEAD_DIM - n_dc)))
    return pl.pallas_call(
        functools.partial(_na_bias_kernel, offs=tuple(offs), pad_lo=pad_lo, starts=tuple(starts)),
        grid=(n_heads,),
        in_specs=[pl.BlockSpec((1, dr_pad, HEAD_DIM), lambda h: (h, 0, 0)),
                  pl.BlockSpec((n_cb, NA_QBLK_C, lanes), lambda h: (0, 0, 0)),
                  pl.BlockSpec((3, NA_WIN_R, n_k), lambda h: (0, 0, 0)),
                  pl.BlockSpec((NA_KEY_C, lanes), lambda h: (0, 0)),
                  pl.BlockSpec((dr_pad, lanes), lambda h: (0, 0))],
        out_specs=pl.BlockSpec((1, 3, n_cb, NA_WIN_R * NA_QBLK_C, n_k), lambda h: (h, 0, 0, 0, 0)),
        out_shape=jax.ShapeDtypeStruct((n_heads, 3, n_cb, NA_WIN_R * NA_QBLK_C, n_k), F32),
        compiler_params=_params("parallel"),
    )(rpb_pad, col_mask, row_mask, tile_w, pick)


def _na_bias_kernel(rpb_ref, cm_ref, rm_ref, w_ref, pick_ref, o_ref, *, offs, pad_lo, starts):
    n_cb = cm_ref.shape[0]
    lanes = cm_ref.shape[-1]
    per = HEAD_DIM // NA_KEY_C
    n_k = o_ref.shape[-1]
    table = rpb_ref[0]
    dr_pad = table.shape[0]
    for cb in range(n_cb):
        windows = []
        for c in range(NA_QBLK_C):
            rolled = pltpu.roll(table, (c - starts[cb]) % HEAD_DIM, axis=1)
            windows.append(rolled[:, :NA_KEY_C])
        win = jnp.concatenate(windows, axis=0)
        hi = win.astype(BF16)
        mid = (win - hi.astype(F32)).astype(BF16)
        lo = (win - hi.astype(F32) - mid.astype(F32)).astype(BF16)
        tiled = _dot(hi, w_ref[...]) + _dot(mid, w_ref[...]) + _dot(lo, w_ref[...])
        t = jnp.sum(tiled.reshape(NA_QBLK_C, dr_pad, lanes) * pick_ref[...][None], axis=1) + cm_ref[cb]
        shifted = [t] + [pltpu.roll(t, lanes - NA_KEY_C * s, axis=1) for s in range(1, per)]
        for cls, off in enumerate(offs):
            for r in range(NA_WIN_R):
                m = off - r + pad_lo
                base = (m // per) * HEAD_DIM
                tile = shifted[m % per][:, base:base + n_k] + rm_ref[cls, r:r + 1, :]
                o_ref[0, cls, cb, r * NA_QBLK_C:(r + 1) * NA_QBLK_C, :] = tile


def _na_kernel(q_ref, k0_ref, k1_ref, k2_ref, k3_ref, v0_ref, v1_ref, v2_ref, v3_ref, kc_ref, vc_ref, bias_ref,
               ow_ref, o_ref):
    n_cb = GRID_W // NA_QBLK_C
    q_all = q_ref[...].astype(F32)
    k_all = jnp.concatenate([r[...] for r in (k0_ref, k1_ref, k2_ref, k3_ref)], axis=0).astype(F32)
    v_all = jnp.concatenate([r[...] for r in (v0_ref, v1_ref, v2_ref, v3_ref)], axis=0).astype(F32)
    def patch(x_all, hh, rows, col0, cols):
        x3 = x_all[:, hh * HEAD_DIM:(hh + 1) * HEAD_DIM].reshape(rows, GRID_W, HEAD_DIM)
        return x3[:, col0:col0 + cols, :].reshape(rows * cols, HEAD_DIM).astype(BF16)

    def scores(hh, cb):
        q0 = cb * NA_QBLK_C
        k_col0 = min(max(q0 - NA_WIN_C // 2, 0), GRID_W - NA_KEY_C)
        qm = patch(q_all, hh, NA_WIN_R, q0, NA_QBLK_C)
        km = patch(k_all, hh, NA_KEY_R, k_col0, NA_KEY_C)
        s_loc = _dot_nt(qm, km) + bias_ref[hh, 0, cb]
        s_ctx = _dot_nt(qm, kc_ref[:, hh * HEAD_DIM:(hh + 1) * HEAD_DIM])
        return s_loc, s_ctx

    def finish(hh, cb, s_loc, s_ctx):
        q0 = cb * NA_QBLK_C
        k_col0 = min(max(q0 - NA_WIN_C // 2, 0), GRID_W - NA_KEY_C)
        hs = slice(hh * HEAD_DIM, (hh + 1) * HEAD_DIM)
        vm = patch(v_all, hh, NA_KEY_R, k_col0, NA_KEY_C)
        mx = jnp.maximum(jnp.max(s_loc, axis=-1, keepdims=True), jnp.max(s_ctx, axis=-1, keepdims=True))
        p_loc = jnp.exp(s_loc - mx)
        p_ctx = jnp.exp(s_ctx - mx)
        den = jnp.sum(p_loc, axis=-1, keepdims=True) + jnp.sum(p_ctx, axis=-1, keepdims=True)
        o = (_dot(p_loc.astype(BF16), vm) + _dot(p_ctx.astype(BF16), vc_ref[:, hs])) / den
        o = _rms(o, ow_ref[hh]).astype(o_ref.dtype)
        for r in range(NA_WIN_R):
            o_ref[r * GRID_W + q0:r * GRID_W + q0 + NA_QBLK_C, hs] = o[r * NA_QBLK_C:(r + 1) * NA_QBLK_C, :]

    tiles = [(hh, cb) for hh in range(NA_HEADS) for cb in range(n_cb)]
    queue = [scores(*tile) for tile in tiles[:TRACE_AHEAD]]
    for t, tile in enumerate(tiles):
        if t + TRACE_AHEAD < len(tiles):
            queue.append(scores(*tiles[t + TRACE_AHEAD]))
        finish(*tile, *queue.pop(0))


def _neighbourhood_attention(u, u_ctx, bias, head0, out_norm, batch, rows, ctx_len):
    assert rows % NA_WIN_R == 0 and rows >= NA_KEY_R
    n_rb = rows // NA_WIN_R
    q_blk = NA_WIN_R * GRID_W
    k_blk = NA_KEY_R * GRID_W // 4
    k_per_batch = rows * GRID_W // k_blk

    width = NA_HEADS * HEAD_DIM
    assert all(off % NA_HEADS == 0 for off in (OFF_NAQ, OFF_NAK, OFF_NAV, head0, N_NA))
    col_q, col_k, col_v = OFF_NAQ // NA_HEADS, OFF_NAK // NA_HEADS, OFF_NAV // NA_HEADS

    def key_spec(col, part):
        def index(h, rb, b):
            first = jnp.clip(rb * NA_WIN_R - NA_WIN_R // 2, 0, rows - NA_KEY_R) * GRID_W // k_blk
            return (b * k_per_batch + first + part, col + h)
        return pl.BlockSpec((k_blk, width), index)

    def row_class(rb):
        return jnp.where(rb == 0, 0, jnp.where(rb == n_rb - 1, 2, 1))

    n_cb = GRID_W // NA_QBLK_C
    in_specs = [pl.BlockSpec((q_blk, width), lambda h, rb, b: (b * n_rb + rb, col_q + h))]
    in_specs += [key_spec(col_k, p) for p in range(4)] + [key_spec(col_v, p) for p in range(4)]
    in_specs += [pl.BlockSpec((ctx_len, width), lambda h, rb, b: (b, col_k + h)),
                 pl.BlockSpec((ctx_len, width), lambda h, rb, b: (b, col_v + h)),
                 pl.BlockSpec((NA_HEADS, 1, n_cb, NA_WIN_R * NA_QBLK_C, NA_KEY_R * NA_KEY_C),
                              lambda h, rb, b: (head0 // NA_HEADS + h, row_class(rb), 0, 0, 0)),
                 pl.BlockSpec((NA_HEADS, 1, HEAD_DIM), lambda h, rb, b: (h, 0, 0))]
    return pl.pallas_call(
        _na_kernel,
        grid=(N_NA // NA_HEADS, n_rb, batch),
        in_specs=in_specs,
        out_specs=pl.BlockSpec((q_blk, width), lambda h, rb, b: (b * n_rb + rb, h)),
        out_shape=jax.ShapeDtypeStruct((u.shape[0], NA_W), BF16),
        compiler_params=_params("parallel", "parallel", "parallel"),
    )(u, u, u, u, u, u, u, u, u, u_ctx, u_ctx, bias, out_norm.reshape(N_NA, 1, HEAD_DIM))


def _ctx_attn_kernel(q_ref, k_ref, v_ref, ow_ref, o_ref):
    s = _dot_nt(q_ref[...].astype(BF16), k_ref[...].astype(BF16))
    p = jnp.exp(s - jnp.max(s, axis=-1, keepdims=True))
    o = _dot(p.astype(BF16), v_ref[...].astype(BF16)) / jnp.sum(p, axis=-1, keepdims=True)
    o_ref[...] = _rms(o, ow_ref[0]).astype(o_ref.dtype)


def _context_attention(u_ctx, out_norm, batch, ctx_len):
    tok = lambda off: pl.BlockSpec((ctx_len, HEAD_DIM), lambda b, h: (b, off + h))
    return pl.pallas_call(
        _ctx_attn_kernel,
        grid=(batch, N_NA),
        in_specs=[tok(OFF_NAQ), tok(OFF_NAK), tok(OFF_NAV),
                  pl.BlockSpec((1, 1, HEAD_DIM), lambda b, h: (h, 0, 0))],
        out_specs=tok(0),
        out_shape=jax.ShapeDtypeStruct((u_ctx.shape[0], NA_W), BF16),
        compiler_params=_params("parallel", "parallel"),
    )(u_ctx, u_ctx, u_ctx, out_norm.reshape(N_NA, 1, HEAD_DIM))


def _row_tile(rows, target):
    t = min(rows, target)
    assert rows % t == 0
    return t


def _token_stream(xs, seq_len, mod_row, p, layer, w_up_b, ada_l, mixers, w_down_b=None, next_norm=None):
    rows = xs.shape[0]
    sh2, sc2, g1, g2 = ada_l["sh2"], ada_l["sc2"], ada_l["g1"], ada_l["g2"]

    def mod_of_tile(tile_rows):
        per = max(seq_len // tile_rows, 1)
        return (lambda i: 1 + i // per) if mod_row is None else (lambda i: mod_row)

    tm_o = _row_tile(rows, TM_OUT)
    x1, h2 = _proj_norm(mixers, p["w_out"], layer, xs, g1, mod_of_tile(tm_o), tm_o,
                        norm=(p["ln2"][layer], sc2, sh2, mod_of_tile(tm_o)))
    tm = _row_tile(rows, TM_TOKENS)
    conv = (p["f_cw"][layer], p["f_cb"][layer])
    if w_down_b is None:
        act, w_down_b = _ffn_up_conv_gate(h2, w_up_b, *conv, seq_len, tm, TN_FF, side_w=p["w_down"], layer=layer)
    else:
        act = _ffn_up_conv_gate(h2, w_up_b, *conv, seq_len, tm, TN_FF)
    tm_d = _row_tile(rows, TM_DOWN)
    norm = None if next_norm is None else (*next_norm, mod_of_tile(tm_d))
    return _proj_norm([act], w_down_b, None, x1, g2, mod_of_tile(tm_d), tm_d, norm=norm), w_down_b


def kernel(x, c, ctx, c_ctx, w_ada, b_ada, ln1_w, ln2_w, w_in, hg_lb_logits, hg_norm_w, na_q_norm_w, na_k_norm_w,
           na_rpb, na_out_norm_w, cv_w, cv_out_norm_w, w_out, w_up, ffn_conv_w, ffn_conv_b, w_down):
    batch, seq, d = x.shape
    ctx_len = ctx.shape[1]
    rows = seq // GRID_W
    depth = w_ada.shape[0]
    xs = x.reshape(batch * seq, d)
    cs = ctx.reshape(batch * ctx_len, d)

    lb_sm = jax.nn.softmax(hg_lb_logits.astype(F32), axis=1)
    lb_all = jnp.cumsum(lb_sm, axis=1) - lb_sm[:, :1]

    cond = jnp.zeros((SUBLANE, d), F32).at[0].set(c_ctx).at[1:1 + batch].set(c)
    ada = _ada_table(cond, w_ada, b_ada).reshape(depth, SUBLANE, 6, 1, d)

    s_zero = jnp.zeros((batch, N_HG, HEAD_DIM, HEAD_DIM), F32)
    p = {"w_out": w_out, "w_down": w_down, "ln2": ln2_w, "f_cw": ffn_conv_w,
         "f_cb": ffn_conv_b}
    ctx_cols = sum(IN_SPLITS[:5])
    bias = _na_bias_table(na_rpb.reshape((depth * N_NA,) + na_rpb.shape[2:]), rows)
    names = ("sh1", "sc1", "g1", "sh2", "sc2", "g2")
    ada_all = [{n: ada[l, :, i] for i, n in enumerate(names)} for l in range(depth)]
    tm = _row_tile(xs.shape[0], TM_TOKENS)
    tm_c = _row_tile(cs.shape[0], TM_TOKENS)
    tiles_per_seq = seq // tm
    h = _norm_mod(xs, ln1_w[0], ada_all[0]["sh1"], ada_all[0]["sc1"], lambda i: 1 + i // tiles_per_seq, tm)
    hc = _norm_mod(cs, ln1_w[0], ada_all[0]["sh1"], ada_all[0]["sc1"], lambda i: 0, tm_c)
    for l in range(depth):
        last = l == depth - 1
        ada_l = ada_all[l]
        next_norm = None if last else (ln1_w[l + 1], ada_all[l + 1]["sc1"], ada_all[l + 1]["sh1"])
        if (IN_W // TN_IN) * (xs.shape[0] // tm) >= w_up.shape[2] // HEAD_DIM:
            u, w_up_b = _in_proj(h, w_in, l, na_q_norm_w[l], na_k_norm_w[l], IN_W, tm, TN_IN, side_w=w_up)
        else:
            u = _in_proj(h, w_in, l, na_q_norm_w[l], na_k_norm_w[l], IN_W, tm, TN_IN)
            w_up_b = w_up[l].astype(BF16)
        if last:
            uc = _in_proj(hc, w_in, l, na_q_norm_w[l], na_k_norm_w[l], ctx_cols, tm_c, TN_IN_CTX)
        else:
            uc = _in_proj(hc, w_in, l, na_q_norm_w[l], na_k_norm_w[l], IN_W, tm_c, TN_IN)

        if last:
            _, s_fw = _hgrn_scan(uc, lb_all[0, l], s_zero, ctx_len, reverse=False, mode="state")
            _, s_bw = _hgrn_scan(uc, lb_all[1, l], s_zero, ctx_len, reverse=True, mode="state")
        else:
            co_fw, s_fw = _hgrn_scan(uc, lb_all[0, l], s_zero, ctx_len, reverse=False, mode="raw")
            hg_c, s_bw = _hgrn_scan(uc, lb_all[1, l], s_zero, ctx_len, reverse=True, mode="final", o_fw=co_fw,
                                    norm_w=hg_norm_w[l])
        o_fw, _ = _hgrn_scan(u, lb_all[0, l], s_fw, seq, reverse=False, mode="raw")
        hg_out, _ = _hgrn_scan(u, lb_all[1, l], s_bw, seq, reverse=True, mode="final", o_fw=o_fw,
                               norm_w=hg_norm_w[l])

        na_out = _neighbourhood_attention(u, uc, bias, l * N_NA, na_out_norm_w[l], batch, rows, ctx_len)
        cv_out = _short_conv(u, cv_w[l], cv_out_norm_w[l], seq, _row_tile(seq, TM_TOKENS))

        res, w_down_b = _token_stream(xs, seq, None, p, l, w_up_b, ada_l, [hg_out, na_out, cv_out],
                                      next_norm=next_norm)
        if last:
            xs = res
        else:
            xs, h = res
            na_c = _context_attention(uc, na_out_norm_w[l], batch, ctx_len)
            cv_c = _short_conv(uc, cv_w[l], cv_out_norm_w[l], ctx_len, _row_tile(ctx_len, TM_TOKENS))
            (cs, hc), _ = _token_stream(cs, ctx_len, 0, p, l, w_up_b, ada_l, [hg_c, na_c, cv_c],
                                        w_down_b=w_down_b, next_norm=next_norm)
    return xs.reshape(batch, seq, d)
```

```python
import functools

import numpy as np
import jax
import jax.numpy as jnp
from jax import lax
from jax.experimental import pallas as pl
from jax.experimental.pallas import tpu as pltpu

F32 = jnp.float32
BF16 = jnp.bfloat16

D_MODEL = 2048
DEPTH = 2
GRID_W = 64
HEAD_DIM = 128
N_HG = 4
N_NA = 8
N_CV = 4
HG_W = N_HG * HEAD_DIM
NA_W = N_NA * HEAD_DIM
CV_W = N_CV * HEAD_DIM
IN_SPLITS = (HG_W, HG_W, HG_W, NA_W, NA_W, HG_W, HG_W, NA_W, CV_W, CV_W, CV_W)
IN_W = sum(IN_SPLITS)
(OFF_FFW, OFF_FBW, OFF_I, OFF_NAK, OFF_NAV, OFF_HGQ, OFF_HGG, OFF_NAQ, OFF_CVB, OFF_CVC,
 OFF_CVV) = [int(v) // HEAD_DIM for v in np.cumsum((0,) + IN_SPLITS[:-1])]
NA_WIN_R = 8
NA_WIN_C = 16
NA_QBLK_C = 16
NA_KEY_C = 32
NA_KEY_R = 16
NA_HEADS = 4
CONV_W = 3
D_FF = 5632
EPS = 1e-6
F_FLOOR = 1e-30
ATTN_SCALE = HEAD_DIM ** -0.5
NEG_INF = -1e30

HG_CHUNK = 128
HG_BLOCK = 1024
TM_TOKENS = 1024
TN_IN = 1024
TN_IN_CTX = 512
TN_FF = 512
TM_OUT = 512
TM_DOWN = 512
SUBLANE = 8
HALO = 16
MXU_N = 256
TRACE_AHEAD = 3
VMEM_LIMIT = 56 * 1024 * 1024
VMEM_LIMIT_DOWN = 60 * 1024 * 1024


def _params(*sem, vmem_limit=VMEM_LIMIT):
    return pltpu.CompilerParams(dimension_semantics=sem, vmem_limit_bytes=vmem_limit)


def _dot(a, b):
    return jnp.dot(a, b, preferred_element_type=F32)


def _dot_nt(a, b):
    return lax.dot_general(a, b, (((1,), (1,)), ((), ())), preferred_element_type=F32)


def _dot_tn(a, b):
    return lax.dot_general(a, b, (((0,), (0,)), ((), ())), preferred_element_type=F32)


def _rms(x, w):
    return x * lax.rsqrt(jnp.mean(x * x, axis=-1, keepdims=True) + EPS) * w


def _silu(x):
    half = 0.5 * x
    return half + half * jnp.tanh(half)


def _ada_kernel(s_ref, w_ref, b_ref, o_ref):
    s = _silu(s_ref[...]).astype(BF16)
    o_ref[0] = _dot(s, w_ref[0].astype(BF16)) + b_ref[0]


def _ada_table(cond, w_ada, b_ada):
    depth, d, n = w_ada.shape
    tn = TN_IN
    return pl.pallas_call(
        _ada_kernel,
        grid=(depth, n // tn),
        in_specs=[pl.BlockSpec((SUBLANE, d), lambda l, j: (0, 0)),
                  pl.BlockSpec((1, d, tn), lambda l, j: (l, 0, j)),
                  pl.BlockSpec((1, 1, tn), lambda l, j: (l, 0, j))],
        out_specs=pl.BlockSpec((1, SUBLANE, tn), lambda l, j: (l, 0, j)),
        out_shape=jax.ShapeDtypeStruct((depth, SUBLANE, n), F32),
        compiler_params=_params("parallel", "parallel"),
    )(cond, w_ada, b_ada.reshape(depth, 1, n))


def _norm_mod_kernel(x_ref, w_ref, sh_ref, sc_ref, o_ref):
    y = _rms(x_ref[...], w_ref[...])
    o_ref[...] = (y * (1.0 + sc_ref[0]) + sh_ref[0]).astype(o_ref.dtype)


def _norm_mod(x, w, shift, scale, mod_of_tile, tr):
    rows, d = x.shape
    mod = lambda i: (mod_of_tile(i), 0, 0)
    return pl.pallas_call(
        _norm_mod_kernel,
        grid=(rows // tr,),
        in_specs=[pl.BlockSpec((tr, d), lambda i: (i, 0)),
                  pl.BlockSpec((1, d), lambda i: (0, 0)),
                  pl.BlockSpec((1, 1, d), mod),
                  pl.BlockSpec((1, 1, d), mod)],
        out_specs=pl.BlockSpec((tr, d), lambda i: (i, 0)),
        out_shape=jax.ShapeDtypeStruct((rows, d), BF16),
        compiler_params=_params("parallel"),
    )(x, w.reshape(1, d), shift, scale)


def _in_proj_kinds(n_cols, tn):
    kinds = []
    for head in range(n_cols // HEAD_DIM):
        if OFF_NAK <= head < OFF_NAV:
            kinds.append("k")
        elif OFF_NAQ <= head < OFF_CVB:
            kinds.append("q")
        elif OFF_HGQ <= head < OFF_HGG:
            kinds.append("silu")
        else:
            kinds.append("id")
    per = tn // HEAD_DIM
    return tuple(tuple(kinds[t * per:(t + 1) * per]) for t in range(n_cols // tn))


def _in_proj_kernel(*refs, kinds, with_cast):
    if with_cast:
        a_ref, w_ref, qw_ref, kw_ref, side_ref, o_ref, side_o_ref, wb_ref = refs
        side_o_ref[...] = side_ref[...].astype(BF16)
    else:
        a_ref, w_ref, qw_ref, kw_ref, o_ref, wb_ref = refs
    j = pl.program_id(0)

    @pl.when(pl.program_id(1) == 0)
    def _():
        wb_ref[...] = w_ref[...].astype(BF16)

    def transform(y, kind):
        if kind == "k":
            return _rms(y, kw_ref[...])
        if kind == "q":
            return _rms(y, qw_ref[...]) * ATTN_SCALE
        if kind == "silu":
            return _silu(y)
        return y

    per_slab = MXU_N // HEAD_DIM
    for pattern in sorted(set(kinds)):
        hit = functools.reduce(jnp.logical_or, [j == jj for jj, tk in enumerate(kinds) if tk == pattern])

        @pl.when(hit)
        def _(pattern=pattern):
            a = a_ref[...]
            for c in range(len(pattern) // per_slab):
                cols = slice(c * MXU_N, (c + 1) * MXU_N)
                acc = _dot(a, wb_ref[:, cols])
                parts = [transform(acc[:, s * HEAD_DIM:(s + 1) * HEAD_DIM], pattern[c * per_slab + s])
                         for s in range(per_slab)]
                o_ref[:, cols] = jnp.concatenate(parts, axis=1).astype(o_ref.dtype)


def _in_proj(h, w_in, layer, q_norm, k_norm, n_cols, tm, tn, side_w=None):
    m, k = h.shape
    kinds = _in_proj_kinds(n_cols, tn)
    nj, ni = n_cols // tn, m // tm
    vec = pl.BlockSpec((1, HEAD_DIM), lambda j, i: (0, 0))
    in_specs = [pl.BlockSpec((tm, k), lambda j, i: (i, 0)),
                pl.BlockSpec((None, k, tn), lambda j, i: (layer, 0, j)),
                vec, vec]
    args = [h, w_in, q_norm.reshape(1, HEAD_DIM), k_norm.reshape(1, HEAD_DIM)]
    out_specs = [pl.BlockSpec((tm, tn), lambda j, i: (i, j))]
    out_shape = [jax.ShapeDtypeStruct((m, n_cols), BF16)]
    if side_w is not None:
        _, k2, n2 = side_w.shape
        strips = n2 // HEAD_DIM
        assert strips <= nj * ni
        strip = lambda j, i: jnp.minimum(j * ni + i, strips - 1)
        in_specs.append(pl.BlockSpec((None, k2, HEAD_DIM), lambda j, i: (layer, 0, strip(j, i))))
        args.append(side_w)
        out_specs.append(pl.BlockSpec((k2, HEAD_DIM), lambda j, i: (0, strip(j, i))))
        out_shape.append(jax.ShapeDtypeStruct((k2, n2), BF16))
    res = pl.pallas_call(
        functools.partial(_in_proj_kernel, kinds=kinds, with_cast=side_w is not None),
        grid=(nj, ni),
        in_specs=in_specs,
        out_specs=out_specs,
        out_shape=out_shape,
        scratch_shapes=[pltpu.VMEM((k, tn), BF16)],
        compiler_params=_params("arbitrary", "arbitrary"),
    )(*args)
    return res if side_w is not None else res[0]


def _proj_norm_kernel(*refs, splits, cast_weight, with_norm):
    n_a = len(splits)
    a_refs, rest = refs[:n_a], list(refs[n_a:])
    w_ref, x_ref, g_ref = rest[:3]
    rest = rest[3:]
    if with_norm:
        lw_ref, sc_ref, sh_ref, x1_ref, h2_ref = rest[:5]
        rest = rest[5:]
    else:
        x1_ref = rest.pop(0)
    if cast_weight:
        wb_ref = rest.pop(0)

        @pl.when(pl.program_id(0) == 0)
        def _():
            wb_ref[...] = w_ref[...].astype(BF16)
    else:
        wb_ref = w_ref

    tm, n = x1_ref.shape
    ssq = jnp.zeros((tm, HEAD_DIM), F32)
    for c in range(n // MXU_N):
        cols = slice(c * MXU_N, (c + 1) * MXU_N)
        acc = None
        off = 0
        for a_ref, width in zip(a_refs, splits):
            part = _dot(a_ref[...], wb_ref[off:off + width, cols])
            acc = part if acc is None else acc + part
            off += width
        x1 = x_ref[:, cols] + g_ref[0][:, cols] * acc
        x1_ref[:, cols] = x1
        if with_norm:
            sq = x1 * x1
            for s in range(MXU_N // HEAD_DIM):
                ssq = ssq + sq[:, s * HEAD_DIM:(s + 1) * HEAD_DIM]
    if with_norm:
        inv = lax.rsqrt(jnp.sum(ssq, axis=-1, keepdims=True) * (1.0 / n) + EPS)
        gain = lw_ref[...] * (1.0 + sc_ref[0])
        for c in range(n // MXU_N):
            cols = slice(c * MXU_N, (c + 1) * MXU_N)
            h2_ref[:, cols] = (x1_ref[:, cols] * inv * gain[:, cols] + sh_ref[0][:, cols]).astype(h2_ref.dtype)


def _proj_norm(a_list, w, layer, x, gate, mod_of_tile, tm, norm=None, vmem_limit=VMEM_LIMIT):
    m, n = x.shape
    cast_weight = w.ndim == 3
    k = w.shape[-2]
    splits = tuple(a.shape[1] for a in a_list)
    assert sum(splits) == k
    in_specs = [pl.BlockSpec((tm, s), lambda i: (i, 0)) for s in splits]
    if cast_weight:
        in_specs.append(pl.BlockSpec((None, k, n), lambda i: (layer, 0, 0), pipeline_mode=pl.Buffered(1)))
    else:
        in_specs.append(pl.BlockSpec((k, n), lambda i: (0, 0), pipeline_mode=pl.Buffered(1)))
    in_specs += [pl.BlockSpec((tm, n), lambda i: (i, 0)),
                 pl.BlockSpec((1, 1, n), lambda i: (mod_of_tile(i), 0, 0))]
    args = [*a_list, w, x, gate]
    out_specs = [pl.BlockSpec((tm, n), lambda i: (i, 0))]
    out_shape = [jax.ShapeDtypeStruct((m, n), F32)]
    if norm is not None:
        ln_w, scale, shift, norm_mod = norm
        mod = lambda i: (norm_mod(i), 0, 0)
        in_specs += [pl.BlockSpec((1, n), lambda i: (0, 0)), pl.BlockSpec((1, 1, n), mod),
                     pl.BlockSpec((1, 1, n), mod)]
        args += [ln_w.reshape(1, n), scale, shift]
        out_specs.append(pl.BlockSpec((tm, n), lambda i: (i, 0)))
        out_shape.append(jax.ShapeDtypeStruct((m, n), BF16))
    res = pl.pallas_call(
        functools.partial(_proj_norm_kernel, splits=splits, cast_weight=cast_weight, with_norm=norm is not None),
        grid=(m // tm,),
        in_specs=in_specs,
        out_specs=out_specs,
        out_shape=out_shape,
        scratch_shapes=[pltpu.VMEM((k, n), BF16)] if cast_weight else [],
        compiler_params=_params("arbitrary", vmem_limit=vmem_limit),
    )(*args)
    return res if norm is not None else res[0]


def _shift_rows(x, prev_row, next_row):
    r = x.shape[0]
    idx = lax.broadcasted_iota(jnp.int32, x.shape, 0)
    down = jnp.where(idx == 0, prev_row, pltpu.roll(x, 1, axis=0))
    up = jnp.where(idx == r - 1, next_row, pltpu.roll(x, r - 1, axis=0))
    return down, up


def _halo_specs(tr, tc, col_of, tiles_per_seq, n_rows):
    per = tr // HALO
    last_blk = n_rows // HALO - 1
    cur = pl.BlockSpec((tr, tc), lambda i, j: (i, col_of(j)))
    prev = pl.BlockSpec((HALO, tc), lambda i, j: (jnp.maximum(i * per - 1, 0), col_of(j)))
    nxt = pl.BlockSpec((HALO, tc), lambda i, j: (jnp.minimum((i + 1) * per, last_blk), col_of(j)))
    return cur, prev, nxt


def _seq_edges(tiles_per_seq):
    i = pl.program_id(0)
    pos = i % tiles_per_seq
    return (pos != 0).astype(F32), (pos != tiles_per_seq - 1).astype(F32)


def _ffn_up_kernel(*refs, tm, seq_len, with_cast):
    if with_cast:
        (h_ref, hp_ref, hn_ref, wg_ref, wv_ref, cg_ref, cv_ref, bg_ref, bv_ref, side_ref,
         o_ref, side_o_ref, lhs_ref) = refs
        side_o_ref[...] = side_ref[...].astype(BF16)
    else:
        h_ref, hp_ref, hn_ref, wg_ref, wv_ref, cg_ref, cv_ref, bg_ref, bv_ref, o_ref, lhs_ref = refs
    tiles_per_seq = max(seq_len // tm, 1)
    pos = pl.program_id(0) % tiles_per_seq
    has_prev = pos != 0
    has_next = pos != tiles_per_seq - 1

    @pl.when(pl.program_id(1) == 0)
    def _():
        lhs_ref[0:HALO, :] = jnp.where(has_prev, hp_ref[...], jnp.zeros_like(hp_ref))
        lhs_ref[HALO:HALO + tm, :] = h_ref[...]
        lhs_ref[HALO + tm:, :] = jnp.where(has_next, hn_ref[...], jnp.zeros_like(hn_ref))

    lhs = lhs_ref[...]

    def branch(w_ref, c_ref, b_ref):
        acc = _dot(lhs, w_ref[...])
        r = acc.shape[0]
        down, up = pltpu.roll(acc, 1, axis=0), pltpu.roll(acc, r - 1, axis=0)
        if seq_len < tm:
            tok = (lax.broadcasted_iota(jnp.int32, acc.shape, 0) + (seq_len - HALO)) % seq_len
            down = jnp.where(tok == 0, 0.0, down)
            up = jnp.where(tok == seq_len - 1, 0.0, up)
        y = down * c_ref[0:1, :] + acc * c_ref[1:2, :] + up * c_ref[2:3, :]
        return y[HALO:HALO + tm, :] + b_ref[...]

    gate = branch(wg_ref, cg_ref, bg_ref)
    val = branch(wv_ref, cv_ref, bv_ref)
    o_ref[...] = (_silu(gate) * val).astype(o_ref.dtype)


def _ffn_up_conv_gate(h, w_up, cw, cb, seq_len, tm, tn, side_w=None, layer=0):
    rows, k = h.shape
    assert seq_len % tm == 0 or (tm % seq_len == 0 and seq_len >= HALO)
    nj = D_FF // tn
    per = tm // HALO
    last_blk = rows // HALO - 1
    cb2 = cb.reshape(1, 2 * D_FF)
    steps = (rows // tm) * nj
    if side_w is not None:
        _, k2, n2 = side_w.shape
        strips = max(d for d in range(1, steps + 1) if k2 % d == 0 and (k2 // d) % HALO == 0)
        strip = lambda i, j: jnp.minimum(i * nj + j, strips - 1)
        side_in = [pl.BlockSpec((None, k2 // strips, n2), lambda i, j: (layer, strip(i, j), 0))]
        side_out = [pl.BlockSpec((k2 // strips, n2), lambda i, j: (strip(i, j), 0))]
        side_shape = [jax.ShapeDtypeStruct((k2, n2), BF16)]
        side_args = [side_w]
    else:
        side_in, side_out, side_shape, side_args = [], [], [], []
    in_specs = [pl.BlockSpec((tm, k), lambda i, j: (i, 0)),
                pl.BlockSpec((HALO, k), lambda i, j: (jnp.maximum(i * per - 1, 0), 0)),
                pl.BlockSpec((HALO, k), lambda i, j: (jnp.minimum((i + 1) * per, last_blk), 0)),
                pl.BlockSpec((k, tn), lambda i, j: (0, j)),
                pl.BlockSpec((k, tn), lambda i, j: (0, j + nj)),
                pl.BlockSpec((CONV_W, tn), lambda i, j: (0, j)),
                pl.BlockSpec((CONV_W, tn), lambda i, j: (0, j + nj)),
                pl.BlockSpec((1, tn), lambda i, j: (0, j)),
                pl.BlockSpec((1, tn), lambda i, j: (0, j + nj))]
    res = pl.pallas_call(
        functools.partial(_ffn_up_kernel, tm=tm, seq_len=seq_len, with_cast=side_w is not None),
        grid=(rows // tm, nj),
        in_specs=in_specs + side_in,
        out_specs=[pl.BlockSpec((tm, tn), lambda i, j: (i, j))] + side_out,
        out_shape=[jax.ShapeDtypeStruct((rows, D_FF), BF16)] + side_shape,
        scratch_shapes=[pltpu.VMEM((tm + 2 * HALO, k), BF16)],
        compiler_params=_params("arbitrary", "arbitrary"),
    )(h, h, h, w_up, w_up, cw, cw, cb2, cb2, *side_args)
    return res if side_w is not None else res[0]


def _short_conv_kernel(b_ref, c_ref, cp_ref, cn_ref, v_ref, vp_ref, vn_ref, w_ref, nw_ref, o_ref, *, tiles_per_seq):
    has_prev, has_next = _seq_edges(tiles_per_seq)
    p = c_ref[...].astype(F32) * v_ref[...].astype(F32)
    prev_row = cp_ref[HALO - 1:HALO, :].astype(F32) * vp_ref[HALO - 1:HALO, :].astype(F32) * has_prev
    next_row = cn_ref[0:1, :].astype(F32) * vn_ref[0:1, :].astype(F32) * has_next
    down, up = _shift_rows(p, prev_row, next_row)
    y = b_ref[...].astype(F32) * (down * w_ref[0:1, :] + p * w_ref[1:2, :] + up * w_ref[2:3, :])
    for h in range(N_CV):
        sl = slice(h * HEAD_DIM, (h + 1) * HEAD_DIM)
        o_ref[:, sl] = _rms(y[:, sl], nw_ref[:, sl]).astype(o_ref.dtype)


def _short_conv(u, cv_w, cv_onorm, seq_len, tr):
    rows = u.shape[0]
    tiles_per_seq = seq_len // tr
    blk = lambda off: (lambda j: off * HEAD_DIM // CV_W)
    b_spec = pl.BlockSpec((tr, CV_W), lambda i, j: (i, OFF_CVB * HEAD_DIM // CV_W))
    c_specs = _halo_specs(tr, CV_W, blk(OFF_CVC), tiles_per_seq, rows)
    v_specs = _halo_specs(tr, CV_W, blk(OFF_CVV), tiles_per_seq, rows)
    return pl.pallas_call(
        functools.partial(_short_conv_kernel, tiles_per_seq=tiles_per_seq),
        grid=(rows // tr, 1),
        in_specs=[b_spec, *c_specs, *v_specs,
                  pl.BlockSpec((CONV_W, CV_W), lambda i, j: (0, 0)),
                  pl.BlockSpec((1, CV_W), lambda i, j: (0, 0))],
        out_specs=pl.BlockSpec((tr, CV_W), lambda i, j: (i, 0)),
        out_shape=jax.ShapeDtypeStruct((rows, CV_W), BF16),
        compiler_params=_params("parallel", "arbitrary"),
    )(u, u, u, u, u, u, u, cv_w, cv_onorm.reshape(1, CV_W))


def _hgrn_structure(chunk, reverse):
    idx = np.arange(chunk)
    i, t = idx[:, None], idx[None, :]
    mats = [t <= i, t > i]
    masks = []
    s = chunk // 2
    while s >= 1:
        blk = idx // (2 * s)
        upper = (idx % (2 * s)) >= s
        mid = blk * 2 * s + s - 1
        a = np.where(upper[:, None], (t > mid[:, None]) & (t <= i), (t > i) & (t <= mid[:, None]))
        mats.append(a)
        masks.append((blk[:, None] == blk[None, :]) & upper[:, None] & ~upper[None, :])
        s //= 2
    mats = np.stack(mats).astype(np.float32)
    masks = np.stack(masks).astype(np.float32)
    if reverse:
        mats = mats[:, ::-1, ::-1]
        masks = masks[:, ::-1, ::-1]
    return mats.reshape(-1, chunk), masks


def _hgrn_kernel(*refs, chunk, n_sub, n_levels, reverse, mode):
    if mode == "final":
        (z_ref, v_ref, q_ref, lb_ref, a_ref, m_ref, s0_ref, g_ref, ofw_ref, nw_ref,
         o_ref, sfin_ref, st_ref) = refs
    elif mode == "raw":
        z_ref, v_ref, q_ref, lb_ref, a_ref, m_ref, s0_ref, o_ref, sfin_ref, st_ref = refs
    else:
        z_ref, v_ref, lb_ref, a_ref, s0_ref, sfin_ref, st_ref = refs
    c = pl.program_id(1)

    @pl.when(c == 0)
    def _():
        st_ref[...] = s0_ref[0]

    lb = lb_ref[...]
    a = a_ref[...]
    tot_row = 0 if reverse else chunk - 1
    subs = list(range(n_sub - 1, -1, -1) if reverse else range(n_sub))

    def gates(sub):
        rows = slice(sub * chunk, (sub + 1) * chunk)
        z = z_ref[rows, :].astype(F32)
        f = lb + (1.0 - lb) * jax.nn.sigmoid(z)
        log_f = jnp.log(jnp.maximum(f, F_FLOOR))
        k_all = (1.0 - lb) * jax.nn.sigmoid(-z)
        expo = _dot(a, log_f.astype(BF16))
        return k_all, v_ref[rows, :].astype(F32), expo

    def products(sub, h, k_all, v_all, expo):
        rows = slice(sub * chunk, (sub + 1) * chunk)
        sl = slice(h * HEAD_DIM, (h + 1) * HEAD_DIM)
        k, v = k_all[:, sl], v_all[:, sl]
        e_rem = jnp.exp(expo[chunk:2 * chunk, sl])
        e_tot = jnp.exp(expo[tot_row:tot_row + 1, sl])
        st = st_ref[h]
        st_ref[h] = st * e_tot + _dot_tn(v.astype(BF16), (k * e_rem).astype(BF16))
        if mode == "state":
            return None
        q = q_ref[rows, sl].astype(F32)
        e_cum = jnp.exp(expo[0:chunk, sl])
        o_inter = _dot_nt((q * e_cum).astype(BF16), st.astype(BF16))
        parts = []
        for lvl in range(n_levels):
            e = jnp.exp(expo[(2 + lvl) * chunk:(3 + lvl) * chunk, sl])
            parts.append(_dot_nt((q * e).astype(BF16), (k * e).astype(BF16)))
        diag = jnp.sum(q * k, axis=-1, keepdims=True)
        return o_inter, parts, diag, v

    def readout(sub, h, o_inter, parts, diag, v):
        rows = slice(sub * chunk, (sub + 1) * chunk)
        sl = slice(h * HEAD_DIM, (h + 1) * HEAD_DIM)
        att = None
        for lvl, part in enumerate(parts):
            att = m_ref[lvl] * part if att is None else att + m_ref[lvl] * part
        o = o_inter + _dot(att.astype(BF16), v.astype(BF16)) + diag * v
        if mode == "final":
            y = _rms(o + ofw_ref[rows, sl], nw_ref[...]) * _silu(g_ref[rows, sl].astype(F32))
            o_ref[rows, sl] = y.astype(o_ref.dtype)
        else:
            o_ref[rows, sl] = o.astype(o_ref.dtype)

    tiles = [(sub, h) for sub in subs for h in range(N_HG)]
    cache = {}

    def stage(sub, h):
        if sub not in cache:
            cache.clear()
            cache[sub] = gates(sub)
        return products(sub, h, *cache[sub])

    queue = [stage(*tile) for tile in tiles[:TRACE_AHEAD]]
    for t, tile in enumerate(tiles):
        if t + TRACE_AHEAD < len(tiles):
            queue.append(stage(*tiles[t + TRACE_AHEAD]))
        done = queue.pop(0)
        if mode != "state":
            readout(*tile, *done)

    @pl.when(c == pl.num_programs(1) - 1)
    def _():
        sfin_ref[0] = st_ref[...]


def _hgrn_scan(u, lb, s0, seq_len, *, reverse, mode, o_fw=None, norm_w=None):
    rows = u.shape[0]
    batch = rows // seq_len
    blk = min(HG_BLOCK, seq_len)
    chunk = min(HG_CHUNK, blk)
    n_blocks = seq_len // blk
    mats, masks = _hgrn_structure(chunk, reverse)
    n_levels = masks.shape[0]
    if mode == "state":
        mats = mats[:2 * chunk]
    per_blk = HG_W // HEAD_DIM

    def tok(off):
        col = off // per_blk
        if reverse:
            return pl.BlockSpec((blk, HG_W), lambda b, c: (b * n_blocks + n_blocks - 1 - c, col))
        return pl.BlockSpec((blk, HG_W), lambda b, c: (b * n_blocks + c, col))

    const2 = lambda shape: pl.BlockSpec(shape, lambda b, c: (0, 0))
    state_spec = pl.BlockSpec((1, N_HG, HEAD_DIM, HEAD_DIM), lambda b, c: (b, 0, 0, 0))
    state_shape = jax.ShapeDtypeStruct((batch, N_HG, HEAD_DIM, HEAD_DIM), F32)
    z_spec = tok(OFF_FBW if reverse else OFF_FFW)
    lb2, mats_b = lb.reshape(1, HG_W), jnp.asarray(mats, BF16)
    if mode == "state":
        in_specs = [z_spec, tok(OFF_I), const2((1, HG_W)), const2(mats.shape), state_spec]
        args = [u, u, lb2, mats_b, s0]
        out_specs, out_shape = [state_spec], [state_shape]
    else:
        in_specs = [z_spec, tok(OFF_I), tok(OFF_HGQ), const2((1, HG_W)), const2(mats.shape),
                    pl.BlockSpec(masks.shape, lambda b, c: (0, 0, 0)), state_spec]
        args = [u, u, u, lb2, mats_b, jnp.asarray(masks, F32), s0]
        if mode == "final":
            in_specs += [tok(OFF_HGG), tok(0), const2((1, HEAD_DIM))]
            args += [u, o_fw, norm_w.reshape(1, HEAD_DIM)]
        out_specs = [tok(0), state_spec]
        out_shape = [jax.ShapeDtypeStruct((rows, HG_W), BF16 if mode == "final" else F32), state_shape]
    res = pl.pallas_call(
        functools.partial(_hgrn_kernel, chunk=chunk, n_sub=blk // chunk, n_levels=n_levels, reverse=reverse,
                          mode=mode),
        grid=(batch, n_blocks),
        in_specs=in_specs,
        out_specs=out_specs,
        out_shape=out_shape,
        scratch_shapes=[pltpu.VMEM((N_HG, HEAD_DIM, HEAD_DIM), F32)],
        compiler_params=_params("parallel", "arbitrary"),
    )(*args)
    return (None, res[0]) if mode == "state" else (res[0], res[1])


def _na_bias_table(rpb, rows):
    n_rb = rows // NA_WIN_R
    n_cb = GRID_W // NA_QBLK_C
    n_dr, n_dc = 2 * NA_WIN_R - 1, 2 * NA_WIN_C - 1
    n_k = NA_KEY_R * NA_KEY_C
    offs, row_mask = [], []
    for rb in (0, min(1, n_rb - 1), n_rb - 1):
        k_row0 = int(np.clip(rb * NA_WIN_R - NA_WIN_R // 2, 0, rows - NA_KEY_R))
        q_r = rb * NA_WIN_R + np.arange(NA_WIN_R)
        k_r = k_row0 + np.arange(NA_KEY_R)
        r_start = np.clip(q_r - NA_WIN_R // 2, 0, rows - NA_WIN_R)
        ok_r = (k_r[None, :] >= r_start[:, None]) & (k_r[None, :] < r_start[:, None] + NA_WIN_R)
        row_mask.append(np.repeat(np.where(ok_r, 0.0, NEG_INF), NA_KEY_C, axis=1))
        offs.append(k_row0 - rb * NA_WIN_R + NA_WIN_R - 1)
    row_mask = jnp.asarray(np.stack(row_mask), F32)
    pad_lo = NA_WIN_R - 1 - min(offs)
    per_vreg = HEAD_DIM // NA_KEY_C
    n_a = -(-(NA_KEY_R - 1 + max(offs) + pad_lo + 1) // per_vreg) * per_vreg
    lanes = n_a * NA_KEY_C
    dr_pad = -(-n_dr // SUBLANE) * SUBLANE
    assert n_dc <= HEAD_DIM and n_a >= n_dr + pad_lo
    starts, col_mask = [], []
    for cb in range(n_cb):
        k_col0 = int(np.clip(cb * NA_QBLK_C - NA_WIN_C // 2, 0, GRID_W - NA_KEY_C))
        q_c = cb * NA_QBLK_C + np.arange(NA_QBLK_C)
        k_c = k_col0 + np.arange(NA_KEY_C)
        c_start = np.clip(q_c - NA_WIN_C // 2, 0, GRID_W - NA_WIN_C)
        ok_c = (k_c[None, :] >= c_start[:, None]) & (k_c[None, :] < c_start[:, None] + NA_WIN_C)
        col_mask.append(np.tile(np.where(ok_c, 0.0, NEG_INF), (1, n_a)))
        starts.append(k_col0 - cb * NA_QBLK_C + NA_WIN_C - 1)
    col_mask = jnp.asarray(np.stack(col_mask), F32)
    lane = np.arange(lanes)
    tile_w = jnp.asarray(np.tile(lane[None, :] % NA_KEY_C == np.arange(NA_KEY_C)[:, None], (3, 1)), BF16)
    pick = jnp.asarray(lane[None, :] // NA_KEY_C == np.arange(dr_pad)[:, None] + pad_lo, F32)
    n_heads = rpb.shape[0]
    rpb_pad = jnp.pad(rpb.astype(F32), ((0, 0), (0, dr_pad - n_dr), (0, HEAD_DIM - n_dc)))
    return pl.pallas_call(
        functools.partial(_na_bias_kernel, offs=tuple(offs), pad_lo=pad_lo, starts=tuple(starts)),
        grid=(n_heads,),
        in_specs=[pl.BlockSpec((1, dr_pad, HEAD_DIM), lambda h: (h, 0, 0)),
                  pl.BlockSpec((n_cb, NA_QBLK_C, lanes), lambda h: (0, 0, 0)),
                  pl.BlockSpec((3, NA_WIN_R, n_k), lambda h: (0, 0, 0)),
                  pl.BlockSpec((3 * NA_KEY_C, lanes), lambda h: (0, 0)),
                  pl.BlockSpec((dr_pad, lanes), lambda h: (0, 0))],
        out_specs=pl.BlockSpec((1, 3, n_cb, NA_WIN_R * NA_QBLK_C, n_k), lambda h: (h, 0, 0, 0, 0)),
        out_shape=jax.ShapeDtypeStruct((n_heads, 3, n_cb, NA_WIN_R * NA_QBLK_C, n_k), F32),
        compiler_params=_params("parallel"),
    )(rpb_pad, col_mask, row_mask, tile_w, pick)


def _na_bias_kernel(rpb_ref, cm_ref, rm_ref, w_ref, pick_ref, o_ref, *, offs, pad_lo, starts):
    n_cb = cm_ref.shape[0]
    lanes = cm_ref.shape[-1]
    per = HEAD_DIM // NA_KEY_C
    n_k = o_ref.shape[-1]
    table = rpb_ref[0]
    dr_pad = table.shape[0]
    for cb in range(n_cb):
        windows = []
        for c in range(NA_QBLK_C):
            rolled = pltpu.roll(table, (c - starts[cb]) % HEAD_DIM, axis=1)
            windows.append(rolled[:, :NA_KEY_C])
        win = jnp.concatenate(windows, axis=0)
        hi = win.astype(BF16)
        mid = (win - hi.astype(F32)).astype(BF16)
        lo = (win - hi.astype(F32) - mid.astype(F32)).astype(BF16)
        tiled = _dot(jnp.concatenate([hi, mid, lo], axis=1), w_ref[...])
        t = jnp.sum(tiled.reshape(NA_QBLK_C, dr_pad, lanes) * pick_ref[...][None], axis=1) + cm_ref[cb]
        shifted = [t] + [pltpu.roll(t, lanes - NA_KEY_C * s, axis=1) for s in range(1, per)]
        for cls, off in enumerate(offs):
            for r in range(NA_WIN_R):
                m = off - r + pad_lo
                base = (m // per) * HEAD_DIM
                tile = shifted[m % per][:, base:base + n_k] + rm_ref[cls, r:r + 1, :]
                o_ref[0, cls, cb, r * NA_QBLK_C:(r + 1) * NA_QBLK_C, :] = tile


def _na_kernel(q_ref, k0_ref, k1_ref, k2_ref, k3_ref, v0_ref, v1_ref, v2_ref, v3_ref, kc_ref, vc_ref, bias_ref,
               ow_ref, o_ref):
    n_cb = GRID_W // NA_QBLK_C
    q_all = q_ref[...].astype(F32)
    k_all = jnp.concatenate([r[...] for r in (k0_ref, k1_ref, k2_ref, k3_ref)], axis=0).astype(F32)
    v_all = jnp.concatenate([r[...] for r in (v0_ref, v1_ref, v2_ref, v3_ref)], axis=0).astype(F32)
    def patch(x_all, hh, rows, col0, cols):
        x3 = x_all[:, hh * HEAD_DIM:(hh + 1) * HEAD_DIM].reshape(rows, GRID_W, HEAD_DIM)
        return x3[:, col0:col0 + cols, :].reshape(rows * cols, HEAD_DIM).astype(BF16)

    def scores(hh, cb):
        q0 = cb * NA_QBLK_C
        k_col0 = min(max(q0 - NA_WIN_C // 2, 0), GRID_W - NA_KEY_C)
        qm = patch(q_all, hh, NA_WIN_R, q0, NA_QBLK_C)
        km = patch(k_all, hh, NA_KEY_R, k_col0, NA_KEY_C)
        s_loc = _dot_nt(qm, km) + bias_ref[hh, 0, cb]
        s_ctx = _dot_nt(qm, kc_ref[:, hh * HEAD_DIM:(hh + 1) * HEAD_DIM])
        return s_loc, s_ctx

    def finish(hh, cb, s_loc, s_ctx):
        q0 = cb * NA_QBLK_C
        k_col0 = min(max(q0 - NA_WIN_C // 2, 0), GRID_W - NA_KEY_C)
        hs = slice(hh * HEAD_DIM, (hh + 1) * HEAD_DIM)
        vm = patch(v_all, hh, NA_KEY_R, k_col0, NA_KEY_C)
        mx = jnp.maximum(jnp.max(s_loc, axis=-1, keepdims=True), jnp.max(s_ctx, axis=-1, keepdims=True))
        p_loc = jnp.exp(s_loc - mx)
        p_ctx = jnp.exp(s_ctx - mx)
        den = jnp.sum(p_loc, axis=-1, keepdims=True) + jnp.sum(p_ctx, axis=-1, keepdims=True)
        o = (_dot(p_loc.astype(BF16), vm) + _dot(p_ctx.astype(BF16), vc_ref[:, hs])) / den
        o = _rms(o, ow_ref[hh]).astype(o_ref.dtype)
        for r in range(NA_WIN_R):
            o_ref[r * GRID_W + q0:r * GRID_W + q0 + NA_QBLK_C, hs] = o[r * NA_QBLK_C:(r + 1) * NA_QBLK_C, :]

    tiles = [(hh, cb) for hh in range(NA_HEADS) for cb in range(n_cb)]
    queue = [scores(*tile) for tile in tiles[:TRACE_AHEAD]]
    for t, tile in enumerate(tiles):
        if t + TRACE_AHEAD < len(tiles):
            queue.append(scores(*tiles[t + TRACE_AHEAD]))
        finish(*tile, *queue.pop(0))


def _neighbourhood_attention(u, u_ctx, bias, head0, out_norm, batch, rows, ctx_len):
    assert rows % NA_WIN_R == 0 and rows >= NA_KEY_R
    n_rb = rows // NA_WIN_R
    q_blk = NA_WIN_R * GRID_W
    k_blk = NA_KEY_R * GRID_W // 4
    k_per_batch = rows * GRID_W // k_blk

    width = NA_HEADS * HEAD_DIM
    assert all(off % NA_HEADS == 0 for off in (OFF_NAQ, OFF_NAK, OFF_NAV, head0, N_NA))
    col_q, col_k, col_v = OFF_NAQ // NA_HEADS, OFF_NAK // NA_HEADS, OFF_NAV // NA_HEADS

    def key_spec(col, part):
        def index(h, rb, b):
            first = jnp.clip(rb * NA_WIN_R - NA_WIN_R // 2, 0, rows - NA_KEY_R) * GRID_W // k_blk
            return (b * k_per_batch + first + part, col + h)
        return pl.BlockSpec((k_blk, width), index)

    def row_class(rb):
        return jnp.where(rb == 0, 0, jnp.where(rb == n_rb - 1, 2, 1))

    n_cb = GRID_W // NA_QBLK_C
    in_specs = [pl.BlockSpec((q_blk, width), lambda h, rb, b: (b * n_rb + rb, col_q + h))]
    in_specs += [key_spec(col_k, p) for p in range(4)] + [key_spec(col_v, p) for p in range(4)]
    in_specs += [pl.BlockSpec((ctx_len, width), lambda h, rb, b: (b, col_k + h)),
                 pl.BlockSpec((ctx_len, width), lambda h, rb, b: (b, col_v + h)),
                 pl.BlockSpec((NA_HEADS, 1, n_cb, NA_WIN_R * NA_QBLK_C, NA_KEY_R * NA_KEY_C),
                              lambda h, rb, b: (head0 // NA_HEADS + h, row_class(rb), 0, 0, 0)),
                 pl.BlockSpec((NA_HEADS, 1, HEAD_DIM), lambda h, rb, b: (h, 0, 0))]
    return pl.pallas_call(
        _na_kernel,
        grid=(N_NA // NA_HEADS, n_rb, batch),
        in_specs=in_specs,
        out_specs=pl.BlockSpec((q_blk, width), lambda h, rb, b: (b * n_rb + rb, h)),
        out_shape=jax.ShapeDtypeStruct((u.shape[0], NA_W), BF16),
        compiler_params=_params("parallel", "parallel", "parallel"),
    )(u, u, u, u, u, u, u, u, u, u_ctx, u_ctx, bias, out_norm.reshape(N_NA, 1, HEAD_DIM))


def _ctx_attn_kernel(q_ref, k_ref, v_ref, ow_ref, o_ref):
    for hh in range(NA_HEADS):
        hs = slice(hh * HEAD_DIM, (hh + 1) * HEAD_DIM)
        s = _dot_nt(q_ref[:, hs], k_ref[:, hs])
        p = jnp.exp(s - jnp.max(s, axis=-1, keepdims=True))
        o = _dot(p.astype(BF16), v_ref[:, hs]) / jnp.sum(p, axis=-1, keepdims=True)
        o_ref[:, hs] = _rms(o, ow_ref[hh]).astype(o_ref.dtype)


def _context_attention(u_ctx, out_norm, batch, ctx_len):
    width = NA_HEADS * HEAD_DIM
    tok = lambda off: pl.BlockSpec((ctx_len, width), lambda b, h: (b, off // NA_HEADS + h))
    return pl.pallas_call(
        _ctx_attn_kernel,
        grid=(batch, N_NA // NA_HEADS),
        in_specs=[tok(OFF_NAQ), tok(OFF_NAK), tok(OFF_NAV),
                  pl.BlockSpec((NA_HEADS, 1, HEAD_DIM), lambda b, h: (h, 0, 0))],
        out_specs=tok(0),
        out_shape=jax.ShapeDtypeStruct((u_ctx.shape[0], NA_W), BF16),
        compiler_params=_params("parallel", "parallel"),
    )(u_ctx, u_ctx, u_ctx, out_norm.reshape(N_NA, 1, HEAD_DIM))


def _row_tile(rows, target):
    t = min(rows, target)
    assert rows % t == 0
    return t


def _token_stream(xs, seq_len, mod_row, p, layer, w_up_b, ada_l, mixers, w_down_b=None, next_norm=None):
    rows = xs.shape[0]
    sh2, sc2, g1, g2 = ada_l["sh2"], ada_l["sc2"], ada_l["g1"], ada_l["g2"]

    def mod_of_tile(tile_rows):
        per = max(seq_len // tile_rows, 1)
        return (lambda i: 1 + i // per) if mod_row is None else (lambda i: mod_row)

    tm_o = _row_tile(rows, TM_OUT)
    x1, h2 = _proj_norm(mixers, p["w_out"], layer, xs, g1, mod_of_tile(tm_o), tm_o,
                        norm=(p["ln2"][layer], sc2, sh2, mod_of_tile(tm_o)))
    tm = _row_tile(rows, TM_TOKENS)
    conv = (p["f_cw"][layer], p["f_cb"][layer])
    if w_down_b is None:
        act, w_down_b = _ffn_up_conv_gate(h2, w_up_b, *conv, seq_len, tm, TN_FF, side_w=p["w_down"], layer=layer)
    else:
        act = _ffn_up_conv_gate(h2, w_up_b, *conv, seq_len, tm, TN_FF)
    tm_d = _row_tile(rows, TM_DOWN)
    norm = None if next_norm is None else (*next_norm, mod_of_tile(tm_d))
    return _proj_norm([act], w_down_b, None, x1, g2, mod_of_tile(tm_d), tm_d, norm=norm,
                      vmem_limit=VMEM_LIMIT_DOWN), w_down_b


def kernel(x, c, ctx, c_ctx, w_ada, b_ada, ln1_w, ln2_w, w_in, hg_lb_logits, hg_norm_w, na_q_norm_w, na_k_norm_w,
           na_rpb, na_out_norm_w, cv_w, cv_out_norm_w, w_out, w_up, ffn_conv_w, ffn_conv_b, w_down):
    batch, seq, d = x.shape
    ctx_len = ctx.shape[1]
    rows = seq // GRID_W
    depth = w_ada.shape[0]
    xs = x.reshape(batch * seq, d)
    cs = ctx.reshape(batch * ctx_len, d)

    lb_sm = jax.nn.softmax(hg_lb_logits.astype(F32), axis=1)
    lb_all = jnp.cumsum(lb_sm, axis=1) - lb_sm[:, :1]

    cond = jnp.zeros((SUBLANE, d), F32).at[0].set(c_ctx).at[1:1 + batch].set(c)
    ada = _ada_table(cond, w_ada, b_ada).reshape(depth, SUBLANE, 6, 1, d)

    s_zero = jnp.zeros((batch, N_HG, HEAD_DIM, HEAD_DIM), F32)
    p = {"w_out": w_out, "w_down": w_down, "ln2": ln2_w, "f_cw": ffn_conv_w,
         "f_cb": ffn_conv_b}
    ctx_cols = sum(IN_SPLITS[:5])
    bias = _na_bias_table(na_rpb.reshape((depth * N_NA,) + na_rpb.shape[2:]), rows)
    names = ("sh1", "sc1", "g1", "sh2", "sc2", "g2")
    ada_all = [{n: ada[l, :, i] for i, n in enumerate(names)} for l in range(depth)]
    tm = _row_tile(xs.shape[0], TM_TOKENS)
    tm_c = _row_tile(cs.shape[0], TM_TOKENS)
    tiles_per_seq = seq // tm
    h = _norm_mod(xs, ln1_w[0], ada_all[0]["sh1"], ada_all[0]["sc1"], lambda i: 1 + i // tiles_per_seq, tm)
    hc = _norm_mod(cs, ln1_w[0], ada_all[0]["sh1"], ada_all[0]["sc1"], lambda i: 0, tm_c)
    for l in range(depth):
        last = l == depth - 1
        ada_l = ada_all[l]
        next_norm = None if last else (ln1_w[l + 1], ada_all[l + 1]["sc1"], ada_all[l + 1]["sh1"])
        if (IN_W // TN_IN) * (xs.shape[0] // tm) >= w_up.shape[2] // HEAD_DIM:
            u, w_up_b = _in_proj(h, w_in, l, na_q_norm_w[l], na_k_norm_w[l], IN_W, tm, TN_IN, side_w=w_up)
        else:
            u = _in_proj(h, w_in, l, na_q_norm_w[l], na_k_norm_w[l], IN_W, tm, TN_IN)
            w_up_b = w_up[l].astype(BF16)
        if last:
            uc = _in_proj(hc, w_in, l, na_q_norm_w[l], na_k_norm_w[l], ctx_cols, tm_c, TN_IN_CTX)
        else:
            uc = _in_proj(hc, w_in, l, na_q_norm_w[l], na_k_norm_w[l], IN_W, tm_c, TN_IN)

        if last:
            _, s_fw = _hgrn_scan(uc, lb_all[0, l], s_zero, ctx_len, reverse=False, mode="state")
            _, s_bw = _hgrn_scan(uc, lb_all[1, l], s_zero, ctx_len, reverse=True, mode="state")
        else:
            co_fw, s_fw = _hgrn_scan(uc, lb_all[0, l], s_zero, ctx_len, reverse=False, mode="raw")
            hg_c, s_bw = _hgrn_scan(uc, lb_all[1, l], s_zero, ctx_len, reverse=True, mode="final", o_fw=co_fw,
                                    norm_w=hg_norm_w[l])
        o_fw, _ = _hgrn_scan(u, lb_all[0, l], s_fw, seq, reverse=False, mode="raw")
        hg_out, _ = _hgrn_scan(u, lb_all[1, l], s_bw, seq, reverse=True, mode="final", o_fw=o_fw,
                               norm_w=hg_norm_w[l])

        na_out = _neighbourhood_attention(u, uc, bias, l * N_NA, na_out_norm_w[l], batch, rows, ctx_len)
        cv_out = _short_conv(u, cv_w[l], cv_out_norm_w[l], seq, _row_tile(seq, TM_TOKENS))

        res, w_down_b = _token_stream(xs, seq, None, p, l, w_up_b, ada_l, [hg_out, na_out, cv_out],
                                      next_norm=next_norm)
        if last:
            xs = res
        else:
            xs, h = res
            na_c = _context_attention(uc, na_out_norm_w[l], batch, ctx_len)
            cv_c = _short_conv(uc, cv_w[l], cv_out_norm_w[l], ctx_len, _row_tile(ctx_len, TM_TOKENS))
            (cs, hc), _ = _token_stream(cs, ctx_len, 0, p, l, w_up_b, ada_l, [hg_c, na_c, cv_c],
                                        w_down_b=w_down_b, next_norm=next_norm)
    return xs.reshape(batch, seq, d)
```

```python
import functools

import numpy as np
import jax
import jax.numpy as jnp
from jax import lax
from jax.experimental import pallas as pl
from jax.experimental.pallas import tpu as pltpu

F32 = jnp.float32
BF16 = jnp.bfloat16

D_MODEL = 2048
DEPTH = 2
GRID_W = 64
HEAD_DIM = 128
N_HG = 4
N_NA = 8
N_CV = 4
HG_W = N_HG * HEAD_DIM
NA_W = N_NA * HEAD_DIM
CV_W = N_CV * HEAD_DIM
IN_SPLITS = (HG_W, HG_W, HG_W, NA_W, NA_W, HG_W, HG_W, NA_W, CV_W, CV_W, CV_W)
IN_W = sum(IN_SPLITS)
(OFF_FFW, OFF_FBW, OFF_I, OFF_NAK, OFF_NAV, OFF_HGQ, OFF_HGG, OFF_NAQ, OFF_CVB, OFF_CVC,
 OFF_CVV) = [int(v) // HEAD_DIM for v in np.cumsum((0,) + IN_SPLITS[:-1])]
NA_WIN_R = 8
NA_WIN_C = 16
NA_QBLK_C = 16
NA_KEY_C = 32
NA_KEY_R = 16
NA_HEADS = 4
CONV_W = 3
D_FF = 5632
EPS = 1e-6
F_FLOOR = 1e-30
ATTN_SCALE = HEAD_DIM ** -0.5
NEG_INF = -1e30

HG_CHUNK = 128
HG_BLOCK = 1024
TM_TOKENS = 1024
TN_IN = 1024
TN_IN_CTX = 512
TN_FF = 512
TM_OUT = 512
TM_DOWN = 512
SUBLANE = 8
HALO = 16
MXU_N = 256
TRACE_AHEAD = 3
VMEM_LIMIT = 56 * 1024 * 1024
VMEM_LIMIT_DOWN = 60 * 1024 * 1024


def _params(*sem, vmem_limit=VMEM_LIMIT):
    return pltpu.CompilerParams(dimension_semantics=sem, vmem_limit_bytes=vmem_limit)


def _dot(a, b):
    return jnp.dot(a, b, preferred_element_type=F32)


def _dot_nt(a, b):
    return lax.dot_general(a, b, (((1,), (1,)), ((), ())), preferred_element_type=F32)


def _dot_tn(a, b):
    return lax.dot_general(a, b, (((0,), (0,)), ((), ())), preferred_element_type=F32)


def _rms(x, w):
    return x * lax.rsqrt(jnp.mean(x * x, axis=-1, keepdims=True) + EPS) * w


def _silu(x):
    half = 0.5 * x
    return half + half * jnp.tanh(half)


def _ada_kernel(s_ref, w_ref, b_ref, o_ref):
    s = _silu(s_ref[...]).astype(BF16)
    o_ref[0] = _dot(s, w_ref[0].astype(BF16)) + b_ref[0]


def _ada_table(cond, w_ada, b_ada):
    depth, d, n = w_ada.shape
    tn = TN_IN
    return pl.pallas_call(
        _ada_kernel,
        grid=(depth, n // tn),
        in_specs=[pl.BlockSpec((SUBLANE, d), lambda l, j: (0, 0)),
                  pl.BlockSpec((1, d, tn), lambda l, j: (l, 0, j)),
                  pl.BlockSpec((1, 1, tn), lambda l, j: (l, 0, j))],
        out_specs=pl.BlockSpec((1, SUBLANE, tn), lambda l, j: (l, 0, j)),
        out_shape=jax.ShapeDtypeStruct((depth, SUBLANE, n), F32),
        compiler_params=_params("parallel", "parallel"),
    )(cond, w_ada, b_ada.reshape(depth, 1, n))


def _norm_mod_kernel(x_ref, w_ref, sh_ref, sc_ref, o_ref):
    y = _rms(x_ref[...], w_ref[...])
    o_ref[...] = (y * (1.0 + sc_ref[0]) + sh_ref[0]).astype(o_ref.dtype)


def _norm_mod(x, w, shift, scale, mod_of_tile, tr):
    rows, d = x.shape
    mod = lambda i: (mod_of_tile(i), 0, 0)
    return pl.pallas_call(
        _norm_mod_kernel,
        grid=(rows // tr,),
        in_specs=[pl.BlockSpec((tr, d), lambda i: (i, 0)),
                  pl.BlockSpec((1, d), lambda i: (0, 0)),
                  pl.BlockSpec((1, 1, d), mod),
                  pl.BlockSpec((1, 1, d), mod)],
        out_specs=pl.BlockSpec((tr, d), lambda i: (i, 0)),
        out_shape=jax.ShapeDtypeStruct((rows, d), BF16),
        compiler_params=_params("parallel"),
    )(x, w.reshape(1, d), shift, scale)


def _in_proj_kinds(n_cols, tn):
    kinds = []
    for head in range(n_cols // HEAD_DIM):
        if OFF_NAK <= head < OFF_NAV:
            kinds.append("k")
        elif OFF_NAQ <= head < OFF_CVB:
            kinds.append("q")
        elif OFF_HGQ <= head < OFF_HGG:
            kinds.append("silu")
        else:
            kinds.append("id")
    per = tn // HEAD_DIM
    return tuple(tuple(kinds[t * per:(t + 1) * per]) for t in range(n_cols // tn))


def _in_proj_kernel(*refs, kinds, with_cast):
    if with_cast:
        a_ref, w_ref, qw_ref, kw_ref, side_ref, o_ref, side_o_ref, wb_ref = refs
        side_o_ref[...] = side_ref[...].astype(BF16)
    else:
        a_ref, w_ref, qw_ref, kw_ref, o_ref, wb_ref = refs
    j = pl.program_id(0)

    @pl.when(pl.program_id(1) == 0)
    def _():
        wb_ref[...] = w_ref[...].astype(BF16)

    def transform(y, kind):
        if kind == "k":
            return _rms(y, kw_ref[...])
        if kind == "q":
            return _rms(y, qw_ref[...]) * ATTN_SCALE
        if kind == "silu":
            return _silu(y)
        return y

    per_slab = MXU_N // HEAD_DIM
    for pattern in sorted(set(kinds)):
        hit = functools.reduce(jnp.logical_or, [j == jj for jj, tk in enumerate(kinds) if tk == pattern])

        @pl.when(hit)
        def _(pattern=pattern):
            a = a_ref[...]
            for c in range(len(pattern) // per_slab):
                cols = slice(c * MXU_N, (c + 1) * MXU_N)
                acc = _dot(a, wb_ref[:, cols])
                parts = [transform(acc[:, s * HEAD_DIM:(s + 1) * HEAD_DIM], pattern[c * per_slab + s])
                         for s in range(per_slab)]
                o_ref[:, cols] = jnp.concatenate(parts, axis=1).astype(o_ref.dtype)


def _in_proj(h, w_in, layer, q_norm, k_norm, n_cols, tm, tn, side_w=None):
    m, k = h.shape
    kinds = _in_proj_kinds(n_cols, tn)
    nj, ni = n_cols // tn, m // tm
    vec = pl.BlockSpec((1, HEAD_DIM), lambda j, i: (0, 0))
    in_specs = [pl.BlockSpec((tm, k), lambda j, i: (i, 0)),
                pl.BlockSpec((None, k, tn), lambda j, i: (layer, 0, j)),
                vec, vec]
    args = [h, w_in, q_norm.reshape(1, HEAD_DIM), k_norm.reshape(1, HEAD_DIM)]
    out_specs = [pl.BlockSpec((tm, tn), lambda j, i: (i, j))]
    out_shape = [jax.ShapeDtypeStruct((m, n_cols), BF16)]
    if side_w is not None:
        _, k2, n2 = side_w.shape
        strips = n2 // HEAD_DIM
        assert strips <= nj * ni
        strip = lambda j, i: jnp.minimum(j * ni + i, strips - 1)
        in_specs.append(pl.BlockSpec((None, k2, HEAD_DIM), lambda j, i: (layer, 0, strip(j, i))))
        args.append(side_w)
        out_specs.append(pl.BlockSpec((k2, HEAD_DIM), lambda j, i: (0, strip(j, i))))
        out_shape.append(jax.ShapeDtypeStruct((k2, n2), BF16))
    res = pl.pallas_call(
        functools.partial(_in_proj_kernel, kinds=kinds, with_cast=side_w is not None),
        grid=(nj, ni),
        in_specs=in_specs,
        out_specs=out_specs,
        out_shape=out_shape,
        scratch_shapes=[pltpu.VMEM((k, tn), BF16)],
        compiler_params=_params("arbitrary", "arbitrary"),
    )(*args)
    return res if side_w is not None else res[0]


def _proj_norm_kernel(*refs, splits, cast_weight, with_norm):
    n_a = len(splits)
    a_refs, rest = refs[:n_a], list(refs[n_a:])
    w_ref, x_ref, g_ref = rest[:3]
    rest = rest[3:]
    if with_norm:
        lw_ref, sc_ref, sh_ref, x1_ref, h2_ref = rest[:5]
        rest = rest[5:]
    else:
        x1_ref = rest.pop(0)
    if cast_weight:
        wb_ref = rest.pop(0)

        @pl.when(pl.program_id(0) == 0)
        def _():
            wb_ref[...] = w_ref[...].astype(BF16)
    else:
        wb_ref = w_ref

    tm, n = x1_ref.shape
    ssq = jnp.zeros((tm, HEAD_DIM), F32)
    for c in range(n // MXU_N):
        cols = slice(c * MXU_N, (c + 1) * MXU_N)
        acc = None
        off = 0
        for a_ref, width in zip(a_refs, splits):
            part = _dot(a_ref[...], wb_ref[off:off + width, cols])
            acc = part if acc is None else acc + part
            off += width
        x1 = x_ref[:, cols] + g_ref[0][:, cols] * acc
        x1_ref[:, cols] = x1
        if with_norm:
            sq = x1 * x1
            for s in range(MXU_N // HEAD_DIM):
                ssq = ssq + sq[:, s * HEAD_DIM:(s + 1) * HEAD_DIM]
    if with_norm:
        inv = lax.rsqrt(jnp.sum(ssq, axis=-1, keepdims=True) * (1.0 / n) + EPS)
        gain = lw_ref[...] * (1.0 + sc_ref[0])
        for c in range(n // MXU_N):
            cols = slice(c * MXU_N, (c + 1) * MXU_N)
            h2_ref[:, cols] = (x1_ref[:, cols] * inv * gain[:, cols] + sh_ref[0][:, cols]).astype(h2_ref.dtype)


def _proj_norm(a_list, w, layer, x, gate, mod_of_tile, tm, norm=None, vmem_limit=VMEM_LIMIT):
    m, n = x.shape
    cast_weight = w.ndim == 3
    k = w.shape[-2]
    splits = tuple(a.shape[1] for a in a_list)
    assert sum(splits) == k
    in_specs = [pl.BlockSpec((tm, s), lambda i: (i, 0)) for s in splits]
    if cast_weight:
        in_specs.append(pl.BlockSpec((None, k, n), lambda i: (layer, 0, 0), pipeline_mode=pl.Buffered(1)))
    else:
        in_specs.append(pl.BlockSpec((k, n), lambda i: (0, 0), pipeline_mode=pl.Buffered(1)))
    in_specs += [pl.BlockSpec((tm, n), lambda i: (i, 0)),
                 pl.BlockSpec((1, 1, n), lambda i: (mod_of_tile(i), 0, 0))]
    args = [*a_list, w, x, gate]
    out_specs = [pl.BlockSpec((tm, n), lambda i: (i, 0))]
    out_shape = [jax.ShapeDtypeStruct((m, n), F32)]
    if norm is not None:
        ln_w, scale, shift, norm_mod = norm
        mod = lambda i: (norm_mod(i), 0, 0)
        in_specs += [pl.BlockSpec((1, n), lambda i: (0, 0)), pl.BlockSpec((1, 1, n), mod),
                     pl.BlockSpec((1, 1, n), mod)]
        args += [ln_w.reshape(1, n), scale, shift]
        out_specs.append(pl.BlockSpec((tm, n), lambda i: (i, 0)))
        out_shape.append(jax.ShapeDtypeStruct((m, n), BF16))
    res = pl.pallas_call(
        functools.partial(_proj_norm_kernel, splits=splits, cast_weight=cast_weight, with_norm=norm is not None),
        grid=(m // tm,),
        in_specs=in_specs,
        out_specs=out_specs,
        out_shape=out_shape,
        scratch_shapes=[pltpu.VMEM((k, n), BF16)] if cast_weight else [],
        compiler_params=_params("arbitrary", vmem_limit=vmem_limit),
    )(*args)
    return res if norm is not None else res[0]


def _shift_rows(x, prev_row, next_row):
    r = x.shape[0]
    idx = lax.broadcasted_iota(jnp.int32, x.shape, 0)
    down = jnp.where(idx == 0, prev_row, pltpu.roll(x, 1, axis=0))
    up = jnp.where(idx == r - 1, next_row, pltpu.roll(x, r - 1, axis=0))
    return down, up


def _halo_specs(tr, tc, col_of, tiles_per_seq, n_rows):
    per = tr // HALO
    last_blk = n_rows // HALO - 1
    cur = pl.BlockSpec((tr, tc), lambda i, j: (i, col_of(j)))
    prev = pl.BlockSpec((HALO, tc), lambda i, j: (jnp.maximum(i * per - 1, 0), col_of(j)))
    nxt = pl.BlockSpec((HALO, tc), lambda i, j: (jnp.minimum((i + 1) * per, last_blk), col_of(j)))
    return cur, prev, nxt


def _seq_edges(tiles_per_seq):
    i = pl.program_id(0)
    pos = i % tiles_per_seq
    return (pos != 0).astype(F32), (pos != tiles_per_seq - 1).astype(F32)


def _ffn_up_kernel(*refs, tm, seq_len, with_cast):
    if with_cast:
        (h_ref, hp_ref, hn_ref, wg_ref, wv_ref, cg_ref, cv_ref, bg_ref, bv_ref, side_ref,
         o_ref, side_o_ref, lhs_ref) = refs
        side_o_ref[...] = side_ref[...].astype(BF16)
    else:
        h_ref, hp_ref, hn_ref, wg_ref, wv_ref, cg_ref, cv_ref, bg_ref, bv_ref, o_ref, lhs_ref = refs
    tiles_per_seq = max(seq_len // tm, 1)
    pos = pl.program_id(0) % tiles_per_seq
    has_prev = pos != 0
    has_next = pos != tiles_per_seq - 1

    @pl.when(pl.program_id(1) == 0)
    def _():
        lhs_ref[0:HALO, :] = jnp.where(has_prev, hp_ref[...], jnp.zeros_like(hp_ref))
        lhs_ref[HALO:HALO + tm, :] = h_ref[...]
        lhs_ref[HALO + tm:, :] = jnp.where(has_next, hn_ref[...], jnp.zeros_like(hn_ref))

    lhs = lhs_ref[...]

    def branch(w_ref, c_ref, b_ref):
        acc = _dot(lhs, w_ref[...])
        r = acc.shape[0]
        down, up = pltpu.roll(acc, 1, axis=0), pltpu.roll(acc, r - 1, axis=0)
        if seq_len < tm:
            tok = (lax.broadcasted_iota(jnp.int32, acc.shape, 0) + (seq_len - HALO)) % seq_len
            down = jnp.where(tok == 0, 0.0, down)
            up = jnp.where(tok == seq_len - 1, 0.0, up)
        y = down * c_ref[0:1, :] + acc * c_ref[1:2, :] + up * c_ref[2:3, :]
        return y[HALO:HALO + tm, :] + b_ref[...]

    gate = branch(wg_ref, cg_ref, bg_ref)
    val = branch(wv_ref, cv_ref, bv_ref)
    o_ref[...] = (_silu(gate) * val).astype(o_ref.dtype)


def _ffn_up_conv_gate(h, w_up, cw, cb, seq_len, tm, tn, side_w=None, layer=0):
    rows, k = h.shape
    assert seq_len % tm == 0 or (tm % seq_len == 0 and seq_len >= HALO)
    nj = D_FF // tn
    per = tm // HALO
    last_blk = rows // HALO - 1
    cb2 = cb.reshape(1, 2 * D_FF)
    steps = (rows // tm) * nj
    if side_w is not None:
        _, k2, n2 = side_w.shape
        strips = max(d for d in range(1, steps + 1) if k2 % d == 0 and (k2 // d) % HALO == 0)
        strip = lambda i, j: jnp.minimum(i * nj + j, strips - 1)
        side_in = [pl.BlockSpec((None, k2 // strips, n2), lambda i, j: (layer, strip(i, j), 0))]
        side_out = [pl.BlockSpec((k2 // strips, n2), lambda i, j: (strip(i, j), 0))]
        side_shape = [jax.ShapeDtypeStruct((k2, n2), BF16)]
        side_args = [side_w]
    else:
        side_in, side_out, side_shape, side_args = [], [], [], []
    in_specs = [pl.BlockSpec((tm, k), lambda i, j: (i, 0)),
                pl.BlockSpec((HALO, k), lambda i, j: (jnp.maximum(i * per - 1, 0), 0)),
                pl.BlockSpec((HALO, k), lambda i, j: (jnp.minimum((i + 1) * per, last_blk), 0)),
                pl.BlockSpec((k, tn), lambda i, j: (0, j)),
                pl.BlockSpec((k, tn), lambda i, j: (0, j + nj)),
                pl.BlockSpec((CONV_W, tn), lambda i, j: (0, j)),
                pl.BlockSpec((CONV_W, tn), lambda i, j: (0, j + nj)),
                pl.BlockSpec((1, tn), lambda i, j: (0, j)),
                pl.BlockSpec((1, tn), lambda i, j: (0, j + nj))]
    res = pl.pallas_call(
        functools.partial(_ffn_up_kernel, tm=tm, seq_len=seq_len, with_cast=side_w is not None),
        grid=(rows // tm, nj),
        in_specs=in_specs + side_in,
        out_specs=[pl.BlockSpec((tm, tn), lambda i, j: (i, j))] + side_out,
        out_shape=[jax.ShapeDtypeStruct((rows, D_FF), BF16)] + side_shape,
        scratch_shapes=[pltpu.VMEM((tm + 2 * HALO, k), BF16)],
        compiler_params=_params("arbitrary", "arbitrary"),
    )(h, h, h, w_up, w_up, cw, cw, cb2, cb2, *side_args)
    return res if side_w is not None else res[0]


def _short_conv_kernel(b_ref, c_ref, cp_ref, cn_ref, v_ref, vp_ref, vn_ref, w_ref, nw_ref, o_ref, *, tiles_per_seq):
    has_prev, has_next = _seq_edges(tiles_per_seq)
    p = c_ref[...].astype(F32) * v_ref[...].astype(F32)
    prev_row = cp_ref[HALO - 1:HALO, :].astype(F32) * vp_ref[HALO - 1:HALO, :].astype(F32) * has_prev
    next_row = cn_ref[0:1, :].astype(F32) * vn_ref[0:1, :].astype(F32) * has_next
    down, up = _shift_rows(p, prev_row, next_row)
    y = b_ref[...].astype(F32) * (down * w_ref[0:1, :] + p * w_ref[1:2, :] + up * w_ref[2:3, :])
    for h in range(N_CV):
        sl = slice(h * HEAD_DIM, (h + 1) * HEAD_DIM)
        o_ref[:, sl] = _rms(y[:, sl], nw_ref[:, sl]).astype(o_ref.dtype)


def _short_conv(u, cv_w, cv_onorm, seq_len, tr):
    rows = u.shape[0]
    tiles_per_seq = seq_len // tr
    blk = lambda off: (lambda j: off * HEAD_DIM // CV_W)
    b_spec = pl.BlockSpec((tr, CV_W), lambda i, j: (i, OFF_CVB * HEAD_DIM // CV_W))
    c_specs = _halo_specs(tr, CV_W, blk(OFF_CVC), tiles_per_seq, rows)
    v_specs = _halo_specs(tr, CV_W, blk(OFF_CVV), tiles_per_seq, rows)
    return pl.pallas_call(
        functools.partial(_short_conv_kernel, tiles_per_seq=tiles_per_seq),
        grid=(rows // tr, 1),
        in_specs=[b_spec, *c_specs, *v_specs,
                  pl.BlockSpec((CONV_W, CV_W), lambda i, j: (0, 0)),
                  pl.BlockSpec((1, CV_W), lambda i, j: (0, 0))],
        out_specs=pl.BlockSpec((tr, CV_W), lambda i, j: (i, 0)),
        out_shape=jax.ShapeDtypeStruct((rows, CV_W), BF16),
        compiler_params=_params("parallel", "arbitrary"),
    )(u, u, u, u, u, u, u, cv_w, cv_onorm.reshape(1, CV_W))


def _hgrn_structure(chunk, reverse):
    idx = np.arange(chunk)
    i, t = idx[:, None], idx[None, :]
    mats = [t <= i, t > i]
    masks = []
    s = chunk // 2
    while s >= 1:
        blk = idx // (2 * s)
        upper = (idx % (2 * s)) >= s
        mid = blk * 2 * s + s - 1
        a = np.where(upper[:, None], (t > mid[:, None]) & (t <= i), (t > i) & (t <= mid[:, None]))
        mats.append(a)
        masks.append((blk[:, None] == blk[None, :]) & upper[:, None] & ~upper[None, :])
        s //= 2
    mats = np.stack(mats).astype(np.float32)
    masks = np.stack(masks).astype(np.float32)
    if reverse:
        mats = mats[:, ::-1, ::-1]
        masks = masks[:, ::-1, ::-1]
    return mats.reshape(-1, chunk), masks


def _hgrn_kernel(*refs, chunk, n_sub, n_levels, reverse, mode):
    if mode == "final":
        (z_ref, v_ref, q_ref, lb_ref, a_ref, m_ref, s0_ref, g_ref, ofw_ref, nw_ref,
         o_ref, sfin_ref, st_ref) = refs
    elif mode == "raw":
        z_ref, v_ref, q_ref, lb_ref, a_ref, m_ref, s0_ref, o_ref, sfin_ref, st_ref = refs
    else:
        z_ref, v_ref, lb_ref, a_ref, s0_ref, sfin_ref, st_ref = refs
    c = pl.program_id(1)

    @pl.when(c == 0)
    def _():
        st_ref[...] = s0_ref[0]

    lb = lb_ref[...]
    a = a_ref[...]
    tot_row = 0 if reverse else chunk - 1
    subs = list(range(n_sub - 1, -1, -1) if reverse else range(n_sub))

    def gates(sub):
        rows = slice(sub * chunk, (sub + 1) * chunk)
        z = z_ref[rows, :].astype(F32)
        f = lb + (1.0 - lb) * jax.nn.sigmoid(z)
        log_f = jnp.log(jnp.maximum(f, F_FLOOR))
        k_all = (1.0 - lb) * jax.nn.sigmoid(-z)
        expo = _dot(a, log_f.astype(BF16))
        return k_all, v_ref[rows, :].astype(F32), expo

    def products(sub, h, k_all, v_all, expo):
        rows = slice(sub * chunk, (sub + 1) * chunk)
        sl = slice(h * HEAD_DIM, (h + 1) * HEAD_DIM)
        k, v = k_all[:, sl], v_all[:, sl]
        e_rem = jnp.exp(expo[chunk:2 * chunk, sl])
        e_tot = jnp.exp(expo[tot_row:tot_row + 1, sl])
        st = st_ref[h]
        st_ref[h] = st * e_tot + _dot_tn(v.astype(BF16), (k * e_rem).astype(BF16))
        if mode == "state":
            return None
        q = q_ref[rows, sl].astype(F32)
        e_cum = jnp.exp(expo[0:chunk, sl])
        o_inter = _dot_nt((q * e_cum).astype(BF16), st.astype(BF16))
        parts = []
        for lvl in range(n_levels):
            e = jnp.exp(expo[(2 + lvl) * chunk:(3 + lvl) * chunk, sl])
            parts.append(_dot_nt((q * e).astype(BF16), (k * e).astype(BF16)))
        diag = jnp.sum(q * k, axis=-1, keepdims=True)
        return o_inter, parts, diag, v

    def readout(sub, h, o_inter, parts, diag, v):
        rows = slice(sub * chunk, (sub + 1) * chunk)
        sl = slice(h * HEAD_DIM, (h + 1) * HEAD_DIM)
        att = None
        for lvl, part in enumerate(parts):
            att = m_ref[lvl] * part if att is None else att + m_ref[lvl] * part
        o = o_inter + _dot(att.astype(BF16), v.astype(BF16)) + diag * v
        if mode == "final":
            y = _rms(o + ofw_ref[rows, sl], nw_ref[...]) * _silu(g_ref[rows, sl].astype(F32))
            o_ref[rows, sl] = y.astype(o_ref.dtype)
        else:
            o_ref[rows, sl] = o.astype(o_ref.dtype)

    tiles = [(sub, h) for sub in subs for h in range(N_HG)]
    cache = {}

    def stage(sub, h):
        if sub not in cache:
            cache.clear()
            cache[sub] = gates(sub)
        return products(sub, h, *cache[sub])

    queue = [stage(*tile) for tile in tiles[:TRACE_AHEAD]]
    for t, tile in enumerate(tiles):
        if t + TRACE_AHEAD < len(tiles):
            queue.append(stage(*tiles[t + TRACE_AHEAD]))
        done = queue.pop(0)
        if mode != "state":
            readout(*tile, *done)

    @pl.when(c == pl.num_programs(1) - 1)
    def _():
        sfin_ref[0] = st_ref[...]


def _hgrn_scan(u, lb, s0, seq_len, *, reverse, mode, o_fw=None, norm_w=None):
    rows = u.shape[0]
    batch = rows // seq_len
    blk = min(HG_BLOCK, seq_len)
    chunk = min(HG_CHUNK, blk)
    n_blocks = seq_len // blk
    mats, masks = _hgrn_structure(chunk, reverse)
    n_levels = masks.shape[0]
    if mode == "state":
        mats = mats[:2 * chunk]
    per_blk = HG_W // HEAD_DIM

    def tok(off):
        col = off // per_blk
        if reverse:
            return pl.BlockSpec((blk, HG_W), lambda b, c: (b * n_blocks + n_blocks - 1 - c, col))
        return pl.BlockSpec((blk, HG_W), lambda b, c: (b * n_blocks + c, col))

    const2 = lambda shape: pl.BlockSpec(shape, lambda b, c: (0, 0))
    state_spec = pl.BlockSpec((1, N_HG, HEAD_DIM, HEAD_DIM), lambda b, c: (b, 0, 0, 0))
    state_shape = jax.ShapeDtypeStruct((batch, N_HG, HEAD_DIM, HEAD_DIM), F32)
    z_spec = tok(OFF_FBW if reverse else OFF_FFW)
    lb2, mats_b = lb.reshape(1, HG_W), jnp.asarray(mats, BF16)
    if mode == "state":
        in_specs = [z_spec, tok(OFF_I), const2((1, HG_W)), const2(mats.shape), state_spec]
        args = [u, u, lb2, mats_b, s0]
        out_specs, out_shape = [state_spec], [state_shape]
    else:
        in_specs = [z_spec, tok(OFF_I), tok(OFF_HGQ), const2((1, HG_W)), const2(mats.shape),
                    pl.BlockSpec(masks.shape, lambda b, c: (0, 0, 0)), state_spec]
        args = [u, u, u, lb2, mats_b, jnp.asarray(masks, F32), s0]
        if mode == "final":
            in_specs += [tok(OFF_HGG), tok(0), const2((1, HEAD_DIM))]
            args += [u, o_fw, norm_w.reshape(1, HEAD_DIM)]
        out_specs = [tok(0), state_spec]
        out_shape = [jax.ShapeDtypeStruct((rows, HG_W), BF16 if mode == "final" else F32), state_shape]
    res = pl.pallas_call(
        functools.partial(_hgrn_kernel, chunk=chunk, n_sub=blk // chunk, n_levels=n_levels, reverse=reverse,
                          mode=mode),
        grid=(batch, n_blocks),
        in_specs=in_specs,
        out_specs=out_specs,
        out_shape=out_shape,
        scratch_shapes=[pltpu.VMEM((N_HG, HEAD_DIM, HEAD_DIM), F32)],
        compiler_params=_params("parallel", "arbitrary"),
    )(*args)
    return (None, res[0]) if mode == "state" else (res[0], res[1])


def _na_bias_table(rpb, rows):
    n_rb = rows // NA_WIN_R
    n_cb = GRID_W // NA_QBLK_C
    n_dr, n_dc = 2 * NA_WIN_R - 1, 2 * NA_WIN_C - 1
    n_k = NA_KEY_R * NA_KEY_C
    offs, row_mask = [], []
    for rb in (0, min(1, n_rb - 1), n_rb - 1):
        k_row0 = int(np.clip(rb * NA_WIN_R - NA_WIN_R // 2, 0, rows - NA_KEY_R))
        q_r = rb * NA_WIN_R + np.arange(NA_WIN_R)
        k_r = k_row0 + np.arange(NA_KEY_R)
        r_start = np.clip(q_r - NA_WIN_R // 2, 0, rows - NA_WIN_R)
        ok_r = (k_r[None, :] >= r_start[:, None]) & (k_r[None, :] < r_start[:, None] + NA_WIN_R)
        row_mask.append(np.repeat(np.where(ok_r, 0.0, NEG_INF), NA_KEY_C, axis=1))
        offs.append(k_row0 - rb * NA_WIN_R + NA_WIN_R - 1)
    row_mask = jnp.asarray(np.stack(row_mask), F32)
    pad_lo = NA_WIN_R - 1 - min(offs)
    per_vreg = HEAD_DIM // NA_KEY_C
    n_a = -(-(NA_KEY_R - 1 + max(offs) + pad_lo + 1) // per_vreg) * per_vreg
    lanes = n_a * NA_KEY_C
    dr_pad = -(-n_dr // SUBLANE) * SUBLANE
    assert n_dc <= HEAD_DIM and n_a >= n_dr + pad_lo
    starts, col_mask = [], []
    for cb in range(n_cb):
        k_col0 = int(np.clip(cb * NA_QBLK_C - NA_WIN_C // 2, 0, GRID_W - NA_KEY_C))
        q_c = cb * NA_QBLK_C + np.arange(NA_QBLK_C)
        k_c = k_col0 + np.arange(NA_KEY_C)
        c_start = np.clip(q_c - NA_WIN_C // 2, 0, GRID_W - NA_WIN_C)
        ok_c = (k_c[None, :] >= c_start[:, None]) & (k_c[None, :] < c_start[:, None] + NA_WIN_C)
        col_mask.append(np.tile(np.where(ok_c, 0.0, NEG_INF), (1, n_a)))
        starts.append(k_col0 - cb * NA_QBLK_C + NA_WIN_C - 1)
    col_mask = jnp.asarray(np.stack(col_mask), F32)
    lane = np.arange(lanes)
    tile_w = jnp.asarray(np.tile(lane[None, :] % NA_KEY_C == np.arange(NA_KEY_C)[:, None], (3, 1)), BF16)
    pick = jnp.asarray(lane[None, :] // NA_KEY_C == np.arange(dr_pad)[:, None] + pad_lo, F32)
    n_heads = rpb.shape[0]
    rpb_pad = jnp.pad(rpb.astype(F32), ((0, 0), (0, dr_pad - n_dr), (0, HEAD_DIM - n_dc)))
    return pl.pallas_call(
        functools.partial(_na_bias_kernel, offs=tuple(offs), pad_lo=pad_lo, starts=tuple(starts)),
        grid=(n_heads,),
        in_specs=[pl.BlockSpec((1, dr_pad, HEAD_DIM), lambda h: (h, 0, 0)),
                  pl.BlockSpec((n_cb, NA_QBLK_C, lanes), lambda h: (0, 0, 0)),
                  pl.BlockSpec((3, NA_WIN_R, n_k), lambda h: (0, 0, 0)),
                  pl.BlockSpec((3 * NA_KEY_C, lanes), lambda h: (0, 0)),
                  pl.BlockSpec((dr_pad, lanes), lambda h: (0, 0))],
        out_specs=pl.BlockSpec((1, 3, n_cb, NA_WIN_R * NA_QBLK_C, n_k), lambda h: (h, 0, 0, 0, 0)),
        out_shape=jax.ShapeDtypeStruct((n_heads, 3, n_cb, NA_WIN_R * NA_QBLK_C, n_k), F32),
        compiler_params=_params("parallel"),
    )(rpb_pad, col_mask, row_mask, tile_w, pick)


def _na_bias_kernel(rpb_ref, cm_ref, rm_ref, w_ref, pick_ref, o_ref, *, offs, pad_lo, starts):
    n_cb = cm_ref.shape[0]
    lanes = cm_ref.shape[-1]
    per = HEAD_DIM // NA_KEY_C
    n_k = o_ref.shape[-1]
    table = rpb_ref[0]
    dr_pad = table.shape[0]
    for cb in range(n_cb):
        windows = []
        for c in range(NA_QBLK_C):
            rolled = pltpu.roll(table, (c - starts[cb]) % HEAD_DIM, axis=1)
            windows.append(rolled[:, :NA_KEY_C])
        win = jnp.concatenate(windows, axis=0)
        hi = win.astype(BF16)
        mid = (win - hi.astype(F32)).astype(BF16)
        lo = (win - hi.astype(F32) - mid.astype(F32)).astype(BF16)
        tiled = _dot(jnp.concatenate([hi, mid, lo], axis=1), w_ref[...])
        t = jnp.sum(tiled.reshape(NA_QBLK_C, dr_pad, lanes) * pick_ref[...][None], axis=1) + cm_ref[cb]
        shifted = [t] + [pltpu.roll(t, lanes - NA_KEY_C * s, axis=1) for s in range(1, per)]
        for cls, off in enumerate(offs):
            for r in range(NA_WIN_R):
                m = off - r + pad_lo
                base = (m // per) * HEAD_DIM
                tile = shifted[m % per][:, base:base + n_k] + rm_ref[cls, r:r + 1, :]
                o_ref[0, cls, cb, r * NA_QBLK_C:(r + 1) * NA_QBLK_C, :] = tile


def _na_kernel(q_ref, k0_ref, k1_ref, k2_ref, k3_ref, v0_ref, v1_ref, v2_ref, v3_ref, kc_ref, vc_ref, bias_ref,
               ow_ref, o_ref):
    n_cb = GRID_W // NA_QBLK_C
    q_all, k_all, v_all = (q_ref,), (k0_ref, k1_ref, k2_ref, k3_ref), (v0_ref, v1_ref, v2_ref, v3_ref)

    def patch(refs, hh, rows, col0, cols):
        hs = slice(hh * HEAD_DIM, (hh + 1) * HEAD_DIM)
        x3 = jnp.concatenate([r[:, hs].astype(F32) for r in refs], axis=0).reshape(rows, GRID_W, HEAD_DIM)
        return x3[:, col0:col0 + cols, :].reshape(rows * cols, HEAD_DIM).astype(BF16)

    def scores(hh, cb):
        q0 = cb * NA_QBLK_C
        k_col0 = min(max(q0 - NA_WIN_C // 2, 0), GRID_W - NA_KEY_C)
        qm = patch(q_all, hh, NA_WIN_R, q0, NA_QBLK_C)
        km = patch(k_all, hh, NA_KEY_R, k_col0, NA_KEY_C)
        s_loc = _dot_nt(qm, km) + bias_ref[hh, 0, cb]
        s_ctx = _dot_nt(qm, kc_ref[:, hh * HEAD_DIM:(hh + 1) * HEAD_DIM])
        return s_loc, s_ctx

    def finish(hh, cb, s_loc, s_ctx):
        q0 = cb * NA_QBLK_C
        k_col0 = min(max(q0 - NA_WIN_C // 2, 0), GRID_W - NA_KEY_C)
        hs = slice(hh * HEAD_DIM, (hh + 1) * HEAD_DIM)
        vm = patch(v_all, hh, NA_KEY_R, k_col0, NA_KEY_C)
        mx = jnp.maximum(jnp.max(s_loc, axis=-1, keepdims=True), jnp.max(s_ctx, axis=-1, keepdims=True))
        p_loc = jnp.exp(s_loc - mx)
        p_ctx = jnp.exp(s_ctx - mx)
        den = jnp.sum(p_loc, axis=-1, keepdims=True) + jnp.sum(p_ctx, axis=-1, keepdims=True)
        o = (_dot(p_loc.astype(BF16), vm) + _dot(p_ctx.astype(BF16), vc_ref[:, hs])) / den
        o = _rms(o, ow_ref[hh]).astype(o_ref.dtype)
        for r in range(NA_WIN_R):
            o_ref[r * GRID_W + q0:r * GRID_W + q0 + NA_QBLK_C, hs] = o[r * NA_QBLK_C:(r + 1) * NA_QBLK_C, :]

    tiles = [(hh, cb) for hh in range(NA_HEADS) for cb in range(n_cb)]
    queue = [scores(*tile) for tile in tiles[:TRACE_AHEAD]]
    for t, tile in enumerate(tiles):
        if t + TRACE_AHEAD < len(tiles):
            queue.append(scores(*tiles[t + TRACE_AHEAD]))
        finish(*tile, *queue.pop(0))


def _neighbourhood_attention(u, u_ctx, bias, head0, out_norm, batch, rows, ctx_len):
    assert rows % NA_WIN_R == 0 and rows >= NA_KEY_R
    n_rb = rows // NA_WIN_R
    q_blk = NA_WIN_R * GRID_W
    k_blk = NA_KEY_R * GRID_W // 4
    k_per_batch = rows * GRID_W // k_blk

    width = NA_HEADS * HEAD_DIM
    assert all(off % NA_HEADS == 0 for off in (OFF_NAQ, OFF_NAK, OFF_NAV, head0, N_NA))
    col_q, col_k, col_v = OFF_NAQ // NA_HEADS, OFF_NAK // NA_HEADS, OFF_NAV // NA_HEADS

    def key_spec(col, part):
        def index(h, rb, b):
            first = jnp.clip(rb * NA_WIN_R - NA_WIN_R // 2, 0, rows - NA_KEY_R) * GRID_W // k_blk
            return (b * k_per_batch + first + part, col + h)
        return pl.BlockSpec((k_blk, width), index)

    def row_class(rb):
        return jnp.where(rb == 0, 0, jnp.where(rb == n_rb - 1, 2, 1))

    n_cb = GRID_W // NA_QBLK_C
    in_specs = [pl.BlockSpec((q_blk, width), lambda h, rb, b: (b * n_rb + rb, col_q + h))]
    in_specs += [key_spec(col_k, p) for p in range(4)] + [key_spec(col_v, p) for p in range(4)]
    in_specs += [pl.BlockSpec((ctx_len, width), lambda h, rb, b: (b, col_k + h)),
                 pl.BlockSpec((ctx_len, width), lambda h, rb, b: (b, col_v + h)),
                 pl.BlockSpec((NA_HEADS, 1, n_cb, NA_WIN_R * NA_QBLK_C, NA_KEY_R * NA_KEY_C),
                              lambda h, rb, b: (head0 // NA_HEADS + h, row_class(rb), 0, 0, 0)),
                 pl.BlockSpec((NA_HEADS, 1, HEAD_DIM), lambda h, rb, b: (h, 0, 0))]
    return pl.pallas_call(
        _na_kernel,
        grid=(N_NA // NA_HEADS, n_rb, batch),
        in_specs=in_specs,
        out_specs=pl.BlockSpec((q_blk, width), lambda h, rb, b: (b * n_rb + rb, h)),
        out_shape=jax.ShapeDtypeStruct((u.shape[0], NA_W), BF16),
        compiler_params=_params("parallel", "parallel", "parallel"),
    )(u, u, u, u, u, u, u, u, u, u_ctx, u_ctx, bias, out_norm.reshape(N_NA, 1, HEAD_DIM))


def _ctx_attn_kernel(q_ref, k_ref, v_ref, ow_ref, o_ref):
    for hh in range(NA_HEADS):
        hs = slice(hh * HEAD_DIM, (hh + 1) * HEAD_DIM)
        s = _dot_nt(q_ref[:, hs], k_ref[:, hs])
        p = jnp.exp(s - jnp.max(s, axis=-1, keepdims=True))
        o = _dot(p.astype(BF16), v_ref[:, hs]) / jnp.sum(p, axis=-1, keepdims=True)
        o_ref[:, hs] = _rms(o, ow_ref[hh]).astype(o_ref.dtype)


def _context_attention(u_ctx, out_norm, batch, ctx_len):
    width = NA_HEADS * HEAD_DIM
    tok = lambda off: pl.BlockSpec((ctx_len, width), lambda b, h: (b, off // NA_HEADS + h))
    return pl.pallas_call(
        _ctx_attn_kernel,
        grid=(batch, N_NA // NA_HEADS),
        in_specs=[tok(OFF_NAQ), tok(OFF_NAK), tok(OFF_NAV),
                  pl.BlockSpec((NA_HEADS, 1, HEAD_DIM), lambda b, h: (h, 0, 0))],
        out_specs=tok(0),
        out_shape=jax.ShapeDtypeStruct((u_ctx.shape[0], NA_W), BF16),
        compiler_params=_params("parallel", "parallel"),
    )(u_ctx, u_ctx, u_ctx, out_norm.reshape(N_NA, 1, HEAD_DIM))


def _row_tile(rows, target):
    t = min(rows, target)
    assert rows % t == 0
    return t


def _token_stream(xs, seq_len, mod_row, p, layer, w_up_b, ada_l, mixers, w_down_b=None, next_norm=None):
    rows = xs.shape[0]
    sh2, sc2, g1, g2 = ada_l["sh2"], ada_l["sc2"], ada_l["g1"], ada_l["g2"]

    def mod_of_tile(tile_rows):
        per = max(seq_len // tile_rows, 1)
        return (lambda i: 1 + i // per) if mod_row is None else (lambda i: mod_row)

    tm_o = _row_tile(rows, TM_OUT)
    x1, h2 = _proj_norm(mixers, p["w_out"], layer, xs, g1, mod_of_tile(tm_o), tm_o,
                        norm=(p["ln2"][layer], sc2, sh2, mod_of_tile(tm_o)))
    tm = _row_tile(rows, TM_TOKENS)
    conv = (p["f_cw"][layer], p["f_cb"][layer])
    if w_down_b is None:
        act, w_down_b = _ffn_up_conv_gate(h2, w_up_b, *conv, seq_len, tm, TN_FF, side_w=p["w_down"], layer=layer)
    else:
        act = _ffn_up_conv_gate(h2, w_up_b, *conv, seq_len, tm, TN_FF)
    tm_d = _row_tile(rows, TM_DOWN)
    norm = None if next_norm is None else (*next_norm, mod_of_tile(tm_d))
    return _proj_norm([act], w_down_b, None, x1, g2, mod_of_tile(tm_d), tm_d, norm=norm,
                      vmem_limit=VMEM_LIMIT_DOWN), w_down_b


def kernel(x, c, ctx, c_ctx, w_ada, b_ada, ln1_w, ln2_w, w_in, hg_lb_logits, hg_norm_w, na_q_norm_w, na_k_norm_w,
           na_rpb, na_out_norm_w, cv_w, cv_out_norm_w, w_out, w_up, ffn_conv_w, ffn_conv_b, w_down):
    batch, seq, d = x.shape
    ctx_len = ctx.shape[1]
    rows = seq // GRID_W
    depth = w_ada.shape[0]
    xs = x.reshape(batch * seq, d)
    cs = ctx.reshape(batch * ctx_len, d)

    lb_sm = jax.nn.softmax(hg_lb_logits.astype(F32), axis=1)
    lb_all = jnp.cumsum(lb_sm, axis=1) - lb_sm[:, :1]

    cond = jnp.zeros((SUBLANE, d), F32).at[0].set(c_ctx).at[1:1 + batch].set(c)
    ada = _ada_table(cond, w_ada, b_ada).reshape(depth, SUBLANE, 6, 1, d)

    s_zero = jnp.zeros((batch, N_HG, HEAD_DIM, HEAD_DIM), F32)
    p = {"w_out": w_out, "w_down": w_down, "ln2": ln2_w, "f_cw": ffn_conv_w,
         "f_cb": ffn_conv_b}
    ctx_cols = sum(IN_SPLITS[:5])
    bias = _na_bias_table(na_rpb.reshape((depth * N_NA,) + na_rpb.shape[2:]), rows)
    names = ("sh1", "sc1", "g1", "sh2", "sc2", "g2")
    ada_all = [{n: ada[l, :, i] for i, n in enumerate(names)} for l in range(depth)]
    tm = _row_tile(xs.shape[0], TM_TOKENS)
    tm_c = _row_tile(cs.shape[0], TM_TOKENS)
    tiles_per_seq = seq // tm
    h = _norm_mod(xs, ln1_w[0], ada_all[0]["sh1"], ada_all[0]["sc1"], lambda i: 1 + i // tiles_per_seq, tm)
    hc = _norm_mod(cs, ln1_w[0], ada_all[0]["sh1"], ada_all[0]["sc1"], lambda i: 0, tm_c)
    for l in range(depth):
        last = l == depth - 1
        ada_l = ada_all[l]
        next_norm = None if last else (ln1_w[l + 1], ada_all[l + 1]["sc1"], ada_all[l + 1]["sh1"])
        if (IN_W // TN_IN) * (xs.shape[0] // tm) >= w_up.shape[2] // HEAD_DIM:
            u, w_up_b = _in_proj(h, w_in, l, na_q_norm_w[l], na_k_norm_w[l], IN_W, tm, TN_IN, side_w=w_up)
        else:
            u = _in_proj(h, w_in, l, na_q_norm_w[l], na_k_norm_w[l], IN_W, tm, TN_IN)
            w_up_b = w_up[l].astype(BF16)
        if last:
            uc = _in_proj(hc, w_in, l, na_q_norm_w[l], na_k_norm_w[l], ctx_cols, tm_c, TN_IN_CTX)
        else:
            uc = _in_proj(hc, w_in, l, na_q_norm_w[l], na_k_norm_w[l], IN_W, tm_c, TN_IN)

        if last:
            _, s_fw = _hgrn_scan(uc, lb_all[0, l], s_zero, ctx_len, reverse=False, mode="state")
            _, s_bw = _hgrn_scan(uc, lb_all[1, l], s_zero, ctx_len, reverse=True, mode="state")
        else:
            co_fw, s_fw = _hgrn_scan(uc, lb_all[0, l], s_zero, ctx_len, reverse=False, mode="raw")
            hg_c, s_bw = _hgrn_scan(uc, lb_all[1, l], s_zero, ctx_len, reverse=True, mode="final", o_fw=co_fw,
                                    norm_w=hg_norm_w[l])
        o_fw, _ = _hgrn_scan(u, lb_all[0, l], s_fw, seq, reverse=False, mode="raw")
        hg_out, _ = _hgrn_scan(u, lb_all[1, l], s_bw, seq, reverse=True, mode="final", o_fw=o_fw,
                               norm_w=hg_norm_w[l])

        na_out = _neighbourhood_attention(u, uc, bias, l * N_NA, na_out_norm_w[l], batch, rows, ctx_len)
        cv_out = _short_conv(u, cv_w[l], cv_out_norm_w[l], seq, _row_tile(seq, TM_TOKENS))

        res, w_down_b = _token_stream(xs, seq, None, p, l, w_up_b, ada_l, [hg_out, na_out, cv_out],
                                      next_norm=next_norm)
        if last:
            xs = res
        else:
            xs, h = res
            na_c = _context_attention(uc, na_out_norm_w[l], batch, ctx_len)
            cv_c = _short_conv(uc, cv_w[l], cv_out_norm_w[l], ctx_len, _row_tile(ctx_len, TM_TOKENS))
            (cs, hc), _ = _token_stream(cs, ctx_len, 0, p, l, w_up_b, ada_l, [hg_c, na_c, cv_c],
                                        w_down_b=w_down_b, next_norm=next_norm)
    return xs.reshape(batch, seq, d)
```
